```python
import math
import jax
import jax.numpy as jnp
from jax import lax
import numpy as np

D_MODEL = 1024
BATCH = 8
SEQ = 2048
DEPTH = 2

CHUNK = 64
N_MEM = 256
NORM_EPS = 1e-6
GROUP_NORM_EPS = 64e-5

A_HEAD_DIM = 128
A_WIDTH = D_MODEL // 2
A_HEADS = A_WIDTH // A_HEAD_DIM
CONV_WIDTH = 4
A_COLS = 4 * A_WIDTH + 2 * A_HEADS

B_HEAD_DIM = 64
B_WIDTH = D_MODEL // 2
B_HEADS = B_WIDTH // B_HEAD_DIM
B_DECAY_LORA = 64
B_ICLR_LORA = 64
B_GATE_LORA = 128
B_COLS = 3 * B_WIDTH + B_DECAY_LORA + B_ICLR_LORA + B_GATE_LORA
B_SPLIT_IDX = (B_WIDTH, 2 * B_WIDTH, 3 * B_WIDTH, 3 * B_WIDTH + B_DECAY_LORA,
               3 * B_WIDTH + B_DECAY_LORA + B_ICLR_LORA)
AB_COLS = A_COLS + B_COLS
MIX_WIDTH_AB = A_WIDTH + B_WIDTH

C_KEY_DIM = 128
C_HEADS = D_MODEL // C_KEY_DIM
C_VAL_DIM = D_MODEL // C_HEADS
C_KEY_WIDTH = C_HEADS * C_KEY_DIM
C_VAL_WIDTH = C_HEADS * C_VAL_DIM
C_COLS = 2 * C_KEY_WIDTH + 2 * C_VAL_WIDTH

XA_HEADS = 4
XA_HEAD_DIM = D_MODEL // XA_HEADS

N_GROUPS = 4
EXPERTS_PER_GROUP = 4
N_EXPERTS = N_GROUPS * EXPERTS_PER_GROUP
TOP_K_IN_GROUP = 2
EXPERT_FF = D_MODEL // 4

N_EVEN = (DEPTH + 1) // 2
N_ODD = DEPTH // 2

kernel_name = 'hybrid_gdn_rwkv7_hgrn2_hmoe_encoder'

F32 = jnp.float32


def rms_norm(x, gain):
    xf = x.astype(F32)
    y = xf * lax.rsqrt(jnp.mean(xf * xf, axis=-1, keepdims=True) + NORM_EPS)
    return (y * gain.astype(F32)).astype(x.dtype)


def layer_norm(x, gain, bias, eps):
    xf = x.astype(F32)
    mu = jnp.mean(xf, axis=-1, keepdims=True)
    var = jnp.mean(jnp.square(xf - mu), axis=-1, keepdims=True)
    return ((xf - mu) * lax.rsqrt(var + eps) * gain.astype(F32) + bias.astype(F32)).astype(x.dtype)


def l2_normalize(x):
    xf = x.astype(F32)
    return (xf * lax.rsqrt(jnp.sum(xf * xf, axis=-1, keepdims=True) + 1e-6)).astype(x.dtype)


def causal_short_conv(x, w):
    return lax.conv_general_dilated(
        x, w[:, None, :].astype(x.dtype), window_strides=(1,),
        padding=[(w.shape[0] - 1, 0)], dimension_numbers=('NWC', 'WIO', 'NWC'),
        feature_group_count=x.shape[-1])


def token_shift_lerp(p, mu):
    prev = jnp.pad(p, ((0, 0), (1, 0), (0, 0)))[:, :-1]
    return p + mu.astype(p.dtype) * (prev - p)


def split_heads(t, n_heads, head_dim):
    b, s, _ = t.shape
    return t.reshape(b, s, n_heads, head_dim).transpose(0, 2, 1, 3).astype(F32)


def gated_delta_rule_chunked(q, k, v, log_alpha, beta):
    b, h, s, dk = q.shape
    dv = v.shape[-1]
    nc = s // CHUNK
    q = q.reshape(b, h, nc, CHUNK, dk)
    k = k.reshape(b, h, nc, CHUNK, dk)
    v = v.reshape(b, h, nc, CHUNK, dv)
    beta = beta.reshape(b, h, nc, CHUNK)
    g = jnp.cumsum(log_alpha.reshape(b, h, nc, CHUNK), axis=-1)
    idx = jnp.arange(CHUNK)
    incl = idx[:, None] >= idx[None, :]
    strict = idx[:, None] > idx[None, :]
    decay = jnp.exp(jnp.where(incl, g[..., :, None] - g[..., None, :], -jnp.inf))
    k_beta = k * beta[..., None]
    a_low = jnp.where(strict, jnp.einsum('bhnid,bhnjd->bhnij', k_beta, k) * decay, 0.0)
    m = a_low + jnp.eye(CHUNK, dtype=F32)
    rhs = jnp.concatenate([v * beta[..., None], k_beta * jnp.exp(g)[..., None]], axis=-1)
    sol = lax.linalg.triangular_solve(m, rhs, left_side=True, lower=True, unit_diagonal=True)
    u, w = sol[..., :dv], sol[..., dv:]
    qk = jnp.einsum('bhnid,bhnjd->bhnij', q, k) * decay
    q_dec = q * jnp.exp(g)[..., None]
    k_dec = k * jnp.exp(g[..., -1:] - g)[..., None]
    g_last = jnp.exp(g[..., -1])

    def step(state, inp):
        qk_c, qd_c, w_c, u_c, kd_c, gl_c = inp
        v_new = u_c - jnp.einsum('bhcd,bhde->bhce', w_c, state)
        o = jnp.einsum('bhcd,bhde->bhce', qd_c, state) + jnp.einsum('bhij,bhje->bhie', qk_c, v_new)
        state = state * gl_c[..., None, None] + jnp.einsum('bhcd,bhce->bhde', kd_c, v_new)
        return state, o

    xs = tuple(jnp.moveaxis(t, 2, 0) for t in (qk, q_dec, w, u, k_dec, g_last))
    _, o = lax.scan(step, jnp.zeros((b, h, dk, dv), F32), xs)
    return jnp.moveaxis(o, 0, 2).reshape(b, h, s, dv)


def rwkv7_recurrence(r, w, k, v, kk, a):
    b, _, h, n = r.shape

    def step(state, inp):
        r_t, w_t, k_t, v_t, kk_t, a_t = inp
        sa = jnp.einsum('bhvk,bhk->bhv', state, -kk_t)
        state = (state * w_t[:, :, None, :] + sa[..., None] * (kk_t * a_t)[:, :, None, :]
                 + v_t[..., None] * k_t[:, :, None, :])
        return state, jnp.einsum('bhvk,bhk->bhv', state, r_t)

    xs = tuple(jnp.moveaxis(t, 1, 0) for t in (r, w, k, v, kk, a))
    _, o = lax.scan(step, jnp.zeros((b, h, n, n), F32), xs)
    return jnp.moveaxis(o, 0, 1)


def hgrn2_chunked(q, k, v, log_f):
    b, h, s, dk = q.shape
    dv = v.shape[-1]
    nc = s // CHUNK

    def to_chunks(t):
        return jnp.moveaxis(t.reshape(b, h, nc, CHUNK, t.shape[-1]), 2, 0)

    qc, kc, vc = to_chunks(q), to_chunks(k), to_chunks(v)
    gc = jnp.cumsum(to_chunks(log_f), axis=3)
    idx = jnp.arange(CHUNK)
    incl = (idx[:, None] >= idx[None, :])[:, :, None]

    def step(state, inp):
        q_c, k_c, v_c, g_c = inp
        g_last = g_c[:, :, -1:, :]
        dec = jnp.exp(jnp.where(incl, g_c[:, :, :, None, :] - g_c[:, :, None, :, :], -jnp.inf))
        scores = jnp.einsum('bhid,bhjd,bhijd->bhij', q_c, k_c, dec)
        o = (jnp.einsum('bhid,bhde->bhie', q_c * jnp.exp(g_c), state)
             + jnp.einsum('bhij,bhje->bhie', scores, v_c))
        state = (state * jnp.exp(g_last)[:, :, 0, :, None]
                 + jnp.einsum('bhjd,bhje->bhde', k_c * jnp.exp(g_last - g_c), v_c))
        return state, o

    _, o = lax.scan(step, jnp.zeros((b, h, dk, dv), F32), (qc, kc, vc, gc))
    return jnp.moveaxis(o, 0, 2).reshape(b, h, s, dv)


def deltanet_rwkv_mixer(h, w_in, conv_w, a_log, dt_bias, onorm_a, shift_mu, w0, w2, a0, a2, g2,
                        k_k, k_a, r_k, ln_w, ln_b, w_out):
    bsz, s, _ = h.shape
    proj = h @ w_in
    p_a, p_b = proj[..., :A_COLS], proj[..., A_COLS:]

    qkv = jax.nn.silu(causal_short_conv(p_a[..., :3 * A_WIDTH], conv_w))
    q, k, v = jnp.split(qkv, 3, axis=-1)
    gate_a = p_a[..., 3 * A_WIDTH:4 * A_WIDTH].astype(F32)
    decay_in = p_a[..., 4 * A_WIDTH:4 * A_WIDTH + A_HEADS].astype(F32)
    beta_in = p_a[..., 4 * A_WIDTH + A_HEADS:].astype(F32)
    q = l2_normalize(split_heads(q, A_HEADS, A_HEAD_DIM)) * (A_HEAD_DIM ** -0.5)
    k = l2_normalize(split_heads(k, A_HEADS, A_HEAD_DIM))
    v = split_heads(v, A_HEADS, A_HEAD_DIM)
    log_alpha = (-jnp.exp(a_log.astype(F32)) * jax.nn.softplus(decay_in + dt_bias.astype(F32))).transpose(0, 2, 1)
    beta = jax.nn.sigmoid(beta_in).transpose(0, 2, 1)
    o_a = gated_delta_rule_chunked(q, k, v, log_alpha, beta).transpose(0, 2, 1, 3)
    o_a = rms_norm(o_a, onorm_a) * jax.nn.silu(gate_a.reshape(bsz, s, A_HEADS, A_HEAD_DIM))
    o_a = o_a.reshape(bsz, s, A_WIDTH)

    p_b = token_shift_lerp(p_b, shift_mu)
    r, kb, vb, wl, al, gl = jnp.split(p_b, B_SPLIT_IDX, axis=-1)
    hd = (bsz, s, B_HEADS, B_HEAD_DIM)
    w_raw = (w0 + jnp.tanh(wl) @ w2).astype(F32)
    decay_b = jnp.exp(-jnp.exp(-jax.nn.softplus(-w_raw) - 0.5)).reshape(hd)
    iclr = jax.nn.sigmoid((a0 + al @ a2).astype(F32)).reshape(hd)
    gate_b = (jax.nn.sigmoid(gl) @ g2).astype(F32)
    r = r.astype(F32).reshape(hd)
    kb = kb.astype(F32).reshape(hd)
    vb = vb.astype(F32).reshape(hd)
    kk = l2_normalize(kb * k_k.astype(F32).reshape(B_HEADS, B_HEAD_DIM))
    kb = kb * (1.0 + (iclr - 1.0) * k_a.astype(F32).reshape(B_HEADS, B_HEAD_DIM))
    o_b = rwkv7_recurrence(r, decay_b, kb, vb, kk, iclr)
    o_b = layer_norm(o_b, ln_w.reshape(B_HEADS, B_HEAD_DIM), ln_b.reshape(B_HEADS, B_HEAD_DIM), GROUP_NORM_EPS)
    o_b = o_b + jnp.sum(r * kb * r_k.astype(F32), axis=-1, keepdims=True) * vb
    o_b = o_b.reshape(bsz, s, B_WIDTH) * gate_b

    o = jnp.concatenate([o_a, o_b], axis=-1).astype(h.dtype)
    return o @ w_out


def hgrn_lower_bounds(lb_param):
    p = jax.nn.softmax(lb_param.astype(F32), axis=0)
    return jnp.cumsum(p, axis=0) - p[0]


def hgrn2_mixer(h, w_in, lower_bound, onorm, w_out):
    bsz, s, _ = h.shape
    proj = h @ w_in
    q = proj[..., :C_KEY_WIDTH]
    f = proj[..., C_KEY_WIDTH:2 * C_KEY_WIDTH]
    i = proj[..., 2 * C_KEY_WIDTH:2 * C_KEY_WIDTH + C_VAL_WIDTH]
    g = proj[..., 2 * C_KEY_WIDTH + C_VAL_WIDTH:]
    q = jax.nn.silu(q.astype(F32))
    lb = lower_bound.astype(F32)
    log_f = jnp.logaddexp(jnp.log(lb), jnp.log1p(-lb) + jax.nn.log_sigmoid(f.astype(F32)))
    k = -jnp.expm1(log_f)
    o = hgrn2_chunked(split_heads(q, C_HEADS, C_KEY_DIM), split_heads(k, C_HEADS, C_KEY_DIM),
                      split_heads(i, C_HEADS, C_VAL_DIM), split_heads(log_f, C_HEADS, C_KEY_DIM))
    o = rms_norm(o.transpose(0, 2, 1, 3), onorm) * jax.nn.silu(g.astype(F32).reshape(bsz, s, C_HEADS, C_VAL_DIM))
    return o.reshape(bsz, s, C_VAL_WIDTH).astype(h.dtype) @ w_out


def memory_cross_attention(h, mem_n, wq, wkv, wo):
    bsz, s, d = h.shape
    m = mem_n.shape[1]
    q = (h @ wq).reshape(bsz, s, XA_HEADS, XA_HEAD_DIM)
    k, v = jnp.split(mem_n @ wkv, 2, axis=-1)
    k = k.reshape(bsz, m, XA_HEADS, XA_HEAD_DIM)
    v = v.reshape(bsz, m, XA_HEADS, XA_HEAD_DIM)
    scores = jnp.einsum('bthd,bmhd->bhtm', q, k).astype(F32) * (XA_HEAD_DIM ** -0.5)
    p = jax.nn.softmax(scores, axis=-1).astype(v.dtype)
    o = jnp.einsum('bhtm,bmhd->bthd', p, v).reshape(bsz, s, d)
    return o @ wo


def hierarchical_moe(h, w_group, b_group, w_expert, b_expert, w1, w3, w2):
    bsz, s, d = h.shape
    x = h.reshape(bsz * s, d)
    group_logits = (x @ w_group + b_group).astype(F32)
    group_prob = jax.nn.softmax(group_logits, axis=-1)
    group_sel = jnp.argmax(group_logits, axis=-1)
    group_gate = jnp.take_along_axis(group_prob, group_sel[:, None], axis=-1)
    expert_logits = (x @ w_expert + b_expert).astype(F32).reshape(-1, N_GROUPS, EXPERTS_PER_GROUP)
    in_group = jnp.take_along_axis(expert_logits, group_sel[:, None, None], axis=1)[:, 0]
    top_p, top_i = lax.top_k(jax.nn.softmax(in_group, axis=-1), TOP_K_IN_GROUP)
    gates = group_gate * top_p / jnp.sum(top_p, axis=-1, keepdims=True)
    expert_id = group_sel[:, None] * EXPERTS_PER_GROUP + top_i
    combine = jnp.einsum('nk,nke->ne', gates, jax.nn.one_hot(expert_id, N_EXPERTS, dtype=F32))
    hid = jax.nn.silu(jnp.einsum('nd,edf->nef', x, w1)) * jnp.einsum('nd,edf->nef', x, w3)
    hid = hid * combine[:, :, None].astype(hid.dtype)
    y = jnp.einsum('nef,efd->nd', hid, w2)
    return y.reshape(bsz, s, d)


def setup_inputs(seed: int = 0) -> dict:
    key = jax.random.key(seed)
    ks = iter(jax.random.split(key, 64))

    def nrm(shape, scale):
        return jax.random.normal(next(ks), shape, F32) * scale

    def uni(shape, lo, hi):
        return jax.random.uniform(next(ks), shape, F32, lo, hi)

    d = D_MODEL
    ne, no = N_EVEN, N_ODD
    dt = jnp.exp(uni((ne, A_HEADS), math.log(1e-3), math.log(1e-1)))
    return {
        'x': nrm((BATCH, SEQ, d), 1.0),
        'mem': nrm((BATCH, N_MEM, d), 1.0),
        'norm_mix': 1.0 + nrm((DEPTH, d), 0.02),
        'norm_xattn': 1.0 + nrm((DEPTH, d), 0.02),
        'norm_mem': 1.0 + nrm((DEPTH, d), 0.02),
        'norm_ffn': 1.0 + nrm((DEPTH, d), 0.02),
        'norm_final': 1.0 + nrm((d,), 0.02),
        'ab_w_in': nrm((ne, d, AB_COLS), d ** -0.5),
        'ab_conv': nrm((ne, CONV_WIDTH, 3 * A_WIDTH), CONV_WIDTH ** -0.5),
        'ab_a_log': jnp.log(uni((ne, A_HEADS), 1.0, 16.0)),
        'ab_dt_bias': dt + jnp.log(-jnp.expm1(-dt)),
        'ab_onorm': 1.0 + nrm((ne, A_HEAD_DIM), 0.02),
        'ab_shift_mu': uni((ne, B_COLS), 0.0, 1.0),
        'ab_w0': uni((ne, B_WIDTH), -3.0, 1.0),
        'ab_w2': nrm((ne, B_DECAY_LORA, B_WIDTH), 0.5 * B_DECAY_LORA ** -0.5),
        'ab_a0': nrm((ne, B_WIDTH), 0.5),
        'ab_a2': nrm((ne, B_ICLR_LORA, B_WIDTH), B_ICLR_LORA ** -0.5),
        'ab_g2': nrm((ne, B_GATE_LORA, B_WIDTH), B_GATE_LORA ** -0.5),
        'ab_k_k': 0.85 + nrm((ne, B_WIDTH), 0.05),
        'ab_k_a': 1.0 + nrm((ne, B_WIDTH), 0.05),
        'ab_r_k': nrm((ne, B_HEADS, B_HEAD_DIM), 0.1),
        'ab_ln_w': 1.0 + nrm((ne, B_WIDTH), 0.02),
        'ab_ln_b': nrm((ne, B_WIDTH), 0.02),
        'ab_w_out': nrm((ne, MIX_WIDTH_AB, d), 0.5 * MIX_WIDTH_AB ** -0.5),
        'c_w_in': nrm((no, d, C_COLS), d ** -0.5),
        'c_lb_param': nrm((DEPTH, C_KEY_WIDTH), 0.5),
        'c_onorm': 1.0 + nrm((no, C_VAL_DIM), 0.02),
        'c_w_out': nrm((no, C_VAL_WIDTH, d), 0.5 * C_VAL_WIDTH ** -0.5),
        'xa_wq': nrm((DEPTH, d, d), d ** -0.5),
        'xa_wkv': nrm((DEPTH, d, 2 * d), d ** -0.5),
        'xa_wo': nrm((DEPTH, d, d), 0.5 * d ** -0.5),
        'moe_w_group': nrm((DEPTH, d, N_GROUPS), d ** -0.5),
        'moe_b_group': nrm((DEPTH, N_GROUPS), 0.01),
        'moe_w_expert': nrm((DEPTH, d, N_EXPERTS), d ** -0.5),
        'moe_b_expert': nrm((DEPTH, N_EXPERTS), 0.01),
        'moe_w1': nrm((DEPTH, N_EXPERTS, d, EXPERT_FF), d ** -0.5),
        'moe_w3': nrm((DEPTH, N_EXPERTS, d, EXPERT_FF), d ** -0.5),
        'moe_w2': nrm((DEPTH, N_EXPERTS, EXPERT_FF, d), EXPERT_FF ** -0.5),
    }


def reference(x, mem, norm_mix, norm_xattn, norm_mem, norm_ffn, norm_final,
              ab_w_in, ab_conv, ab_a_log, ab_dt_bias, ab_onorm, ab_shift_mu, ab_w0, ab_w2,
              ab_a0, ab_a2, ab_g2, ab_k_k, ab_k_a, ab_r_k, ab_ln_w, ab_ln_b, ab_w_out,
              c_w_in, c_lb_param, c_onorm, c_w_out,
              xa_wq, xa_wkv, xa_wo,
              moe_w_group, moe_b_group, moe_w_expert, moe_b_expert, moe_w1, moe_w3, moe_w2):
    lower_bounds = hgrn_lower_bounds(c_lb_param)
    for layer in range(DEPTH):
        h = rms_norm(x, norm_mix[layer])
        if layer % 2 == 0:
            e = layer // 2
            mix = deltanet_rwkv_mixer(h, ab_w_in[e], ab_conv[e], ab_a_log[e], ab_dt_bias[e], ab_onorm[e],
                                      ab_shift_mu[e], ab_w0[e], ab_w2[e], ab_a0[e], ab_a2[e], ab_g2[e],
                                      ab_k_k[e], ab_k_a[e], ab_r_k[e], ab_ln_w[e], ab_ln_b[e], ab_w_out[e])
        else:
            o = layer // 2
            mix = hgrn2_mixer(h, c_w_in[o], lower_bounds[layer], c_onorm[o], c_w_out[o])
        x = x + mix.astype(x.dtype)
        xa = memory_cross_attention(rms_norm(x, norm_xattn[layer]), rms_norm(mem, norm_mem[layer]),
                                    xa_wq[layer], xa_wkv[layer], xa_wo[layer])
        x = x + xa.astype(x.dtype)
        ff = hierarchical_moe(rms_norm(x, norm_ffn[layer]), moe_w_group[layer], moe_b_group[layer],
                              moe_w_expert[layer], moe_b_expert[layer], moe_w1[layer], moe_w3[layer],
                              moe_w2[layer])
        x = x + ff.astype(x.dtype)
    return rms_norm(x, norm_final)
```

```python
import functools

import jax
import jax.numpy as jnp
from jax import lax
from jax.experimental import pallas as pl
from jax.experimental.pallas import tpu as pltpu

F32 = jnp.float32
BF16 = jnp.bfloat16
HI = lax.Precision.HIGHEST

CHUNK = 64
SUB = 16
NORM_EPS = 1e-6
GROUP_NORM_EPS = 64e-5
CONV_WIDTH = 4

A_HEAD_DIM = 128
B_HEAD_DIM = 64
B_DECAY_LORA = 64
B_ICLR_LORA = 64
B_GATE_LORA = 128
C_KEY_DIM = 128
XA_HEADS = 4
N_GROUPS = 4
EXPERTS_PER_GROUP = 4
N_EXPERTS = N_GROUPS * EXPERTS_PER_GROUP

LANES = 128
VMEM_LIMIT = 56 * 1024 * 1024


def _cparams(sem):
    return pltpu.CompilerParams(dimension_semantics=sem, vmem_limit_bytes=VMEM_LIMIT)


def _dot(a, b, prec=None):
    return jnp.dot(a, b, preferred_element_type=F32, precision=prec)


def _dot_nt(a, b, prec=None):
    return lax.dot_general(a, b, (((1,), (1,)), ((), ())), preferred_element_type=F32, precision=prec)


def _dot_tn(a, b, prec=None):
    return lax.dot_general(a, b, (((0,), (0,)), ((), ())), preferred_element_type=F32, precision=prec)


def _rms(x, gain):
    return x * lax.rsqrt(jnp.mean(x * x, axis=-1, keepdims=True) + NORM_EPS) * gain


def _sigmoid(x):
    return 1.0 / (1.0 + jnp.exp(-x))


def _silu(x):
    return x * _sigmoid(x)


def _softplus(x):
    return jnp.maximum(x, 0.0) + jnp.log(1.0 + jnp.exp(-jnp.abs(x)))


def _iota2(shape, dim):
    return lax.broadcasted_iota(jnp.int32, shape, dim)


def _tri_inverse(a):
    n = a.shape[0]
    row = _iota2((n, n), 0)
    col = _iota2((n, n), 1)
    same16 = (row >> 4) == (col >> 4)
    same32 = (row >> 5) == (col >> 5)
    eye = (row == col).astype(F32)
    ad = jnp.where(same16, a, 0.0)
    x = eye - ad
    p = _dot(ad, ad, HI)
    x = x + _dot(x, p, HI)
    p = _dot(p, p, HI)
    x = x + _dot(x, p, HI)
    p = _dot(p, p, HI)
    x = x + _dot(x, p, HI)
    l32 = jnp.where(same32 & jnp.logical_not(same16), a, 0.0)
    x = x - _dot(x, _dot(l32, x, HI), HI)
    l64 = jnp.where(same32, 0.0, a)
    x = x - _dot(x, _dot(l64, x, HI), HI)
    return x


def _norm_proj_kernel(n_w, x_ref, g_ref, *refs):
    w_refs, o_refs = refs[:n_w], refs[n_w:]
    xn = _rms(x_ref[...], g_ref[...]).astype(BF16)
    for w_ref, o_ref in zip(w_refs, o_refs):
        n = w_ref.shape[1]
        for c0 in range(0, n, 512):
            c1 = min(n, c0 + 512)
            o_ref[:, c0:c1] = _dot(xn, w_ref[:, c0:c1]).astype(o_ref.dtype)


def _norm_proj(x, gain, ws, out_dtype=F32, tm=256):
    n, d = x.shape
    tm = min(tm, n)
    return pl.pallas_call(
        functools.partial(_norm_proj_kernel, len(ws)),
        grid=(n // tm,),
        in_specs=[pl.BlockSpec((tm, d), lambda i: (i, 0)),
                  pl.BlockSpec((1, d), lambda i: (0, 0))]
                 + [pl.BlockSpec(w.shape, lambda i: (0, 0)) for w in ws],
        out_specs=[pl.BlockSpec((tm, w.shape[1]), lambda i: (i, 0)) for w in ws],
        out_shape=[jax.ShapeDtypeStruct((n, w.shape[1]), out_dtype) for w in ws],
        compiler_params=_cparams(("parallel",)),
        name="norm_proj",
    )(x, gain.reshape(1, d), *ws)


def _out_proj_kernel(n_a, res_ref, *refs):
    a_refs, w_refs, o_ref = refs[:n_a], refs[n_a:2 * n_a], refs[2 * n_a]
    acc = res_ref[...]
    for a_ref, w_ref in zip(a_refs, w_refs):
        acc = acc + _dot(a_ref[...].astype(BF16), w_ref[...])
    o_ref[...] = acc


def _out_proj(res, acts, ws, tm=256):
    n, d = res.shape
    tm = min(tm, n)
    return pl.pallas_call(
        functools.partial(_out_proj_kernel, len(acts)),
        grid=(n // tm,),
        in_specs=[pl.BlockSpec((tm, d), lambda i: (i, 0))]
                 + [pl.BlockSpec((tm, a.shape[1]), lambda i: (i, 0)) for a in acts]
                 + [pl.BlockSpec(w.shape, lambda i: (0, 0)) for w in ws],
        out_specs=pl.BlockSpec((tm, d), lambda i: (i, 0)),
        out_shape=jax.ShapeDtypeStruct((n, d), F32),
        compiler_params=_cparams(("parallel",)),
        name="out_proj",
    )(res, *acts, *ws)


def _gdn_kernel(pa_ref, pdb_ref, convw_ref, alog_ref, dtb_ref, onorm_ref, o_ref, xbuf_ref, state_ref):
    n_heads = state_ref.shape[0]
    width = n_heads * A_HEAD_DIM

    @pl.when(pl.program_id(1) == 0)
    def _():
        xbuf_ref[0:8, :] = jnp.zeros((8, 3 * width), F32)
        state_ref[...] = jnp.zeros_like(state_ref)

    x = pa_ref[:, 0:3 * width]
    xbuf_ref[8:8 + CHUNK, :] = x
    acc = convw_ref[0:1, :] * xbuf_ref[5:5 + CHUNK, :]
    for j in range(1, CONV_WIDTH):
        acc = acc + convw_ref[j:j + 1, :] * xbuf_ref[5 + j:5 + j + CHUNK, :]
    xbuf_ref[0:8, :] = x[CHUNK - 8:CHUNK, :]
    qkv = _silu(acc)

    pdb = pdb_ref[...]
    log_alpha = -jnp.exp(alog_ref[...]) * _softplus(pdb + dtb_ref[...])
    beta_all = _sigmoid(pdb)

    row = _iota2((CHUNK, CHUNK), 0)
    col = _iota2((CHUNK, CHUNK), 1)
    incl = row >= col
    strict = row > col
    g_all = _dot(incl.astype(F32), log_alpha, HI)
    g_all_t = _dot_tn(log_alpha, (row <= col).astype(F32), HI)

    for h in range(n_heads):
        sl = slice(h * A_HEAD_DIM, (h + 1) * A_HEAD_DIM)
        q = qkv[:, sl]
        k = qkv[:, width + h * A_HEAD_DIM: width + (h + 1) * A_HEAD_DIM]
        v = qkv[:, 2 * width + h * A_HEAD_DIM: 2 * width + (h + 1) * A_HEAD_DIM]
        q = q * lax.rsqrt(jnp.sum(q * q, axis=-1, keepdims=True) + 1e-6) * (A_HEAD_DIM ** -0.5)
        k = k * lax.rsqrt(jnp.sum(k * k, axis=-1, keepdims=True) + 1e-6)
        beta = beta_all[:, n_heads + h:n_heads + h + 1]
        g = g_all[:, h:h + 1]
        g_row = g_all_t[h:h + 1, :]
        g_last = g_all[CHUNK - 1:CHUNK, h:h + 1]
        decay = jnp.where(incl, jnp.exp(jnp.where(incl, g - g_row, 0.0)), 0.0)
        k_beta = k * beta
        a_low = jnp.where(strict, _dot_nt(k_beta, k, HI) * decay, 0.0)
        t_inv = _tri_inverse(a_low)
        eg = jnp.exp(g)
        u = _dot(t_inv, v * beta, HI)
        w = _dot(t_inv, k_beta * eg, HI)
        qk = _dot_nt(q, k, HI) * decay
        s = state_ref[h]
        v_new = u - _dot(w, s, HI)
        o = _dot(q * eg, s, HI) + _dot(qk, v_new, HI)
        state_ref[h] = s * jnp.exp(g_last) + _dot_tn(k * jnp.exp(g_last - g), v_new, HI)
        gate = pa_ref[:, 3 * width + h * A_HEAD_DIM: 3 * width + (h + 1) * A_HEAD_DIM]
        o_ref[:, sl] = _rms(o, onorm_ref[...]) * _silu(gate)


def _gdn(pa, pdb, conv_w, alog_row, dtb_row, onorm):
    b, t, cols = pa.shape
    width = cols // 4
    n_heads = width // A_HEAD_DIM
    return pl.pallas_call(
        _gdn_kernel,
        grid=(b, t // CHUNK),
        in_specs=[pl.BlockSpec((None, CHUNK, cols), lambda i, c: (i, c, 0)),
                  pl.BlockSpec((None, CHUNK, LANES), lambda i, c: (i, c, 0)),
                  pl.BlockSpec(conv_w.shape, lambda i, c: (0, 0)),
                  pl.BlockSpec((1, LANES), lambda i, c: (0, 0)),
                  pl.BlockSpec((1, LANES), lambda i, c: (0, 0)),
                  pl.BlockSpec((1, A_HEAD_DIM), lambda i, c: (0, 0))],
        out_specs=pl.BlockSpec((None, CHUNK, width), lambda i, c: (i, c, 0)),
        out_shape=jax.ShapeDtypeStruct((b, t, width), F32),
        scratch_shapes=[pltpu.VMEM((8 + CHUNK, 3 * width), F32),
                        pltpu.VMEM((n_heads, A_HEAD_DIM, A_HEAD_DIM), F32)],
        compiler_params=_cparams(("parallel", "arbitrary")),
        name="gdn",
    )(pa, pdb, conv_w, alog_row, dtb_row, onorm)


def _rwkv_kernel(pb_ref, mu_ref, w0_ref, w2_ref, a0_ref, a2_ref, g2_ref, kk_ref, ka_ref, rk_ref,
                 lnw_ref, lnb_ref, bd_ref, o_ref, carry_ref, state_ref, obuf_ref):
    n_heads = state_ref.shape[0]
    width = n_heads * B_HEAD_DIM

    @pl.when(pl.program_id(1) == 0)
    def _():
        carry_ref[...] = jnp.zeros_like(carry_ref)
        state_ref[...] = jnp.zeros_like(state_ref)

    p = pb_ref[...]
    rows = _iota2(p.shape, 0)
    prev = jnp.where(rows == 0, carry_ref[0:1, :], pltpu.roll(p, 1, 0))
    carry_ref[0:1, :] = p[CHUNK - 1:CHUNK, :]
    ps = p + mu_ref[...] * (prev - p)

    r = ps[:, 0:width]
    k = ps[:, width:2 * width]
    v = ps[:, 2 * width:3 * width]
    wl = ps[:, 3 * width:3 * width + LANES]
    al = ps[:, 3 * width + LANES:3 * width + 2 * LANES]
    gl = ps[:, 3 * width + 2 * LANES:3 * width + 3 * LANES]

    w_raw = w0_ref[...] + _dot(jnp.tanh(wl), w2_ref[...], HI)
    log_w = -jnp.exp(-_softplus(-w_raw) - 0.5)
    iclr = _sigmoid(a0_ref[...] + _dot(al, a2_ref[...], HI))
    gate = _dot(_sigmoid(gl), g2_ref[...], HI)

    bd = bd_ref[...]
    kk = k * kk_ref[...]
    kk = kk * lax.rsqrt(_dot(kk * kk, bd, HI) + 1e-6)
    k2 = k * (1.0 + (iclr - 1.0) * ka_ref[...])
    bvec = kk * iclr

    row = _iota2((CHUNK, CHUNK), 0)
    col = _iota2((CHUNK, CHUNK), 1)
    incl = row >= col
    strict = row > col
    g = _dot(incl.astype(F32), log_w, HI)
    g_last = g[CHUNK - 1:CHUNK, :]
    eng = jnp.exp(-g)
    egl = jnp.exp(g_last - g)
    r_t = r * jnp.exp(g)
    kk_t = kk * jnp.exp(g - log_w)
    b_h = bvec * eng
    k_h = k2 * eng
    b_l = bvec * egl
    k_l = k2 * egl
    p_c = jnp.exp(g_last)

    incl2 = _iota2((CHUNK, 2 * CHUNK), 0) >= (_iota2((CHUNK, 2 * CHUNK), 1) & (CHUNK - 1))
    for h in range(n_heads):
        sl = slice(h * B_HEAD_DIM, (h + 1) * B_HEAD_DIM)
        lhs = jnp.concatenate([kk_t[:, sl], r_t[:, sl]], axis=0)
        rhs = jnp.concatenate([b_h[:, sl], k_h[:, sl]], axis=0)
        m = _dot_nt(lhs, rhs, HI)
        a_bb = jnp.where(strict, m[0:CHUNK, 0:CHUNK], 0.0)
        a_bk = jnp.where(strict, m[0:CHUNK, CHUNK:2 * CHUNK], 0.0)
        a_r = jnp.where(incl2, m[CHUNK:2 * CHUNK, :], 0.0)
        t_inv = _tri_inverse(a_bb)
        s = state_ref[h]
        ls = _dot_nt(lhs, s, HI)
        vh = v[:, sl]
        u = -_dot(t_inv, ls[0:CHUNK] + _dot(a_bk, vh, HI), HI)
        uv = jnp.concatenate([u, vh], axis=0)
        obuf_ref[:, sl] = ls[CHUNK:2 * CHUNK] + _dot(a_r, uv, HI)
        bl_kl = jnp.concatenate([b_l[:, sl], k_l[:, sl]], axis=0)
        state_ref[h] = s * p_c[:, sl] + _dot_tn(uv, bl_kl, HI)

    o = obuf_ref[...]
    inv_n = 1.0 / B_HEAD_DIM
    mean = _dot(o, bd, HI) * inv_n
    cen = o - mean
    var = _dot(cen * cen, bd, HI) * inv_n
    ln = cen * lax.rsqrt(var + GROUP_NORM_EPS) * lnw_ref[...] + lnb_ref[...]
    bonus = _dot(r * k2 * rk_ref[...], bd, HI) * v
    o_ref[...] = (ln + bonus) * gate


def _rwkv(pb, params, bd):
    b, t, cols = pb.shape
    width = (cols - 3 * LANES) // 3
    n_heads = width // B_HEAD_DIM
    return pl.pallas_call(
        _rwkv_kernel,
        grid=(b, t // CHUNK),
        in_specs=[pl.BlockSpec((None, CHUNK, cols), lambda i, c: (i, c, 0))]
                 + [pl.BlockSpec(a.shape, lambda i, c: (0, 0)) for a in params]
                 + [pl.BlockSpec(bd.shape, lambda i, c: (0, 0))],
        out_specs=pl.BlockSpec((None, CHUNK, width), lambda i, c: (i, c, 0)),
        out_shape=jax.ShapeDtypeStruct((b, t, width), F32),
        scratch_shapes=[pltpu.VMEM((8, cols), F32),
                        pltpu.VMEM((n_heads, B_HEAD_DIM, B_HEAD_DIM), F32),
                        pltpu.VMEM((CHUNK, width), F32)],
        compiler_params=_cparams(("parallel", "arbitrary")),
        name="rwkv7",
    )(pb, *params, bd)


def _hgrn_kernel(p_ref, loglb_ref, log1mlb_ref, onorm_ref, o_ref, state_ref):
    n_heads = state_ref.shape[0]
    kw = n_heads * C_KEY_DIM

    @pl.when(pl.program_id(1) == 0)
    def _():
        state_ref[...] = jnp.zeros_like(state_ref)

    q_all = _silu(p_ref[:, 0:kw])
    f = p_ref[:, kw:2 * kw]
    a1 = jnp.broadcast_to(loglb_ref[...], f.shape)
    b1 = log1mlb_ref[...] - _softplus(-f)
    log_f = jnp.maximum(a1, b1) + jnp.log(1.0 + jnp.exp(-jnp.abs(a1 - b1)))
    k_all = 1.0 - jnp.exp(log_f)

    row = _iota2((CHUNK, CHUNK), 0)
    col = _iota2((CHUNK, CHUNK), 1)
    g_all = _dot((row >= col).astype(F32), log_f, HI)
    sub_row = _iota2((SUB, 1), 0)
    n_sub = CHUNK // SUB

    for h in range(n_heads):
        sl = slice(h * C_KEY_DIM, (h + 1) * C_KEY_DIM)
        q, k, g = q_all[:, sl], k_all[:, sl], g_all[:, sl]
        v = p_ref[:, 2 * kw + h * C_KEY_DIM: 2 * kw + (h + 1) * C_KEY_DIM]
        s = state_ref[h]
        o_inter = _dot_nt(q * jnp.exp(g), s, HI)
        pieces = []
        for i in range(n_sub):
            r0 = i * SUB
            qi, ki, gi, vi = q[r0:r0 + SUB], k[r0:r0 + SUB], g[r0:r0 + SUB], v[r0:r0 + SUB]
            oi = jnp.zeros((SUB, C_KEY_DIM), F32)
            for j in range(SUB):
                dec = jnp.exp(jnp.minimum(gi - gi[j:j + 1, :], 0.0))
                sc = jnp.sum(qi * dec * ki[j:j + 1, :], axis=-1, keepdims=True)
                oi = oi + jnp.where(sub_row >= j, sc, 0.0) * vi[j:j + 1, :]
            if i > 0:
                g_ref = g[r0 - 1:r0, :]
                sc = _dot_nt(qi * jnp.exp(gi - g_ref), k[0:r0] * jnp.exp(g_ref - g[0:r0]), HI)
                oi = oi + _dot(sc, v[0:r0], HI)
            pieces.append(oi)
        o = o_inter + jnp.concatenate(pieces, axis=0)
        g_last = g[CHUNK - 1:CHUNK, :]
        state_ref[h] = s * jnp.exp(g_last) + _dot_tn(v, k * jnp.exp(g_last - g), HI)
        gate = p_ref[:, 3 * kw + h * C_KEY_DIM: 3 * kw + (h + 1) * C_KEY_DIM]
        o_ref[:, sl] = _rms(o, onorm_ref[...]) * _silu(gate)


def _hgrn(proj, log_lb, log1m_lb, onorm):
    b, t, cols = proj.shape
    kw = cols // 4
    n_heads = kw // C_KEY_DIM
    return pl.pallas_call(
        _hgrn_kernel,
        grid=(b, t // CHUNK),
        in_specs=[pl.BlockSpec((None, CHUNK, cols), lambda i, c: (i, c, 0)),
                  pl.BlockSpec((1, kw), lambda i, c: (0, 0)),
                  pl.BlockSpec((1, kw), lambda i, c: (0, 0)),
                  pl.BlockSpec((1, C_KEY_DIM), lambda i, c: (0, 0))],
        out_specs=pl.BlockSpec((None, CHUNK, kw), lambda i, c: (i, c, 0)),
        out_shape=jax.ShapeDtypeStruct((b, t, kw), F32),
        scratch_shapes=[pltpu.VMEM((n_heads, C_KEY_DIM, C_KEY_DIM), F32)],
        compiler_params=_cparams(("parallel", "arbitrary")),
        name="hgrn2",
    )(proj, log_lb, log1m_lb, onorm)


def _xattn_kernel(x_ref, gain_ref, wq_ref, k_ref, v_ref, wo_ref, o_ref):
    x = x_ref[...]
    d = x.shape[1]
    hd = d // XA_HEADS
    q = _dot(_rms(x, gain_ref[...]).astype(BF16), wq_ref[...]).astype(BF16)
    outs = []
    for h in range(XA_HEADS):
        sl = slice(h * hd, (h + 1) * hd)
        s = _dot_nt(q[:, sl], k_ref[:, sl]) * (hd ** -0.5)
        p = jnp.exp(s - jnp.max(s, axis=-1, keepdims=True))
        p = p / jnp.sum(p, axis=-1, keepdims=True)
        outs.append(_dot(p.astype(BF16), v_ref[:, sl]).astype(BF16))
    o_ref[...] = x + _dot(jnp.concatenate(outs, axis=-1), wo_ref[...])


def _xattn(x, gain, wq, k, v, wo, tq=256):
    b, t, d = x.shape
    m = k.shape[1]
    tq = min(tq, t)
    return pl.pallas_call(
        _xattn_kernel,
        grid=(b, t // tq),
        in_specs=[pl.BlockSpec((None, tq, d), lambda i, j: (i, j, 0)),
                  pl.BlockSpec((1, d), lambda i, j: (0, 0)),
                  pl.BlockSpec((d, d), lambda i, j: (0, 0)),
                  pl.BlockSpec((None, m, d), lambda i, j: (i, 0, 0)),
                  pl.BlockSpec((None, m, d), lambda i, j: (i, 0, 0)),
                  pl.BlockSpec((d, d), lambda i, j: (0, 0))],
        out_specs=pl.BlockSpec((None, tq, d), lambda i, j: (i, j, 0)),
        out_shape=jax.ShapeDtypeStruct((b, t, d), F32),
        compiler_params=_cparams(("parallel", "parallel")),
        name="xattn",
    )(x, gain.reshape(1, d), wq, k, v, wo)


def _route(logits):
    lane = _iota2(logits.shape, 1)
    neg = -jnp.inf
    gmask = lane < N_GROUPS
    gl = jnp.where(gmask, logits, neg)
    gmax = jnp.max(gl, axis=-1, keepdims=True)
    gsel = jnp.min(jnp.where(gl == gmax, lane, LANES), axis=-1, keepdims=True)
    group_gate = 1.0 / jnp.sum(jnp.where(gmask, jnp.exp(gl - gmax), 0.0), axis=-1, keepdims=True)
    lo = N_GROUPS + EXPERTS_PER_GROUP * gsel
    emask = (lane >= lo) & (lane < lo + EXPERTS_PER_GROUP)
    el = jnp.where(emask, logits, neg)
    ee = jnp.where(emask, jnp.exp(el - jnp.max(el, axis=-1, keepdims=True)), 0.0)
    prob = ee / jnp.sum(ee, axis=-1, keepdims=True)
    p1 = jnp.max(jnp.where(emask, prob, -1.0), axis=-1, keepdims=True)
    i1 = jnp.min(jnp.where(emask & (prob == p1), lane, LANES), axis=-1, keepdims=True)
    rest = emask & (lane != i1)
    p2 = jnp.max(jnp.where(rest, prob, -1.0), axis=-1, keepdims=True)
    i2 = jnp.min(jnp.where(rest & (prob == p2), lane, LANES), axis=-1, keepdims=True)
    scale = group_gate / (p1 + p2)
    return jnp.where(lane == i1, p1 * scale, 0.0) + jnp.where(lane == i2, p2 * scale, 0.0)


def _moe_kernel(final_norm, x_ref, gain_ref, wr_ref, br_ref, w1_ref, w3_ref, w2_ref, gfin_ref, o_ref,
                xn_ref, comb_ref, acc_ref):
    grp = pl.program_id(1)
    ff = w1_ref.shape[1] // EXPERTS_PER_GROUP

    @pl.when(grp == 0)
    def _():
        xn = _rms(x_ref[...], gain_ref[...])
        xn_ref[...] = xn.astype(BF16)
        comb_ref[...] = _route(_dot(xn, wr_ref[...], HI) + br_ref[...])
        acc_ref[...] = jnp.zeros_like(acc_ref)

    xn = xn_ref[...]
    comb = comb_ref[...]
    lane = _iota2(comb.shape, 1)
    acc = acc_ref[...]
    for e in range(EXPERTS_PER_GROUP):
        cs = slice(e * ff, (e + 1) * ff)
        weight = jnp.sum(jnp.where(lane == N_GROUPS + EXPERTS_PER_GROUP * grp + e, comb, 0.0),
                         axis=-1, keepdims=True)
        hid = _silu(_dot(xn, w1_ref[:, cs])) * _dot(xn, w3_ref[:, cs]) * weight
        acc = acc + _dot(hid.astype(BF16), w2_ref[cs, :])
    acc_ref[...] = acc

    @pl.when(grp == N_GROUPS - 1)
    def _():
        y = x_ref[...] + acc_ref[...]
        if final_norm:
            y = _rms(y, gfin_ref[...])
        o_ref[...] = y


def _moe(x, gain, w_router, b_router, w1g, w3g, w2g, gain_final, final_norm, tm=512):
    n, d = x.shape
    tm = min(tm, n)
    gf = w1g.shape[2]
    return pl.pallas_call(
        functools.partial(_moe_kernel, final_norm),
        grid=(n // tm, N_GROUPS),
        in_specs=[pl.BlockSpec((tm, d), lambda i, g: (i, 0)),
                  pl.BlockSpec((1, d), lambda i, g: (0, 0)),
                  pl.BlockSpec((d, LANES), lambda i, g: (0, 0)),
                  pl.BlockSpec((1, LANES), lambda i, g: (0, 0)),
                  pl.BlockSpec((None, d, gf), lambda i, g: (g, 0, 0)),
                  pl.BlockSpec((None, d, gf), lambda i, g: (g, 0, 0)),
                  pl.BlockSpec((None, gf, d), lambda i, g: (g, 0, 0)),
                  pl.BlockSpec((1, d), lambda i, g: (0, 0))],
        out_specs=pl.BlockSpec((tm, d), lambda i, g: (i, 0)),
        out_shape=jax.ShapeDtypeStruct((n, d), F32),
        scratch_shapes=[pltpu.VMEM((tm, d), BF16),
                        pltpu.VMEM((tm, LANES), F32),
                        pltpu.VMEM((tm, d), F32)],
        compiler_params=_cparams(("parallel", "arbitrary")),
        name="hmoe",
    )(x, gain.reshape(1, d), w_router, b_router, w1g, w3g, w2g, gain_final.reshape(1, d))


def _pad_cols(w, n):
    return jnp.pad(w, ((0, 0), (0, n - w.shape[1])))


def _pad_rows(w, n):
    return jnp.pad(w, ((0, n - w.shape[0]), (0, 0)))


def _row(v, n=None):
    v = v.reshape(1, -1).astype(F32)
    return v if n is None else _pad_cols(v, n)


def _group_cols(w):
    e, d, f = w.shape
    return (w.reshape(N_GROUPS, EXPERTS_PER_GROUP, d, f).transpose(0, 2, 1, 3)
            .reshape(N_GROUPS, d, EXPERTS_PER_GROUP * f).astype(BF16))


def kernel(x, mem, norm_mix, norm_xattn, norm_mem, norm_ffn, norm_final, ab_w_in, ab_conv, ab_a_log, ab_dt_bias, ab_onorm, ab_shift_mu, ab_w0, ab_w2, ab_a0, ab_a2, ab_g2, ab_k_k, ab_k_a, ab_r_k, ab_ln_w, ab_ln_b, ab_w_out, c_w_in, c_lb_param, c_onorm, c_w_out, xa_wq, xa_wkv, xa_wo, moe_w_group, moe_b_group, moe_w_expert, moe_b_expert, moe_w1, moe_w3, moe_w2):
    bsz, seq, d = x.shape
    n_mem = mem.shape[1]
    depth = norm_mix.shape[0]
    a_width = d // 2
    b_width = d // 2
    a_heads = a_width // A_HEAD_DIM
    a_cols = 4 * a_width + 2 * a_heads
    n_tok = bsz * seq

    lb_p = jax.nn.softmax(c_lb_param.astype(F32), axis=0)
    lower_bounds = jnp.cumsum(lb_p, axis=0) - lb_p[0]

    bd = None
    x2 = x.reshape(n_tok, d)
    mem2 = mem.reshape(bsz * n_mem, d)
    for layer in range(depth):
        if layer % 2 == 0:
            e = layer // 2
            w_in = ab_w_in[e]
            w_a = w_in[:, :4 * a_width].astype(BF16)
            w_db = _pad_cols(w_in[:, 4 * a_width:a_cols], LANES).astype(BF16)
            wb = w_in[:, a_cols:]
            o1, o2 = 3 * b_width, 3 * b_width + B_DECAY_LORA
            o3 = o2 + B_ICLR_LORA
            w_b = jnp.concatenate([wb[:, :o1], _pad_cols(wb[:, o1:o2], LANES), _pad_cols(wb[:, o2:o3], LANES),
                                   wb[:, o3:]], axis=1).astype(BF16)
            mu = ab_shift_mu[e].reshape(1, -1)
            mu_p = jnp.concatenate([mu[:, :o1], _pad_cols(mu[:, o1:o2], LANES), _pad_cols(mu[:, o2:o3], LANES),
                                    mu[:, o3:]], axis=1)
            pa, pdb, pb = _norm_proj(x2, norm_mix[layer], [w_a, w_db, w_b])
            o_a = _gdn(pa.reshape(bsz, seq, -1), pdb.reshape(bsz, seq, LANES), ab_conv[e],
                       _row(ab_a_log[e], LANES), _row(ab_dt_bias[e], LANES), _row(ab_onorm[e]))
            if bd is None:
                hid = jnp.arange(b_width) // B_HEAD_DIM
                bd = (hid[:, None] == hid[None, :]).astype(F32)
            rwkv_params = [mu_p, _row(ab_w0[e]), _pad_rows(ab_w2[e], LANES), _row(ab_a0[e]),
                           _pad_rows(ab_a2[e], LANES), ab_g2[e], _row(ab_k_k[e]), _row(ab_k_a[e]),
                           _row(ab_r_k[e]), _row(ab_ln_w[e]), _row(ab_ln_b[e])]
            o_b = _rwkv(pb.reshape(bsz, seq, -1), rwkv_params, bd)
            w_out = ab_w_out[e].astype(BF16)
            x2 = _out_proj(x2, [o_a.reshape(n_tok, a_width), o_b.reshape(n_tok, b_width)],
                           [w_out[:a_width], w_out[a_width:]])
        else:
            o = layer // 2
            (proj,) = _norm_proj(x2, norm_mix[layer], [c_w_in[o].astype(BF16)])
            lb = lower_bounds[layer].reshape(1, -1)
            o_c = _hgrn(proj.reshape(bsz, seq, -1), jnp.log(lb), jnp.log1p(-lb), _row(c_onorm[o]))
            x2 = _out_proj(x2, [o_c.reshape(n_tok, -1)], [c_w_out[o].astype(BF16)])

        wkv = xa_wkv[layer].astype(BF16)
        k_mem, v_mem = _norm_proj(mem2, norm_mem[layer], [wkv[:, :d], wkv[:, d:]], out_dtype=BF16)
        x3 = _xattn(x2.reshape(bsz, seq, d), norm_xattn[layer], xa_wq[layer].astype(BF16),
                    k_mem.reshape(bsz, n_mem, d), v_mem.reshape(bsz, n_mem, d), xa_wo[layer].astype(BF16))
        x2 = x3.reshape(n_tok, d)

        w_router = _pad_cols(jnp.concatenate([moe_w_group[layer], moe_w_expert[layer]], axis=1), LANES)
        b_router = _row(jnp.concatenate([moe_b_group[layer], moe_b_expert[layer]]), LANES)
        ff = moe_w2.shape[2]
        x2 = _moe(x2, norm_ffn[layer], w_router, b_router, _group_cols(moe_w1[layer]), _group_cols(moe_w3[layer]),
                  moe_w2[layer].reshape(N_GROUPS, EXPERTS_PER_GROUP * ff, d).astype(BF16),
                  norm_final, final_norm=(layer == depth - 1))
    return x2.reshape(bsz, seq, d)
```

```python
import functools

import jax
import jax.numpy as jnp
from jax import lax
from jax.experimental import pallas as pl
from jax.experimental.pallas import tpu as pltpu

F32 = jnp.float32
BF16 = jnp.bfloat16
HI = lax.Precision.HIGHEST
ONE_PASS = "bf16"

PREC_CUM = HI
PREC_INV = ONE_PASS
PREC_MIX = ONE_PASS
PREC_AUX = ONE_PASS

CHUNK = 64
SUB = 16
NORM_EPS = 1e-6
GROUP_NORM_EPS = 64e-5
CONV_WIDTH = 4

A_HEAD_DIM = 128
B_HEAD_DIM = 64
B_DECAY_LORA = 64
B_ICLR_LORA = 64
B_GATE_LORA = 128
C_KEY_DIM = 128
XA_HEADS = 4
N_GROUPS = 4
EXPERTS_PER_GROUP = 4
N_EXPERTS = N_GROUPS * EXPERTS_PER_GROUP

LANES = 128
VMEM_LIMIT = 56 * 1024 * 1024


def _cparams(sem):
    return pltpu.CompilerParams(dimension_semantics=sem, vmem_limit_bytes=VMEM_LIMIT)


def _dot_dims(a, b, dims, prec):
    if prec == ONE_PASS:
        a, b, prec = a.astype(BF16), b.astype(BF16), None
    return lax.dot_general(a, b, (dims, ((), ())), preferred_element_type=F32, precision=prec)


def _dot(a, b, prec=None):
    return _dot_dims(a, b, ((1,), (0,)), prec)


def _dot_nt(a, b, prec=None):
    return _dot_dims(a, b, ((1,), (1,)), prec)


def _dot_tn(a, b, prec=None):
    return _dot_dims(a, b, ((0,), (0,)), prec)


def _rms(x, gain):
    return x * lax.rsqrt(jnp.mean(x * x, axis=-1, keepdims=True) + NORM_EPS) * gain


def _sigmoid(x):
    return 1.0 / (1.0 + jnp.exp(-x))


def _silu(x):
    return x * _sigmoid(x)


def _softplus(x):
    return jnp.maximum(x, 0.0) + jnp.log(1.0 + jnp.exp(-jnp.abs(x)))


def _iota2(shape, dim):
    return lax.broadcasted_iota(jnp.int32, shape, dim)


def _tri_inverse(mats):
    n = mats[0].shape[0]
    row = _iota2((n, n), 0)
    col = _iota2((n, n), 1)
    same16 = (row >> 4) == (col >> 4)
    same32 = (row >> 5) == (col >> 5)
    eye = (row == col).astype(F32)
    ads = [jnp.where(same16, a, 0.0) for a in mats]
    xs = [eye - ad for ad in ads]
    ps = [_dot(ad, ad, PREC_INV) for ad in ads]
    for stage in range(3):
        xs = [x + _dot(x, p, PREC_INV) for x, p in zip(xs, ps)]
        if stage < 2:
            ps = [_dot(p, p, PREC_INV) for p in ps]
    for lower in ([jnp.where(same32 & jnp.logical_not(same16), a, 0.0) for a in mats],
                  [jnp.where(same32, 0.0, a) for a in mats]):
        ts = [_dot(l, x, PREC_INV) for l, x in zip(lower, xs)]
        xs = [x - _dot(x, t, PREC_INV) for x, t in zip(xs, ts)]
    return xs


def _norm_proj_kernel(n_w, x_ref, g_ref, *refs):
    w_refs, o_refs = refs[:n_w], refs[n_w:]
    xn = _rms(x_ref[...], g_ref[...]).astype(BF16)
    for w_ref, o_ref in zip(w_refs, o_refs):
        n = w_ref.shape[1]
        for c0 in range(0, n, 512):
            c1 = min(n, c0 + 512)
            o_ref[:, c0:c1] = _dot(xn, w_ref[:, c0:c1]).astype(o_ref.dtype)


def _norm_proj(x, gain, ws, out_dtype=F32, tm=256):
    n, d = x.shape
    tm = min(tm, n)
    return pl.pallas_call(
        functools.partial(_norm_proj_kernel, len(ws)),
        grid=(n // tm,),
        in_specs=[pl.BlockSpec((tm, d), lambda i: (i, 0)),
                  pl.BlockSpec((1, d), lambda i: (0, 0))]
                 + [pl.BlockSpec(w.shape, lambda i: (0, 0)) for w in ws],
        out_specs=[pl.BlockSpec((tm, w.shape[1]), lambda i: (i, 0)) for w in ws],
        out_shape=[jax.ShapeDtypeStruct((n, w.shape[1]), out_dtype) for w in ws],
        compiler_params=_cparams(("parallel",)),
        name="norm_proj",
    )(x, gain.reshape(1, d), *ws)


def _out_proj_kernel(n_a, res_ref, *refs):
    a_refs, w_refs, o_ref = refs[:n_a], refs[n_a:2 * n_a], refs[2 * n_a]
    acc = res_ref[...]
    for a_ref, w_ref in zip(a_refs, w_refs):
        acc = acc + _dot(a_ref[...].astype(BF16), w_ref[...])
    o_ref[...] = acc


def _out_proj(res, acts, ws, tm=256):
    n, d = res.shape
    tm = min(tm, n)
    return pl.pallas_call(
        functools.partial(_out_proj_kernel, len(acts)),
        grid=(n // tm,),
        in_specs=[pl.BlockSpec((tm, d), lambda i: (i, 0))]
                 + [pl.BlockSpec((tm, a.shape[1]), lambda i: (i, 0)) for a in acts]
                 + [pl.BlockSpec(w.shape, lambda i: (0, 0)) for w in ws],
        out_specs=pl.BlockSpec((tm, d), lambda i: (i, 0)),
        out_shape=jax.ShapeDtypeStruct((n, d), F32),
        compiler_params=_cparams(("parallel",)),
        name="out_proj",
    )(res, *acts, *ws)


def _gdn_kernel(pa_ref, pdb_ref, convw_ref, alog_ref, dtb_ref, onorm_ref, o_ref, xbuf_ref, state_ref):
    n_heads = state_ref.shape[0]
    width = n_heads * A_HEAD_DIM

    @pl.when(pl.program_id(1) == 0)
    def _():
        xbuf_ref[0:8, :] = jnp.zeros((8, 3 * width), F32)
        state_ref[...] = jnp.zeros_like(state_ref)

    x = pa_ref[:, 0:3 * width]
    xbuf_ref[8:8 + CHUNK, :] = x
    acc = convw_ref[0:1, :] * xbuf_ref[5:5 + CHUNK, :]
    for j in range(1, CONV_WIDTH):
        acc = acc + convw_ref[j:j + 1, :] * xbuf_ref[5 + j:5 + j + CHUNK, :]
    xbuf_ref[0:8, :] = x[CHUNK - 8:CHUNK, :]
    qkv = _silu(acc)

    pdb = pdb_ref[...]
    log_alpha = -jnp.exp(alog_ref[...]) * _softplus(pdb + dtb_ref[...])
    beta_all = _sigmoid(pdb)

    row = _iota2((CHUNK, CHUNK), 0)
    col = _iota2((CHUNK, CHUNK), 1)
    incl = row >= col
    strict = row > col
    g_all = _dot(incl.astype(F32), log_alpha, PREC_CUM)
    g_all_t = _dot_tn(log_alpha, (row <= col).astype(F32), PREC_CUM)

    heads = range(n_heads)
    hd = A_HEAD_DIM
    qs = [qkv[:, h * hd:(h + 1) * hd] for h in heads]
    ks = [qkv[:, width + h * hd: width + (h + 1) * hd] for h in heads]
    vs = [qkv[:, 2 * width + h * hd: 2 * width + (h + 1) * hd] for h in heads]
    qs = [q * lax.rsqrt(jnp.sum(q * q, axis=-1, keepdims=True) + 1e-6) * (hd ** -0.5) for q in qs]
    ks = [k * lax.rsqrt(jnp.sum(k * k, axis=-1, keepdims=True) + 1e-6) for k in ks]
    betas = [beta_all[:, n_heads + h:n_heads + h + 1] for h in heads]
    gs = [g_all[:, h:h + 1] for h in heads]
    g_lasts = [g_all[CHUNK - 1:CHUNK, h:h + 1] for h in heads]
    decays = [jnp.where(incl, jnp.exp(jnp.where(incl, gs[h] - g_all_t[h:h + 1, :], 0.0)), 0.0) for h in heads]
    k_betas = [ks[h] * betas[h] for h in heads]
    egs = [jnp.exp(g) for g in gs]
    a_lows = [jnp.where(strict, _dot_nt(k_betas[h], ks[h], PREC_MIX) * decays[h], 0.0) for h in heads]
    qks = [_dot_nt(qs[h], ks[h], PREC_MIX) * decays[h] for h in heads]
    states = [state_ref[h] for h in heads]
    o_inters = [_dot(qs[h] * egs[h], states[h], PREC_MIX) for h in heads]
    t_invs = _tri_inverse(a_lows)
    us = [_dot(t_invs[h], vs[h] * betas[h], PREC_MIX) for h in heads]
    ws = [_dot(t_invs[h], k_betas[h] * egs[h], PREC_MIX) for h in heads]
    v_news = [us[h] - _dot(ws[h], states[h], PREC_MIX) for h in heads]
    os_ = [o_inters[h] + _dot(qks[h], v_news[h], PREC_MIX) for h in heads]
    for h in heads:
        state_ref[h] = (states[h] * jnp.exp(g_lasts[h])
                        + _dot_tn(ks[h] * jnp.exp(g_lasts[h] - gs[h]), v_news[h], PREC_MIX))
    for h in heads:
        gate = pa_ref[:, 3 * width + h * hd: 3 * width + (h + 1) * hd]
        o_ref[:, h * hd:(h + 1) * hd] = _rms(os_[h], onorm_ref[...]) * _silu(gate)


def _gdn(pa, pdb, conv_w, alog_row, dtb_row, onorm):
    b, t, cols = pa.shape
    width = cols // 4
    n_heads = width // A_HEAD_DIM
    return pl.pallas_call(
        _gdn_kernel,
        grid=(b, t // CHUNK),
        in_specs=[pl.BlockSpec((None, CHUNK, cols), lambda i, c: (i, c, 0)),
                  pl.BlockSpec((None, CHUNK, LANES), lambda i, c: (i, c, 0)),
                  pl.BlockSpec(conv_w.shape, lambda i, c: (0, 0)),
                  pl.BlockSpec((1, LANES), lambda i, c: (0, 0)),
                  pl.BlockSpec((1, LANES), lambda i, c: (0, 0)),
                  pl.BlockSpec((1, A_HEAD_DIM), lambda i, c: (0, 0))],
        out_specs=pl.BlockSpec((None, CHUNK, width), lambda i, c: (i, c, 0)),
        out_shape=jax.ShapeDtypeStruct((b, t, width), F32),
        scratch_shapes=[pltpu.VMEM((8 + CHUNK, 3 * width), F32),
                        pltpu.VMEM((n_heads, A_HEAD_DIM, A_HEAD_DIM), F32)],
        compiler_params=_cparams(("parallel", "arbitrary")),
        name="gdn",
    )(pa, pdb, conv_w, alog_row, dtb_row, onorm)


def _rwkv_kernel(pb_ref, mu_ref, w0_ref, w2_ref, a0_ref, a2_ref, g2_ref, kk_ref, ka_ref, rk_ref,
                 lnw_ref, lnb_ref, bd_ref, o_ref, carry_ref, state_ref, obuf_ref):
    n_heads = state_ref.shape[0]
    width = n_heads * B_HEAD_DIM

    @pl.when(pl.program_id(1) == 0)
    def _():
        carry_ref[...] = jnp.zeros_like(carry_ref)
        state_ref[...] = jnp.zeros_like(state_ref)

    p = pb_ref[...]
    rows = _iota2(p.shape, 0)
    prev = jnp.where(rows == 0, carry_ref[0:1, :], pltpu.roll(p, 1, 0))
    carry_ref[0:1, :] = p[CHUNK - 1:CHUNK, :]
    ps = p + mu_ref[...] * (prev - p)

    r = ps[:, 0:width]
    k = ps[:, width:2 * width]
    v = ps[:, 2 * width:3 * width]
    wl = ps[:, 3 * width:3 * width + LANES]
    al = ps[:, 3 * width + LANES:3 * width + 2 * LANES]
    gl = ps[:, 3 * width + 2 * LANES:3 * width + 3 * LANES]

    w_raw = w0_ref[...] + _dot(jnp.tanh(wl), w2_ref[...], PREC_AUX)
    log_w = -jnp.exp(-_softplus(-w_raw) - 0.5)
    iclr = _sigmoid(a0_ref[...] + _dot(al, a2_ref[...], PREC_AUX))
    gate = _dot(_sigmoid(gl), g2_ref[...], PREC_AUX)

    bd = bd_ref[...]
    kk = k * kk_ref[...]
    kk = kk * lax.rsqrt(_dot(kk * kk, bd, PREC_AUX) + 1e-6)
    k2 = k * (1.0 + (iclr - 1.0) * ka_ref[...])
    bvec = kk * iclr

    row = _iota2((CHUNK, CHUNK), 0)
    col = _iota2((CHUNK, CHUNK), 1)
    incl = row >= col
    strict = row > col
    g = _dot(incl.astype(F32), log_w, PREC_CUM)
    g_last = g[CHUNK - 1:CHUNK, :]
    eng = jnp.exp(-g)
    egl = jnp.exp(g_last - g)
    r_t = r * jnp.exp(g)
    kk_t = kk * jnp.exp(g - log_w)
    b_h = bvec * eng
    k_h = k2 * eng
    b_l = bvec * egl
    k_l = k2 * egl
    p_c = jnp.exp(g_last)

    incl2 = _iota2((CHUNK, 2 * CHUNK), 0) >= (_iota2((CHUNK, 2 * CHUNK), 1) & (CHUNK - 1))
    heads = range(n_heads)
    sls = [slice(h * B_HEAD_DIM, (h + 1) * B_HEAD_DIM) for h in heads]
    lhss = [jnp.concatenate([kk_t[:, sl], r_t[:, sl]], axis=0) for sl in sls]
    rhss = [jnp.concatenate([b_h[:, sl], k_h[:, sl]], axis=0) for sl in sls]
    ms = [_dot_nt(lhss[h], rhss[h], PREC_MIX) for h in heads]
    states = [state_ref[h] for h in heads]
    lss = [_dot_nt(lhss[h], states[h], PREC_MIX) for h in heads]
    vhs = [v[:, sl] for sl in sls]
    a_bks = [jnp.where(strict, m[0:CHUNK, CHUNK:2 * CHUNK], 0.0) for m in ms]
    a_rs = [jnp.where(incl2, m[CHUNK:2 * CHUNK, :], 0.0) for m in ms]
    rhs_u = [lss[h][0:CHUNK] + _dot(a_bks[h], vhs[h], PREC_MIX) for h in heads]
    t_invs = _tri_inverse([jnp.where(strict, m[0:CHUNK, 0:CHUNK], 0.0) for m in ms])
    uvs = [jnp.concatenate([-_dot(t_invs[h], rhs_u[h], PREC_MIX), vhs[h]], axis=0) for h in heads]
    for h in heads:
        obuf_ref[:, sls[h]] = lss[h][CHUNK:2 * CHUNK] + _dot(a_rs[h], uvs[h], PREC_MIX)
    for h in heads:
        bl_kl = jnp.concatenate([b_l[:, sls[h]], k_l[:, sls[h]]], axis=0)
        state_ref[h] = states[h] * p_c[:, sls[h]] + _dot_tn(uvs[h], bl_kl, PREC_MIX)

    o = obuf_ref[...]
    inv_n = 1.0 / B_HEAD_DIM
    mean = _dot(o, bd, PREC_AUX) * inv_n
    cen = o - mean
    var = _dot(cen * cen, bd, PREC_AUX) * inv_n
    ln = cen * lax.rsqrt(var + GROUP_NORM_EPS) * lnw_ref[...] + lnb_ref[...]
    bonus = _dot(r * k2 * rk_ref[...], bd, PREC_AUX) * v
    o_ref[...] = (ln + bonus) * gate


def _rwkv(pb, params, bd):
    b, t, cols = pb.shape
    width = (cols - 3 * LANES) // 3
    n_heads = width // B_HEAD_DIM
    return pl.pallas_call(
        _rwkv_kernel,
        grid=(b, t // CHUNK),
        in_specs=[pl.BlockSpec((None, CHUNK, cols), lambda i, c: (i, c, 0))]
                 + [pl.BlockSpec(a.shape, lambda i, c: (0, 0)) for a in params]
                 + [pl.BlockSpec(bd.shape, lambda i, c: (0, 0))],
        out_specs=pl.BlockSpec((None, CHUNK, width), lambda i, c: (i, c, 0)),
        out_shape=jax.ShapeDtypeStruct((b, t, width), F32),
        scratch_shapes=[pltpu.VMEM((8, cols), F32),
                        pltpu.VMEM((n_heads, B_HEAD_DIM, B_HEAD_DIM), F32),
                        pltpu.VMEM((CHUNK, width), F32)],
        compiler_params=_cparams(("parallel", "arbitrary")),
        name="rwkv7",
    )(pb, *params, bd)


def _hgrn_kernel(p_ref, loglb_ref, log1mlb_ref, onorm_ref, o_ref, state_ref):
    n_heads = state_ref.shape[0]
    kw = n_heads * C_KEY_DIM

    @pl.when(pl.program_id(1) == 0)
    def _():
        state_ref[...] = jnp.zeros_like(state_ref)

    q_all = _silu(p_ref[:, 0:kw])
    f = p_ref[:, kw:2 * kw]
    a1 = jnp.broadcast_to(loglb_ref[...], f.shape)
    b1 = log1mlb_ref[...] - _softplus(-f)
    log_f = jnp.maximum(a1, b1) + jnp.log(1.0 + jnp.exp(-jnp.abs(a1 - b1)))
    k_all = 1.0 - jnp.exp(log_f)

    row = _iota2((CHUNK, CHUNK), 0)
    col = _iota2((CHUNK, CHUNK), 1)
    g_all = _dot((row >= col).astype(F32), log_f, PREC_CUM)
    sub_row = _iota2((SUB, 1), 0)
    n_sub = CHUNK // SUB

    for h in range(n_heads):
        sl = slice(h * C_KEY_DIM, (h + 1) * C_KEY_DIM)
        q, k, g = q_all[:, sl], k_all[:, sl], g_all[:, sl]
        v = p_ref[:, 2 * kw + h * C_KEY_DIM: 2 * kw + (h + 1) * C_KEY_DIM]
        s = state_ref[h]
        o_inter = _dot_nt(q * jnp.exp(g), s, PREC_MIX)
        pieces = []
        for i in range(n_sub):
            r0 = i * SUB
            qi, ki, gi, vi = q[r0:r0 + SUB], k[r0:r0 + SUB], g[r0:r0 + SUB], v[r0:r0 + SUB]
            oi = jnp.zeros((SUB, C_KEY_DIM), F32)
            for j in range(SUB):
                dec = jnp.exp(jnp.minimum(gi - gi[j:j + 1, :], 0.0))
                sc = jnp.sum(qi * dec * ki[j:j + 1, :], axis=-1, keepdims=True)
                oi = oi + jnp.where(sub_row >= j, sc, 0.0) * vi[j:j + 1, :]
            if i > 0:
                g_ref = g[r0 - 1:r0, :]
                sc = _dot_nt(qi * jnp.exp(gi - g_ref), k[0:r0] * jnp.exp(g_ref - g[0:r0]), PREC_MIX)
                oi = oi + _dot(sc, v[0:r0], PREC_MIX)
            pieces.append(oi)
        o = o_inter + jnp.concatenate(pieces, axis=0)
        g_last = g[CHUNK - 1:CHUNK, :]
        state_ref[h] = s * jnp.exp(g_last) + _dot_tn(v, k * jnp.exp(g_last - g), PREC_MIX)
        gate = p_ref[:, 3 * kw + h * C_KEY_DIM: 3 * kw + (h + 1) * C_KEY_DIM]
        o_ref[:, sl] = _rms(o, onorm_ref[...]) * _silu(gate)


def _hgrn(proj, log_lb, log1m_lb, onorm):
    b, t, cols = proj.shape
    kw = cols // 4
    n_heads = kw // C_KEY_DIM
    return pl.pallas_call(
        _hgrn_kernel,
        grid=(b, t // CHUNK),
        in_specs=[pl.BlockSpec((None, CHUNK, cols), lambda i, c: (i, c, 0)),
                  pl.BlockSpec((1, kw), lambda i, c: (0, 0)),
                  pl.BlockSpec((1, kw), lambda i, c: (0, 0)),
                  pl.BlockSpec((1, C_KEY_DIM), lambda i, c: (0, 0))],
        out_specs=pl.BlockSpec((None, CHUNK, kw), lambda i, c: (i, c, 0)),
        out_shape=jax.ShapeDtypeStruct((b, t, kw), F32),
        scratch_shapes=[pltpu.VMEM((n_heads, C_KEY_DIM, C_KEY_DIM), F32)],
        compiler_params=_cparams(("parallel", "arbitrary")),
        name="hgrn2",
    )(proj, log_lb, log1m_lb, onorm)


def _xattn_kernel(x_ref, gain_ref, wq_ref, k_ref, v_ref, wo_ref, o_ref):
    x = x_ref[...]
    d = x.shape[1]
    hd = d // XA_HEADS
    q = _dot(_rms(x, gain_ref[...]).astype(BF16), wq_ref[...]).astype(BF16)
    outs = []
    for h in range(XA_HEADS):
        sl = slice(h * hd, (h + 1) * hd)
        s = _dot_nt(q[:, sl], k_ref[:, sl]) * (hd ** -0.5)
        p = jnp.exp(s - jnp.max(s, axis=-1, keepdims=True))
        p = p / jnp.sum(p, axis=-1, keepdims=True)
        outs.append(_dot(p.astype(BF16), v_ref[:, sl]).astype(BF16))
    o_ref[...] = x + _dot(jnp.concatenate(outs, axis=-1), wo_ref[...])


def _xattn(x, gain, wq, k, v, wo, tq=256):
    b, t, d = x.shape
    m = k.shape[1]
    tq = min(tq, t)
    return pl.pallas_call(
        _xattn_kernel,
        grid=(b, t // tq),
        in_specs=[pl.BlockSpec((None, tq, d), lambda i, j: (i, j, 0)),
                  pl.BlockSpec((1, d), lambda i, j: (0, 0)),
                  pl.BlockSpec((d, d), lambda i, j: (0, 0)),
                  pl.BlockSpec((None, m, d), lambda i, j: (i, 0, 0)),
                  pl.BlockSpec((None, m, d), lambda i, j: (i, 0, 0)),
                  pl.BlockSpec((d, d), lambda i, j: (0, 0))],
        out_specs=pl.BlockSpec((None, tq, d), lambda i, j: (i, j, 0)),
        out_shape=jax.ShapeDtypeStruct((b, t, d), F32),
        compiler_params=_cparams(("parallel", "parallel")),
        name="xattn",
    )(x, gain.reshape(1, d), wq, k, v, wo)


def _route(logits):
    lane = _iota2(logits.shape, 1)
    neg = -jnp.inf
    gmask = lane < N_GROUPS
    gl = jnp.where(gmask, logits, neg)
    gmax = jnp.max(gl, axis=-1, keepdims=True)
    gsel = jnp.min(jnp.where(gl == gmax, lane, LANES), axis=-1, keepdims=True)
    group_gate = 1.0 / jnp.sum(jnp.where(gmask, jnp.exp(gl - gmax), 0.0), axis=-1, keepdims=True)
    lo = N_GROUPS + EXPERTS_PER_GROUP * gsel
    emask = (lane >= lo) & (lane < lo + EXPERTS_PER_GROUP)
    el = jnp.where(emask, logits, neg)
    ee = jnp.where(emask, jnp.exp(el - jnp.max(el, axis=-1, keepdims=True)), 0.0)
    prob = ee / jnp.sum(ee, axis=-1, keepdims=True)
    p1 = jnp.max(jnp.where(emask, prob, -1.0), axis=-1, keepdims=True)
    i1 = jnp.min(jnp.where(emask & (prob == p1), lane, LANES), axis=-1, keepdims=True)
    rest = emask & (lane != i1)
    p2 = jnp.max(jnp.where(rest, prob, -1.0), axis=-1, keepdims=True)
    i2 = jnp.min(jnp.where(rest & (prob == p2), lane, LANES), axis=-1, keepdims=True)
    scale = group_gate / (p1 + p2)
    return jnp.where(lane == i1, p1 * scale, 0.0) + jnp.where(lane == i2, p2 * scale, 0.0)


def _moe_kernel(final_norm, x_ref, gain_ref, wr_ref, br_ref, w1_ref, w3_ref, w2_ref, gfin_ref, o_ref,
                xn_ref, comb_ref, acc_ref):
    grp = pl.program_id(1)
    ff = w1_ref.shape[1] // EXPERTS_PER_GROUP

    @pl.when(grp == 0)
    def _():
        xn = _rms(x_ref[...], gain_ref[...])
        xn_ref[...] = xn.astype(BF16)
        comb_ref[...] = _route(_dot(xn, wr_ref[...], HI) + br_ref[...])
        acc_ref[...] = jnp.zeros_like(acc_ref)

    xn = xn_ref[...]
    comb = comb_ref[...]
    lane = _iota2(comb.shape, 1)
    acc = acc_ref[...]
    for e in range(EXPERTS_PER_GROUP):
        cs = slice(e * ff, (e + 1) * ff)
        weight = jnp.sum(jnp.where(lane == N_GROUPS + EXPERTS_PER_GROUP * grp + e, comb, 0.0),
                         axis=-1, keepdims=True)
        hid = _silu(_dot(xn, w1_ref[:, cs])) * _dot(xn, w3_ref[:, cs]) * weight
        acc = acc + _dot(hid.astype(BF16), w2_ref[cs, :])
    acc_ref[...] = acc

    @pl.when(grp == N_GROUPS - 1)
    def _():
        y = x_ref[...] + acc_ref[...]
        if final_norm:
            y = _rms(y, gfin_ref[...])
        o_ref[...] = y


def _moe(x, gain, w_router, b_router, w1g, w3g, w2g, gain_final, final_norm, tm=512):
    n, d = x.shape
    tm = min(tm, n)
    gf = w1g.shape[2]
    return pl.pallas_call(
        functools.partial(_moe_kernel, final_norm),
        grid=(n // tm, N_GROUPS),
        in_specs=[pl.BlockSpec((tm, d), lambda i, g: (i, 0)),
                  pl.BlockSpec((1, d), lambda i, g: (0, 0)),
                  pl.BlockSpec((d, LANES), lambda i, g: (0, 0)),
                  pl.BlockSpec((1, LANES), lambda i, g: (0, 0)),
                  pl.BlockSpec((None, d, gf), lambda i, g: (g, 0, 0)),
                  pl.BlockSpec((None, d, gf), lambda i, g: (g, 0, 0)),
                  pl.BlockSpec((None, gf, d), lambda i, g: (g, 0, 0)),
                  pl.BlockSpec((1, d), lambda i, g: (0, 0))],
        out_specs=pl.BlockSpec((tm, d), lambda i, g: (i, 0)),
        out_shape=jax.ShapeDtypeStruct((n, d), F32),
        scratch_shapes=[pltpu.VMEM((tm, d), BF16),
                        pltpu.VMEM((tm, LANES), F32),
                        pltpu.VMEM((tm, d), F32)],
        compiler_params=_cparams(("parallel", "arbitrary")),
        name="hmoe",
    )(x, gain.reshape(1, d), w_router, b_router, w1g, w3g, w2g, gain_final.reshape(1, d))


def _pad_cols(w, n):
    return jnp.pad(w, ((0, 0), (0, n - w.shape[1])))


def _pad_rows(w, n):
    return jnp.pad(w, ((0, n - w.shape[0]), (0, 0)))


def _row(v, n=None):
    v = v.reshape(1, -1).astype(F32)
    return v if n is None else _pad_cols(v, n)


def _group_cols(w):
    e, d, f = w.shape
    return (w.reshape(N_GROUPS, EXPERTS_PER_GROUP, d, f).transpose(0, 2, 1, 3)
            .reshape(N_GROUPS, d, EXPERTS_PER_GROUP * f).astype(BF16))


def kernel(x, mem, norm_mix, norm_xattn, norm_mem, norm_ffn, norm_final, ab_w_in, ab_conv, ab_a_log, ab_dt_bias, ab_onorm, ab_shift_mu, ab_w0, ab_w2, ab_a0, ab_a2, ab_g2, ab_k_k, ab_k_a, ab_r_k, ab_ln_w, ab_ln_b, ab_w_out, c_w_in, c_lb_param, c_onorm, c_w_out, xa_wq, xa_wkv, xa_wo, moe_w_group, moe_b_group, moe_w_expert, moe_b_expert, moe_w1, moe_w3, moe_w2):
    bsz, seq, d = x.shape
    n_mem = mem.shape[1]
    depth = norm_mix.shape[0]
    a_width = d // 2
    b_width = d // 2
    a_heads = a_width // A_HEAD_DIM
    a_cols = 4 * a_width + 2 * a_heads
    n_tok = bsz * seq

    lb_p = jax.nn.softmax(c_lb_param.astype(F32), axis=0)
    lower_bounds = jnp.cumsum(lb_p, axis=0) - lb_p[0]

    bd = None
    x2 = x.reshape(n_tok, d)
    mem2 = mem.reshape(bsz * n_mem, d)
    for layer in range(depth):
        if layer % 2 == 0:
            e = layer // 2
            w_in = ab_w_in[e]
            w_a = w_in[:, :4 * a_width].astype(BF16)
            w_db = _pad_cols(w_in[:, 4 * a_width:a_cols], LANES).astype(BF16)
            wb = w_in[:, a_cols:]
            o1, o2 = 3 * b_width, 3 * b_width + B_DECAY_LORA
            o3 = o2 + B_ICLR_LORA
            w_b = jnp.concatenate([wb[:, :o1], _pad_cols(wb[:, o1:o2], LANES), _pad_cols(wb[:, o2:o3], LANES),
                                   wb[:, o3:]], axis=1).astype(BF16)
            mu = ab_shift_mu[e].reshape(1, -1)
            mu_p = jnp.concatenate([mu[:, :o1], _pad_cols(mu[:, o1:o2], LANES), _pad_cols(mu[:, o2:o3], LANES),
                                    mu[:, o3:]], axis=1)
            pa, pdb, pb = _norm_proj(x2, norm_mix[layer], [w_a, w_db, w_b])
            o_a = _gdn(pa.reshape(bsz, seq, -1), pdb.reshape(bsz, seq, LANES), ab_conv[e],
                       _row(ab_a_log[e], LANES), _row(ab_dt_bias[e], LANES), _row(ab_onorm[e]))
            if bd is None:
                hid = jnp.arange(b_width) // B_HEAD_DIM
                bd = (hid[:, None] == hid[None, :]).astype(F32)
            rwkv_params = [mu_p, _row(ab_w0[e]), _pad_rows(ab_w2[e], LANES), _row(ab_a0[e]),
                           _pad_rows(ab_a2[e], LANES), ab_g2[e], _row(ab_k_k[e]), _row(ab_k_a[e]),
                           _row(ab_r_k[e]), _row(ab_ln_w[e]), _row(ab_ln_b[e])]
            o_b = _rwkv(pb.reshape(bsz, seq, -1), rwkv_params, bd)
            w_out = ab_w_out[e].astype(BF16)
            x2 = _out_proj(x2, [o_a.reshape(n_tok, a_width), o_b.reshape(n_tok, b_width)],
                           [w_out[:a_width], w_out[a_width:]])
        else:
            o = layer // 2
            (proj,) = _norm_proj(x2, norm_mix[layer], [c_w_in[o].astype(BF16)])
            lb = lower_bounds[layer].reshape(1, -1)
            o_c = _hgrn(proj.reshape(bsz, seq, -1), jnp.log(lb), jnp.log1p(-lb), _row(c_onorm[o]))
            x2 = _out_proj(x2, [o_c.reshape(n_tok, -1)], [c_w_out[o].astype(BF16)])

        wkv = xa_wkv[layer].astype(BF16)
        k_mem, v_mem = _norm_proj(mem2, norm_mem[layer], [wkv[:, :d], wkv[:, d:]], out_dtype=BF16)
        x3 = _xattn(x2.reshape(bsz, seq, d), norm_xattn[layer], xa_wq[layer].astype(BF16),
                    k_mem.reshape(bsz, n_mem, d), v_mem.reshape(bsz, n_mem, d), xa_wo[layer].astype(BF16))
        x2 = x3.reshape(n_tok, d)

        w_router = _pad_cols(jnp.concatenate([moe_w_group[layer], moe_w_expert[layer]], axis=1), LANES)
        b_router = _row(jnp.concatenate([moe_b_group[layer], moe_b_expert[layer]]), LANES)
        ff = moe_w2.shape[2]
        x2 = _moe(x2, norm_ffn[layer], w_router, b_router, _group_cols(moe_w1[layer]), _group_cols(moe_w3[layer]),
                  moe_w2[layer].reshape(N_GROUPS, EXPERTS_PER_GROUP * ff, d).astype(BF16),
                  norm_final, final_norm=(layer == depth - 1))
    return x2.reshape(bsz, seq, d)
```

```python
import functools

import jax
import jax.numpy as jnp
from jax import lax
from jax.experimental import pallas as pl
from jax.experimental.pallas import tpu as pltpu

F32 = jnp.float32
BF16 = jnp.bfloat16
HI = lax.Precision.HIGHEST
ONE_PASS = "bf16"

PREC_CUM = HI
PREC_INV = ONE_PASS
PREC_MIX = ONE_PASS
PREC_AUX = ONE_PASS

CHUNK = 64
SUB = 16
NORM_EPS = 1e-6
GROUP_NORM_EPS = 64e-5
CONV_WIDTH = 4

A_HEAD_DIM = 128
B_HEAD_DIM = 64
B_DECAY_LORA = 64
B_ICLR_LORA = 64
B_GATE_LORA = 128
C_KEY_DIM = 128
XA_HEADS = 4
N_GROUPS = 4
EXPERTS_PER_GROUP = 4
N_EXPERTS = N_GROUPS * EXPERTS_PER_GROUP

LANES = 128
VMEM_LIMIT = 56 * 1024 * 1024
MIXER_BATCH = 2


def _cparams(sem):
    return pltpu.CompilerParams(dimension_semantics=sem, vmem_limit_bytes=VMEM_LIMIT)


def _dot_dims(a, b, dims, prec):
    if prec == ONE_PASS:
        a, b, prec = a.astype(BF16), b.astype(BF16), None
    return lax.dot_general(a, b, (dims, ((), ())), preferred_element_type=F32, precision=prec)


def _dot(a, b, prec=None):
    return _dot_dims(a, b, ((1,), (0,)), prec)


def _dot_nt(a, b, prec=None):
    return _dot_dims(a, b, ((1,), (1,)), prec)


def _dot_tn(a, b, prec=None):
    return _dot_dims(a, b, ((0,), (0,)), prec)


def _rms(x, gain):
    return x * lax.rsqrt(jnp.mean(x * x, axis=-1, keepdims=True) + NORM_EPS) * gain


def _sigmoid(x):
    return 1.0 / (1.0 + jnp.exp(-x))


def _silu(x):
    return x * _sigmoid(x)


def _softplus(x):
    return jnp.maximum(x, 0.0) + jnp.log(1.0 + jnp.exp(-jnp.abs(x)))


def _iota2(shape, dim):
    return lax.broadcasted_iota(jnp.int32, shape, dim)


def _interleave(stage_generators):
    live = list(stage_generators)
    while live:
        for gen in list(live):
            try:
                next(gen)
            except StopIteration:
                live.remove(gen)


def _tri_inverse_stages(mats):
    n = mats[0].shape[0]
    row = _iota2((n, n), 0)
    col = _iota2((n, n), 1)
    same16 = (row >> 4) == (col >> 4)
    same32 = (row >> 5) == (col >> 5)
    eye = (row == col).astype(F32)
    ads = [jnp.where(same16, a, 0.0) for a in mats]
    xs = [eye - ad for ad in ads]
    ps = [_dot(ad, ad, PREC_INV) for ad in ads]
    yield
    for stage in range(3):
        xs = [x + _dot(x, p, PREC_INV) for x, p in zip(xs, ps)]
        if stage < 2:
            ps = [_dot(p, p, PREC_INV) for p in ps]
        yield
    for lower in ([jnp.where(same32 & jnp.logical_not(same16), a, 0.0) for a in mats],
                  [jnp.where(same32, 0.0, a) for a in mats]):
        ts = [_dot(l, x, PREC_INV) for l, x in zip(lower, xs)]
        yield
        xs = [x - _dot(x, t, PREC_INV) for x, t in zip(xs, ts)]
        yield
    return xs


def _norm_proj_kernel(n_w, x_ref, g_ref, *refs):
    w_refs, o_refs = refs[:n_w], refs[n_w:]
    xn = _rms(x_ref[...], g_ref[...]).astype(BF16)
    for w_ref, o_ref in zip(w_refs, o_refs):
        n = w_ref.shape[1]
        for c0 in range(0, n, 512):
            c1 = min(n, c0 + 512)
            o_ref[:, c0:c1] = _dot(xn, w_ref[:, c0:c1]).astype(o_ref.dtype)


def _norm_proj(x, gain, ws, out_dtype=F32, tm=256):
    n, d = x.shape
    tm = min(tm, n)
    return pl.pallas_call(
        functools.partial(_norm_proj_kernel, len(ws)),
        grid=(n // tm,),
        in_specs=[pl.BlockSpec((tm, d), lambda i: (i, 0)),
                  pl.BlockSpec((1, d), lambda i: (0, 0))]
                 + [pl.BlockSpec(w.shape, lambda i: (0, 0)) for w in ws],
        out_specs=[pl.BlockSpec((tm, w.shape[1]), lambda i: (i, 0)) for w in ws],
        out_shape=[jax.ShapeDtypeStruct((n, w.shape[1]), out_dtype) for w in ws],
        compiler_params=_cparams(("parallel",)),
        name="norm_proj",
    )(x, gain.reshape(1, d), *ws)


def _out_proj_kernel(res_ref, a_ref, w_ref, o_ref):
    o_ref[...] = res_ref[...] + _dot(a_ref[...].astype(BF16), w_ref[...])


def _out_proj(res, act, w, tm=256):
    n, d = res.shape
    tm = min(tm, n)
    return pl.pallas_call(
        _out_proj_kernel,
        grid=(n // tm,),
        in_specs=[pl.BlockSpec((tm, d), lambda i: (i, 0)),
                  pl.BlockSpec((tm, act.shape[1]), lambda i: (i, 0)),
                  pl.BlockSpec(w.shape, lambda i: (0, 0))],
        out_specs=pl.BlockSpec((tm, d), lambda i: (i, 0)),
        out_shape=jax.ShapeDtypeStruct((n, d), F32),
        compiler_params=_cparams(("parallel",)),
        name="out_proj",
    )(res, act, w)


def _gdn_stages(nb, pa_ref, pdb_ref, convw_ref, alog_ref, dtb_ref, onorm_ref, o_ref, xbuf_ref, state_ref):
    n_heads = state_ref.shape[0] // nb
    hd = A_HEAD_DIM
    width = n_heads * hd
    row = _iota2((CHUNK, CHUNK), 0)
    col = _iota2((CHUNK, CHUNK), 1)
    incl = row >= col
    strict = row > col
    tril = incl.astype(F32)
    triu = (row <= col).astype(F32)

    qs, ks, vs, betas, gs, g_lasts, decays = [], [], [], [], [], [], []
    for bb in range(nb):
        x = pa_ref[bb, :, 0:3 * width]
        xbuf_ref[bb, 8:8 + CHUNK, :] = x
        acc = convw_ref[0:1, :] * xbuf_ref[bb, 5:5 + CHUNK, :]
        for j in range(1, CONV_WIDTH):
            acc = acc + convw_ref[j:j + 1, :] * xbuf_ref[bb, 5 + j:5 + j + CHUNK, :]
        xbuf_ref[bb, 0:8, :] = x[CHUNK - 8:CHUNK, :]
        qkv = _silu(acc)
        pdb = pdb_ref[bb]
        log_alpha = -jnp.exp(alog_ref[...]) * _softplus(pdb + dtb_ref[...])
        beta_all = _sigmoid(pdb)
        g_all = _dot(tril, log_alpha, PREC_CUM)
        g_all_t = _dot_tn(log_alpha, triu, PREC_CUM)
        for h in range(n_heads):
            q = qkv[:, h * hd:(h + 1) * hd]
            k = qkv[:, width + h * hd: width + (h + 1) * hd]
            qs.append(q * lax.rsqrt(jnp.sum(q * q, axis=-1, keepdims=True) + 1e-6) * (hd ** -0.5))
            ks.append(k * lax.rsqrt(jnp.sum(k * k, axis=-1, keepdims=True) + 1e-6))
            vs.append(qkv[:, 2 * width + h * hd: 2 * width + (h + 1) * hd])
            betas.append(beta_all[:, n_heads + h:n_heads + h + 1])
            g = g_all[:, h:h + 1]
            gs.append(g)
            g_lasts.append(g_all[CHUNK - 1:CHUNK, h:h + 1])
            decays.append(jnp.where(incl, jnp.exp(jnp.where(incl, g - g_all_t[h:h + 1, :], 0.0)), 0.0))
    yield
    idx = range(nb * n_heads)
    k_betas = [ks[i] * betas[i] for i in idx]
    egs = [jnp.exp(g) for g in gs]
    a_lows = [jnp.where(strict, _dot_nt(k_betas[i], ks[i], PREC_MIX) * decays[i], 0.0) for i in idx]
    qks = [_dot_nt(qs[i], ks[i], PREC_MIX) * decays[i] for i in idx]
    states = [state_ref[i] for i in idx]
    o_inters = [_dot(qs[i] * egs[i], states[i], PREC_MIX) for i in idx]
    yield
    t_invs = yield from _tri_inverse_stages(a_lows)
    us = [_dot(t_invs[i], vs[i] * betas[i], PREC_MIX) for i in idx]
    ws = [_dot(t_invs[i], k_betas[i] * egs[i], PREC_MIX) for i in idx]
    yield
    v_news = [us[i] - _dot(ws[i], states[i], PREC_MIX) for i in idx]
    yield
    os_ = [o_inters[i] + _dot(qks[i], v_news[i], PREC_MIX) for i in idx]
    for i in idx:
        state_ref[i] = (states[i] * jnp.exp(g_lasts[i])
                        + _dot_tn(ks[i] * jnp.exp(g_lasts[i] - gs[i]), v_news[i], PREC_MIX))
    yield
    for i in idx:
        bb, h = divmod(i, n_heads)
        gate = pa_ref[bb, :, 3 * width + h * hd: 3 * width + (h + 1) * hd]
        o_ref[bb, :, h * hd:(h + 1) * hd] = _rms(os_[i], onorm_ref[...]) * _silu(gate)


def _rwkv_stages(nb, col0, pb_ref, mu_ref, w0_ref, w2_ref, a0_ref, a2_ref, g2_ref, kk_ref, ka_ref, rk_ref,
                 lnw_ref, lnb_ref, bd_ref, o_ref, carry_ref, state_ref, obuf_ref):
    n_heads = state_ref.shape[0] // nb
    hd = B_HEAD_DIM
    width = n_heads * hd
    row = _iota2((CHUNK, CHUNK), 0)
    col = _iota2((CHUNK, CHUNK), 1)
    strict = row > col
    tril = (row >= col).astype(F32)
    incl2 = _iota2((CHUNK, 2 * CHUNK), 0) >= (_iota2((CHUNK, 2 * CHUNK), 1) & (CHUNK - 1))
    bd = bd_ref[...]

    pss = []
    for bb in range(nb):
        p = pb_ref[bb]
        prev = jnp.where(_iota2(p.shape, 0) == 0, carry_ref[bb, 0:1, :], pltpu.roll(p, 1, 0))
        carry_ref[bb, 0:1, :] = p[CHUNK - 1:CHUNK, :]
        pss.append(p + mu_ref[...] * (prev - p))
    rs = [ps[:, 0:width] for ps in pss]
    kraw = [ps[:, width:2 * width] for ps in pss]
    vfull = [ps[:, 2 * width:3 * width] for ps in pss]
    w_raws = [w0_ref[...] + _dot(jnp.tanh(ps[:, 3 * width:3 * width + LANES]), w2_ref[...], PREC_AUX) for ps in pss]
    iclrs = [_sigmoid(a0_ref[...] + _dot(ps[:, 3 * width + LANES:3 * width + 2 * LANES], a2_ref[...], PREC_AUX))
             for ps in pss]
    gates = [_dot(_sigmoid(ps[:, 3 * width + 2 * LANES:3 * width + 3 * LANES]), g2_ref[...], PREC_AUX) for ps in pss]
    kks = [k * kk_ref[...] for k in kraw]
    kk_ss = [_dot(kk * kk, bd, PREC_AUX) for kk in kks]
    yield
    log_ws = [-jnp.exp(-_softplus(-w_raw) - 0.5) for w_raw in w_raws]
    g_cums = [_dot(tril, log_w, PREC_CUM) for log_w in log_ws]
    yield
    lhss, rhss, bl_kls, p_cs, vhs, k2s = [], [], [], [], [], []
    for bb in range(nb):
        kk = kks[bb] * lax.rsqrt(kk_ss[bb] + 1e-6)
        k2 = kraw[bb] * (1.0 + (iclrs[bb] - 1.0) * ka_ref[...])
        k2s.append(k2)
        bvec = kk * iclrs[bb]
        g = g_cums[bb]
        g_last = g[CHUNK - 1:CHUNK, :]
        eng = jnp.exp(-g)
        egl = jnp.exp(g_last - g)
        r_t = rs[bb] * jnp.exp(g)
        kk_t = kk * jnp.exp(g - log_ws[bb])
        b_h, k_h = bvec * eng, k2 * eng
        b_l, k_l = bvec * egl, k2 * egl
        p_c = jnp.exp(g_last)
        for h in range(n_heads):
            sl = slice(h * hd, (h + 1) * hd)
            lhss.append(jnp.concatenate([kk_t[:, sl], r_t[:, sl]], axis=0))
            rhss.append(jnp.concatenate([b_h[:, sl], k_h[:, sl]], axis=0))
            bl_kls.append(jnp.concatenate([b_l[:, sl], k_l[:, sl]], axis=0))
            p_cs.append(p_c[:, sl])
            vhs.append(vfull[bb][:, sl])
    idx = range(nb * n_heads)
    ms = [_dot_nt(lhss[i], rhss[i], PREC_MIX) for i in idx]
    states = [state_ref[i] for i in idx]
    lss = [_dot_nt(lhss[i], states[i], PREC_MIX) for i in idx]
    yield
    a_bbs = [jnp.where(strict, m[0:CHUNK, 0:CHUNK], 0.0) for m in ms]
    a_bks = [jnp.where(strict, m[0:CHUNK, CHUNK:2 * CHUNK], 0.0) for m in ms]
    a_rs = [jnp.where(incl2, m[CHUNK:2 * CHUNK, :], 0.0) for m in ms]
    rhs_u = [lss[i][0:CHUNK] + _dot(a_bks[i], vhs[i], PREC_MIX) for i in idx]
    t_invs = yield from _tri_inverse_stages(a_bbs)
    uvs = [jnp.concatenate([-_dot(t_invs[i], rhs_u[i], PREC_MIX), vhs[i]], axis=0) for i in idx]
    yield
    for i in idx:
        bb, h = divmod(i, n_heads)
        obuf_ref[bb, :, h * hd:(h + 1) * hd] = lss[i][CHUNK:2 * CHUNK] + _dot(a_rs[i], uvs[i], PREC_MIX)
        state_ref[i] = states[i] * p_cs[i] + _dot_tn(uvs[i], bl_kls[i], PREC_MIX)
    yield
    inv_n = 1.0 / hd
    os_ = [obuf_ref[bb] for bb in range(nb)]
    means = [_dot(o, bd, PREC_AUX) * inv_n for o in os_]
    bonus = [_dot(rs[bb] * k2s[bb] * rk_ref[...], bd, PREC_AUX) * vfull[bb] for bb in range(nb)]
    yield
    cens = [o - mean for o, mean in zip(os_, means)]
    vars_ = [_dot(cen * cen, bd, PREC_AUX) * inv_n for cen in cens]
    yield
    for bb in range(nb):
        ln = cens[bb] * lax.rsqrt(vars_[bb] + GROUP_NORM_EPS) * lnw_ref[...] + lnb_ref[...]
        o_ref[bb, :, col0:col0 + width] = (ln + bonus[bb]) * gates[bb]


def _mixer_ab_kernel(n_gdn, n_rwkv, pa_ref, pdb_ref, pb_ref, *refs):
    gdn_params = refs[:n_gdn]
    rwkv_params = refs[n_gdn:n_gdn + n_rwkv]
    o_ref, xbuf_ref, gstate_ref, carry_ref, rstate_ref, obuf_ref = refs[n_gdn + n_rwkv:]
    nb = pa_ref.shape[0]

    @pl.when(pl.program_id(1) == 0)
    def _():
        xbuf_ref[:, 0:8, :] = jnp.zeros((nb, 8, xbuf_ref.shape[2]), F32)
        gstate_ref[...] = jnp.zeros_like(gstate_ref)
        carry_ref[...] = jnp.zeros_like(carry_ref)
        rstate_ref[...] = jnp.zeros_like(rstate_ref)

    a_width = pa_ref.shape[2] // 4
    _interleave([
        _gdn_stages(nb, pa_ref, pdb_ref, *gdn_params, o_ref, xbuf_ref, gstate_ref),
        _rwkv_stages(nb, a_width, pb_ref, *rwkv_params, o_ref, carry_ref, rstate_ref, obuf_ref),
    ])


def _mixer_ab(pa, pdb, pb, gdn_params, rwkv_params):
    b, t, a_cols = pa.shape
    b_cols = pb.shape[2]
    a_width = a_cols // 4
    b_width = (b_cols - 3 * LANES) // 3
    a_heads = a_width // A_HEAD_DIM
    b_heads = b_width // B_HEAD_DIM
    nb = MIXER_BATCH if b % MIXER_BATCH == 0 else 1
    params = list(gdn_params) + list(rwkv_params)
    return pl.pallas_call(
        functools.partial(_mixer_ab_kernel, len(gdn_params), len(rwkv_params)),
        grid=(b // nb, t // CHUNK),
        in_specs=[pl.BlockSpec((nb, CHUNK, a_cols), lambda i, c: (i, c, 0)),
                  pl.BlockSpec((nb, CHUNK, LANES), lambda i, c: (i, c, 0)),
                  pl.BlockSpec((nb, CHUNK, b_cols), lambda i, c: (i, c, 0))]
                 + [pl.BlockSpec(a.shape, lambda i, c: (0, 0)) for a in params],
        out_specs=pl.BlockSpec((nb, CHUNK, a_width + b_width), lambda i, c: (i, c, 0)),
        out_shape=jax.ShapeDtypeStruct((b, t, a_width + b_width), F32),
        scratch_shapes=[pltpu.VMEM((nb, 8 + CHUNK, 3 * a_width), F32),
                        pltpu.VMEM((nb * a_heads, A_HEAD_DIM, A_HEAD_DIM), F32),
                        pltpu.VMEM((nb, 8, b_cols), F32),
                        pltpu.VMEM((nb * b_heads, B_HEAD_DIM, B_HEAD_DIM), F32),
                        pltpu.VMEM((nb, CHUNK, b_width), F32)],
        compiler_params=_cparams(("parallel", "arbitrary")),
        name="mixer_ab",
    )(pa, pdb, pb, *params)


def _hgrn_kernel(p_ref, loglb_ref, log1mlb_ref, onorm_ref, o_ref, state_ref):
    n_heads = state_ref.shape[0]
    kw = n_heads * C_KEY_DIM

    @pl.when(pl.program_id(1) == 0)
    def _():
        state_ref[...] = jnp.zeros_like(state_ref)

    q_all = _silu(p_ref[:, 0:kw])
    f = p_ref[:, kw:2 * kw]
    a1 = jnp.broadcast_to(loglb_ref[...], f.shape)
    b1 = log1mlb_ref[...] - _softplus(-f)
    log_f = jnp.maximum(a1, b1) + jnp.log(1.0 + jnp.exp(-jnp.abs(a1 - b1)))
    k_all = 1.0 - jnp.exp(log_f)

    row = _iota2((CHUNK, CHUNK), 0)
    col = _iota2((CHUNK, CHUNK), 1)
    g_all = _dot((row >= col).astype(F32), log_f, PREC_CUM)
    sub_row = _iota2((SUB, 1), 0)
    n_sub = CHUNK // SUB

    for h in range(n_heads):
        sl = slice(h * C_KEY_DIM, (h + 1) * C_KEY_DIM)
        q, k, g = q_all[:, sl], k_all[:, sl], g_all[:, sl]
        v = p_ref[:, 2 * kw + h * C_KEY_DIM: 2 * kw + (h + 1) * C_KEY_DIM]
        s = state_ref[h]
        o_inter = _dot_nt(q * jnp.exp(g), s, PREC_MIX)
        pieces = []
        for i in range(n_sub):
            r0 = i * SUB
            qi, ki, gi, vi = q[r0:r0 + SUB], k[r0:r0 + SUB], g[r0:r0 + SUB], v[r0:r0 + SUB]
            oi = jnp.zeros((SUB, C_KEY_DIM), F32)
            for j in range(SUB):
                dec = jnp.exp(jnp.minimum(gi - gi[j:j + 1, :], 0.0))
                sc = jnp.sum(qi * dec * ki[j:j + 1, :], axis=-1, keepdims=True)
                oi = oi + jnp.where(sub_row >= j, sc, 0.0) * vi[j:j + 1, :]
            if i > 0:
                g_ref = g[r0 - 1:r0, :]
                sc = _dot_nt(qi * jnp.exp(gi - g_ref), k[0:r0] * jnp.exp(g_ref - g[0:r0]), PREC_MIX)
                oi = oi + _dot(sc, v[0:r0], PREC_MIX)
            pieces.append(oi)
        o = o_inter + jnp.concatenate(pieces, axis=0)
        g_last = g[CHUNK - 1:CHUNK, :]
        state_ref[h] = s * jnp.exp(g_last) + _dot_tn(v, k * jnp.exp(g_last - g), PREC_MIX)
        gate = p_ref[:, 3 * kw + h * C_KEY_DIM: 3 * kw + (h + 1) * C_KEY_DIM]
        o_ref[:, sl] = _rms(o, onorm_ref[...]) * _silu(gate)


def _hgrn(proj, log_lb, log1m_lb, onorm):
    b, t, cols = proj.shape
    kw = cols // 4
    n_heads = kw // C_KEY_DIM
    return pl.pallas_call(
        _hgrn_kernel,
        grid=(b, t // CHUNK),
        in_specs=[pl.BlockSpec((None, CHUNK, cols), lambda i, c: (i, c, 0)),
                  pl.BlockSpec((1, kw), lambda i, c: (0, 0)),
                  pl.BlockSpec((1, kw), lambda i, c: (0, 0)),
                  pl.BlockSpec((1, C_KEY_DIM), lambda i, c: (0, 0))],
        out_specs=pl.BlockSpec((None, CHUNK, kw), lambda i, c: (i, c, 0)),
        out_shape=jax.ShapeDtypeStruct((b, t, kw), F32),
        scratch_shapes=[pltpu.VMEM((n_heads, C_KEY_DIM, C_KEY_DIM), F32)],
        compiler_params=_cparams(("parallel", "arbitrary")),
        name="hgrn2",
    )(proj, log_lb, log1m_lb, onorm)


def _xattn_kernel(x_ref, gain_ref, wq_ref, k_ref, v_ref, wo_ref, o_ref):
    x = x_ref[...]
    d = x.shape[1]
    hd = d // XA_HEADS
    q = _dot(_rms(x, gain_ref[...]).astype(BF16), wq_ref[...]).astype(BF16)
    outs = []
    for h in range(XA_HEADS):
        sl = slice(h * hd, (h + 1) * hd)
        s = _dot_nt(q[:, sl], k_ref[:, sl]) * (hd ** -0.5)
        p = jnp.exp(s - jnp.max(s, axis=-1, keepdims=True))
        p = p / jnp.sum(p, axis=-1, keepdims=True)
        outs.append(_dot(p.astype(BF16), v_ref[:, sl]).astype(BF16))
    o_ref[...] = x + _dot(jnp.concatenate(outs, axis=-1), wo_ref[...])


def _xattn(x, gain, wq, k, v, wo, tq=256):
    b, t, d = x.shape
    m = k.shape[1]
    tq = min(tq, t)
    return pl.pallas_call(
        _xattn_kernel,
        grid=(b, t // tq),
        in_specs=[pl.BlockSpec((None, tq, d), lambda i, j: (i, j, 0)),
                  pl.BlockSpec((1, d), lambda i, j: (0, 0)),
                  pl.BlockSpec((d, d), lambda i, j: (0, 0)),
                  pl.BlockSpec((None, m, d), lambda i, j: (i, 0, 0)),
                  pl.BlockSpec((None, m, d), lambda i, j: (i, 0, 0)),
                  pl.BlockSpec((d, d), lambda i, j: (0, 0))],
        out_specs=pl.BlockSpec((None, tq, d), lambda i, j: (i, j, 0)),
        out_shape=jax.ShapeDtypeStruct((b, t, d), F32),
        compiler_params=_cparams(("parallel", "parallel")),
        name="xattn",
    )(x, gain.reshape(1, d), wq, k, v, wo)


def _route(logits):
    lane = _iota2(logits.shape, 1)
    neg = -jnp.inf
    gmask = lane < N_GROUPS
    gl = jnp.where(gmask, logits, neg)
    gmax = jnp.max(gl, axis=-1, keepdims=True)
    gsel = jnp.min(jnp.where(gl == gmax, lane, LANES), axis=-1, keepdims=True)
    group_gate = 1.0 / jnp.sum(jnp.where(gmask, jnp.exp(gl - gmax), 0.0), axis=-1, keepdims=True)
    lo = N_GROUPS + EXPERTS_PER_GROUP * gsel
    emask = (lane >= lo) & (lane < lo + EXPERTS_PER_GROUP)
    el = jnp.where(emask, logits, neg)
    ee = jnp.where(emask, jnp.exp(el - jnp.max(el, axis=-1, keepdims=True)), 0.0)
    prob = ee / jnp.sum(ee, axis=-1, keepdims=True)
    p1 = jnp.max(jnp.where(emask, prob, -1.0), axis=-1, keepdims=True)
    i1 = jnp.min(jnp.where(emask & (prob == p1), lane, LANES), axis=-1, keepdims=True)
    rest = emask & (lane != i1)
    p2 = jnp.max(jnp.where(rest, prob, -1.0), axis=-1, keepdims=True)
    i2 = jnp.min(jnp.where(rest & (prob == p2), lane, LANES), axis=-1, keepdims=True)
    scale = group_gate / (p1 + p2)
    return jnp.where(lane == i1, p1 * scale, 0.0) + jnp.where(lane == i2, p2 * scale, 0.0)


def _moe_kernel(final_norm, x_ref, gain_ref, wr_ref, br_ref, w1_ref, w3_ref, w2_ref, gfin_ref, o_ref,
                xn_ref, comb_ref, acc_ref):
    grp = pl.program_id(1)
    ff = w1_ref.shape[2]

    @pl.when(grp == 0)
    def _():
        xn = _rms(x_ref[...], gain_ref[...])
        xn_hi = xn.astype(BF16)
        xn_lo = (xn - xn_hi.astype(F32)).astype(BF16)
        xn_ref[...] = xn_hi
        hi_part = _dot(xn_hi, wr_ref[...])
        logits = hi_part[:, 0:LANES] + hi_part[:, LANES:2 * LANES] + _dot(xn_lo, wr_ref[:, 0:LANES])
        comb_ref[...] = _route(logits + br_ref[...])
        acc_ref[...] = jnp.zeros_like(acc_ref)

    xn = xn_ref[...]
    comb = comb_ref[...]
    lane = _iota2(comb.shape, 1)
    acc = acc_ref[...]
    for e in range(EXPERTS_PER_GROUP):
        weight = jnp.sum(jnp.where(lane == N_GROUPS + EXPERTS_PER_GROUP * grp + e, comb, 0.0),
                         axis=-1, keepdims=True)
        hid = _silu(_dot(xn, w1_ref[e])) * _dot(xn, w3_ref[e]) * weight
        acc = acc + _dot(hid.astype(BF16), w2_ref[e * ff:(e + 1) * ff, :])
    acc_ref[...] = acc

    @pl.when(grp == N_GROUPS - 1)
    def _():
        y = x_ref[...] + acc_ref[...]
        if final_norm:
            y = _rms(y, gfin_ref[...])
        o_ref[...] = y


def _moe(x, gain, w_router, b_router, w1, w3, w2g, gain_final, final_norm, tm=512):
    n, d = x.shape
    tm = min(tm, n)
    ff = w1.shape[2]
    return pl.pallas_call(
        functools.partial(_moe_kernel, final_norm),
        grid=(n // tm, N_GROUPS),
        in_specs=[pl.BlockSpec((tm, d), lambda i, g: (i, 0)),
                  pl.BlockSpec((1, d), lambda i, g: (0, 0)),
                  pl.BlockSpec((d, 2 * LANES), lambda i, g: (0, 0)),
                  pl.BlockSpec((1, LANES), lambda i, g: (0, 0)),
                  pl.BlockSpec((EXPERTS_PER_GROUP, d, ff), lambda i, g: (g, 0, 0)),
                  pl.BlockSpec((EXPERTS_PER_GROUP, d, ff), lambda i, g: (g, 0, 0)),
                  pl.BlockSpec((None, EXPERTS_PER_GROUP * ff, d), lambda i, g: (g, 0, 0)),
                  pl.BlockSpec((1, d), lambda i, g: (0, 0))],
        out_specs=pl.BlockSpec((tm, d), lambda i, g: (i, 0)),
        out_shape=jax.ShapeDtypeStruct((n, d), F32),
        scratch_shapes=[pltpu.VMEM((tm, d), BF16),
                        pltpu.VMEM((tm, LANES), F32),
                        pltpu.VMEM((tm, d), F32)],
        compiler_params=_cparams(("parallel", "arbitrary")),
        name="hmoe",
    )(x, gain.reshape(1, d), w_router, b_router, w1, w3, w2g, gain_final.reshape(1, d))


def _pad_cols(w, n):
    return jnp.pad(w, ((0, 0), (0, n - w.shape[1])))


def _pad_rows(w, n):
    return jnp.pad(w, ((0, n - w.shape[0]), (0, 0)))


def _row(v, n=None):
    v = v.reshape(1, -1).astype(F32)
    return v if n is None else _pad_cols(v, n)


def kernel(x, mem, norm_mix, norm_xattn, norm_mem, norm_ffn, norm_final, ab_w_in, ab_conv, ab_a_log, ab_dt_bias, ab_onorm, ab_shift_mu, ab_w0, ab_w2, ab_a0, ab_a2, ab_g2, ab_k_k, ab_k_a, ab_r_k, ab_ln_w, ab_ln_b, ab_w_out, c_w_in, c_lb_param, c_onorm, c_w_out, xa_wq, xa_wkv, xa_wo, moe_w_group, moe_b_group, moe_w_expert, moe_b_expert, moe_w1, moe_w3, moe_w2):
    bsz, seq, d = x.shape
    n_mem = mem.shape[1]
    depth = norm_mix.shape[0]
    a_width = d // 2
    b_width = d // 2
    a_heads = a_width // A_HEAD_DIM
    a_cols = 4 * a_width + 2 * a_heads
    n_tok = bsz * seq

    lb_p = jax.nn.softmax(c_lb_param.astype(F32), axis=0)
    lower_bounds = jnp.cumsum(lb_p, axis=0) - lb_p[0]

    x2 = x.reshape(n_tok, d)
    mem2 = mem.reshape(bsz * n_mem, d)
    for layer in range(depth):
        if layer % 2 == 0:
            e = layer // 2
            w_in = ab_w_in[e]
            w_a = w_in[:, :4 * a_width].astype(BF16)
            w_db = _pad_cols(w_in[:, 4 * a_width:a_cols], LANES).astype(BF16)
            wb = w_in[:, a_cols:]
            o1, o2 = 3 * b_width, 3 * b_width + B_DECAY_LORA
            o3 = o2 + B_ICLR_LORA
            w_b = jnp.concatenate([wb[:, :o1], _pad_cols(wb[:, o1:o2], LANES), _pad_cols(wb[:, o2:o3], LANES),
                                   wb[:, o3:]], axis=1).astype(BF16)
            mu = ab_shift_mu[e].reshape(1, -1)
            mu_p = jnp.concatenate([mu[:, :o1], _pad_cols(mu[:, o1:o2], LANES), _pad_cols(mu[:, o2:o3], LANES),
                                    mu[:, o3:]], axis=1)
            pa, pdb, pb = _norm_proj(x2, norm_mix[layer], [w_a, w_db, w_b])
            head_id = jnp.arange(b_width) // B_HEAD_DIM
            bd = (head_id[:, None] == head_id[None, :]).astype(F32)
            gdn_params = [ab_conv[e], _row(ab_a_log[e], LANES), _row(ab_dt_bias[e], LANES), _row(ab_onorm[e])]
            rwkv_params = [mu_p, _row(ab_w0[e]), _pad_rows(ab_w2[e], LANES), _row(ab_a0[e]),
                           _pad_rows(ab_a2[e], LANES), ab_g2[e], _row(ab_k_k[e]), _row(ab_k_a[e]),
                           _row(ab_r_k[e]), _row(ab_ln_w[e]), _row(ab_ln_b[e]), bd]
            o_ab = _mixer_ab(pa.reshape(bsz, seq, -1), pdb.reshape(bsz, seq, LANES), pb.reshape(bsz, seq, -1),
                             gdn_params, rwkv_params)
            x2 = _out_proj(x2, o_ab.reshape(n_tok, a_width + b_width), ab_w_out[e].astype(BF16))
        else:
            o = layer // 2
            (proj,) = _norm_proj(x2, norm_mix[layer], [c_w_in[o].astype(BF16)])
            lb = lower_bounds[layer].reshape(1, -1)
            o_c = _hgrn(proj.reshape(bsz, seq, -1), jnp.log(lb), jnp.log1p(-lb), _row(c_onorm[o]))
            x2 = _out_proj(x2, o_c.reshape(n_tok, -1), c_w_out[o].astype(BF16))

        wkv = xa_wkv[layer].astype(BF16)
        k_mem, v_mem = _norm_proj(mem2, norm_mem[layer], [wkv[:, :d], wkv[:, d:]], out_dtype=BF16)
        x3 = _xattn(x2.reshape(bsz, seq, d), norm_xattn[layer], xa_wq[layer].astype(BF16),
                    k_mem.reshape(bsz, n_mem, d), v_mem.reshape(bsz, n_mem, d), xa_wo[layer].astype(BF16))
        x2 = x3.reshape(n_tok, d)

        w_r = _pad_cols(jnp.concatenate([moe_w_group[layer], moe_w_expert[layer]], axis=1), LANES).astype(F32)
        w_r_hi = w_r.astype(BF16)
        w_r_lo = (w_r - w_r_hi.astype(F32)).astype(BF16)
        b_router = _row(jnp.concatenate([moe_b_group[layer], moe_b_expert[layer]]), LANES)
        ff = moe_w2.shape[2]
        x2 = _moe(x2, norm_ffn[layer], jnp.concatenate([w_r_hi, w_r_lo], axis=1), b_router,
                  moe_w1[layer].astype(BF16), moe_w3[layer].astype(BF16),
                  moe_w2[layer].reshape(N_GROUPS, EXPERTS_PER_GROUP * ff, d).astype(BF16),
                  norm_final, final_norm=(layer == depth - 1))
    return x2.reshape(bsz, seq, d)
```

```python
import functools

import jax
import jax.numpy as jnp
import numpy as np
from jax import lax
from jax.experimental import pallas as pl
from jax.experimental.pallas import tpu as pltpu

F32 = jnp.float32
BF16 = jnp.bfloat16
HI = lax.Precision.HIGHEST
ONE_PASS = "bf16"

PREC_CUM = HI
PREC_INV = ONE_PASS
PREC_MIX = ONE_PASS
PREC_AUX = ONE_PASS

CHUNK = 64
HGRN_LEVELS = 6
NORM_EPS = 1e-6
GROUP_NORM_EPS = 64e-5
CONV_WIDTH = 4

A_HEAD_DIM = 128
B_HEAD_DIM = 64
B_DECAY_LORA = 64
B_ICLR_LORA = 64
B_GATE_LORA = 128
C_KEY_DIM = 128
XA_HEADS = 4
N_GROUPS = 4
EXPERTS_PER_GROUP = 4
N_EXPERTS = N_GROUPS * EXPERTS_PER_GROUP

LANES = 128
VMEM_LIMIT = 56 * 1024 * 1024
MIXER_BATCH = 2


def _cparams(sem):
    return pltpu.CompilerParams(dimension_semantics=sem, vmem_limit_bytes=VMEM_LIMIT)


def _dot_dims(a, b, dims, prec):
    if prec == ONE_PASS:
        a, b, prec = a.astype(BF16), b.astype(BF16), None
    return lax.dot_general(a, b, (dims, ((), ())), preferred_element_type=F32, precision=prec)


def _dot(a, b, prec=None):
    return _dot_dims(a, b, ((1,), (0,)), prec)


def _dot_nt(a, b, prec=None):
    return _dot_dims(a, b, ((1,), (1,)), prec)


def _dot_tn(a, b, prec=None):
    return _dot_dims(a, b, ((0,), (0,)), prec)


def _rms(x, gain):
    return x * lax.rsqrt(jnp.mean(x * x, axis=-1, keepdims=True) + NORM_EPS) * gain


def _sigmoid(x):
    return 1.0 / (1.0 + jnp.exp(-x))


def _silu(x):
    return x * _sigmoid(x)


def _softplus(x):
    return jnp.maximum(x, 0.0) + jnp.log(1.0 + jnp.exp(-jnp.abs(x)))


def _iota2(shape, dim):
    return lax.broadcasted_iota(jnp.int32, shape, dim)


def _interleave(stage_generators):
    live = list(stage_generators)
    while live:
        for gen in list(live):
            try:
                next(gen)
            except StopIteration:
                live.remove(gen)


def _tri_inverse_stages(mats):
    n = mats[0].shape[0]
    row = _iota2((n, n), 0)
    col = _iota2((n, n), 1)
    same16 = (row >> 4) == (col >> 4)
    same32 = (row >> 5) == (col >> 5)
    eye = (row == col).astype(F32)
    ads = [jnp.where(same16, a, 0.0) for a in mats]
    xs = [eye - ad for ad in ads]
    ps = [_dot(ad, ad, PREC_INV) for ad in ads]
    yield
    for stage in range(3):
        xs = [x + _dot(x, p, PREC_INV) for x, p in zip(xs, ps)]
        if stage < 2:
            ps = [_dot(p, p, PREC_INV) for p in ps]
        yield
    for lower in ([jnp.where(same32 & jnp.logical_not(same16), a, 0.0) for a in mats],
                  [jnp.where(same32, 0.0, a) for a in mats]):
        ts = [_dot(l, x, PREC_INV) for l, x in zip(lower, xs)]
        yield
        xs = [x - _dot(x, t, PREC_INV) for x, t in zip(xs, ts)]
        yield
    return xs


def _norm_proj_kernel(n_w, x_ref, g_ref, *refs):
    w_refs, o_refs = refs[:n_w], refs[n_w:]
    xn = _rms(x_ref[...], g_ref[...]).astype(BF16)
    for w_ref, o_ref in zip(w_refs, o_refs):
        n = w_ref.shape[1]
        for c0 in range(0, n, 512):
            c1 = min(n, c0 + 512)
            o_ref[:, c0:c1] = _dot(xn, w_ref[:, c0:c1]).astype(o_ref.dtype)


def _norm_proj(x, gain, ws, out_dtype=F32, tm=256):
    n, d = x.shape
    tm = min(tm, n)
    return pl.pallas_call(
        functools.partial(_norm_proj_kernel, len(ws)),
        grid=(n // tm,),
        in_specs=[pl.BlockSpec((tm, d), lambda i: (i, 0)),
                  pl.BlockSpec((1, d), lambda i: (0, 0))]
                 + [pl.BlockSpec(w.shape, lambda i: (0, 0)) for w in ws],
        out_specs=[pl.BlockSpec((tm, w.shape[1]), lambda i: (i, 0)) for w in ws],
        out_shape=[jax.ShapeDtypeStruct((n, w.shape[1]), out_dtype) for w in ws],
        compiler_params=_cparams(("parallel",)),
        name="norm_proj",
    )(x, gain.reshape(1, d), *ws)


def _out_proj_kernel(res_ref, a_ref, w_ref, o_ref):
    o_ref[...] = res_ref[...] + _dot(a_ref[...].astype(BF16), w_ref[...])


def _out_proj(res, act, w, tm=256):
    n, d = res.shape
    tm = min(tm, n)
    return pl.pallas_call(
        _out_proj_kernel,
        grid=(n // tm,),
        in_specs=[pl.BlockSpec((tm, d), lambda i: (i, 0)),
                  pl.BlockSpec((tm, act.shape[1]), lambda i: (i, 0)),
                  pl.BlockSpec(w.shape, lambda i: (0, 0))],
        out_specs=pl.BlockSpec((tm, d), lambda i: (i, 0)),
        out_shape=jax.ShapeDtypeStruct((n, d), F32),
        compiler_params=_cparams(("parallel",)),
        name="out_proj",
    )(res, act, w)


def _gdn_stages(nb, pa_ref, pdb_ref, convw_ref, alog_ref, dtb_ref, onorm_ref, o_ref, xbuf_ref, state_ref):
    n_heads = state_ref.shape[0] // nb
    hd = A_HEAD_DIM
    width = n_heads * hd
    row = _iota2((CHUNK, CHUNK), 0)
    col = _iota2((CHUNK, CHUNK), 1)
    incl = row >= col
    strict = row > col
    tril = incl.astype(F32)
    triu = (row <= col).astype(F32)

    qs, ks, vs, betas, gs, g_lasts, decays = [], [], [], [], [], [], []
    for bb in range(nb):
        x = pa_ref[bb, :, 0:3 * width]
        xbuf_ref[bb, 8:8 + CHUNK, :] = x
        acc = convw_ref[0:1, :] * xbuf_ref[bb, 5:5 + CHUNK, :]
        for j in range(1, CONV_WIDTH):
            acc = acc + convw_ref[j:j + 1, :] * xbuf_ref[bb, 5 + j:5 + j + CHUNK, :]
        xbuf_ref[bb, 0:8, :] = x[CHUNK - 8:CHUNK, :]
        qkv = _silu(acc)
        pdb = pdb_ref[bb]
        log_alpha = -jnp.exp(alog_ref[...]) * _softplus(pdb + dtb_ref[...])
        beta_all = _sigmoid(pdb)
        g_all = _dot(tril, log_alpha, PREC_CUM)
        g_all_t = _dot_tn(log_alpha, triu, PREC_CUM)
        for h in range(n_heads):
            q = qkv[:, h * hd:(h + 1) * hd]
            k = qkv[:, width + h * hd: width + (h + 1) * hd]
            qs.append(q * lax.rsqrt(jnp.sum(q * q, axis=-1, keepdims=True) + 1e-6) * (hd ** -0.5))
            ks.append(k * lax.rsqrt(jnp.sum(k * k, axis=-1, keepdims=True) + 1e-6))
            vs.append(qkv[:, 2 * width + h * hd: 2 * width + (h + 1) * hd])
            betas.append(beta_all[:, n_heads + h:n_heads + h + 1])
            g = g_all[:, h:h + 1]
            gs.append(g)
            g_lasts.append(g_all[CHUNK - 1:CHUNK, h:h + 1])
            decays.append(jnp.where(incl, jnp.exp(jnp.where(incl, g - g_all_t[h:h + 1, :], 0.0)), 0.0))
    yield
    idx = range(nb * n_heads)
    k_betas = [ks[i] * betas[i] for i in idx]
    egs = [jnp.exp(g) for g in gs]
    a_lows = [jnp.where(strict, _dot_nt(k_betas[i], ks[i], PREC_MIX) * decays[i], 0.0) for i in idx]
    qks = [_dot_nt(qs[i], ks[i], PREC_MIX) * decays[i] for i in idx]
    states = [state_ref[i] for i in idx]
    o_inters = [_dot(qs[i] * egs[i], states[i], PREC_MIX) for i in idx]
    yield
    t_invs = yield from _tri_inverse_stages(a_lows)
    us = [_dot(t_invs[i], vs[i] * betas[i], PREC_MIX) for i in idx]
    ws = [_dot(t_invs[i], k_betas[i] * egs[i], PREC_MIX) for i in idx]
    yield
    v_news = [us[i] - _dot(ws[i], states[i], PREC_MIX) for i in idx]
    yield
    os_ = [o_inters[i] + _dot(qks[i], v_news[i], PREC_MIX) for i in idx]
    for i in idx:
        state_ref[i] = (states[i] * jnp.exp(g_lasts[i])
                        + _dot_tn(ks[i] * jnp.exp(g_lasts[i] - gs[i]), v_news[i], PREC_MIX))
    yield
    for i in idx:
        bb, h = divmod(i, n_heads)
        gate = pa_ref[bb, :, 3 * width + h * hd: 3 * width + (h + 1) * hd]
        o_ref[bb, :, h * hd:(h + 1) * hd] = _rms(os_[i], onorm_ref[...]) * _silu(gate)


def _rwkv_stages(nb, col0, pb_ref, mu_ref, w0_ref, w2_ref, a0_ref, a2_ref, g2_ref, kk_ref, ka_ref, rk_ref,
                 lnw_ref, lnb_ref, bd_ref, o_ref, carry_ref, state_ref, obuf_ref):
    n_heads = state_ref.shape[0] // nb
    hd = B_HEAD_DIM
    width = n_heads * hd
    row = _iota2((CHUNK, CHUNK), 0)
    col = _iota2((CHUNK, CHUNK), 1)
    strict = row > col
    tril = (row >= col).astype(F32)
    incl2 = _iota2((CHUNK, 2 * CHUNK), 0) >= (_iota2((CHUNK, 2 * CHUNK), 1) & (CHUNK - 1))
    bd = bd_ref[...]

    pss = []
    for bb in range(nb):
        p = pb_ref[bb]
        prev = jnp.where(_iota2(p.shape, 0) == 0, carry_ref[bb, 0:1, :], pltpu.roll(p, 1, 0))
        carry_ref[bb, 0:1, :] = p[CHUNK - 1:CHUNK, :]
        pss.append(p + mu_ref[...] * (prev - p))
    rs = [ps[:, 0:width] for ps in pss]
    kraw = [ps[:, width:2 * width] for ps in pss]
    vfull = [ps[:, 2 * width:3 * width] for ps in pss]
    w_raws = [w0_ref[...] + _dot(jnp.tanh(ps[:, 3 * width:3 * width + LANES]), w2_ref[...], PREC_AUX) for ps in pss]
    iclrs = [_sigmoid(a0_ref[...] + _dot(ps[:, 3 * width + LANES:3 * width + 2 * LANES], a2_ref[...], PREC_AUX))
             for ps in pss]
    gates = [_dot(_sigmoid(ps[:, 3 * width + 2 * LANES:3 * width + 3 * LANES]), g2_ref[...], PREC_AUX) for ps in pss]
    kks = [k * kk_ref[...] for k in kraw]
    kk_ss = [_dot(kk * kk, bd, PREC_AUX) for kk in kks]
    yield
    log_ws = [-jnp.exp(-_softplus(-w_raw) - 0.5) for w_raw in w_raws]
    g_cums = [_dot(tril, log_w, PREC_CUM) for log_w in log_ws]
    yield
    lhss, rhss, bl_kls, p_cs, vhs, k2s = [], [], [], [], [], []
    for bb in range(nb):
        kk = kks[bb] * lax.rsqrt(kk_ss[bb] + 1e-6)
        k2 = kraw[bb] * (1.0 + (iclrs[bb] - 1.0) * ka_ref[...])
        k2s.append(k2)
        bvec = kk * iclrs[bb]
        g = g_cums[bb]
        g_last = g[CHUNK - 1:CHUNK, :]
        eng = jnp.exp(-g)
        egl = jnp.exp(g_last - g)
        r_t = rs[bb] * jnp.exp(g)
        kk_t = kk * jnp.exp(g - log_ws[bb])
        b_h, k_h = bvec * eng, k2 * eng
        b_l, k_l = bvec * egl, k2 * egl
        p_c = jnp.exp(g_last)
        for h in range(n_heads):
            sl = slice(h * hd, (h + 1) * hd)
            lhss.append(jnp.concatenate([kk_t[:, sl], r_t[:, sl]], axis=0))
            rhss.append(jnp.concatenate([b_h[:, sl], k_h[:, sl]], axis=0))
            bl_kls.append(jnp.concatenate([b_l[:, sl], k_l[:, sl]], axis=0))
            p_cs.append(p_c[:, sl])
            vhs.append(vfull[bb][:, sl])
    idx = range(nb * n_heads)
    ms = [_dot_nt(lhss[i], rhss[i], PREC_MIX) for i in idx]
    states = [state_ref[i] for i in idx]
    lss = [_dot_nt(lhss[i], states[i], PREC_MIX) for i in idx]
    yield
    a_bbs = [jnp.where(strict, m[0:CHUNK, 0:CHUNK], 0.0) for m in ms]
    a_bks = [jnp.where(strict, m[0:CHUNK, CHUNK:2 * CHUNK], 0.0) for m in ms]
    a_rs = [jnp.where(incl2, m[CHUNK:2 * CHUNK, :], 0.0) for m in ms]
    rhs_u = [lss[i][0:CHUNK] + _dot(a_bks[i], vhs[i], PREC_MIX) for i in idx]
    t_invs = yield from _tri_inverse_stages(a_bbs)
    uvs = [jnp.concatenate([-_dot(t_invs[i], rhs_u[i], PREC_MIX), vhs[i]], axis=0) for i in idx]
    yield
    for i in idx:
        bb, h = divmod(i, n_heads)
        obuf_ref[bb, :, h * hd:(h + 1) * hd] = lss[i][CHUNK:2 * CHUNK] + _dot(a_rs[i], uvs[i], PREC_MIX)
        state_ref[i] = states[i] * p_cs[i] + _dot_tn(uvs[i], bl_kls[i], PREC_MIX)
    yield
    inv_n = 1.0 / hd
    os_ = [obuf_ref[bb] for bb in range(nb)]
    means = [_dot(o, bd, PREC_AUX) * inv_n for o in os_]
    bonus = [_dot(rs[bb] * k2s[bb] * rk_ref[...], bd, PREC_AUX) * vfull[bb] for bb in range(nb)]
    yield
    cens = [o - mean for o, mean in zip(os_, means)]
    vars_ = [_dot(cen * cen, bd, PREC_AUX) * inv_n for cen in cens]
    yield
    for bb in range(nb):
        ln = cens[bb] * lax.rsqrt(vars_[bb] + GROUP_NORM_EPS) * lnw_ref[...] + lnb_ref[...]
        o_ref[bb, :, col0:col0 + width] = (ln + bonus[bb]) * gates[bb]


def _mixer_ab_kernel(n_gdn, n_rwkv, pa_ref, pdb_ref, pb_ref, *refs):
    gdn_params = refs[:n_gdn]
    rwkv_params = refs[n_gdn:n_gdn + n_rwkv]
    o_ref, xbuf_ref, gstate_ref, carry_ref, rstate_ref, obuf_ref = refs[n_gdn + n_rwkv:]
    nb = pa_ref.shape[0]

    @pl.when(pl.program_id(1) == 0)
    def _():
        xbuf_ref[:, 0:8, :] = jnp.zeros((nb, 8, xbuf_ref.shape[2]), F32)
        gstate_ref[...] = jnp.zeros_like(gstate_ref)
        carry_ref[...] = jnp.zeros_like(carry_ref)
        rstate_ref[...] = jnp.zeros_like(rstate_ref)

    a_width = pa_ref.shape[2] // 4
    _interleave([
        _gdn_stages(nb, pa_ref, pdb_ref, *gdn_params, o_ref, xbuf_ref, gstate_ref),
        _rwkv_stages(nb, a_width, pb_ref, *rwkv_params, o_ref, carry_ref, rstate_ref, obuf_ref),
    ])


def _mixer_ab(pa, pdb, pb, gdn_params, rwkv_params):
    b, t, a_cols = pa.shape
    b_cols = pb.shape[2]
    a_width = a_cols // 4
    b_width = (b_cols - 3 * LANES) // 3
    a_heads = a_width // A_HEAD_DIM
    b_heads = b_width // B_HEAD_DIM
    nb = MIXER_BATCH if b % MIXER_BATCH == 0 else 1
    params = list(gdn_params) + list(rwkv_params)
    return pl.pallas_call(
        functools.partial(_mixer_ab_kernel, len(gdn_params), len(rwkv_params)),
        grid=(b // nb, t // CHUNK),
        in_specs=[pl.BlockSpec((nb, CHUNK, a_cols), lambda i, c: (i, c, 0)),
                  pl.BlockSpec((nb, CHUNK, LANES), lambda i, c: (i, c, 0)),
                  pl.BlockSpec((nb, CHUNK, b_cols), lambda i, c: (i, c, 0))]
                 + [pl.BlockSpec(a.shape, lambda i, c: (0, 0)) for a in params],
        out_specs=pl.BlockSpec((nb, CHUNK, a_width + b_width), lambda i, c: (i, c, 0)),
        out_shape=jax.ShapeDtypeStruct((b, t, a_width + b_width), F32),
        scratch_shapes=[pltpu.VMEM((nb, 8 + CHUNK, 3 * a_width), F32),
                        pltpu.VMEM((nb * a_heads, A_HEAD_DIM, A_HEAD_DIM), F32),
                        pltpu.VMEM((nb, 8, b_cols), F32),
                        pltpu.VMEM((nb * b_heads, B_HEAD_DIM, B_HEAD_DIM), F32),
                        pltpu.VMEM((nb, CHUNK, b_width), F32)],
        compiler_params=_cparams(("parallel", "arbitrary")),
        name="mixer_ab",
    )(pa, pdb, pb, *params)


def _hgrn_tables():
    i = np.arange(CHUNK)[:, None]
    t = np.arange(CHUNK)[None, :]
    blocks, pair = [], []
    for ls in range(HGRN_LEVELS - 1, -1, -1):
        s = 1 << ls
        mid = (i >> (ls + 1) << (ls + 1)) + s
        blocks.append((i >= mid) & (t >= mid) & (t <= i))
        blocks.append((i < mid) & (t > i) & (t <= mid - 1))
        pair.append((((i >> ls) ^ (t >> ls)) == 1) & (i > t))
    blocks.append(t <= i)
    blocks.append(t > i)
    sums = np.concatenate(blocks, axis=0).astype(np.float32)
    return np.concatenate([sums] * 3, axis=1), np.stack(pair).astype(np.float32)


def _hgrn_kernel(p_ref, loglb_ref, log1mlb_ref, onorm_ref, sums_ref, pair_ref, o_ref, state_ref):
    nb = p_ref.shape[0]
    n_heads = state_ref.shape[0] // nb
    hd = C_KEY_DIM
    kw = n_heads * hd

    @pl.when(pl.program_id(1) == 0)
    def _():
        state_ref[...] = jnp.zeros_like(state_ref)

    def block(e, n):
        return e[n * CHUNK:(n + 1) * CHUNK]

    qs, ks, exps = [], [], []
    for bb in range(nb):
        qs.append(_silu(p_ref[bb, :, 0:kw]))
        f = p_ref[bb, :, kw:2 * kw]
        a1 = jnp.broadcast_to(loglb_ref[...], f.shape)
        b1 = log1mlb_ref[...] - _softplus(-f)
        log_f = jnp.maximum(a1, b1) + jnp.log(1.0 + jnp.exp(-jnp.abs(a1 - b1)))
        ks.append(1.0 - jnp.exp(log_f))
        f1 = log_f.astype(BF16)
        f2 = (log_f - f1.astype(F32)).astype(BF16)
        f3 = (log_f - f1.astype(F32) - f2.astype(F32)).astype(BF16)
        exps.append(_dot(sums_ref[...], jnp.concatenate([f1, f2, f3], axis=0)))

    idx = [(bb, h) for bb in range(nb) for h in range(n_heads)]
    states = [state_ref[bb * n_heads + h] for bb, h in idx]
    scores = [None] * len(idx)
    for lvl in range(HGRN_LEVELS):
        q_l = [(qs[bb] * jnp.exp(block(exps[bb], 2 * lvl))).astype(BF16) for bb in range(nb)]
        k_l = [(ks[bb] * jnp.exp(block(exps[bb], 2 * lvl + 1))).astype(BF16) for bb in range(nb)]
        mask = pair_ref[lvl]
        for n, (bb, h) in enumerate(idx):
            sl = slice(h * hd, (h + 1) * hd)
            part = _dot_nt(q_l[bb][:, sl], k_l[bb][:, sl]) * mask
            scores[n] = part if scores[n] is None else scores[n] + part
    g_cum = [block(exps[bb], 2 * HGRN_LEVELS) for bb in range(nb)]
    q_g = [qs[bb] * jnp.exp(g_cum[bb]) for bb in range(nb)]
    k_r = [ks[bb] * jnp.exp(block(exps[bb], 2 * HGRN_LEVELS + 1)) for bb in range(nb)]
    diag = [qs[bb] * ks[bb] for bb in range(nb)]
    outs = []
    for n, (bb, h) in enumerate(idx):
        sl = slice(h * hd, (h + 1) * hd)
        v = p_ref[bb, :, 2 * kw + h * hd: 2 * kw + (h + 1) * hd]
        o = _dot_nt(q_g[bb][:, sl], states[n], PREC_MIX) + _dot(scores[n], v, PREC_MIX)
        outs.append(o + jnp.sum(diag[bb][:, sl], axis=-1, keepdims=True) * v)
        p_last = jnp.exp(g_cum[bb][CHUNK - 1:CHUNK, sl])
        state_ref[bb * n_heads + h] = states[n] * p_last + _dot_tn(v, k_r[bb][:, sl], PREC_MIX)
    for n, (bb, h) in enumerate(idx):
        gate = p_ref[bb, :, 3 * kw + h * hd: 3 * kw + (h + 1) * hd]
        o_ref[bb, :, h * hd:(h + 1) * hd] = _rms(outs[n], onorm_ref[...]) * _silu(gate)


def _hgrn(proj, log_lb, log1m_lb, onorm):
    b, t, cols = proj.shape
    kw = cols // 4
    n_heads = kw // C_KEY_DIM
    nb = MIXER_BATCH if b % MIXER_BATCH == 0 else 1
    sums, pair = _hgrn_tables()
    sums = jnp.asarray(sums, BF16)
    pair = jnp.asarray(pair, F32)
    return pl.pallas_call(
        _hgrn_kernel,
        grid=(b // nb, t // CHUNK),
        in_specs=[pl.BlockSpec((nb, CHUNK, cols), lambda i, c: (i, c, 0)),
                  pl.BlockSpec((1, kw), lambda i, c: (0, 0)),
                  pl.BlockSpec((1, kw), lambda i, c: (0, 0)),
                  pl.BlockSpec((1, C_KEY_DIM), lambda i, c: (0, 0)),
                  pl.BlockSpec(sums.shape, lambda i, c: (0, 0)),
                  pl.BlockSpec(pair.shape, lambda i, c: (0, 0, 0))],
        out_specs=pl.BlockSpec((nb, CHUNK, kw), lambda i, c: (i, c, 0)),
        out_shape=jax.ShapeDtypeStruct((b, t, kw), F32),
        scratch_shapes=[pltpu.VMEM((nb * n_heads, C_KEY_DIM, C_KEY_DIM), F32)],
        compiler_params=_cparams(("parallel", "arbitrary")),
        name="hgrn2",
    )(proj, log_lb, log1m_lb, onorm, sums, pair)


def _xattn_kernel(x_ref, gain_ref, wq_ref, k_ref, v_ref, wo_ref, o_ref):
    x = x_ref[...]
    d = x.shape[1]
    hd = d // XA_HEADS
    q = _dot(_rms(x, gain_ref[...]).astype(BF16), wq_ref[...]).astype(BF16)
    outs = []
    for h in range(XA_HEADS):
        sl = slice(h * hd, (h + 1) * hd)
        s = _dot_nt(q[:, sl], k_ref[:, sl]) * (hd ** -0.5)
        p = jnp.exp(s - jnp.max(s, axis=-1, keepdims=True))
        p = p / jnp.sum(p, axis=-1, keepdims=True)
        outs.append(_dot(p.astype(BF16), v_ref[:, sl]).astype(BF16))
    o_ref[...] = x + _dot(jnp.concatenate(outs, axis=-1), wo_ref[...])


def _xattn(x, gain, wq, k, v, wo, tq=256):
    b, t, d = x.shape
    m = k.shape[1]
    tq = min(tq, t)
    return pl.pallas_call(
        _xattn_kernel,
        grid=(b, t // tq),
        in_specs=[pl.BlockSpec((None, tq, d), lambda i, j: (i, j, 0)),
                  pl.BlockSpec((1, d), lambda i, j: (0, 0)),
                  pl.BlockSpec((d, d), lambda i, j: (0, 0)),
                  pl.BlockSpec((None, m, d), lambda i, j: (i, 0, 0)),
                  pl.BlockSpec((None, m, d), lambda i, j: (i, 0, 0)),
                  pl.BlockSpec((d, d), lambda i, j: (0, 0))],
        out_specs=pl.BlockSpec((None, tq, d), lambda i, j: (i, j, 0)),
        out_shape=jax.ShapeDtypeStruct((b, t, d), F32),
        compiler_params=_cparams(("parallel", "parallel")),
        name="xattn",
    )(x, gain.reshape(1, d), wq, k, v, wo)


def _route(logits):
    lane = _iota2(logits.shape, 1)
    neg = -jnp.inf
    gmask = lane < N_GROUPS
    gl = jnp.where(gmask, logits, neg)
    gmax = jnp.max(gl, axis=-1, keepdims=True)
    gsel = jnp.min(jnp.where(gl == gmax, lane, LANES), axis=-1, keepdims=True)
    group_gate = 1.0 / jnp.sum(jnp.where(gmask, jnp.exp(gl - gmax), 0.0), axis=-1, keepdims=True)
    lo = N_GROUPS + EXPERTS_PER_GROUP * gsel
    emask = (lane >= lo) & (lane < lo + EXPERTS_PER_GROUP)
    el = jnp.where(emask, logits, neg)
    ee = jnp.where(emask, jnp.exp(el - jnp.max(el, axis=-1, keepdims=True)), 0.0)
    prob = ee / jnp.sum(ee, axis=-1, keepdims=True)
    p1 = jnp.max(jnp.where(emask, prob, -1.0), axis=-1, keepdims=True)
    i1 = jnp.min(jnp.where(emask & (prob == p1), lane, LANES), axis=-1, keepdims=True)
    rest = emask & (lane != i1)
    p2 = jnp.max(jnp.where(rest, prob, -1.0), axis=-1, keepdims=True)
    i2 = jnp.min(jnp.where(rest & (prob == p2), lane, LANES), axis=-1, keepdims=True)
    scale = group_gate / (p1 + p2)
    return jnp.where(lane == i1, p1 * scale, 0.0) + jnp.where(lane == i2, p2 * scale, 0.0)


def _moe_kernel(final_norm, x_ref, gain_ref, wr_ref, br_ref, w1_ref, w3_ref, w2_ref, gfin_ref, o_ref,
                xn_ref, comb_ref, acc_ref):
    grp = pl.program_id(1)
    ff = w1_ref.shape[2]

    @pl.when(grp == 0)
    def _():
        xn = _rms(x_ref[...], gain_ref[...])
        xn_hi = xn.astype(BF16)
        xn_lo = (xn - xn_hi.astype(F32)).astype(BF16)
        xn_ref[...] = xn_hi
        hi_part = _dot(xn_hi, wr_ref[...])
        logits = hi_part[:, 0:LANES] + hi_part[:, LANES:2 * LANES] + _dot(xn_lo, wr_ref[:, 0:LANES])
        comb_ref[...] = _route(logits + br_ref[...])
        acc_ref[...] = jnp.zeros_like(acc_ref)

    xn = xn_ref[...]
    comb = comb_ref[...]
    lane = _iota2(comb.shape, 1)
    acc = acc_ref[...]
    for e in range(EXPERTS_PER_GROUP):
        weight = jnp.sum(jnp.where(lane == N_GROUPS + EXPERTS_PER_GROUP * grp + e, comb, 0.0),
                         axis=-1, keepdims=True)
        hid = _silu(_dot(xn, w1_ref[e])) * _dot(xn, w3_ref[e]) * weight
        acc = acc + _dot(hid.astype(BF16), w2_ref[e * ff:(e + 1) * ff, :])
    acc_ref[...] = acc

    @pl.when(grp == N_GROUPS - 1)
    def _():
        y = x_ref[...] + acc_ref[...]
        if final_norm:
            y = _rms(y, gfin_ref[...])
        o_ref[...] = y


def _moe(x, gain, w_router, b_router, w1, w3, w2g, gain_final, final_norm, tm=512):
    n, d = x.shape
    tm = min(tm, n)
    ff = w1.shape[2]
    return pl.pallas_call(
        functools.partial(_moe_kernel, final_norm),
        grid=(n // tm, N_GROUPS),
        in_specs=[pl.BlockSpec((tm, d), lambda i, g: (i, 0)),
                  pl.BlockSpec((1, d), lambda i, g: (0, 0)),
                  pl.BlockSpec((d, 2 * LANES), lambda i, g: (0, 0)),
                  pl.BlockSpec((1, LANES), lambda i, g: (0, 0)),
                  pl.BlockSpec((EXPERTS_PER_GROUP, d, ff), lambda i, g: (g, 0, 0)),
                  pl.BlockSpec((EXPERTS_PER_GROUP, d, ff), lambda i, g: (g, 0, 0)),
                  pl.BlockSpec((None, EXPERTS_PER_GROUP * ff, d), lambda i, g: (g, 0, 0)),
                  pl.BlockSpec((1, d), lambda i, g: (0, 0))],
        out_specs=pl.BlockSpec((tm, d), lambda i, g: (i, 0)),
        out_shape=jax.ShapeDtypeStruct((n, d), F32),
        scratch_shapes=[pltpu.VMEM((tm, d), BF16),
                        pltpu.VMEM((tm, LANES), F32),
                        pltpu.VMEM((tm, d), F32)],
        compiler_params=_cparams(("parallel", "arbitrary")),
        name="hmoe",
    )(x, gain.reshape(1, d), w_router, b_router, w1, w3, w2g, gain_final.reshape(1, d))


def _pad_cols(w, n):
    return jnp.pad(w, ((0, 0), (0, n - w.shape[1])))


def _pad_rows(w, n):
    return jnp.pad(w, ((0, n - w.shape[0]), (0, 0)))


def _row(v, n=None):
    v = v.reshape(1, -1).astype(F32)
    return v if n is None else _pad_cols(v, n)


def kernel(x, mem, norm_mix, norm_xattn, norm_mem, norm_ffn, norm_final, ab_w_in, ab_conv, ab_a_log, ab_dt_bias, ab_onorm, ab_shift_mu, ab_w0, ab_w2, ab_a0, ab_a2, ab_g2, ab_k_k, ab_k_a, ab_r_k, ab_ln_w, ab_ln_b, ab_w_out, c_w_in, c_lb_param, c_onorm, c_w_out, xa_wq, xa_wkv, xa_wo, moe_w_group, moe_b_group, moe_w_expert, moe_b_expert, moe_w1, moe_w3, moe_w2):
    bsz, seq, d = x.shape
    n_mem = mem.shape[1]
    depth = norm_mix.shape[0]
    a_width = d // 2
    b_width = d // 2
    a_heads = a_width // A_HEAD_DIM
    a_cols = 4 * a_width + 2 * a_heads
    n_tok = bsz * seq

    lb_p = jax.nn.softmax(c_lb_param.astype(F32), axis=0)
    lower_bounds = jnp.cumsum(lb_p, axis=0) - lb_p[0]

    x2 = x.reshape(n_tok, d)
    mem2 = mem.reshape(bsz * n_mem, d)
    for layer in range(depth):
        if layer % 2 == 0:
            e = layer // 2
            w_in = ab_w_in[e]
            w_a = w_in[:, :4 * a_width].astype(BF16)
            w_db = _pad_cols(w_in[:, 4 * a_width:a_cols], LANES).astype(BF16)
            wb = w_in[:, a_cols:]
            o1, o2 = 3 * b_width, 3 * b_width + B_DECAY_LORA
            o3 = o2 + B_ICLR_LORA
            w_b = jnp.concatenate([wb[:, :o1], _pad_cols(wb[:, o1:o2], LANES), _pad_cols(wb[:, o2:o3], LANES),
                                   wb[:, o3:]], axis=1).astype(BF16)
            mu = ab_shift_mu[e].reshape(1, -1)
            mu_p = jnp.concatenate([mu[:, :o1], _pad_cols(mu[:, o1:o2], LANES), _pad_cols(mu[:, o2:o3], LANES),
                                    mu[:, o3:]], axis=1)
            pa, pdb, pb = _norm_proj(x2, norm_mix[layer], [w_a, w_db, w_b])
            head_id = jnp.arange(b_width) // B_HEAD_DIM
            bd = (head_id[:, None] == head_id[None, :]).astype(F32)
            gdn_params = [ab_conv[e], _row(ab_a_log[e], LANES), _row(ab_dt_bias[e], LANES), _row(ab_onorm[e])]
            rwkv_params = [mu_p, _row(ab_w0[e]), _pad_rows(ab_w2[e], LANES), _row(ab_a0[e]),
                           _pad_rows(ab_a2[e], LANES), ab_g2[e], _row(ab_k_k[e]), _row(ab_k_a[e]),
                           _row(ab_r_k[e]), _row(ab_ln_w[e]), _row(ab_ln_b[e]), bd]
            o_ab = _mixer_ab(pa.reshape(bsz, seq, -1), pdb.reshape(bsz, seq, LANES), pb.reshape(bsz, seq, -1),
                             gdn_params, rwkv_params)
            x2 = _out_proj(x2, o_ab.reshape(n_tok, a_width + b_width), ab_w_out[e].astype(BF16))
        else:
            o = layer // 2
            (proj,) = _norm_proj(x2, norm_mix[layer], [c_w_in[o].astype(BF16)])
            lb = lower_bounds[layer].reshape(1, -1)
            o_c = _hgrn(proj.reshape(bsz, seq, -1), jnp.log(lb), jnp.log1p(-lb), _row(c_onorm[o]))
            x2 = _out_proj(x2, o_c.reshape(n_tok, -1), c_w_out[o].astype(BF16))

        wkv = xa_wkv[layer].astype(BF16)
        k_mem, v_mem = _norm_proj(mem2, norm_mem[layer], [wkv[:, :d], wkv[:, d:]], out_dtype=BF16)
        x3 = _xattn(x2.reshape(bsz, seq, d), norm_xattn[layer], xa_wq[layer].astype(BF16),
                    k_mem.reshape(bsz, n_mem, d), v_mem.reshape(bsz, n_mem, d), xa_wo[layer].astype(BF16))
        x2 = x3.reshape(n_tok, d)

        w_r = _pad_cols(jnp.concatenate([moe_w_group[layer], moe_w_expert[layer]], axis=1), LANES).astype(F32)
        w_r_hi = w_r.astype(BF16)
        w_r_lo = (w_r - w_r_hi.astype(F32)).astype(BF16)
        b_router = _row(jnp.concatenate([moe_b_group[layer], moe_b_expert[layer]]), LANES)
        ff = moe_w2.shape[2]
        x2 = _moe(x2, norm_ffn[layer], jnp.concatenate([w_r_hi, w_r_lo], axis=1), b_router,
                  moe_w1[layer].astype(BF16), moe_w3[layer].astype(BF16),
                  moe_w2[layer].reshape(N_GROUPS, EXPERTS_PER_GROUP * ff, d).astype(BF16),
                  norm_final, final_norm=(layer == depth - 1))
    return x2.reshape(bsz, seq, d)
```

```python
import functools

import jax
import jax.numpy as jnp
import numpy as np
from jax import lax
from jax.experimental import pallas as pl
from jax.experimental.pallas import tpu as pltpu

F32 = jnp.float32
BF16 = jnp.bfloat16
HI = lax.Precision.HIGHEST
ONE_PASS = "bf16"

PREC_CUM = HI
PREC_INV = ONE_PASS
PREC_MIX = ONE_PASS
PREC_AUX = ONE_PASS

CHUNK = 64
HGRN_LEVELS = 6
NORM_EPS = 1e-6
GROUP_NORM_EPS = 64e-5
CONV_WIDTH = 4

A_HEAD_DIM = 128
B_HEAD_DIM = 64
B_DECAY_LORA = 64
B_ICLR_LORA = 64
B_GATE_LORA = 128
C_KEY_DIM = 128
XA_HEADS = 4
N_GROUPS = 4
EXPERTS_PER_GROUP = 4
N_EXPERTS = N_GROUPS * EXPERTS_PER_GROUP

LANES = 128
VMEM_LIMIT = 56 * 1024 * 1024
MIXER_BATCH = 2


def _cparams(sem):
    return pltpu.CompilerParams(dimension_semantics=sem, vmem_limit_bytes=VMEM_LIMIT)


def _dot_dims(a, b, dims, prec):
    if prec == ONE_PASS:
        a, b, prec = a.astype(BF16), b.astype(BF16), None
    return lax.dot_general(a, b, (dims, ((), ())), preferred_element_type=F32, precision=prec)


def _dot(a, b, prec=None):
    return _dot_dims(a, b, ((1,), (0,)), prec)


def _dot_nt(a, b, prec=None):
    return _dot_dims(a, b, ((1,), (1,)), prec)


def _dot_tn(a, b, prec=None):
    return _dot_dims(a, b, ((0,), (0,)), prec)


def _rms(x, gain):
    return x * lax.rsqrt(jnp.mean(x * x, axis=-1, keepdims=True) + NORM_EPS) * gain


def _sigmoid(x):
    return 1.0 / (1.0 + jnp.exp(-x))


def _silu(x):
    return x * _sigmoid(x)


def _softplus(x):
    return jnp.maximum(x, 0.0) + jnp.log(1.0 + jnp.exp(-jnp.abs(x)))


def _iota2(shape, dim):
    return lax.broadcasted_iota(jnp.int32, shape, dim)


def _interleave(stage_generators):
    live = list(stage_generators)
    while live:
        for gen in list(live):
            try:
                next(gen)
            except StopIteration:
                live.remove(gen)


def _tri_inverse_stages(mats):
    n = mats[0].shape[0]
    row = _iota2((n, n), 0)
    col = _iota2((n, n), 1)
    same16 = (row >> 4) == (col >> 4)
    same32 = (row >> 5) == (col >> 5)
    eye = (row == col).astype(F32)
    ads = [jnp.where(same16, a, 0.0) for a in mats]
    xs = [eye - ad for ad in ads]
    ps = [_dot(ad, ad, PREC_INV) for ad in ads]
    yield
    for stage in range(3):
        xs = [x + _dot(x, p, PREC_INV) for x, p in zip(xs, ps)]
        if stage < 2:
            ps = [_dot(p, p, PREC_INV) for p in ps]
        yield
    for lower in ([jnp.where(same32 & jnp.logical_not(same16), a, 0.0) for a in mats],
                  [jnp.where(same32, 0.0, a) for a in mats]):
        ts = [_dot(l, x, PREC_INV) for l, x in zip(lower, xs)]
        yield
        xs = [x - _dot(x, t, PREC_INV) for x, t in zip(xs, ts)]
        yield
    return xs


def _norm_proj_kernel(n_w, x_ref, g_ref, *refs):
    w_refs, o_refs = refs[:n_w], refs[n_w:]
    xn = _rms(x_ref[...], g_ref[...]).astype(BF16)
    for w_ref, o_ref in zip(w_refs, o_refs):
        n = w_ref.shape[1]
        for c0 in range(0, n, 512):
            c1 = min(n, c0 + 512)
            o_ref[:, c0:c1] = _dot(xn, w_ref[:, c0:c1]).astype(o_ref.dtype)


def _norm_proj(x, gain, ws, out_dtype=F32, tm=512):
    n, d = x.shape
    tm = min(tm, n)
    return pl.pallas_call(
        functools.partial(_norm_proj_kernel, len(ws)),
        grid=(n // tm,),
        in_specs=[pl.BlockSpec((tm, d), lambda i: (i, 0)),
                  pl.BlockSpec((1, d), lambda i: (0, 0))]
                 + [pl.BlockSpec(w.shape, lambda i: (0, 0)) for w in ws],
        out_specs=[pl.BlockSpec((tm, w.shape[1]), lambda i: (i, 0)) for w in ws],
        out_shape=[jax.ShapeDtypeStruct((n, w.shape[1]), out_dtype) for w in ws],
        compiler_params=_cparams(("parallel",)),
        name="norm_proj",
    )(x, gain.reshape(1, d), *ws)


def _gdn_stages(nb, pa_ref, pdb_ref, convw_ref, alog_ref, dtb_ref, onorm_ref, o_ref, xbuf_ref, state_ref):
    n_heads = state_ref.shape[0] // nb
    hd = A_HEAD_DIM
    width = n_heads * hd
    row = _iota2((CHUNK, CHUNK), 0)
    col = _iota2((CHUNK, CHUNK), 1)
    incl = row >= col
    strict = row > col
    tril = incl.astype(F32)
    triu = (row <= col).astype(F32)

    qs, ks, vs, betas, gs, g_lasts, decays = [], [], [], [], [], [], []
    for bb in range(nb):
        x = pa_ref[bb, :, 0:3 * width]
        xbuf_ref[bb, 8:8 + CHUNK, :] = x
        acc = convw_ref[0:1, :] * xbuf_ref[bb, 5:5 + CHUNK, :]
        for j in range(1, CONV_WIDTH):
            acc = acc + convw_ref[j:j + 1, :] * xbuf_ref[bb, 5 + j:5 + j + CHUNK, :]
        xbuf_ref[bb, 0:8, :] = x[CHUNK - 8:CHUNK, :]
        qkv = _silu(acc)
        pdb = pdb_ref[bb]
        log_alpha = -jnp.exp(alog_ref[...]) * _softplus(pdb + dtb_ref[...])
        beta_all = _sigmoid(pdb)
        g_all = _dot(tril, log_alpha, PREC_CUM)
        g_all_t = _dot_tn(log_alpha, triu, PREC_CUM)
        for h in range(n_heads):
            q = qkv[:, h * hd:(h + 1) * hd]
            k = qkv[:, width + h * hd: width + (h + 1) * hd]
            qs.append(q * lax.rsqrt(jnp.sum(q * q, axis=-1, keepdims=True) + 1e-6) * (hd ** -0.5))
            ks.append(k * lax.rsqrt(jnp.sum(k * k, axis=-1, keepdims=True) + 1e-6))
            vs.append(qkv[:, 2 * width + h * hd: 2 * width + (h + 1) * hd])
            betas.append(beta_all[:, n_heads + h:n_heads + h + 1])
            g = g_all[:, h:h + 1]
            gs.append(g)
            g_lasts.append(g_all[CHUNK - 1:CHUNK, h:h + 1])
            decays.append(jnp.where(incl, jnp.exp(jnp.where(incl, g - g_all_t[h:h + 1, :], 0.0)), 0.0))
    yield
    idx = range(nb * n_heads)
    k_betas = [ks[i] * betas[i] for i in idx]
    egs = [jnp.exp(g) for g in gs]
    a_lows = [jnp.where(strict, _dot_nt(k_betas[i], ks[i], PREC_MIX) * decays[i], 0.0) for i in idx]
    qks = [_dot_nt(qs[i], ks[i], PREC_MIX) * decays[i] for i in idx]
    states = [state_ref[i] for i in idx]
    o_inters = [_dot(qs[i] * egs[i], states[i], PREC_MIX) for i in idx]
    yield
    t_invs = yield from _tri_inverse_stages(a_lows)
    us = [_dot(t_invs[i], vs[i] * betas[i], PREC_MIX) for i in idx]
    ws = [_dot(t_invs[i], k_betas[i] * egs[i], PREC_MIX) for i in idx]
    yield
    v_news = [us[i] - _dot(ws[i], states[i], PREC_MIX) for i in idx]
    yield
    os_ = [o_inters[i] + _dot(qks[i], v_news[i], PREC_MIX) for i in idx]
    for i in idx:
        state_ref[i] = (states[i] * jnp.exp(g_lasts[i])
                        + _dot_tn(ks[i] * jnp.exp(g_lasts[i] - gs[i]), v_news[i], PREC_MIX))
    yield
    for i in idx:
        bb, h = divmod(i, n_heads)
        gate = pa_ref[bb, :, 3 * width + h * hd: 3 * width + (h + 1) * hd]
        o_ref[bb, :, h * hd:(h + 1) * hd] = _rms(os_[i], onorm_ref[...]) * _silu(gate)


def _rwkv_stages(nb, col0, pb_ref, mu_ref, w0_ref, w2_ref, a0_ref, a2_ref, g2_ref, kk_ref, ka_ref, rk_ref,
                 lnw_ref, lnb_ref, bd_ref, o_ref, carry_ref, state_ref, obuf_ref):
    n_heads = state_ref.shape[0] // nb
    hd = B_HEAD_DIM
    width = n_heads * hd
    row = _iota2((CHUNK, CHUNK), 0)
    col = _iota2((CHUNK, CHUNK), 1)
    strict = row > col
    tril = (row >= col).astype(F32)
    incl2 = _iota2((CHUNK, 2 * CHUNK), 0) >= (_iota2((CHUNK, 2 * CHUNK), 1) & (CHUNK - 1))
    bd = bd_ref[...]

    pss = []
    for bb in range(nb):
        p = pb_ref[bb]
        prev = jnp.where(_iota2(p.shape, 0) == 0, carry_ref[bb, 0:1, :], pltpu.roll(p, 1, 0))
        carry_ref[bb, 0:1, :] = p[CHUNK - 1:CHUNK, :]
        pss.append(p + mu_ref[...] * (prev - p))
    rs = [ps[:, 0:width] for ps in pss]
    kraw = [ps[:, width:2 * width] for ps in pss]
    vfull = [ps[:, 2 * width:3 * width] for ps in pss]
    w_raws = [w0_ref[...] + _dot(jnp.tanh(ps[:, 3 * width:3 * width + LANES]), w2_ref[...], PREC_AUX) for ps in pss]
    iclrs = [_sigmoid(a0_ref[...] + _dot(ps[:, 3 * width + LANES:3 * width + 2 * LANES], a2_ref[...], PREC_AUX))
             for ps in pss]
    gates = [_dot(_sigmoid(ps[:, 3 * width + 2 * LANES:3 * width + 3 * LANES]), g2_ref[...], PREC_AUX) for ps in pss]
    kks = [k * kk_ref[...] for k in kraw]
    kk_ss = [_dot(kk * kk, bd, PREC_AUX) for kk in kks]
    yield
    log_ws = [-jnp.exp(-_softplus(-w_raw) - 0.5) for w_raw in w_raws]
    g_cums = [_dot(tril, log_w, PREC_CUM) for log_w in log_ws]
    yield
    lhss, rhss, bl_kls, p_cs, vhs, k2s = [], [], [], [], [], []
    for bb in range(nb):
        kk = kks[bb] * lax.rsqrt(kk_ss[bb] + 1e-6)
        k2 = kraw[bb] * (1.0 + (iclrs[bb] - 1.0) * ka_ref[...])
        k2s.append(k2)
        bvec = kk * iclrs[bb]
        g = g_cums[bb]
        g_last = g[CHUNK - 1:CHUNK, :]
        eng = jnp.exp(-g)
        egl = jnp.exp(g_last - g)
        r_t = rs[bb] * jnp.exp(g)
        kk_t = kk * jnp.exp(g - log_ws[bb])
        b_h, k_h = bvec * eng, k2 * eng
        b_l, k_l = bvec * egl, k2 * egl
        p_c = jnp.exp(g_last)
        for h in range(n_heads):
            sl = slice(h * hd, (h + 1) * hd)
            lhss.append(jnp.concatenate([kk_t[:, sl], r_t[:, sl]], axis=0))
            rhss.append(jnp.concatenate([b_h[:, sl], k_h[:, sl]], axis=0))
            bl_kls.append(jnp.concatenate([b_l[:, sl], k_l[:, sl]], axis=0))
            p_cs.append(p_c[:, sl])
            vhs.append(vfull[bb][:, sl])
    idx = range(nb * n_heads)
    ms = [_dot_nt(lhss[i], rhss[i], PREC_MIX) for i in idx]
    states = [state_ref[i] for i in idx]
    lss = [_dot_nt(lhss[i], states[i], PREC_MIX) for i in idx]
    yield
    a_bbs = [jnp.where(strict, m[0:CHUNK, 0:CHUNK], 0.0) for m in ms]
    a_bks = [jnp.where(strict, m[0:CHUNK, CHUNK:2 * CHUNK], 0.0) for m in ms]
    a_rs = [jnp.where(incl2, m[CHUNK:2 * CHUNK, :], 0.0) for m in ms]
    rhs_u = [lss[i][0:CHUNK] + _dot(a_bks[i], vhs[i], PREC_MIX) for i in idx]
    t_invs = yield from _tri_inverse_stages(a_bbs)
    uvs = [jnp.concatenate([-_dot(t_invs[i], rhs_u[i], PREC_MIX), vhs[i]], axis=0) for i in idx]
    yield
    for i in idx:
        bb, h = divmod(i, n_heads)
        obuf_ref[bb, :, h * hd:(h + 1) * hd] = lss[i][CHUNK:2 * CHUNK] + _dot(a_rs[i], uvs[i], PREC_MIX)
        state_ref[i] = states[i] * p_cs[i] + _dot_tn(uvs[i], bl_kls[i], PREC_MIX)
    yield
    inv_n = 1.0 / hd
    os_ = [obuf_ref[bb] for bb in range(nb)]
    means = [_dot(o, bd, PREC_AUX) * inv_n for o in os_]
    bonus = [_dot(rs[bb] * k2s[bb] * rk_ref[...], bd, PREC_AUX) * vfull[bb] for bb in range(nb)]
    yield
    cens = [o - mean for o, mean in zip(os_, means)]
    vars_ = [_dot(cen * cen, bd, PREC_AUX) * inv_n for cen in cens]
    yield
    for bb in range(nb):
        ln = cens[bb] * lax.rsqrt(vars_[bb] + GROUP_NORM_EPS) * lnw_ref[...] + lnb_ref[...]
        o_ref[bb, :, col0:col0 + width] = (ln + bonus[bb]) * gates[bb]


def _mixer_ab_kernel(n_gdn, n_rwkv, pa_ref, pdb_ref, pb_ref, *refs):
    gdn_params = refs[:n_gdn]
    rwkv_params = refs[n_gdn:n_gdn + n_rwkv]
    o_ref, xbuf_ref, gstate_ref, carry_ref, rstate_ref, obuf_ref = refs[n_gdn + n_rwkv:]
    nb = pa_ref.shape[0]

    @pl.when(pl.program_id(1) == 0)
    def _():
        xbuf_ref[:, 0:8, :] = jnp.zeros((nb, 8, xbuf_ref.shape[2]), F32)
        gstate_ref[...] = jnp.zeros_like(gstate_ref)
        carry_ref[...] = jnp.zeros_like(carry_ref)
        rstate_ref[...] = jnp.zeros_like(rstate_ref)

    a_width = pa_ref.shape[2] // 4
    _interleave([
        _gdn_stages(nb, pa_ref, pdb_ref, *gdn_params, o_ref, xbuf_ref, gstate_ref),
        _rwkv_stages(nb, a_width, pb_ref, *rwkv_params, o_ref, carry_ref, rstate_ref, obuf_ref),
    ])


def _mixer_ab(pa, pdb, pb, gdn_params, rwkv_params):
    b, t, a_cols = pa.shape
    b_cols = pb.shape[2]
    a_width = a_cols // 4
    b_width = (b_cols - 3 * LANES) // 3
    a_heads = a_width // A_HEAD_DIM
    b_heads = b_width // B_HEAD_DIM
    nb = MIXER_BATCH if b % MIXER_BATCH == 0 else 1
    params = list(gdn_params) + list(rwkv_params)
    return pl.pallas_call(
        functools.partial(_mixer_ab_kernel, len(gdn_params), len(rwkv_params)),
        grid=(b // nb, t // CHUNK),
        in_specs=[pl.BlockSpec((nb, CHUNK, a_cols), lambda i, c: (i, c, 0)),
                  pl.BlockSpec((nb, CHUNK, LANES), lambda i, c: (i, c, 0)),
                  pl.BlockSpec((nb, CHUNK, b_cols), lambda i, c: (i, c, 0))]
                 + [pl.BlockSpec(a.shape, lambda i, c: (0, 0)) for a in params],
        out_specs=pl.BlockSpec((nb, CHUNK, a_width + b_width), lambda i, c: (i, c, 0)),
        out_shape=jax.ShapeDtypeStruct((b, t, a_width + b_width), F32),
        scratch_shapes=[pltpu.VMEM((nb, 8 + CHUNK, 3 * a_width), F32),
                        pltpu.VMEM((nb * a_heads, A_HEAD_DIM, A_HEAD_DIM), F32),
                        pltpu.VMEM((nb, 8, b_cols), F32),
                        pltpu.VMEM((nb * b_heads, B_HEAD_DIM, B_HEAD_DIM), F32),
                        pltpu.VMEM((nb, CHUNK, b_width), F32)],
        compiler_params=_cparams(("parallel", "arbitrary")),
        name="mixer_ab",
    )(pa, pdb, pb, *params)


def _hgrn_tables():
    i = np.arange(CHUNK)[:, None]
    t = np.arange(CHUNK)[None, :]
    blocks, pair = [], []
    for ls in range(HGRN_LEVELS - 1, -1, -1):
        s = 1 << ls
        mid = (i >> (ls + 1) << (ls + 1)) + s
        blocks.append((i >= mid) & (t >= mid) & (t <= i))
        blocks.append((i < mid) & (t > i) & (t <= mid - 1))
        pair.append((((i >> ls) ^ (t >> ls)) == 1) & (i > t))
    blocks.append(t <= i)
    blocks.append(t > i)
    sums = np.concatenate(blocks, axis=0).astype(np.float32)
    return np.concatenate([sums] * 3, axis=1), np.stack(pair).astype(np.float32)


def _hgrn_kernel(p_ref, loglb_ref, log1mlb_ref, onorm_ref, sums_ref, pair_ref, o_ref, state_ref):
    nb = p_ref.shape[0]
    n_heads = state_ref.shape[0] // nb
    hd = C_KEY_DIM
    kw = n_heads * hd

    @pl.when(pl.program_id(1) == 0)
    def _():
        state_ref[...] = jnp.zeros_like(state_ref)

    def block(e, n):
        return e[n * CHUNK:(n + 1) * CHUNK]

    qs, ks, exps = [], [], []
    for bb in range(nb):
        qs.append(_silu(p_ref[bb, :, 0:kw]))
        f = p_ref[bb, :, kw:2 * kw]
        a1 = jnp.broadcast_to(loglb_ref[...], f.shape)
        b1 = log1mlb_ref[...] - _softplus(-f)
        log_f = jnp.maximum(a1, b1) + jnp.log(1.0 + jnp.exp(-jnp.abs(a1 - b1)))
        ks.append(1.0 - jnp.exp(log_f))
        f1 = log_f.astype(BF16)
        f2 = (log_f - f1.astype(F32)).astype(BF16)
        f3 = (log_f - f1.astype(F32) - f2.astype(F32)).astype(BF16)
        exps.append(_dot(sums_ref[...], jnp.concatenate([f1, f2, f3], axis=0)))

    idx = [(bb, h) for bb in range(nb) for h in range(n_heads)]
    states = [state_ref[bb * n_heads + h] for bb, h in idx]
    scores = [None] * len(idx)
    for lvl in range(HGRN_LEVELS):
        q_l = [(qs[bb] * jnp.exp(block(exps[bb], 2 * lvl))).astype(BF16) for bb in range(nb)]
        k_l = [(ks[bb] * jnp.exp(block(exps[bb], 2 * lvl + 1))).astype(BF16) for bb in range(nb)]
        mask = pair_ref[lvl]
        for n, (bb, h) in enumerate(idx):
            sl = slice(h * hd, (h + 1) * hd)
            part = _dot_nt(q_l[bb][:, sl], k_l[bb][:, sl]) * mask
            scores[n] = part if scores[n] is None else scores[n] + part
    g_cum = [block(exps[bb], 2 * HGRN_LEVELS) for bb in range(nb)]
    q_g = [qs[bb] * jnp.exp(g_cum[bb]) for bb in range(nb)]
    k_r = [ks[bb] * jnp.exp(block(exps[bb], 2 * HGRN_LEVELS + 1)) for bb in range(nb)]
    diag = [qs[bb] * ks[bb] for bb in range(nb)]
    outs = []
    for n, (bb, h) in enumerate(idx):
        sl = slice(h * hd, (h + 1) * hd)
        v = p_ref[bb, :, 2 * kw + h * hd: 2 * kw + (h + 1) * hd]
        o = _dot_nt(q_g[bb][:, sl], states[n], PREC_MIX) + _dot(scores[n], v, PREC_MIX)
        outs.append(o + jnp.sum(diag[bb][:, sl], axis=-1, keepdims=True) * v)
        p_last = jnp.exp(g_cum[bb][CHUNK - 1:CHUNK, sl])
        state_ref[bb * n_heads + h] = states[n] * p_last + _dot_tn(v, k_r[bb][:, sl], PREC_MIX)
    for n, (bb, h) in enumerate(idx):
        gate = p_ref[bb, :, 3 * kw + h * hd: 3 * kw + (h + 1) * hd]
        o_ref[bb, :, h * hd:(h + 1) * hd] = _rms(outs[n], onorm_ref[...]) * _silu(gate)


def _hgrn(proj, log_lb, log1m_lb, onorm):
    b, t, cols = proj.shape
    kw = cols // 4
    n_heads = kw // C_KEY_DIM
    nb = MIXER_BATCH if b % MIXER_BATCH == 0 else 1
    sums, pair = _hgrn_tables()
    sums = jnp.asarray(sums, BF16)
    pair = jnp.asarray(pair, F32)
    return pl.pallas_call(
        _hgrn_kernel,
        grid=(b // nb, t // CHUNK),
        in_specs=[pl.BlockSpec((nb, CHUNK, cols), lambda i, c: (i, c, 0)),
                  pl.BlockSpec((1, kw), lambda i, c: (0, 0)),
                  pl.BlockSpec((1, kw), lambda i, c: (0, 0)),
                  pl.BlockSpec((1, C_KEY_DIM), lambda i, c: (0, 0)),
                  pl.BlockSpec(sums.shape, lambda i, c: (0, 0)),
                  pl.BlockSpec(pair.shape, lambda i, c: (0, 0, 0))],
        out_specs=pl.BlockSpec((nb, CHUNK, kw), lambda i, c: (i, c, 0)),
        out_shape=jax.ShapeDtypeStruct((b, t, kw), F32),
        scratch_shapes=[pltpu.VMEM((nb * n_heads, C_KEY_DIM, C_KEY_DIM), F32)],
        compiler_params=_cparams(("parallel", "arbitrary")),
        name="hgrn2",
    )(proj, log_lb, log1m_lb, onorm, sums, pair)


def _xattn_kernel(x_ref, a_ref, wmix_ref, gain_ref, wq_ref, k_ref, v_ref, wo_ref, o_ref):
    x = x_ref[...] + _dot(a_ref[...].astype(BF16), wmix_ref[...])
    d = x.shape[1]
    hd = d // XA_HEADS
    q = _dot(_rms(x, gain_ref[...]).astype(BF16), wq_ref[...]).astype(BF16)
    outs = []
    for h in range(XA_HEADS):
        sl = slice(h * hd, (h + 1) * hd)
        s = _dot_nt(q[:, sl], k_ref[:, sl]) * (hd ** -0.5)
        p = jnp.exp(s - jnp.max(s, axis=-1, keepdims=True))
        p = p / jnp.sum(p, axis=-1, keepdims=True)
        outs.append(_dot(p.astype(BF16), v_ref[:, sl]).astype(BF16))
    o_ref[...] = x + _dot(jnp.concatenate(outs, axis=-1), wo_ref[...])


def _xattn(x, act, w_mix, gain, wq, k, v, wo, tq=512):
    b, t, d = x.shape
    m = k.shape[1]
    tq = min(tq, t)
    return pl.pallas_call(
        _xattn_kernel,
        grid=(b, t // tq),
        in_specs=[pl.BlockSpec((None, tq, d), lambda i, j: (i, j, 0)),
                  pl.BlockSpec((None, tq, act.shape[2]), lambda i, j: (i, j, 0)),
                  pl.BlockSpec(w_mix.shape, lambda i, j: (0, 0)),
                  pl.BlockSpec((1, d), lambda i, j: (0, 0)),
                  pl.BlockSpec((d, d), lambda i, j: (0, 0)),
                  pl.BlockSpec((None, m, d), lambda i, j: (i, 0, 0)),
                  pl.BlockSpec((None, m, d), lambda i, j: (i, 0, 0)),
                  pl.BlockSpec((d, d), lambda i, j: (0, 0))],
        out_specs=pl.BlockSpec((None, tq, d), lambda i, j: (i, j, 0)),
        out_shape=jax.ShapeDtypeStruct((b, t, d), F32),
        compiler_params=_cparams(("parallel", "parallel")),
        name="xattn",
    )(x, act, w_mix, gain.reshape(1, d), wq, k, v, wo)


def _route(logits):
    lane = _iota2(logits.shape, 1)
    neg = -jnp.inf
    gmask = lane < N_GROUPS
    gl = jnp.where(gmask, logits, neg)
    gmax = jnp.max(gl, axis=-1, keepdims=True)
    gsel = jnp.min(jnp.where(gl == gmax, lane, LANES), axis=-1, keepdims=True)
    group_gate = 1.0 / jnp.sum(jnp.where(gmask, jnp.exp(gl - gmax), 0.0), axis=-1, keepdims=True)
    lo = N_GROUPS + EXPERTS_PER_GROUP * gsel
    emask = (lane >= lo) & (lane < lo + EXPERTS_PER_GROUP)
    el = jnp.where(emask, logits, neg)
    ee = jnp.where(emask, jnp.exp(el - jnp.max(el, axis=-1, keepdims=True)), 0.0)
    prob = ee / jnp.sum(ee, axis=-1, keepdims=True)
    p1 = jnp.max(jnp.where(emask, prob, -1.0), axis=-1, keepdims=True)
    i1 = jnp.min(jnp.where(emask & (prob == p1), lane, LANES), axis=-1, keepdims=True)
    rest = emask & (lane != i1)
    p2 = jnp.max(jnp.where(rest, prob, -1.0), axis=-1, keepdims=True)
    i2 = jnp.min(jnp.where(rest & (prob == p2), lane, LANES), axis=-1, keepdims=True)
    scale = group_gate / (p1 + p2)
    return jnp.where(lane == i1, p1 * scale, 0.0) + jnp.where(lane == i2, p2 * scale, 0.0)


def _moe_kernel(final_norm, x_ref, gain_ref, wr_ref, br_ref, w1_ref, w3_ref, w2_ref, gfin_ref, o_ref,
                xn_ref, comb_ref, acc_ref):
    grp = pl.program_id(1)
    ff = w1_ref.shape[2]

    @pl.when(grp == 0)
    def _():
        xn = _rms(x_ref[...], gain_ref[...])
        xn_hi = xn.astype(BF16)
        xn_lo = (xn - xn_hi.astype(F32)).astype(BF16)
        xn_ref[...] = xn_hi
        hi_part = _dot(xn_hi, wr_ref[...])
        logits = hi_part[:, 0:LANES] + hi_part[:, LANES:2 * LANES] + _dot(xn_lo, wr_ref[:, 0:LANES])
        comb_ref[...] = _route(logits + br_ref[...])
        acc_ref[...] = jnp.zeros_like(acc_ref)

    xn = xn_ref[...]
    comb = comb_ref[...]
    lane = _iota2(comb.shape, 1)
    acc = acc_ref[...]
    for e in range(EXPERTS_PER_GROUP):
        weight = jnp.sum(jnp.where(lane == N_GROUPS + EXPERTS_PER_GROUP * grp + e, comb, 0.0),
                         axis=-1, keepdims=True)
        hid = _silu(_dot(xn, w1_ref[e])) * _dot(xn, w3_ref[e]) * weight
        acc = acc + _dot(hid.astype(BF16), w2_ref[e * ff:(e + 1) * ff, :])
    acc_ref[...] = acc

    @pl.when(grp == N_GROUPS - 1)
    def _():
        y = x_ref[...] + acc_ref[...]
        if final_norm:
            y = _rms(y, gfin_ref[...])
        o_ref[...] = y


def _moe(x, gain, w_router, b_router, w1, w3, w2g, gain_final, final_norm, tm=512):
    n, d = x.shape
    tm = min(tm, n)
    ff = w1.shape[2]
    return pl.pallas_call(
        functools.partial(_moe_kernel, final_norm),
        grid=(n // tm, N_GROUPS),
        in_specs=[pl.BlockSpec((tm, d), lambda i, g: (i, 0)),
                  pl.BlockSpec((1, d), lambda i, g: (0, 0)),
                  pl.BlockSpec((d, 2 * LANES), lambda i, g: (0, 0)),
                  pl.BlockSpec((1, LANES), lambda i, g: (0, 0)),
                  pl.BlockSpec((EXPERTS_PER_GROUP, d, ff), lambda i, g: (g, 0, 0)),
                  pl.BlockSpec((EXPERTS_PER_GROUP, d, ff), lambda i, g: (g, 0, 0)),
                  pl.BlockSpec((None, EXPERTS_PER_GROUP * ff, d), lambda i, g: (g, 0, 0)),
                  pl.BlockSpec((1, d), lambda i, g: (0, 0))],
        out_specs=pl.BlockSpec((tm, d), lambda i, g: (i, 0)),
        out_shape=jax.ShapeDtypeStruct((n, d), F32),
        scratch_shapes=[pltpu.VMEM((tm, d), BF16),
                        pltpu.VMEM((tm, LANES), F32),
                        pltpu.VMEM((tm, d), F32)],
        compiler_params=_cparams(("parallel", "arbitrary")),
        name="hmoe",
    )(x, gain.reshape(1, d), w_router, b_router, w1, w3, w2g, gain_final.reshape(1, d))


def _pad_cols(w, n):
    return jnp.pad(w, ((0, 0), (0, n - w.shape[1])))


def _pad_rows(w, n):
    return jnp.pad(w, ((0, n - w.shape[0]), (0, 0)))


def _row(v, n=None):
    v = v.reshape(1, -1).astype(F32)
    return v if n is None else _pad_cols(v, n)


def kernel(x, mem, norm_mix, norm_xattn, norm_mem, norm_ffn, norm_final, ab_w_in, ab_conv, ab_a_log, ab_dt_bias, ab_onorm, ab_shift_mu, ab_w0, ab_w2, ab_a0, ab_a2, ab_g2, ab_k_k, ab_k_a, ab_r_k, ab_ln_w, ab_ln_b, ab_w_out, c_w_in, c_lb_param, c_onorm, c_w_out, xa_wq, xa_wkv, xa_wo, moe_w_group, moe_b_group, moe_w_expert, moe_b_expert, moe_w1, moe_w3, moe_w2):
    bsz, seq, d = x.shape
    n_mem = mem.shape[1]
    depth = norm_mix.shape[0]
    a_width = d // 2
    b_width = d // 2
    a_heads = a_width // A_HEAD_DIM
    a_cols = 4 * a_width + 2 * a_heads
    n_tok = bsz * seq

    lb_p = jax.nn.softmax(c_lb_param.astype(F32), axis=0)
    lower_bounds = jnp.cumsum(lb_p, axis=0) - lb_p[0]

    x2 = x.reshape(n_tok, d)
    mem2 = mem.reshape(bsz * n_mem, d)
    for layer in range(depth):
        if layer % 2 == 0:
            e = layer // 2
            w_in = ab_w_in[e]
            w_a = w_in[:, :4 * a_width].astype(BF16)
            w_db = _pad_cols(w_in[:, 4 * a_width:a_cols], LANES).astype(BF16)
            wb = w_in[:, a_cols:]
            o1, o2 = 3 * b_width, 3 * b_width + B_DECAY_LORA
            o3 = o2 + B_ICLR_LORA
            w_b = jnp.concatenate([wb[:, :o1], _pad_cols(wb[:, o1:o2], LANES), _pad_cols(wb[:, o2:o3], LANES),
                                   wb[:, o3:]], axis=1).astype(BF16)
            mu = ab_shift_mu[e].reshape(1, -1)
            mu_p = jnp.concatenate([mu[:, :o1], _pad_cols(mu[:, o1:o2], LANES), _pad_cols(mu[:, o2:o3], LANES),
                                    mu[:, o3:]], axis=1)
            pa, pdb, pb = _norm_proj(x2, norm_mix[layer], [w_a, w_db, w_b])
            head_id = jnp.arange(b_width) // B_HEAD_DIM
            bd = (head_id[:, None] == head_id[None, :]).astype(F32)
            gdn_params = [ab_conv[e], _row(ab_a_log[e], LANES), _row(ab_dt_bias[e], LANES), _row(ab_onorm[e])]
            rwkv_params = [mu_p, _row(ab_w0[e]), _pad_rows(ab_w2[e], LANES), _row(ab_a0[e]),
                           _pad_rows(ab_a2[e], LANES), ab_g2[e], _row(ab_k_k[e]), _row(ab_k_a[e]),
                           _row(ab_r_k[e]), _row(ab_ln_w[e]), _row(ab_ln_b[e]), bd]
            mix = _mixer_ab(pa.reshape(bsz, seq, -1), pdb.reshape(bsz, seq, LANES), pb.reshape(bsz, seq, -1),
                            gdn_params, rwkv_params)
            w_mix = ab_w_out[e].astype(BF16)
        else:
            o = layer // 2
            (proj,) = _norm_proj(x2, norm_mix[layer], [c_w_in[o].astype(BF16)])
            lb = lower_bounds[layer].reshape(1, -1)
            mix = _hgrn(proj.reshape(bsz, seq, -1), jnp.log(lb), jnp.log1p(-lb), _row(c_onorm[o]))
            w_mix = c_w_out[o].astype(BF16)

        wkv = xa_wkv[layer].astype(BF16)
        k_mem, v_mem = _norm_proj(mem2, norm_mem[layer], [wkv[:, :d], wkv[:, d:]], out_dtype=BF16)
        x3 = _xattn(x2.reshape(bsz, seq, d), mix, w_mix, norm_xattn[layer], xa_wq[layer].astype(BF16),
                    k_mem.reshape(bsz, n_mem, d), v_mem.reshape(bsz, n_mem, d), xa_wo[layer].astype(BF16))
        x2 = x3.reshape(n_tok, d)

        w_r = _pad_cols(jnp.concatenate([moe_w_group[layer], moe_w_expert[layer]], axis=1), LANES).astype(F32)
        w_r_hi = w_r.astype(BF16)
        w_r_lo = (w_r - w_r_hi.astype(F32)).astype(BF16)
        b_router = _row(jnp.concatenate([moe_b_group[layer], moe_b_expert[layer]]), LANES)
        ff = moe_w2.shape[2]
        x2 = _moe(x2, norm_ffn[layer], jnp.concatenate([w_r_hi, w_r_lo], axis=1), b_router,
                  moe_w1[layer].astype(BF16), moe_w3[layer].astype(BF16),
                  moe_w2[layer].reshape(N_GROUPS, EXPERTS_PER_GROUP * ff, d).astype(BF16),
                  norm_final, final_norm=(layer == depth - 1))
    return x2.reshape(bsz, seq, d)
```

```python
import functools

import jax
import jax.numpy as jnp
import numpy as np
from jax import lax
from jax.experimental import pallas as pl
from jax.experimental.pallas import tpu as pltpu

F32 = jnp.float32
BF16 = jnp.bfloat16
HI = lax.Precision.HIGHEST
ONE_PASS = "bf16"

PREC_CUM = HI
PREC_INV = ONE_PASS
PREC_MIX = ONE_PASS
PREC_AUX = ONE_PASS

CHUNK = 64
HGRN_LEVELS = 6
NORM_EPS = 1e-6
GROUP_NORM_EPS = 64e-5
CONV_WIDTH = 4

A_HEAD_DIM = 128
B_HEAD_DIM = 64
B_DECAY_LORA = 64
B_ICLR_LORA = 64
B_GATE_LORA = 128
C_KEY_DIM = 128
XA_HEADS = 4
N_GROUPS = 4
EXPERTS_PER_GROUP = 4
N_EXPERTS = N_GROUPS * EXPERTS_PER_GROUP

LANES = 128
VMEM_LIMIT = 56 * 1024 * 1024
MIXER_BATCH = 2
SEG_ALIGN = 16
MOE_WINDOW = 4
MOE_CHUNK = 256


def _cparams(sem):
    return pltpu.CompilerParams(dimension_semantics=sem, vmem_limit_bytes=VMEM_LIMIT)


def _dot_dims(a, b, dims, prec):
    if prec == ONE_PASS:
        a, b, prec = a.astype(BF16), b.astype(BF16), None
    return lax.dot_general(a, b, (dims, ((), ())), preferred_element_type=F32, precision=prec)


def _dot(a, b, prec=None):
    return _dot_dims(a, b, ((1,), (0,)), prec)


def _dot_nt(a, b, prec=None):
    return _dot_dims(a, b, ((1,), (1,)), prec)


def _dot_tn(a, b, prec=None):
    return _dot_dims(a, b, ((0,), (0,)), prec)


def _rms(x, gain):
    return x * lax.rsqrt(jnp.mean(x * x, axis=-1, keepdims=True) + NORM_EPS) * gain


def _sigmoid(x):
    return 1.0 / (1.0 + jnp.exp(-x))


def _silu(x):
    return x * _sigmoid(x)


def _softplus(x):
    return jnp.maximum(x, 0.0) + jnp.log(1.0 + jnp.exp(-jnp.abs(x)))


def _iota2(shape, dim):
    return lax.broadcasted_iota(jnp.int32, shape, dim)


def _interleave(stage_generators):
    live = list(stage_generators)
    while live:
        for gen in list(live):
            try:
                next(gen)
            except StopIteration:
                live.remove(gen)


def _tri_inverse_stages(mats):
    n = mats[0].shape[0]
    row = _iota2((n, n), 0)
    col = _iota2((n, n), 1)
    same16 = (row >> 4) == (col >> 4)
    same32 = (row >> 5) == (col >> 5)
    eye = (row == col).astype(F32)
    ads = [jnp.where(same16, a, 0.0) for a in mats]
    xs = [eye - ad for ad in ads]
    ps = [_dot(ad, ad, PREC_INV) for ad in ads]
    yield
    for stage in range(3):
        xs = [x + _dot(x, p, PREC_INV) for x, p in zip(xs, ps)]
        if stage < 2:
            ps = [_dot(p, p, PREC_INV) for p in ps]
        yield
    for lower in ([jnp.where(same32 & jnp.logical_not(same16), a, 0.0) for a in mats],
                  [jnp.where(same32, 0.0, a) for a in mats]):
        ts = [_dot(l, x, PREC_INV) for l, x in zip(lower, xs)]
        yield
        xs = [x - _dot(x, t, PREC_INV) for x, t in zip(xs, ts)]
        yield
    return xs


def _norm_proj_kernel(n_w, x_ref, g_ref, *refs):
    w_refs, o_refs = refs[:n_w], refs[n_w:]
    xn = _rms(x_ref[...], g_ref[...]).astype(BF16)
    for w_ref, o_ref in zip(w_refs, o_refs):
        n = w_ref.shape[1]
        for c0 in range(0, n, 512):
            c1 = min(n, c0 + 512)
            o_ref[:, c0:c1] = _dot(xn, w_ref[:, c0:c1]).astype(o_ref.dtype)


def _norm_proj(x, gain, ws, out_dtype=F32, tm=512):
    n, d = x.shape
    tm = min(tm, n)
    return pl.pallas_call(
        functools.partial(_norm_proj_kernel, len(ws)),
        grid=(n // tm,),
        in_specs=[pl.BlockSpec((tm, d), lambda i: (i, 0)),
                  pl.BlockSpec((1, d), lambda i: (0, 0))]
                 + [pl.BlockSpec(w.shape, lambda i: (0, 0)) for w in ws],
        out_specs=[pl.BlockSpec((tm, w.shape[1]), lambda i: (i, 0)) for w in ws],
        out_shape=[jax.ShapeDtypeStruct((n, w.shape[1]), out_dtype) for w in ws],
        compiler_params=_cparams(("parallel",)),
        name="norm_proj",
    )(x, gain.reshape(1, d), *ws)


def _gdn_stages(nb, pa_ref, pdb_ref, convw_ref, alog_ref, dtb_ref, onorm_ref, o_ref, xbuf_ref, state_ref):
    n_heads = state_ref.shape[0] // nb
    hd = A_HEAD_DIM
    width = n_heads * hd
    row = _iota2((CHUNK, CHUNK), 0)
    col = _iota2((CHUNK, CHUNK), 1)
    incl = row >= col
    strict = row > col
    tril = incl.astype(F32)
    triu = (row <= col).astype(F32)

    qs, ks, vs, betas, gs, g_lasts, decays = [], [], [], [], [], [], []
    for bb in range(nb):
        x = pa_ref[bb, :, 0:3 * width]
        xbuf_ref[bb, 8:8 + CHUNK, :] = x
        acc = convw_ref[0:1, :] * xbuf_ref[bb, 5:5 + CHUNK, :]
        for j in range(1, CONV_WIDTH):
            acc = acc + convw_ref[j:j + 1, :] * xbuf_ref[bb, 5 + j:5 + j + CHUNK, :]
        xbuf_ref[bb, 0:8, :] = x[CHUNK - 8:CHUNK, :]
        qkv = _silu(acc)
        pdb = pdb_ref[bb]
        log_alpha = -jnp.exp(alog_ref[...]) * _softplus(pdb + dtb_ref[...])
        beta_all = _sigmoid(pdb)
        g_all = _dot(tril, log_alpha, PREC_CUM)
        g_all_t = _dot_tn(log_alpha, triu, PREC_CUM)
        for h in range(n_heads):
            q = qkv[:, h * hd:(h + 1) * hd]
            k = qkv[:, width + h * hd: width + (h + 1) * hd]
            qs.append(q * lax.rsqrt(jnp.sum(q * q, axis=-1, keepdims=True) + 1e-6) * (hd ** -0.5))
            ks.append(k * lax.rsqrt(jnp.sum(k * k, axis=-1, keepdims=True) + 1e-6))
            vs.append(qkv[:, 2 * width + h * hd: 2 * width + (h + 1) * hd])
            betas.append(beta_all[:, n_heads + h:n_heads + h + 1])
            g = g_all[:, h:h + 1]
            gs.append(g)
            g_lasts.append(g_all[CHUNK - 1:CHUNK, h:h + 1])
            decays.append(jnp.where(incl, jnp.exp(jnp.where(incl, g - g_all_t[h:h + 1, :], 0.0)), 0.0))
    yield
    idx = range(nb * n_heads)
    k_betas = [ks[i] * betas[i] for i in idx]
    egs = [jnp.exp(g) for g in gs]
    a_lows = [jnp.where(strict, _dot_nt(k_betas[i], ks[i], PREC_MIX) * decays[i], 0.0) for i in idx]
    qks = [_dot_nt(qs[i], ks[i], PREC_MIX) * decays[i] for i in idx]
    states = [state_ref[i] for i in idx]
    o_inters = [_dot(qs[i] * egs[i], states[i], PREC_MIX) for i in idx]
    yield
    t_invs = yield from _tri_inverse_stages(a_lows)
    us = [_dot(t_invs[i], vs[i] * betas[i], PREC_MIX) for i in idx]
    ws = [_dot(t_invs[i], k_betas[i] * egs[i], PREC_MIX) for i in idx]
    yield
    v_news = [us[i] - _dot(ws[i], states[i], PREC_MIX) for i in idx]
    yield
    os_ = [o_inters[i] + _dot(qks[i], v_news[i], PREC_MIX) for i in idx]
    for i in idx:
        state_ref[i] = (states[i] * jnp.exp(g_lasts[i])
                        + _dot_tn(ks[i] * jnp.exp(g_lasts[i] - gs[i]), v_news[i], PREC_MIX))
    yield
    for i in idx:
        bb, h = divmod(i, n_heads)
        gate = pa_ref[bb, :, 3 * width + h * hd: 3 * width + (h + 1) * hd]
        o_ref[bb, :, h * hd:(h + 1) * hd] = _rms(os_[i], onorm_ref[...]) * _silu(gate)


def _rwkv_stages(nb, col0, pb_ref, mu_ref, w0_ref, w2_ref, a0_ref, a2_ref, g2_ref, kk_ref, ka_ref, rk_ref,
                 lnw_ref, lnb_ref, bd_ref, o_ref, carry_ref, state_ref, obuf_ref):
    n_heads = state_ref.shape[0] // nb
    hd = B_HEAD_DIM
    width = n_heads * hd
    row = _iota2((CHUNK, CHUNK), 0)
    col = _iota2((CHUNK, CHUNK), 1)
    strict = row > col
    tril = (row >= col).astype(F32)
    incl2 = _iota2((CHUNK, 2 * CHUNK), 0) >= (_iota2((CHUNK, 2 * CHUNK), 1) & (CHUNK - 1))
    bd = bd_ref[...]

    pss = []
    for bb in range(nb):
        p = pb_ref[bb]
        prev = jnp.where(_iota2(p.shape, 0) == 0, carry_ref[bb, 0:1, :], pltpu.roll(p, 1, 0))
        carry_ref[bb, 0:1, :] = p[CHUNK - 1:CHUNK, :]
        pss.append(p + mu_ref[...] * (prev - p))
    rs = [ps[:, 0:width] for ps in pss]
    kraw = [ps[:, width:2 * width] for ps in pss]
    vfull = [ps[:, 2 * width:3 * width] for ps in pss]
    w_raws = [w0_ref[...] + _dot(jnp.tanh(ps[:, 3 * width:3 * width + LANES]), w2_ref[...], PREC_AUX) for ps in pss]
    iclrs = [_sigmoid(a0_ref[...] + _dot(ps[:, 3 * width + LANES:3 * width + 2 * LANES], a2_ref[...], PREC_AUX))
             for ps in pss]
    gates = [_dot(_sigmoid(ps[:, 3 * width + 2 * LANES:3 * width + 3 * LANES]), g2_ref[...], PREC_AUX) for ps in pss]
    kks = [k * kk_ref[...] for k in kraw]
    kk_ss = [_dot(kk * kk, bd, PREC_AUX) for kk in kks]
    yield
    log_ws = [-jnp.exp(-_softplus(-w_raw) - 0.5) for w_raw in w_raws]
    g_cums = [_dot(tril, log_w, PREC_CUM) for log_w in log_ws]
    yield
    lhss, rhss, bl_kls, p_cs, vhs, k2s = [], [], [], [], [], []
    for bb in range(nb):
        kk = kks[bb] * lax.rsqrt(kk_ss[bb] + 1e-6)
        k2 = kraw[bb] * (1.0 + (iclrs[bb] - 1.0) * ka_ref[...])
        k2s.append(k2)
        bvec = kk * iclrs[bb]
        g = g_cums[bb]
        g_last = g[CHUNK - 1:CHUNK, :]
        eng = jnp.exp(-g)
        egl = jnp.exp(g_last - g)
        r_t = rs[bb] * jnp.exp(g)
        kk_t = kk * jnp.exp(g - log_ws[bb])
        b_h, k_h = bvec * eng, k2 * eng
        b_l, k_l = bvec * egl, k2 * egl
        p_c = jnp.exp(g_last)
        for h in range(n_heads):
            sl = slice(h * hd, (h + 1) * hd)
            lhss.append(jnp.concatenate([kk_t[:, sl], r_t[:, sl]], axis=0))
            rhss.append(jnp.concatenate([b_h[:, sl], k_h[:, sl]], axis=0))
            bl_kls.append(jnp.concatenate([b_l[:, sl], k_l[:, sl]], axis=0))
            p_cs.append(p_c[:, sl])
            vhs.append(vfull[bb][:, sl])
    idx = range(nb * n_heads)
    ms = [_dot_nt(lhss[i], rhss[i], PREC_MIX) for i in idx]
    states = [state_ref[i] for i in idx]
    lss = [_dot_nt(lhss[i], states[i], PREC_MIX) for i in idx]
    yield
    a_bbs = [jnp.where(strict, m[0:CHUNK, 0:CHUNK], 0.0) for m in ms]
    a_bks = [jnp.where(strict, m[0:CHUNK, CHUNK:2 * CHUNK], 0.0) for m in ms]
    a_rs = [jnp.where(incl2, m[CHUNK:2 * CHUNK, :], 0.0) for m in ms]
    rhs_u = [lss[i][0:CHUNK] + _dot(a_bks[i], vhs[i], PREC_MIX) for i in idx]
    t_invs = yield from _tri_inverse_stages(a_bbs)
    uvs = [jnp.concatenate([-_dot(t_invs[i], rhs_u[i], PREC_MIX), vhs[i]], axis=0) for i in idx]
    yield
    for i in idx:
        bb, h = divmod(i, n_heads)
        obuf_ref[bb, :, h * hd:(h + 1) * hd] = lss[i][CHUNK:2 * CHUNK] + _dot(a_rs[i], uvs[i], PREC_MIX)
        state_ref[i] = states[i] * p_cs[i] + _dot_tn(uvs[i], bl_kls[i], PREC_MIX)
    yield
    inv_n = 1.0 / hd
    os_ = [obuf_ref[bb] for bb in range(nb)]
    means = [_dot(o, bd, PREC_AUX) * inv_n for o in os_]
    bonus = [_dot(rs[bb] * k2s[bb] * rk_ref[...], bd, PREC_AUX) * vfull[bb] for bb in range(nb)]
    yield
    cens = [o - mean for o, mean in zip(os_, means)]
    vars_ = [_dot(cen * cen, bd, PREC_AUX) * inv_n for cen in cens]
    yield
    for bb in range(nb):
        ln = cens[bb] * lax.rsqrt(vars_[bb] + GROUP_NORM_EPS) * lnw_ref[...] + lnb_ref[...]
        o_ref[bb, :, col0:col0 + width] = (ln + bonus[bb]) * gates[bb]


def _mixer_ab_kernel(n_gdn, n_rwkv, pa_ref, pdb_ref, pb_ref, *refs):
    gdn_params = refs[:n_gdn]
    rwkv_params = refs[n_gdn:n_gdn + n_rwkv]
    o_ref, xbuf_ref, gstate_ref, carry_ref, rstate_ref, obuf_ref = refs[n_gdn + n_rwkv:]
    nb = pa_ref.shape[0]

    @pl.when(pl.program_id(1) == 0)
    def _():
        xbuf_ref[:, 0:8, :] = jnp.zeros((nb, 8, xbuf_ref.shape[2]), F32)
        gstate_ref[...] = jnp.zeros_like(gstate_ref)
        carry_ref[...] = jnp.zeros_like(carry_ref)
        rstate_ref[...] = jnp.zeros_like(rstate_ref)

    a_width = pa_ref.shape[2] // 4
    _interleave([
        _gdn_stages(nb, pa_ref, pdb_ref, *gdn_params, o_ref, xbuf_ref, gstate_ref),
        _rwkv_stages(nb, a_width, pb_ref, *rwkv_params, o_ref, carry_ref, rstate_ref, obuf_ref),
    ])


def _mixer_ab(pa, pdb, pb, gdn_params, rwkv_params):
    b, t, a_cols = pa.shape
    b_cols = pb.shape[2]
    a_width = a_cols // 4
    b_width = (b_cols - 3 * LANES) // 3
    a_heads = a_width // A_HEAD_DIM
    b_heads = b_width // B_HEAD_DIM
    nb = MIXER_BATCH if b % MIXER_BATCH == 0 else 1
    params = list(gdn_params) + list(rwkv_params)
    return pl.pallas_call(
        functools.partial(_mixer_ab_kernel, len(gdn_params), len(rwkv_params)),
        grid=(b // nb, t // CHUNK),
        in_specs=[pl.BlockSpec((nb, CHUNK, a_cols), lambda i, c: (i, c, 0)),
                  pl.BlockSpec((nb, CHUNK, LANES), lambda i, c: (i, c, 0)),
                  pl.BlockSpec((nb, CHUNK, b_cols), lambda i, c: (i, c, 0))]
                 + [pl.BlockSpec(a.shape, lambda i, c: (0, 0)) for a in params],
        out_specs=pl.BlockSpec((nb, CHUNK, a_width + b_width), lambda i, c: (i, c, 0)),
        out_shape=jax.ShapeDtypeStruct((b, t, a_width + b_width), F32),
        scratch_shapes=[pltpu.VMEM((nb, 8 + CHUNK, 3 * a_width), F32),
                        pltpu.VMEM((nb * a_heads, A_HEAD_DIM, A_HEAD_DIM), F32),
                        pltpu.VMEM((nb, 8, b_cols), F32),
                        pltpu.VMEM((nb * b_heads, B_HEAD_DIM, B_HEAD_DIM), F32),
                        pltpu.VMEM((nb, CHUNK, b_width), F32)],
        compiler_params=_cparams(("parallel", "arbitrary")),
        name="mixer_ab",
    )(pa, pdb, pb, *params)


def _hgrn_tables():
    i = np.arange(CHUNK)[:, None]
    t = np.arange(CHUNK)[None, :]
    blocks, pair = [], []
    for ls in range(HGRN_LEVELS - 1, -1, -1):
        s = 1 << ls
        mid = (i >> (ls + 1) << (ls + 1)) + s
        blocks.append((i >= mid) & (t >= mid) & (t <= i))
        blocks.append((i < mid) & (t > i) & (t <= mid - 1))
        pair.append((((i >> ls) ^ (t >> ls)) == 1) & (i > t))
    blocks.append(t <= i)
    blocks.append(t > i)
    sums = np.concatenate(blocks, axis=0).astype(np.float32)
    return np.concatenate([sums] * 3, axis=1), np.stack(pair).astype(np.float32)


def _hgrn_kernel(p_ref, loglb_ref, log1mlb_ref, onorm_ref, sums_ref, pair_ref, o_ref, state_ref):
    nb = p_ref.shape[0]
    n_heads = state_ref.shape[0] // nb
    hd = C_KEY_DIM
    kw = n_heads * hd

    @pl.when(pl.program_id(1) == 0)
    def _():
        state_ref[...] = jnp.zeros_like(state_ref)

    def block(e, n):
        return e[n * CHUNK:(n + 1) * CHUNK]

    qs, ks, exps = [], [], []
    for bb in range(nb):
        qs.append(_silu(p_ref[bb, :, 0:kw]))
        f = p_ref[bb, :, kw:2 * kw]
        a1 = jnp.broadcast_to(loglb_ref[...], f.shape)
        b1 = log1mlb_ref[...] - _softplus(-f)
        log_f = jnp.maximum(a1, b1) + jnp.log(1.0 + jnp.exp(-jnp.abs(a1 - b1)))
        ks.append(1.0 - jnp.exp(log_f))
        f1 = log_f.astype(BF16)
        f2 = (log_f - f1.astype(F32)).astype(BF16)
        f3 = (log_f - f1.astype(F32) - f2.astype(F32)).astype(BF16)
        exps.append(_dot(sums_ref[...], jnp.concatenate([f1, f2, f3], axis=0)))

    idx = [(bb, h) for bb in range(nb) for h in range(n_heads)]
    states = [state_ref[bb * n_heads + h] for bb, h in idx]
    scores = [None] * len(idx)
    for lvl in range(HGRN_LEVELS):
        q_l = [(qs[bb] * jnp.exp(block(exps[bb], 2 * lvl))).astype(BF16) for bb in range(nb)]
        k_l = [(ks[bb] * jnp.exp(block(exps[bb], 2 * lvl + 1))).astype(BF16) for bb in range(nb)]
        mask = pair_ref[lvl]
        for n, (bb, h) in enumerate(idx):
            sl = slice(h * hd, (h + 1) * hd)
            part = _dot_nt(q_l[bb][:, sl], k_l[bb][:, sl]) * mask
            scores[n] = part if scores[n] is None else scores[n] + part
    g_cum = [block(exps[bb], 2 * HGRN_LEVELS) for bb in range(nb)]
    q_g = [qs[bb] * jnp.exp(g_cum[bb]) for bb in range(nb)]
    k_r = [ks[bb] * jnp.exp(block(exps[bb], 2 * HGRN_LEVELS + 1)) for bb in range(nb)]
    diag = [qs[bb] * ks[bb] for bb in range(nb)]
    outs = []
    for n, (bb, h) in enumerate(idx):
        sl = slice(h * hd, (h + 1) * hd)
        v = p_ref[bb, :, 2 * kw + h * hd: 2 * kw + (h + 1) * hd]
        o = _dot_nt(q_g[bb][:, sl], states[n], PREC_MIX) + _dot(scores[n], v, PREC_MIX)
        outs.append(o + jnp.sum(diag[bb][:, sl], axis=-1, keepdims=True) * v)
        p_last = jnp.exp(g_cum[bb][CHUNK - 1:CHUNK, sl])
        state_ref[bb * n_heads + h] = states[n] * p_last + _dot_tn(v, k_r[bb][:, sl], PREC_MIX)
    for n, (bb, h) in enumerate(idx):
        gate = p_ref[bb, :, 3 * kw + h * hd: 3 * kw + (h + 1) * hd]
        o_ref[bb, :, h * hd:(h + 1) * hd] = _rms(outs[n], onorm_ref[...]) * _silu(gate)


def _hgrn(proj, log_lb, log1m_lb, onorm):
    b, t, cols = proj.shape
    kw = cols // 4
    n_heads = kw // C_KEY_DIM
    nb = MIXER_BATCH if b % MIXER_BATCH == 0 else 1
    sums, pair = _hgrn_tables()
    sums = jnp.asarray(sums, BF16)
    pair = jnp.asarray(pair, F32)
    return pl.pallas_call(
        _hgrn_kernel,
        grid=(b // nb, t // CHUNK),
        in_specs=[pl.BlockSpec((nb, CHUNK, cols), lambda i, c: (i, c, 0)),
                  pl.BlockSpec((1, kw), lambda i, c: (0, 0)),
                  pl.BlockSpec((1, kw), lambda i, c: (0, 0)),
                  pl.BlockSpec((1, C_KEY_DIM), lambda i, c: (0, 0)),
                  pl.BlockSpec(sums.shape, lambda i, c: (0, 0)),
                  pl.BlockSpec(pair.shape, lambda i, c: (0, 0, 0))],
        out_specs=pl.BlockSpec((nb, CHUNK, kw), lambda i, c: (i, c, 0)),
        out_shape=jax.ShapeDtypeStruct((b, t, kw), F32),
        scratch_shapes=[pltpu.VMEM((nb * n_heads, C_KEY_DIM, C_KEY_DIM), F32)],
        compiler_params=_cparams(("parallel", "arbitrary")),
        name="hgrn2",
    )(proj, log_lb, log1m_lb, onorm, sums, pair)


def _route(logits):
    lane = _iota2(logits.shape, 1)
    neg = -jnp.inf
    gmask = lane < N_GROUPS
    gl = jnp.where(gmask, logits, neg)
    gmax = jnp.max(gl, axis=-1, keepdims=True)
    gsel = jnp.min(jnp.where(gl == gmax, lane, LANES), axis=-1, keepdims=True)
    group_gate = 1.0 / jnp.sum(jnp.where(gmask, jnp.exp(gl - gmax), 0.0), axis=-1, keepdims=True)
    lo = N_GROUPS + EXPERTS_PER_GROUP * gsel
    emask = (lane >= lo) & (lane < lo + EXPERTS_PER_GROUP)
    el = jnp.where(emask, logits, neg)
    ee = jnp.where(emask, jnp.exp(el - jnp.max(el, axis=-1, keepdims=True)), 0.0)
    prob = ee / jnp.sum(ee, axis=-1, keepdims=True)
    p1 = jnp.max(jnp.where(emask, prob, -1.0), axis=-1, keepdims=True)
    i1 = jnp.min(jnp.where(emask & (prob == p1), lane, LANES), axis=-1, keepdims=True)
    rest = emask & (lane != i1)
    p2 = jnp.max(jnp.where(rest, prob, -1.0), axis=-1, keepdims=True)
    i2 = jnp.min(jnp.where(rest & (prob == p2), lane, LANES), axis=-1, keepdims=True)
    scale = group_gate / (p1 + p2)
    comb = jnp.where(lane == i1, p1 * scale, 0.0) + jnp.where(lane == i2, p2 * scale, 0.0)
    return comb, gsel


def _sorted_rows(tile_rows):
    return tile_rows + N_GROUPS * SEG_ALIGN


def _xattn_route_kernel(x_ref, a_ref, wmix_ref, gain_ref, wq_ref, k_ref, v_ref, wo_ref, gffn_ref, wr_ref, br_ref,
                        x2_ref, xs_ref, cs_ref, info_ref, seg_ref):
    x = x_ref[...] + _dot(a_ref[...].astype(BF16), wmix_ref[...])
    tq, d = x.shape
    hd = d // XA_HEADS
    q = _dot(_rms(x, gain_ref[...]).astype(BF16), wq_ref[...]).astype(BF16)
    outs = []
    for h in range(XA_HEADS):
        sl = slice(h * hd, (h + 1) * hd)
        s = _dot_nt(q[:, sl], k_ref[:, sl]) * (hd ** -0.5)
        p = jnp.exp(s - jnp.max(s, axis=-1, keepdims=True))
        p = p / jnp.sum(p, axis=-1, keepdims=True)
        outs.append(_dot(p.astype(BF16), v_ref[:, sl]).astype(BF16))
    x2 = x + _dot(jnp.concatenate(outs, axis=-1), wo_ref[...])
    x2_ref[...] = x2

    xn = _rms(x2, gffn_ref[...])
    xn_hi = xn.astype(BF16)
    xn_lo = (xn - xn_hi.astype(F32)).astype(BF16)
    hi_part = _dot(xn_hi, wr_ref[...])
    logits = hi_part[:, 0:LANES] + hi_part[:, LANES:2 * LANES] + _dot(xn_lo, wr_ref[:, 0:LANES])
    comb, gsel = _route(logits + br_ref[...])

    lane = _iota2((tq, LANES), 1)
    onehot = (lane == gsel).astype(F32)
    earlier = (_iota2((tq, tq), 0) > _iota2((tq, tq), 1)).astype(BF16)
    before = _dot(earlier, onehot.astype(BF16))
    rank = jnp.sum(onehot * before, axis=-1, keepdims=True)
    counts = before[tq - 1:tq, :] + onehot[tq - 1:tq, :]
    padded = jnp.floor((counts + (SEG_ALIGN - 1)) * (1.0 / SEG_ALIGN)) * SEG_ALIGN
    lane1 = _iota2((1, LANES), 1)
    starts = jnp.zeros((1, LANES), F32)
    for g in range(N_GROUPS - 1):
        starts = starts + jnp.where(lane1 > g, padded[:, g:g + 1], 0.0)
    dest = jnp.sum(onehot * starts, axis=-1, keepdims=True) + rank
    rows = xs_ref.shape[0]
    place = (_iota2((tq, rows), 1).astype(F32) == dest).astype(BF16)
    xs_ref[...] = _dot_tn(place, xn_hi).astype(BF16)
    comb_hi = comb.astype(BF16)
    comb_lo = (comb - comb_hi.astype(F32)).astype(BF16)
    cs_ref[...] = _dot_tn(place, comb_hi) + _dot_tn(place, comb_lo)
    info_ref[...] = jnp.where(lane == 0, dest, 0.0)
    seg_ref[...] = jnp.concatenate([starts, padded, jnp.zeros((6, LANES), F32)], axis=0)


def _xattn_route(x, act, w_mix, gain, wq, k, v, wo, gain_ffn, w_router, b_router, tq=512):
    b, t, d = x.shape
    m = k.shape[1]
    tq = min(tq, t)
    tiles_per_row = t // tq
    n_tiles = b * tiles_per_row
    rows = _sorted_rows(tq)

    def tile(i, j):
        return (i * tiles_per_row + j, 0, 0)

    return pl.pallas_call(
        _xattn_route_kernel,
        grid=(b, tiles_per_row),
        in_specs=[pl.BlockSpec((None, tq, d), lambda i, j: (i, j, 0)),
                  pl.BlockSpec((None, tq, act.shape[2]), lambda i, j: (i, j, 0)),
                  pl.BlockSpec(w_mix.shape, lambda i, j: (0, 0)),
                  pl.BlockSpec((1, d), lambda i, j: (0, 0)),
                  pl.BlockSpec((d, d), lambda i, j: (0, 0)),
                  pl.BlockSpec((None, m, d), lambda i, j: (i, 0, 0)),
                  pl.BlockSpec((None, m, d), lambda i, j: (i, 0, 0)),
                  pl.BlockSpec((d, d), lambda i, j: (0, 0)),
                  pl.BlockSpec((1, d), lambda i, j: (0, 0)),
                  pl.BlockSpec((d, 2 * LANES), lambda i, j: (0, 0)),
                  pl.BlockSpec((1, LANES), lambda i, j: (0, 0))],
        out_specs=[pl.BlockSpec((None, tq, d), lambda i, j: (i, j, 0)),
                   pl.BlockSpec((None, rows, d), tile),
                   pl.BlockSpec((None, rows, LANES), tile),
                   pl.BlockSpec((None, tq, LANES), lambda i, j: (i, j, 0)),
                   pl.BlockSpec((None, 8, LANES), tile)],
        out_shape=[jax.ShapeDtypeStruct((b, t, d), F32),
                   jax.ShapeDtypeStruct((n_tiles, rows, d), BF16),
                   jax.ShapeDtypeStruct((n_tiles, rows, LANES), F32),
                   jax.ShapeDtypeStruct((b, t, LANES), F32),
                   jax.ShapeDtypeStruct((n_tiles, 8, LANES), F32)],
        compiler_params=_cparams(("parallel", "parallel")),
        name="xattn_route",
    )(x, act, w_mix, gain.reshape(1, d), wq, k, v, wo, gain_ffn.reshape(1, d), w_router, b_router)


def _moe_ffn_kernel(seg_ref, xs_ref, cs_ref, w1_ref, w3_ref, w2_ref, ys_ref, xbuf_ref, cbuf_ref, ybuf_ref):
    win = pl.program_id(0)
    grp = pl.program_id(1)
    n_win = xs_ref.shape[0]

    @pl.when((win == 0) & (grp == 0))
    def _():
        xbuf_ref[...] = jnp.zeros_like(xbuf_ref)
        cbuf_ref[...] = jnp.zeros_like(cbuf_ref)

    @pl.when(grp == 0)
    def _():
        ys_ref[...] = jnp.zeros_like(ys_ref)

    def segment(t):
        base = (win * n_win + t) * 8
        return seg_ref[base + grp], seg_ref[base + N_GROUPS + grp]

    def copy_segments(copy_piece):
        packed = jnp.int32(0)
        for t in range(n_win):
            start, size = segment(t)

            def body(j, carry, t=t, start=start, packed=packed):
                copy_piece(t, pl.multiple_of(start + j * SEG_ALIGN, SEG_ALIGN),
                           pl.multiple_of(packed + j * SEG_ALIGN, SEG_ALIGN))
                return carry

            lax.fori_loop(0, size // SEG_ALIGN, body, 0)
            packed = packed + size
        return packed

    def gather(t, src, dst):
        xbuf_ref[pl.ds(dst, SEG_ALIGN), :] = xs_ref[t, pl.ds(src, SEG_ALIGN), :]
        cbuf_ref[pl.ds(dst, SEG_ALIGN), :] = cs_ref[t, pl.ds(src, SEG_ALIGN), :]

    total = copy_segments(gather)

    def chunk(c, carry):
        r0 = pl.multiple_of(c * MOE_CHUNK, MOE_CHUNK)
        xc = xbuf_ref[pl.ds(r0, MOE_CHUNK), :]
        cb = cbuf_ref[pl.ds(r0, MOE_CHUNK), :]
        lane = _iota2(cb.shape, 1)
        hids = []
        for e in range(EXPERTS_PER_GROUP):
            weight = jnp.sum(jnp.where(lane == N_GROUPS + EXPERTS_PER_GROUP * grp + e, cb, 0.0),
                             axis=-1, keepdims=True)
            hids.append((_silu(_dot(xc, w1_ref[e])) * _dot(xc, w3_ref[e]) * weight).astype(BF16))
        ybuf_ref[pl.ds(r0, MOE_CHUNK), :] = _dot(jnp.concatenate(hids, axis=-1), w2_ref[...]).astype(ybuf_ref.dtype)
        return carry

    lax.fori_loop(0, (total + (MOE_CHUNK - 1)) // MOE_CHUNK, chunk, 0)

    def scatter(t, dst, src):
        ys_ref[t, pl.ds(dst, SEG_ALIGN), :] = ybuf_ref[pl.ds(src, SEG_ALIGN), :]

    copy_segments(scatter)


def _moe_ffn(seg, xs, cs, w1, w3, w2g):
    n_tiles, rows, d = xs.shape
    ff = w1.shape[2]
    n_win = next(w for w in (MOE_WINDOW, 2, 1) if n_tiles % w == 0)
    buf_rows = -(-(n_win * rows) // MOE_CHUNK) * MOE_CHUNK
    grid_spec = pltpu.PrefetchScalarGridSpec(
        num_scalar_prefetch=1,
        grid=(n_tiles // n_win, N_GROUPS),
        in_specs=[pl.BlockSpec((n_win, rows, d), lambda w, g, seg: (w, 0, 0)),
                  pl.BlockSpec((n_win, rows, LANES), lambda w, g, seg: (w, 0, 0)),
                  pl.BlockSpec((EXPERTS_PER_GROUP, d, ff), lambda w, g, seg: (g, 0, 0)),
                  pl.BlockSpec((EXPERTS_PER_GROUP, d, ff), lambda w, g, seg: (g, 0, 0)),
                  pl.BlockSpec((None, EXPERTS_PER_GROUP * ff, d), lambda w, g, seg: (g, 0, 0))],
        out_specs=pl.BlockSpec((n_win, rows, d), lambda w, g, seg: (w, 0, 0)),
        scratch_shapes=[pltpu.VMEM((buf_rows, d), BF16),
                        pltpu.VMEM((buf_rows, LANES), F32),
                        pltpu.VMEM((buf_rows, d), BF16)])
    return pl.pallas_call(
        _moe_ffn_kernel,
        grid_spec=grid_spec,
        out_shape=jax.ShapeDtypeStruct((n_tiles, rows, d), BF16),
        compiler_params=_cparams(("arbitrary", "arbitrary")),
        name="moe_ffn",
    )(seg, xs, cs, w1, w3, w2g)


def _moe_unsort_kernel(final_norm, x_ref, info_ref, ys_ref, gfin_ref, o_ref):
    info = info_ref[...]
    dest = jnp.sum(jnp.where(_iota2(info.shape, 1) == 0, info, 0.0), axis=-1, keepdims=True)
    rows = ys_ref.shape[0]
    place = (_iota2((info.shape[0], rows), 1).astype(F32) == dest).astype(BF16)
    y = x_ref[...] + _dot(place, ys_ref[...])
    if final_norm:
        y = _rms(y, gfin_ref[...])
    o_ref[...] = y


def _moe_unsort(x, info, ys, gain_final, final_norm):
    b, t, d = x.shape
    n_tiles, rows, _ = ys.shape
    tiles_per_row = n_tiles // b
    tq = t // tiles_per_row
    return pl.pallas_call(
        functools.partial(_moe_unsort_kernel, final_norm),
        grid=(b, tiles_per_row),
        in_specs=[pl.BlockSpec((None, tq, d), lambda i, j: (i, j, 0)),
                  pl.BlockSpec((None, tq, LANES), lambda i, j: (i, j, 0)),
                  pl.BlockSpec((None, rows, d), lambda i, j: (i * tiles_per_row + j, 0, 0)),
                  pl.BlockSpec((1, d), lambda i, j: (0, 0))],
        out_specs=pl.BlockSpec((None, tq, d), lambda i, j: (i, j, 0)),
        out_shape=jax.ShapeDtypeStruct((b, t, d), F32),
        compiler_params=_cparams(("parallel", "parallel")),
        name="moe_unsort",
    )(x, info, ys, gain_final.reshape(1, d))


def _pad_cols(w, n):
    return jnp.pad(w, ((0, 0), (0, n - w.shape[1])))


def _pad_rows(w, n):
    return jnp.pad(w, ((0, n - w.shape[0]), (0, 0)))


def _row(v, n=None):
    v = v.reshape(1, -1).astype(F32)
    return v if n is None else _pad_cols(v, n)


def kernel(x, mem, norm_mix, norm_xattn, norm_mem, norm_ffn, norm_final, ab_w_in, ab_conv, ab_a_log, ab_dt_bias, ab_onorm, ab_shift_mu, ab_w0, ab_w2, ab_a0, ab_a2, ab_g2, ab_k_k, ab_k_a, ab_r_k, ab_ln_w, ab_ln_b, ab_w_out, c_w_in, c_lb_param, c_onorm, c_w_out, xa_wq, xa_wkv, xa_wo, moe_w_group, moe_b_group, moe_w_expert, moe_b_expert, moe_w1, moe_w3, moe_w2):
    bsz, seq, d = x.shape
    n_mem = mem.shape[1]
    depth = norm_mix.shape[0]
    a_width = d // 2
    b_width = d // 2
    a_heads = a_width // A_HEAD_DIM
    a_cols = 4 * a_width + 2 * a_heads
    n_tok = bsz * seq

    lb_p = jax.nn.softmax(c_lb_param.astype(F32), axis=0)
    lower_bounds = jnp.cumsum(lb_p, axis=0) - lb_p[0]

    x2 = x.reshape(n_tok, d)
    mem2 = mem.reshape(bsz * n_mem, d)
    for layer in range(depth):
        if layer % 2 == 0:
            e = layer // 2
            w_in = ab_w_in[e]
            w_a = w_in[:, :4 * a_width].astype(BF16)
            w_db = _pad_cols(w_in[:, 4 * a_width:a_cols], LANES).astype(BF16)
            wb = w_in[:, a_cols:]
            o1, o2 = 3 * b_width, 3 * b_width + B_DECAY_LORA
            o3 = o2 + B_ICLR_LORA
            w_b = jnp.concatenate([wb[:, :o1], _pad_cols(wb[:, o1:o2], LANES), _pad_cols(wb[:, o2:o3], LANES),
                                   wb[:, o3:]], axis=1).astype(BF16)
            mu = ab_shift_mu[e].reshape(1, -1)
            mu_p = jnp.concatenate([mu[:, :o1], _pad_cols(mu[:, o1:o2], LANES), _pad_cols(mu[:, o2:o3], LANES),
                                    mu[:, o3:]], axis=1)
            pa, pdb, pb = _norm_proj(x2, norm_mix[layer], [w_a, w_db, w_b])
            head_id = jnp.arange(b_width) // B_HEAD_DIM
            bd = (head_id[:, None] == head_id[None, :]).astype(F32)
            gdn_params = [ab_conv[e], _row(ab_a_log[e], LANES), _row(ab_dt_bias[e], LANES), _row(ab_onorm[e])]
            rwkv_params = [mu_p, _row(ab_w0[e]), _pad_rows(ab_w2[e], LANES), _row(ab_a0[e]),
                           _pad_rows(ab_a2[e], LANES), ab_g2[e], _row(ab_k_k[e]), _row(ab_k_a[e]),
                           _row(ab_r_k[e]), _row(ab_ln_w[e]), _row(ab_ln_b[e]), bd]
            mix = _mixer_ab(pa.reshape(bsz, seq, -1), pdb.reshape(bsz, seq, LANES), pb.reshape(bsz, seq, -1),
                            gdn_params, rwkv_params)
            w_mix = ab_w_out[e].astype(BF16)
        else:
            o = layer // 2
            (proj,) = _norm_proj(x2, norm_mix[layer], [c_w_in[o].astype(BF16)])
            lb = lower_bounds[layer].reshape(1, -1)
            mix = _hgrn(proj.reshape(bsz, seq, -1), jnp.log(lb), jnp.log1p(-lb), _row(c_onorm[o]))
            w_mix = c_w_out[o].astype(BF16)

        wkv = xa_wkv[layer].astype(BF16)
        k_mem, v_mem = _norm_proj(mem2, norm_mem[layer], [wkv[:, :d], wkv[:, d:]], out_dtype=BF16)
        w_r = _pad_cols(jnp.concatenate([moe_w_group[layer], moe_w_expert[layer]], axis=1), LANES).astype(F32)
        w_r_hi = w_r.astype(BF16)
        w_r_lo = (w_r - w_r_hi.astype(F32)).astype(BF16)
        b_router = _row(jnp.concatenate([moe_b_group[layer], moe_b_expert[layer]]), LANES)
        x3, xs, cs, info, seg = _xattn_route(
            x2.reshape(bsz, seq, d), mix, w_mix, norm_xattn[layer], xa_wq[layer].astype(BF16),
            k_mem.reshape(bsz, n_mem, d), v_mem.reshape(bsz, n_mem, d), xa_wo[layer].astype(BF16),
            norm_ffn[layer], jnp.concatenate([w_r_hi, w_r_lo], axis=1), b_router)
        seg_i = jnp.concatenate([seg[:, 0, :N_GROUPS], seg[:, 1, :N_GROUPS]], axis=1).astype(jnp.int32).reshape(-1)
        ff = moe_w2.shape[2]
        ys = _moe_ffn(seg_i, xs, cs, moe_w1[layer].astype(BF16), moe_w3[layer].astype(BF16),
                      moe_w2[layer].reshape(N_GROUPS, EXPERTS_PER_GROUP * ff, d).astype(BF16))
        x2 = _moe_unsort(x3, info, ys, norm_final, final_norm=(layer == depth - 1)).reshape(n_tok, d)
    return x2.reshape(bsz, seq, d)
```

```python
import functools

import jax
import jax.numpy as jnp
import numpy as np
from jax import lax
from jax.experimental import pallas as pl
from jax.experimental.pallas import tpu as pltpu

F32 = jnp.float32
BF16 = jnp.bfloat16
HI = lax.Precision.HIGHEST
ONE_PASS = "bf16"

PREC_CUM = HI
PREC_INV = ONE_PASS
PREC_MIX = ONE_PASS
PREC_AUX = ONE_PASS

CHUNK = 64
HGRN_LEVELS = 6
NORM_EPS = 1e-6
GROUP_NORM_EPS = 64e-5
CONV_WIDTH = 4

A_HEAD_DIM = 128
B_HEAD_DIM = 64
B_DECAY_LORA = 64
B_ICLR_LORA = 64
B_GATE_LORA = 128
C_KEY_DIM = 128
XA_HEADS = 4
N_GROUPS = 4
EXPERTS_PER_GROUP = 4
N_EXPERTS = N_GROUPS * EXPERTS_PER_GROUP

LANES = 128
SUBLANES = 8
VMEM_LIMIT = 56 * 1024 * 1024
MIXER_BATCH = 4
SEG_ALIGN = 16
MOE_WINDOW = 4
MOE_CHUNK = 256


def _cparams(sem):
    return pltpu.CompilerParams(dimension_semantics=sem, vmem_limit_bytes=VMEM_LIMIT)


def _dot_dims(a, b, dims, prec):
    if prec == ONE_PASS:
        a, b, prec = a.astype(BF16), b.astype(BF16), None
    return lax.dot_general(a, b, (dims, ((), ())), preferred_element_type=F32, precision=prec)


def _dot(a, b, prec=None):
    return _dot_dims(a, b, ((1,), (0,)), prec)


def _dot_nt(a, b, prec=None):
    return _dot_dims(a, b, ((1,), (1,)), prec)


def _dot_tn(a, b, prec=None):
    return _dot_dims(a, b, ((0,), (0,)), prec)


def _rms(x, gain):
    return x * lax.rsqrt(jnp.mean(x * x, axis=-1, keepdims=True) + NORM_EPS) * gain


def _sigmoid(x):
    return 1.0 / (1.0 + jnp.exp(-x))


def _silu(x):
    return x * _sigmoid(x)


def _softplus(x):
    return jnp.maximum(x, 0.0) + jnp.log(1.0 + jnp.exp(-jnp.abs(x)))


def _iota2(shape, dim):
    return lax.broadcasted_iota(jnp.int32, shape, dim)


def _interleave(stage_generators):
    live = list(stage_generators)
    while live:
        for gen in list(live):
            try:
                next(gen)
            except StopIteration:
                live.remove(gen)


def _tri_inverse_stages(mats):
    n = mats[0].shape[0]
    row = _iota2((n, n), 0)
    col = _iota2((n, n), 1)
    same16 = (row >> 4) == (col >> 4)
    same32 = (row >> 5) == (col >> 5)
    eye = (row == col).astype(F32)
    ads = [jnp.where(same16, a, 0.0) for a in mats]
    xs = [eye - ad for ad in ads]
    ps = [_dot(ad, ad, PREC_INV) for ad in ads]
    yield
    for stage in range(3):
        xs = [x + _dot(x, p, PREC_INV) for x, p in zip(xs, ps)]
        if stage < 2:
            ps = [_dot(p, p, PREC_INV) for p in ps]
        yield
    for lower in ([jnp.where(same32 & jnp.logical_not(same16), a, 0.0) for a in mats],
                  [jnp.where(same32, 0.0, a) for a in mats]):
        ts = [_dot(l, x, PREC_INV) for l, x in zip(lower, xs)]
        yield
        xs = [x - _dot(x, t, PREC_INV) for x, t in zip(xs, ts)]
        yield
    return xs


def _norm_proj_kernel(n_w, x_ref, g_ref, *refs):
    w_refs, o_refs = refs[:n_w], refs[n_w:]
    xn = _rms(x_ref[...], g_ref[...]).astype(BF16)
    for w_ref, o_ref in zip(w_refs, o_refs):
        n = w_ref.shape[1]
        for c0 in range(0, n, 512):
            c1 = min(n, c0 + 512)
            o_ref[:, c0:c1] = _dot(xn, w_ref[:, c0:c1]).astype(o_ref.dtype)


def _norm_proj(x, gain, ws, out_dtype=F32, tm=512):
    n, d = x.shape
    tm = min(tm, n)
    return pl.pallas_call(
        functools.partial(_norm_proj_kernel, len(ws)),
        grid=(n // tm,),
        in_specs=[pl.BlockSpec((tm, d), lambda i: (i, 0)),
                  pl.BlockSpec((1, d), lambda i: (0, 0))]
                 + [pl.BlockSpec(w.shape, lambda i: (0, 0)) for w in ws],
        out_specs=[pl.BlockSpec((tm, w.shape[1]), lambda i: (i, 0)) for w in ws],
        out_shape=[jax.ShapeDtypeStruct((n, w.shape[1]), out_dtype) for w in ws],
        compiler_params=_cparams(("parallel",)),
        name="norm_proj",
    )(x, gain.reshape(1, d), *ws)


def _gdn_stages(nb, pa_ref, pdb_ref, convw_ref, alog_ref, dtb_ref, onorm_ref, o_ref, xbuf_ref, state_ref):
    n_heads = state_ref.shape[0] // nb
    hd = A_HEAD_DIM
    width = n_heads * hd
    row = _iota2((CHUNK, CHUNK), 0)
    col = _iota2((CHUNK, CHUNK), 1)
    incl = row >= col
    strict = row > col
    tril = incl.astype(F32)
    triu = (row <= col).astype(F32)

    qs, ks, vs, betas, gs, g_lasts, decays = [], [], [], [], [], [], []
    for bb in range(nb):
        x = pa_ref[bb, :, 0:3 * width]
        xbuf_ref[bb, 8:8 + CHUNK, :] = x
        acc = convw_ref[0:1, :] * xbuf_ref[bb, 5:5 + CHUNK, :]
        for j in range(1, CONV_WIDTH):
            acc = acc + convw_ref[j:j + 1, :] * xbuf_ref[bb, 5 + j:5 + j + CHUNK, :]
        xbuf_ref[bb, 0:8, :] = x[CHUNK - 8:CHUNK, :]
        qkv = _silu(acc)
        pdb = pdb_ref[bb]
        log_alpha = -jnp.exp(alog_ref[...]) * _softplus(pdb + dtb_ref[...])
        beta_all = _sigmoid(pdb)
        g_all = _dot(tril, log_alpha, PREC_CUM)
        g_all_t = _dot_tn(log_alpha, triu, PREC_CUM)
        for h in range(n_heads):
            q = qkv[:, h * hd:(h + 1) * hd]
            k = qkv[:, width + h * hd: width + (h + 1) * hd]
            qs.append(q * lax.rsqrt(jnp.sum(q * q, axis=-1, keepdims=True) + 1e-6) * (hd ** -0.5))
            ks.append(k * lax.rsqrt(jnp.sum(k * k, axis=-1, keepdims=True) + 1e-6))
            vs.append(qkv[:, 2 * width + h * hd: 2 * width + (h + 1) * hd])
            betas.append(beta_all[:, n_heads + h:n_heads + h + 1])
            g = g_all[:, h:h + 1]
            gs.append(g)
            g_lasts.append(g_all[CHUNK - 1:CHUNK, h:h + 1])
            decays.append(jnp.where(incl, jnp.exp(jnp.where(incl, g - g_all_t[h:h + 1, :], 0.0)), 0.0))
    yield
    idx = range(nb * n_heads)
    k_betas = [ks[i] * betas[i] for i in idx]
    egs = [jnp.exp(g) for g in gs]
    a_lows = [jnp.where(strict, _dot_nt(k_betas[i], ks[i], PREC_MIX) * decays[i], 0.0) for i in idx]
    qks = [_dot_nt(qs[i], ks[i], PREC_MIX) * decays[i] for i in idx]
    states = [state_ref[i] for i in idx]
    o_inters = [_dot(qs[i] * egs[i], states[i], PREC_MIX) for i in idx]
    yield
    t_invs = yield from _tri_inverse_stages(a_lows)
    us = [_dot(t_invs[i], vs[i] * betas[i], PREC_MIX) for i in idx]
    ws = [_dot(t_invs[i], k_betas[i] * egs[i], PREC_MIX) for i in idx]
    yield
    v_news = [us[i] - _dot(ws[i], states[i], PREC_MIX) for i in idx]
    yield
    os_ = [o_inters[i] + _dot(qks[i], v_news[i], PREC_MIX) for i in idx]
    for i in idx:
        state_ref[i] = (states[i] * jnp.exp(g_lasts[i])
                        + _dot_tn(ks[i] * jnp.exp(g_lasts[i] - gs[i]), v_news[i], PREC_MIX))
    yield
    for i in idx:
        bb, h = divmod(i, n_heads)
        gate = pa_ref[bb, :, 3 * width + h * hd: 3 * width + (h + 1) * hd]
        o_ref[bb, :, h * hd:(h + 1) * hd] = _rms(os_[i], onorm_ref[...]) * _silu(gate)


def _rwkv_stages(nb, col0, pb_ref, mu_ref, w0_ref, w2_ref, a0_ref, a2_ref, g2_ref, kk_ref, ka_ref, rk_ref,
                 lnw_ref, lnb_ref, bd_ref, o_ref, carry_ref, state_ref, obuf_ref):
    n_heads = state_ref.shape[0] // nb
    hd = B_HEAD_DIM
    width = n_heads * hd
    row = _iota2((CHUNK, CHUNK), 0)
    col = _iota2((CHUNK, CHUNK), 1)
    strict = row > col
    tril = (row >= col).astype(F32)
    incl2 = _iota2((CHUNK, 2 * CHUNK), 0) >= (_iota2((CHUNK, 2 * CHUNK), 1) & (CHUNK - 1))
    bd = bd_ref[...]

    pss = []
    for bb in range(nb):
        p = pb_ref[bb]
        prev = jnp.where(_iota2(p.shape, 0) == 0, carry_ref[bb, 0:1, :], pltpu.roll(p, 1, 0))
        carry_ref[bb, 0:1, :] = p[CHUNK - 1:CHUNK, :]
        pss.append(p + mu_ref[...] * (prev - p))
    rs = [ps[:, 0:width] for ps in pss]
    kraw = [ps[:, width:2 * width] for ps in pss]
    vfull = [ps[:, 2 * width:3 * width] for ps in pss]
    w_raws = [w0_ref[...] + _dot(jnp.tanh(ps[:, 3 * width:3 * width + LANES]), w2_ref[...], PREC_AUX) for ps in pss]
    iclrs = [_sigmoid(a0_ref[...] + _dot(ps[:, 3 * width + LANES:3 * width + 2 * LANES], a2_ref[...], PREC_AUX))
             for ps in pss]
    gates = [_dot(_sigmoid(ps[:, 3 * width + 2 * LANES:3 * width + 3 * LANES]), g2_ref[...], PREC_AUX) for ps in pss]
    kks = [k * kk_ref[...] for k in kraw]
    kk_ss = [_dot(kk * kk, bd, PREC_AUX) for kk in kks]
    yield
    log_ws = [-jnp.exp(-_softplus(-w_raw) - 0.5) for w_raw in w_raws]
    g_cums = [_dot(tril, log_w, PREC_CUM) for log_w in log_ws]
    yield
    lhss, rhss, bl_kls, p_cs, vhs, k2s = [], [], [], [], [], []
    for bb in range(nb):
        kk = kks[bb] * lax.rsqrt(kk_ss[bb] + 1e-6)
        k2 = kraw[bb] * (1.0 + (iclrs[bb] - 1.0) * ka_ref[...])
        k2s.append(k2)
        bvec = kk * iclrs[bb]
        g = g_cums[bb]
        g_last = g[CHUNK - 1:CHUNK, :]
        eng = jnp.exp(-g)
        egl = jnp.exp(g_last - g)
        r_t = rs[bb] * jnp.exp(g)
        kk_t = kk * jnp.exp(g - log_ws[bb])
        b_h, k_h = bvec * eng, k2 * eng
        b_l, k_l = bvec * egl, k2 * egl
        p_c = jnp.exp(g_last)
        for h in range(n_heads):
            sl = slice(h * hd, (h + 1) * hd)
            lhss.append(jnp.concatenate([kk_t[:, sl], r_t[:, sl]], axis=0))
            rhss.append(jnp.concatenate([b_h[:, sl], k_h[:, sl]], axis=0))
            bl_kls.append(jnp.concatenate([b_l[:, sl], k_l[:, sl]], axis=0))
            p_cs.append(p_c[:, sl])
            vhs.append(vfull[bb][:, sl])
    idx = range(nb * n_heads)
    ms = [_dot_nt(lhss[i], rhss[i], PREC_MIX) for i in idx]
    states = [state_ref[i] for i in idx]
    lss = [_dot_nt(lhss[i], states[i], PREC_MIX) for i in idx]
    yield
    a_bbs = [jnp.where(strict, m[0:CHUNK, 0:CHUNK], 0.0) for m in ms]
    a_bks = [jnp.where(strict, m[0:CHUNK, CHUNK:2 * CHUNK], 0.0) for m in ms]
    a_rs = [jnp.where(incl2, m[CHUNK:2 * CHUNK, :], 0.0) for m in ms]
    rhs_u = [lss[i][0:CHUNK] + _dot(a_bks[i], vhs[i], PREC_MIX) for i in idx]
    t_invs = yield from _tri_inverse_stages(a_bbs)
    uvs = [jnp.concatenate([-_dot(t_invs[i], rhs_u[i], PREC_MIX), vhs[i]], axis=0) for i in idx]
    yield
    for i in idx:
        bb, h = divmod(i, n_heads)
        obuf_ref[bb, :, h * hd:(h + 1) * hd] = lss[i][CHUNK:2 * CHUNK] + _dot(a_rs[i], uvs[i], PREC_MIX)
        state_ref[i] = states[i] * p_cs[i] + _dot_tn(uvs[i], bl_kls[i], PREC_MIX)
    yield
    inv_n = 1.0 / hd
    os_ = [obuf_ref[bb] for bb in range(nb)]
    means = [_dot(o, bd, PREC_AUX) * inv_n for o in os_]
    bonus = [_dot(rs[bb] * k2s[bb] * rk_ref[...], bd, PREC_AUX) * vfull[bb] for bb in range(nb)]
    yield
    cens = [o - mean for o, mean in zip(os_, means)]
    vars_ = [_dot(cen * cen, bd, PREC_AUX) * inv_n for cen in cens]
    yield
    for bb in range(nb):
        ln = cens[bb] * lax.rsqrt(vars_[bb] + GROUP_NORM_EPS) * lnw_ref[...] + lnb_ref[...]
        o_ref[bb, :, col0:col0 + width] = (ln + bonus[bb]) * gates[bb]


def _mixer_ab_kernel(n_gdn, n_rwkv, pa_ref, pdb_ref, pb_ref, *refs):
    gdn_params = refs[:n_gdn]
    rwkv_params = refs[n_gdn:n_gdn + n_rwkv]
    o_ref, xbuf_ref, gstate_ref, carry_ref, rstate_ref, obuf_ref = refs[n_gdn + n_rwkv:]
    nb = pa_ref.shape[0]

    @pl.when(pl.program_id(1) == 0)
    def _():
        xbuf_ref[:, 0:8, :] = jnp.zeros((nb, 8, xbuf_ref.shape[2]), F32)
        gstate_ref[...] = jnp.zeros_like(gstate_ref)
        carry_ref[...] = jnp.zeros_like(carry_ref)
        rstate_ref[...] = jnp.zeros_like(rstate_ref)

    a_width = pa_ref.shape[2] // 4
    _interleave([
        _gdn_stages(nb, pa_ref, pdb_ref, *gdn_params, o_ref, xbuf_ref, gstate_ref),
        _rwkv_stages(nb, a_width, pb_ref, *rwkv_params, o_ref, carry_ref, rstate_ref, obuf_ref),
    ])


def _mixer_ab(pa, pdb, pb, gdn_params, rwkv_params):
    b, t, a_cols = pa.shape
    b_cols = pb.shape[2]
    a_width = a_cols // 4
    b_width = (b_cols - 3 * LANES) // 3
    a_heads = a_width // A_HEAD_DIM
    b_heads = b_width // B_HEAD_DIM
    nb = MIXER_BATCH if b % MIXER_BATCH == 0 else 1
    params = list(gdn_params) + list(rwkv_params)
    return pl.pallas_call(
        functools.partial(_mixer_ab_kernel, len(gdn_params), len(rwkv_params)),
        grid=(b // nb, t // CHUNK),
        in_specs=[pl.BlockSpec((nb, CHUNK, a_cols), lambda i, c: (i, c, 0)),
                  pl.BlockSpec((nb, CHUNK, LANES), lambda i, c: (i, c, 0)),
                  pl.BlockSpec((nb, CHUNK, b_cols), lambda i, c: (i, c, 0))]
                 + [pl.BlockSpec(a.shape, lambda i, c: (0, 0)) for a in params],
        out_specs=pl.BlockSpec((nb, CHUNK, a_width + b_width), lambda i, c: (i, c, 0)),
        out_shape=jax.ShapeDtypeStruct((b, t, a_width + b_width), F32),
        scratch_shapes=[pltpu.VMEM((nb, 8 + CHUNK, 3 * a_width), F32),
                        pltpu.VMEM((nb * a_heads, A_HEAD_DIM, A_HEAD_DIM), F32),
                        pltpu.VMEM((nb, 8, b_cols), F32),
                        pltpu.VMEM((nb * b_heads, B_HEAD_DIM, B_HEAD_DIM), F32),
                        pltpu.VMEM((nb, CHUNK, b_width), F32)],
        compiler_params=_cparams(("parallel", "arbitrary")),
        name="mixer_ab",
    )(pa, pdb, pb, *params)


def _hgrn_tables():
    i = np.arange(CHUNK)[:, None]
    t = np.arange(CHUNK)[None, :]
    blocks, pair = [], []
    for ls in range(HGRN_LEVELS - 1, -1, -1):
        s = 1 << ls
        mid = (i >> (ls + 1) << (ls + 1)) + s
        q_rows, k_rows = _hgrn_level_rows(ls)
        blocks.append(((i >= mid) & (t >= mid) & (t <= i))[q_rows])
        blocks.append(((i < mid) & (t > i) & (t <= mid - 1))[k_rows])
        pair.append((((i >> ls) ^ (t >> ls)) == 1) & (i > t))
    blocks.append(t <= i)
    blocks.append(t > i)
    sums = np.concatenate(blocks, axis=0).astype(np.float32)
    return np.concatenate([sums] * 3, axis=1), np.stack(pair).astype(np.float32)


def _hgrn_level_rows(ls):
    rows = np.arange(CHUNK)
    if (1 << ls) < SUBLANES:
        return rows, rows
    upper = ((rows >> ls) & 1) == 1
    return rows[upper], rows[~upper]


def _hgrn_kernel(p_ref, loglb_ref, log1mlb_ref, onorm_ref, sums_ref, pair_ref, o_ref, state_ref):
    nb = p_ref.shape[0]
    n_heads = state_ref.shape[0] // nb
    hd = C_KEY_DIM
    kw = n_heads * hd

    @pl.when(pl.program_id(1) == 0)
    def _():
        state_ref[...] = jnp.zeros_like(state_ref)

    def take_rows(a, rows):
        runs, start = [], 0
        for n in range(1, len(rows) + 1):
            if n == len(rows) or rows[n] != rows[n - 1] + 1:
                runs.append(a[int(rows[start]):int(rows[n - 1]) + 1])
                start = n
        return runs[0] if len(runs) == 1 else jnp.concatenate(runs, axis=0)

    def spread_rows(a, rows):
        if len(rows) == CHUNK:
            return a
        pieces, pos, taken = [], 0, 0
        while pos < CHUNK:
            run = 1
            inside = pos in rows
            while pos + run < CHUNK and ((pos + run) in rows) == inside:
                run += 1
            if inside:
                pieces.append(a[taken:taken + run])
                taken += run
            else:
                pieces.append(jnp.zeros((run, a.shape[1]), a.dtype))
            pos += run
        return jnp.concatenate(pieces, axis=0)

    qs, ks, exps = [], [], []
    for bb in range(nb):
        qs.append(_silu(p_ref[bb, :, 0:kw]))
        f = p_ref[bb, :, kw:2 * kw]
        a1 = jnp.broadcast_to(loglb_ref[...], f.shape)
        b1 = log1mlb_ref[...] - _softplus(-f)
        log_f = jnp.maximum(a1, b1) + jnp.log(1.0 + jnp.exp(-jnp.abs(a1 - b1)))
        ks.append(1.0 - jnp.exp(log_f))
        f1 = log_f.astype(BF16)
        f2 = (log_f - f1.astype(F32)).astype(BF16)
        f3 = (log_f - f1.astype(F32) - f2.astype(F32)).astype(BF16)
        exps.append(_dot(sums_ref[...], jnp.concatenate([f1, f2, f3], axis=0)))

    idx = [(bb, h) for bb in range(nb) for h in range(n_heads)]
    states = [state_ref[bb * n_heads + h] for bb, h in idx]
    scores = [None] * len(idx)
    offset = 0
    for lvl in range(HGRN_LEVELS):
        q_rows, k_rows = _hgrn_level_rows(HGRN_LEVELS - 1 - lvl)
        q_rows, k_rows = [int(r) for r in q_rows], [int(r) for r in k_rows]
        e_q = [e[offset:offset + len(q_rows)] for e in exps]
        e_k = [e[offset + len(q_rows):offset + len(q_rows) + len(k_rows)] for e in exps]
        offset += len(q_rows) + len(k_rows)
        q_l = [(take_rows(qs[bb], q_rows) * jnp.exp(e_q[bb])).astype(BF16) for bb in range(nb)]
        k_l = [spread_rows(take_rows(ks[bb], k_rows) * jnp.exp(e_k[bb]), k_rows).astype(BF16) for bb in range(nb)]
        mask = pair_ref[lvl]
        for n, (bb, h) in enumerate(idx):
            sl = slice(h * hd, (h + 1) * hd)
            part = spread_rows(_dot_nt(q_l[bb][:, sl], k_l[bb][:, sl]), q_rows) * mask
            scores[n] = part if scores[n] is None else scores[n] + part
    g_cum = [e[offset:offset + CHUNK] for e in exps]
    q_g = [qs[bb] * jnp.exp(g_cum[bb]) for bb in range(nb)]
    k_r = [ks[bb] * jnp.exp(exps[bb][offset + CHUNK:offset + 2 * CHUNK]) for bb in range(nb)]
    diag = [qs[bb] * ks[bb] for bb in range(nb)]
    outs = []
    for n, (bb, h) in enumerate(idx):
        sl = slice(h * hd, (h + 1) * hd)
        v = p_ref[bb, :, 2 * kw + h * hd: 2 * kw + (h + 1) * hd]
        o = _dot_nt(q_g[bb][:, sl], states[n], PREC_MIX) + _dot(scores[n], v, PREC_MIX)
        outs.append(o + jnp.sum(diag[bb][:, sl], axis=-1, keepdims=True) * v)
        p_last = jnp.exp(g_cum[bb][CHUNK - 1:CHUNK, sl])
        state_ref[bb * n_heads + h] = states[n] * p_last + _dot_tn(v, k_r[bb][:, sl], PREC_MIX)
    for n, (bb, h) in enumerate(idx):
        gate = p_ref[bb, :, 3 * kw + h * hd: 3 * kw + (h + 1) * hd]
        o_ref[bb, :, h * hd:(h + 1) * hd] = _rms(outs[n], onorm_ref[...]) * _silu(gate)


def _hgrn(proj, log_lb, log1m_lb, onorm):
    b, t, cols = proj.shape
    kw = cols // 4
    n_heads = kw // C_KEY_DIM
    nb = MIXER_BATCH if b % MIXER_BATCH == 0 else 1
    sums, pair = _hgrn_tables()
    sums = jnp.asarray(sums, BF16)
    pair = jnp.asarray(pair, F32)
    return pl.pallas_call(
        _hgrn_kernel,
        grid=(b // nb, t // CHUNK),
        in_specs=[pl.BlockSpec((nb, CHUNK, cols), lambda i, c: (i, c, 0)),
                  pl.BlockSpec((1, kw), lambda i, c: (0, 0)),
                  pl.BlockSpec((1, kw), lambda i, c: (0, 0)),
                  pl.BlockSpec((1, C_KEY_DIM), lambda i, c: (0, 0)),
                  pl.BlockSpec(sums.shape, lambda i, c: (0, 0)),
                  pl.BlockSpec(pair.shape, lambda i, c: (0, 0, 0))],
        out_specs=pl.BlockSpec((nb, CHUNK, kw), lambda i, c: (i, c, 0)),
        out_shape=jax.ShapeDtypeStruct((b, t, kw), F32),
        scratch_shapes=[pltpu.VMEM((nb * n_heads, C_KEY_DIM, C_KEY_DIM), F32)],
        compiler_params=_cparams(("parallel", "arbitrary")),
        name="hgrn2",
    )(proj, log_lb, log1m_lb, onorm, sums, pair)


def _route(logits):
    lane = _iota2(logits.shape, 1)
    neg = -jnp.inf
    gmask = lane < N_GROUPS
    gl = jnp.where(gmask, logits, neg)
    gmax = jnp.max(gl, axis=-1, keepdims=True)
    gsel = jnp.min(jnp.where(gl == gmax, lane, LANES), axis=-1, keepdims=True)
    group_gate = 1.0 / jnp.sum(jnp.where(gmask, jnp.exp(gl - gmax), 0.0), axis=-1, keepdims=True)
    lo = N_GROUPS + EXPERTS_PER_GROUP * gsel
    emask = (lane >= lo) & (lane < lo + EXPERTS_PER_GROUP)
    el = jnp.where(emask, logits, neg)
    ee = jnp.where(emask, jnp.exp(el - jnp.max(el, axis=-1, keepdims=True)), 0.0)
    prob = ee / jnp.sum(ee, axis=-1, keepdims=True)
    p1 = jnp.max(jnp.where(emask, prob, -1.0), axis=-1, keepdims=True)
    i1 = jnp.min(jnp.where(emask & (prob == p1), lane, LANES), axis=-1, keepdims=True)
    rest = emask & (lane != i1)
    p2 = jnp.max(jnp.where(rest, prob, -1.0), axis=-1, keepdims=True)
    i2 = jnp.min(jnp.where(rest & (prob == p2), lane, LANES), axis=-1, keepdims=True)
    scale = group_gate / (p1 + p2)
    comb = jnp.where(lane == i1, p1 * scale, 0.0) + jnp.where(lane == i2, p2 * scale, 0.0)
    return comb, gsel


def _sorted_rows(tile_rows):
    return tile_rows + N_GROUPS * SEG_ALIGN


def _xattn_route_kernel(x_ref, a_ref, wmix_ref, gain_ref, wq_ref, k_ref, v_ref, wo_ref, gffn_ref, wr_ref, br_ref,
                        x2_ref, xs_ref, cs_ref, info_ref, seg_ref):
    x = x_ref[...] + _dot(a_ref[...].astype(BF16), wmix_ref[...])
    tq, d = x.shape
    hd = d // XA_HEADS
    q = _dot(_rms(x, gain_ref[...]).astype(BF16), wq_ref[...]).astype(BF16)
    outs = []
    for h in range(XA_HEADS):
        sl = slice(h * hd, (h + 1) * hd)
        s = _dot_nt(q[:, sl], k_ref[:, sl]) * (hd ** -0.5)
        p = jnp.exp(s - jnp.max(s, axis=-1, keepdims=True))
        p = p / jnp.sum(p, axis=-1, keepdims=True)
        outs.append(_dot(p.astype(BF16), v_ref[:, sl]).astype(BF16))
    x2 = x + _dot(jnp.concatenate(outs, axis=-1), wo_ref[...])
    x2_ref[...] = x2

    xn = _rms(x2, gffn_ref[...])
    xn_hi = xn.astype(BF16)
    xn_lo = (xn - xn_hi.astype(F32)).astype(BF16)
    hi_part = _dot(xn_hi, wr_ref[...])
    logits = hi_part[:, 0:LANES] + hi_part[:, LANES:2 * LANES] + _dot(xn_lo, wr_ref[:, 0:LANES])
    comb, gsel = _route(logits + br_ref[...])

    lane = _iota2((tq, LANES), 1)
    onehot = (lane == gsel).astype(F32)
    earlier = (_iota2((tq, tq), 0) > _iota2((tq, tq), 1)).astype(BF16)
    before = _dot(earlier, onehot.astype(BF16))
    rank = jnp.sum(onehot * before, axis=-1, keepdims=True)
    counts = before[tq - 1:tq, :] + onehot[tq - 1:tq, :]
    padded = jnp.floor((counts + (SEG_ALIGN - 1)) * (1.0 / SEG_ALIGN)) * SEG_ALIGN
    lane1 = _iota2((1, LANES), 1)
    starts = jnp.zeros((1, LANES), F32)
    for g in range(N_GROUPS - 1):
        starts = starts + jnp.where(lane1 > g, padded[:, g:g + 1], 0.0)
    dest = jnp.sum(onehot * starts, axis=-1, keepdims=True) + rank
    rows = xs_ref.shape[0]
    place = (_iota2((tq, rows), 1).astype(F32) == dest).astype(BF16)
    xs_ref[...] = _dot_tn(place, xn_hi).astype(BF16)
    comb_hi = comb.astype(BF16)
    comb_lo = (comb - comb_hi.astype(F32)).astype(BF16)
    cs_ref[...] = _dot_tn(place, comb_hi) + _dot_tn(place, comb_lo)
    info_ref[...] = jnp.where(lane == 0, dest, 0.0)
    seg_ref[...] = jnp.concatenate([starts, padded, jnp.zeros((6, LANES), F32)], axis=0)


def _xattn_route(x, act, w_mix, gain, wq, k, v, wo, gain_ffn, w_router, b_router, tq=512):
    b, t, d = x.shape
    m = k.shape[1]
    tq = min(tq, t)
    tiles_per_row = t // tq
    n_tiles = b * tiles_per_row
    rows = _sorted_rows(tq)

    def tile(i, j):
        return (i * tiles_per_row + j, 0, 0)

    return pl.pallas_call(
        _xattn_route_kernel,
        grid=(b, tiles_per_row),
        in_specs=[pl.BlockSpec((None, tq, d), lambda i, j: (i, j, 0)),
                  pl.BlockSpec((None, tq, act.shape[2]), lambda i, j: (i, j, 0)),
                  pl.BlockSpec(w_mix.shape, lambda i, j: (0, 0)),
                  pl.BlockSpec((1, d), lambda i, j: (0, 0)),
                  pl.BlockSpec((d, d), lambda i, j: (0, 0)),
                  pl.BlockSpec((None, m, d), lambda i, j: (i, 0, 0)),
                  pl.BlockSpec((None, m, d), lambda i, j: (i, 0, 0)),
                  pl.BlockSpec((d, d), lambda i, j: (0, 0)),
                  pl.BlockSpec((1, d), lambda i, j: (0, 0)),
                  pl.BlockSpec((d, 2 * LANES), lambda i, j: (0, 0)),
                  pl.BlockSpec((1, LANES), lambda i, j: (0, 0))],
        out_specs=[pl.BlockSpec((None, tq, d), lambda i, j: (i, j, 0)),
                   pl.BlockSpec((None, rows, d), tile),
                   pl.BlockSpec((None, rows, LANES), tile),
                   pl.BlockSpec((None, tq, LANES), lambda i, j: (i, j, 0)),
                   pl.BlockSpec((None, 8, LANES), tile)],
        out_shape=[jax.ShapeDtypeStruct((b, t, d), F32),
                   jax.ShapeDtypeStruct((n_tiles, rows, d), BF16),
                   jax.ShapeDtypeStruct((n_tiles, rows, LANES), F32),
                   jax.ShapeDtypeStruct((b, t, LANES), F32),
                   jax.ShapeDtypeStruct((n_tiles, 8, LANES), F32)],
        compiler_params=_cparams(("parallel", "parallel")),
        name="xattn_route",
    )(x, act, w_mix, gain.reshape(1, d), wq, k, v, wo, gain_ffn.reshape(1, d), w_router, b_router)


def _moe_ffn_kernel(seg_ref, xs_ref, cs_ref, w1_ref, w3_ref, w2_ref, ys_ref, xbuf_ref, cbuf_ref, ybuf_ref):
    win = pl.program_id(0)
    grp = pl.program_id(1)
    n_win = xs_ref.shape[0]

    @pl.when((win == 0) & (grp == 0))
    def _():
        xbuf_ref[...] = jnp.zeros_like(xbuf_ref)
        cbuf_ref[...] = jnp.zeros_like(cbuf_ref)

    @pl.when(grp == 0)
    def _():
        ys_ref[...] = jnp.zeros_like(ys_ref)

    def segment(t):
        base = (win * n_win + t) * 8
        return seg_ref[base + grp], seg_ref[base + N_GROUPS + grp]

    def copy_segments(copy_piece):
        packed = jnp.int32(0)
        for t in range(n_win):
            start, size = segment(t)

            def body(j, carry, t=t, start=start, packed=packed):
                copy_piece(t, pl.multiple_of(start + j * SEG_ALIGN, SEG_ALIGN),
                           pl.multiple_of(packed + j * SEG_ALIGN, SEG_ALIGN))
                return carry

            lax.fori_loop(0, size // SEG_ALIGN, body, 0)
            packed = packed + size
        return packed

    def gather(t, src, dst):
        xbuf_ref[pl.ds(dst, SEG_ALIGN), :] = xs_ref[t, pl.ds(src, SEG_ALIGN), :]
        cbuf_ref[pl.ds(dst, SEG_ALIGN), :] = cs_ref[t, pl.ds(src, SEG_ALIGN), :]

    total = copy_segments(gather)

    def ffn_piece(r0, n_rows):
        xc = xbuf_ref[pl.ds(r0, n_rows), :]
        cb = cbuf_ref[pl.ds(r0, n_rows), :]
        lane = _iota2(cb.shape, 1)
        hids = []
        for e in range(EXPERTS_PER_GROUP):
            weight = jnp.sum(jnp.where(lane == N_GROUPS + EXPERTS_PER_GROUP * grp + e, cb, 0.0),
                             axis=-1, keepdims=True)
            hids.append((_silu(_dot(xc, w1_ref[e])) * _dot(xc, w3_ref[e]) * weight).astype(BF16))
        ybuf_ref[pl.ds(r0, n_rows), :] = _dot(jnp.concatenate(hids, axis=-1), w2_ref[...]).astype(ybuf_ref.dtype)

    def chunk(c, carry):
        ffn_piece(pl.multiple_of(c * MOE_CHUNK, MOE_CHUNK), MOE_CHUNK)
        return carry

    n_full = total // MOE_CHUNK
    lax.fori_loop(0, n_full, chunk, 0)
    rest = total - n_full * MOE_CHUNK
    rest_start = pl.multiple_of(n_full * MOE_CHUNK, MOE_CHUNK)

    @pl.when((rest > 0) & (rest <= MOE_CHUNK // 2))
    def _():
        ffn_piece(rest_start, MOE_CHUNK // 2)

    @pl.when(rest > MOE_CHUNK // 2)
    def _():
        ffn_piece(rest_start, MOE_CHUNK)

    def scatter(t, dst, src):
        ys_ref[t, pl.ds(dst, SEG_ALIGN), :] = ybuf_ref[pl.ds(src, SEG_ALIGN), :]

    copy_segments(scatter)


def _moe_ffn(seg, xs, cs, w1, w3, w2g):
    n_tiles, rows, d = xs.shape
    ff = w1.shape[2]
    n_win = next(w for w in (MOE_WINDOW, 2, 1) if n_tiles % w == 0)
    buf_rows = -(-(n_win * rows) // MOE_CHUNK) * MOE_CHUNK
    grid_spec = pltpu.PrefetchScalarGridSpec(
        num_scalar_prefetch=1,
        grid=(n_tiles // n_win, N_GROUPS),
        in_specs=[pl.BlockSpec((n_win, rows, d), lambda w, g, seg: (w, 0, 0)),
                  pl.BlockSpec((n_win, rows, LANES), lambda w, g, seg: (w, 0, 0)),
                  pl.BlockSpec((EXPERTS_PER_GROUP, d, ff), lambda w, g, seg: (g, 0, 0)),
                  pl.BlockSpec((EXPERTS_PER_GROUP, d, ff), lambda w, g, seg: (g, 0, 0)),
                  pl.BlockSpec((None, EXPERTS_PER_GROUP * ff, d), lambda w, g, seg: (g, 0, 0))],
        out_specs=pl.BlockSpec((n_win, rows, d), lambda w, g, seg: (w, 0, 0)),
        scratch_shapes=[pltpu.VMEM((buf_rows, d), BF16),
                        pltpu.VMEM((buf_rows, LANES), F32),
                        pltpu.VMEM((buf_rows, d), BF16)])
    return pl.pallas_call(
        _moe_ffn_kernel,
        grid_spec=grid_spec,
        out_shape=jax.ShapeDtypeStruct((n_tiles, rows, d), BF16),
        compiler_params=_cparams(("arbitrary", "arbitrary")),
        name="moe_ffn",
    )(seg, xs, cs, w1, w3, w2g)


def _moe_unsort_kernel(final_norm, x_ref, info_ref, ys_ref, gfin_ref, o_ref):
    info = info_ref[...]
    dest = jnp.sum(jnp.where(_iota2(info.shape, 1) == 0, info, 0.0), axis=-1, keepdims=True)
    rows = ys_ref.shape[0]
    place = (_iota2((info.shape[0], rows), 1).astype(F32) == dest).astype(BF16)
    y = x_ref[...] + _dot(place, ys_ref[...])
    if final_norm:
        y = _rms(y, gfin_ref[...])
    o_ref[...] = y


def _moe_unsort(x, info, ys, gain_final, final_norm):
    b, t, d = x.shape
    n_tiles, rows, _ = ys.shape
    tiles_per_row = n_tiles // b
    tq = t // tiles_per_row
    return pl.pallas_call(
        functools.partial(_moe_unsort_kernel, final_norm),
        grid=(b, tiles_per_row),
        in_specs=[pl.BlockSpec((None, tq, d), lambda i, j: (i, j, 0)),
                  pl.BlockSpec((None, tq, LANES), lambda i, j: (i, j, 0)),
                  pl.BlockSpec((None, rows, d), lambda i, j: (i * tiles_per_row + j, 0, 0)),
                  pl.BlockSpec((1, d), lambda i, j: (0, 0))],
        out_specs=pl.BlockSpec((None, tq, d), lambda i, j: (i, j, 0)),
        out_shape=jax.ShapeDtypeStruct((b, t, d), F32),
        compiler_params=_cparams(("parallel", "parallel")),
        name="moe_unsort",
    )(x, info, ys, gain_final.reshape(1, d))


def _pad_cols(w, n):
    return jnp.pad(w, ((0, 0), (0, n - w.shape[1])))


def _pad_rows(w, n):
    return jnp.pad(w, ((0, n - w.shape[0]), (0, 0)))


def _row(v, n=None):
    v = v.reshape(1, -1).astype(F32)
    return v if n is None else _pad_cols(v, n)


def kernel(x, mem, norm_mix, norm_xattn, norm_mem, norm_ffn, norm_final, ab_w_in, ab_conv, ab_a_log, ab_dt_bias, ab_onorm, ab_shift_mu, ab_w0, ab_w2, ab_a0, ab_a2, ab_g2, ab_k_k, ab_k_a, ab_r_k, ab_ln_w, ab_ln_b, ab_w_out, c_w_in, c_lb_param, c_onorm, c_w_out, xa_wq, xa_wkv, xa_wo, moe_w_group, moe_b_group, moe_w_expert, moe_b_expert, moe_w1, moe_w3, moe_w2):
    bsz, seq, d = x.shape
    n_mem = mem.shape[1]
    depth = norm_mix.shape[0]
    a_width = d // 2
    b_width = d // 2
    a_heads = a_width // A_HEAD_DIM
    a_cols = 4 * a_width + 2 * a_heads
    n_tok = bsz * seq

    lb_p = jax.nn.softmax(c_lb_param.astype(F32), axis=0)
    lower_bounds = jnp.cumsum(lb_p, axis=0) - lb_p[0]

    x2 = x.reshape(n_tok, d)
    mem2 = mem.reshape(bsz * n_mem, d)
    for layer in range(depth):
        if layer % 2 == 0:
            e = layer // 2
            w_in = ab_w_in[e]
            w_a = w_in[:, :4 * a_width].astype(BF16)
            w_db = _pad_cols(w_in[:, 4 * a_width:a_cols], LANES).astype(BF16)
            wb = w_in[:, a_cols:]
            o1, o2 = 3 * b_width, 3 * b_width + B_DECAY_LORA
            o3 = o2 + B_ICLR_LORA
            w_b = jnp.concatenate([wb[:, :o1], _pad_cols(wb[:, o1:o2], LANES), _pad_cols(wb[:, o2:o3], LANES),
                                   wb[:, o3:]], axis=1).astype(BF16)
            mu = ab_shift_mu[e].reshape(1, -1)
            mu_p = jnp.concatenate([mu[:, :o1], _pad_cols(mu[:, o1:o2], LANES), _pad_cols(mu[:, o2:o3], LANES),
                                    mu[:, o3:]], axis=1)
            pa, pdb, pb = _norm_proj(x2, norm_mix[layer], [w_a, w_db, w_b])
            head_id = jnp.arange(b_width) // B_HEAD_DIM
            bd = (head_id[:, None] == head_id[None, :]).astype(F32)
            gdn_params = [ab_conv[e], _row(ab_a_log[e], LANES), _row(ab_dt_bias[e], LANES), _row(ab_onorm[e])]
            rwkv_params = [mu_p, _row(ab_w0[e]), _pad_rows(ab_w2[e], LANES), _row(ab_a0[e]),
                           _pad_rows(ab_a2[e], LANES), ab_g2[e], _row(ab_k_k[e]), _row(ab_k_a[e]),
                           _row(ab_r_k[e]), _row(ab_ln_w[e]), _row(ab_ln_b[e]), bd]
            mix = _mixer_ab(pa.reshape(bsz, seq, -1), pdb.reshape(bsz, seq, LANES), pb.reshape(bsz, seq, -1),
                            gdn_params, rwkv_params)
            w_mix = ab_w_out[e].astype(BF16)
        else:
            o = layer // 2
            (proj,) = _norm_proj(x2, norm_mix[layer], [c_w_in[o].astype(BF16)])
            lb = lower_bounds[layer].reshape(1, -1)
            mix = _hgrn(proj.reshape(bsz, seq, -1), jnp.log(lb), jnp.log1p(-lb), _row(c_onorm[o]))
            w_mix = c_w_out[o].astype(BF16)

        wkv = xa_wkv[layer].astype(BF16)
        k_mem, v_mem = _norm_proj(mem2, norm_mem[layer], [wkv[:, :d], wkv[:, d:]], out_dtype=BF16)
        w_r = _pad_cols(jnp.concatenate([moe_w_group[layer], moe_w_expert[layer]], axis=1), LANES).astype(F32)
        w_r_hi = w_r.astype(BF16)
        w_r_lo = (w_r - w_r_hi.astype(F32)).astype(BF16)
        b_router = _row(jnp.concatenate([moe_b_group[layer], moe_b_expert[layer]]), LANES)
        x3, xs, cs, info, seg = _xattn_route(
            x2.reshape(bsz, seq, d), mix, w_mix, norm_xattn[layer], xa_wq[layer].astype(BF16),
            k_mem.reshape(bsz, n_mem, d), v_mem.reshape(bsz, n_mem, d), xa_wo[layer].astype(BF16),
            norm_ffn[layer], jnp.concatenate([w_r_hi, w_r_lo], axis=1), b_router)
        seg_i = jnp.concatenate([seg[:, 0, :N_GROUPS], seg[:, 1, :N_GROUPS]], axis=1).astype(jnp.int32).reshape(-1)
        ff = moe_w2.shape[2]
        ys = _moe_ffn(seg_i, xs, cs, moe_w1[layer].astype(BF16), moe_w3[layer].astype(BF16),
                      moe_w2[layer].reshape(N_GROUPS, EXPERTS_PER_GROUP * ff, d).astype(BF16))
        x2 = _moe_unsort(x3, info, ys, norm_final, final_norm=(layer == depth - 1)).reshape(n_tok, d)
    return x2.reshape(bsz, seq, d)
```

```python
import functools

import jax
import jax.numpy as jnp
import numpy as np
from jax import lax
from jax.experimental import pallas as pl
from jax.experimental.pallas import tpu as pltpu

F32 = jnp.float32
BF16 = jnp.bfloat16
HI = lax.Precision.HIGHEST
ONE_PASS = "bf16"

PREC_CUM = HI
PREC_INV = ONE_PASS
PREC_MIX = ONE_PASS
PREC_AUX = ONE_PASS

CHUNK = 64
HGRN_LEVELS = 6
NORM_EPS = 1e-6
GROUP_NORM_EPS = 64e-5
CONV_WIDTH = 4

A_HEAD_DIM = 128
B_HEAD_DIM = 64
B_DECAY_LORA = 64
B_ICLR_LORA = 64
B_GATE_LORA = 128
C_KEY_DIM = 128
XA_HEADS = 4
N_GROUPS = 4
EXPERTS_PER_GROUP = 4
N_EXPERTS = N_GROUPS * EXPERTS_PER_GROUP

LANES = 128
SUBLANES = 8
VMEM_LIMIT = 56 * 1024 * 1024
MIXER_BATCH = 4
SEG_ALIGN = 16
MOE_WINDOW = 4
MOE_CHUNK = 256
XATTN_TILES_PER_STEP = 2


def _cparams(sem):
    return pltpu.CompilerParams(dimension_semantics=sem, vmem_limit_bytes=VMEM_LIMIT)


def _dot_dims(a, b, dims, prec):
    if prec == ONE_PASS:
        a, b, prec = a.astype(BF16), b.astype(BF16), None
    return lax.dot_general(a, b, (dims, ((), ())), preferred_element_type=F32, precision=prec)


def _dot(a, b, prec=None):
    return _dot_dims(a, b, ((1,), (0,)), prec)


def _dot_nt(a, b, prec=None):
    return _dot_dims(a, b, ((1,), (1,)), prec)


def _dot_tn(a, b, prec=None):
    return _dot_dims(a, b, ((0,), (0,)), prec)


def _rms(x, gain):
    return x * lax.rsqrt(jnp.mean(x * x, axis=-1, keepdims=True) + NORM_EPS) * gain


def _sigmoid(x):
    return 1.0 / (1.0 + jnp.exp(-x))


def _silu(x):
    return x * _sigmoid(x)


def _softplus(x):
    return jnp.maximum(x, 0.0) + jnp.log(1.0 + jnp.exp(-jnp.abs(x)))


def _iota2(shape, dim):
    return lax.broadcasted_iota(jnp.int32, shape, dim)


def _interleave(stage_generators):
    live = list(stage_generators)
    while live:
        for gen in list(live):
            try:
                next(gen)
            except StopIteration:
                live.remove(gen)


def _tri_inverse_stages(mats):
    n = mats[0].shape[0]
    row = _iota2((n, n), 0)
    col = _iota2((n, n), 1)
    same16 = (row >> 4) == (col >> 4)
    same32 = (row >> 5) == (col >> 5)
    eye = (row == col).astype(F32)
    ads = [jnp.where(same16, a, 0.0) for a in mats]
    xs = [eye - ad for ad in ads]
    ps = [_dot(ad, ad, PREC_INV) for ad in ads]
    yield
    for stage in range(3):
        xs = [x + _dot(x, p, PREC_INV) for x, p in zip(xs, ps)]
        if stage < 2:
            ps = [_dot(p, p, PREC_INV) for p in ps]
        yield
    for lower in ([jnp.where(same32 & jnp.logical_not(same16), a, 0.0) for a in mats],
                  [jnp.where(same32, 0.0, a) for a in mats]):
        ts = [_dot(l, x, PREC_INV) for l, x in zip(lower, xs)]
        yield
        xs = [x - _dot(x, t, PREC_INV) for x, t in zip(xs, ts)]
        yield
    return xs


def _norm_proj_kernel(n_w, x_ref, g_ref, *refs):
    w_refs, o_refs = refs[:n_w], refs[n_w:]
    xn = _rms(x_ref[...], g_ref[...]).astype(BF16)
    for w_ref, o_ref in zip(w_refs, o_refs):
        n = w_ref.shape[1]
        for c0 in range(0, n, 512):
            c1 = min(n, c0 + 512)
            o_ref[:, c0:c1] = _dot(xn, w_ref[:, c0:c1]).astype(o_ref.dtype)


def _norm_proj(x, gain, ws, out_dtype=F32, tm=512):
    n, d = x.shape
    tm = min(tm, n)
    return pl.pallas_call(
        functools.partial(_norm_proj_kernel, len(ws)),
        grid=(n // tm,),
        in_specs=[pl.BlockSpec((tm, d), lambda i: (i, 0)),
                  pl.BlockSpec((1, d), lambda i: (0, 0))]
                 + [pl.BlockSpec(w.shape, lambda i: (0, 0)) for w in ws],
        out_specs=[pl.BlockSpec((tm, w.shape[1]), lambda i: (i, 0)) for w in ws],
        out_shape=[jax.ShapeDtypeStruct((n, w.shape[1]), out_dtype) for w in ws],
        compiler_params=_cparams(("parallel",)),
        name="norm_proj",
    )(x, gain.reshape(1, d), *ws)


def _gdn_stages(nb, pa_ref, pdb_ref, convw_ref, alog_ref, dtb_ref, onorm_ref, o_ref, xbuf_ref, state_ref):
    n_heads = state_ref.shape[0] // nb
    hd = A_HEAD_DIM
    width = n_heads * hd
    row = _iota2((CHUNK, CHUNK), 0)
    col = _iota2((CHUNK, CHUNK), 1)
    incl = row >= col
    strict = row > col
    tril = incl.astype(F32)
    triu = (row <= col).astype(F32)

    qs, ks, vs, betas, gs, g_lasts, decays = [], [], [], [], [], [], []
    for bb in range(nb):
        x = pa_ref[bb, :, 0:3 * width]
        xbuf_ref[bb, 8:8 + CHUNK, :] = x
        acc = convw_ref[0:1, :] * xbuf_ref[bb, 5:5 + CHUNK, :]
        for j in range(1, CONV_WIDTH):
            acc = acc + convw_ref[j:j + 1, :] * xbuf_ref[bb, 5 + j:5 + j + CHUNK, :]
        xbuf_ref[bb, 0:8, :] = x[CHUNK - 8:CHUNK, :]
        qkv = _silu(acc)
        pdb = pdb_ref[bb]
        log_alpha = -jnp.exp(alog_ref[...]) * _softplus(pdb + dtb_ref[...])
        beta_all = _sigmoid(pdb)
        g_all = _dot(tril, log_alpha, PREC_CUM)
        g_all_t = _dot_tn(log_alpha, triu, PREC_CUM)
        for h in range(n_heads):
            q = qkv[:, h * hd:(h + 1) * hd]
            k = qkv[:, width + h * hd: width + (h + 1) * hd]
            qs.append(q * lax.rsqrt(jnp.sum(q * q, axis=-1, keepdims=True) + 1e-6) * (hd ** -0.5))
            ks.append(k * lax.rsqrt(jnp.sum(k * k, axis=-1, keepdims=True) + 1e-6))
            vs.append(qkv[:, 2 * width + h * hd: 2 * width + (h + 1) * hd])
            betas.append(beta_all[:, n_heads + h:n_heads + h + 1])
            g = g_all[:, h:h + 1]
            gs.append(g)
            g_lasts.append(g_all[CHUNK - 1:CHUNK, h:h + 1])
            decays.append(jnp.where(incl, jnp.exp(jnp.where(incl, g - g_all_t[h:h + 1, :], 0.0)), 0.0))
    yield
    idx = range(nb * n_heads)
    k_betas = [ks[i] * betas[i] for i in idx]
    egs = [jnp.exp(g) for g in gs]
    a_lows = [jnp.where(strict, _dot_nt(k_betas[i], ks[i], PREC_MIX) * decays[i], 0.0) for i in idx]
    qks = [_dot_nt(qs[i], ks[i], PREC_MIX) * decays[i] for i in idx]
    states = [state_ref[i] for i in idx]
    o_inters = [_dot(qs[i] * egs[i], states[i], PREC_MIX) for i in idx]
    yield
    t_invs = yield from _tri_inverse_stages(a_lows)
    us = [_dot(t_invs[i], vs[i] * betas[i], PREC_MIX) for i in idx]
    ws = [_dot(t_invs[i], k_betas[i] * egs[i], PREC_MIX) for i in idx]
    yield
    v_news = [us[i] - _dot(ws[i], states[i], PREC_MIX) for i in idx]
    yield
    os_ = [o_inters[i] + _dot(qks[i], v_news[i], PREC_MIX) for i in idx]
    for i in idx:
        state_ref[i] = (states[i] * jnp.exp(g_lasts[i])
                        + _dot_tn(ks[i] * jnp.exp(g_lasts[i] - gs[i]), v_news[i], PREC_MIX))
    yield
    for i in idx:
        bb, h = divmod(i, n_heads)
        gate = pa_ref[bb, :, 3 * width + h * hd: 3 * width + (h + 1) * hd]
        o_ref[bb, :, h * hd:(h + 1) * hd] = _rms(os_[i], onorm_ref[...]) * _silu(gate)


def _rwkv_stages(nb, col0, pb_ref, mu_ref, w0_ref, w2_ref, a0_ref, a2_ref, g2_ref, kk_ref, ka_ref, rk_ref,
                 lnw_ref, lnb_ref, bd_ref, o_ref, carry_ref, state_ref, obuf_ref):
    n_heads = state_ref.shape[0] // nb
    hd = B_HEAD_DIM
    width = n_heads * hd
    row = _iota2((CHUNK, CHUNK), 0)
    col = _iota2((CHUNK, CHUNK), 1)
    strict = row > col
    tril = (row >= col).astype(F32)
    incl2 = _iota2((CHUNK, 2 * CHUNK), 0) >= (_iota2((CHUNK, 2 * CHUNK), 1) & (CHUNK - 1))
    bd = bd_ref[...]

    pss = []
    for bb in range(nb):
        p = pb_ref[bb]
        prev = jnp.where(_iota2(p.shape, 0) == 0, carry_ref[bb, 0:1, :], pltpu.roll(p, 1, 0))
        carry_ref[bb, 0:1, :] = p[CHUNK - 1:CHUNK, :]
        pss.append(p + mu_ref[...] * (prev - p))
    rs = [ps[:, 0:width] for ps in pss]
    kraw = [ps[:, width:2 * width] for ps in pss]
    vfull = [ps[:, 2 * width:3 * width] for ps in pss]
    w_raws = [w0_ref[...] + _dot(jnp.tanh(ps[:, 3 * width:3 * width + LANES]), w2_ref[...], PREC_AUX) for ps in pss]
    iclrs = [_sigmoid(a0_ref[...] + _dot(ps[:, 3 * width + LANES:3 * width + 2 * LANES], a2_ref[...], PREC_AUX))
             for ps in pss]
    gates = [_dot(_sigmoid(ps[:, 3 * width + 2 * LANES:3 * width + 3 * LANES]), g2_ref[...], PREC_AUX) for ps in pss]
    kks = [k * kk_ref[...] for k in kraw]
    kk_ss = [_dot(kk * kk, bd, PREC_AUX) for kk in kks]
    yield
    log_ws = [-jnp.exp(-_softplus(-w_raw) - 0.5) for w_raw in w_raws]
    g_cums = [_dot(tril, log_w, PREC_CUM) for log_w in log_ws]
    yield
    lhss, rhss, bl_kls, p_cs, vhs, k2s = [], [], [], [], [], []
    for bb in range(nb):
        kk = kks[bb] * lax.rsqrt(kk_ss[bb] + 1e-6)
        k2 = kraw[bb] * (1.0 + (iclrs[bb] - 1.0) * ka_ref[...])
        k2s.append(k2)
        bvec = kk * iclrs[bb]
        g = g_cums[bb]
        g_last = g[CHUNK - 1:CHUNK, :]
        eng = jnp.exp(-g)
        egl = jnp.exp(g_last - g)
        r_t = rs[bb] * jnp.exp(g)
        kk_t = kk * jnp.exp(g - log_ws[bb])
        b_h, k_h = bvec * eng, k2 * eng
        b_l, k_l = bvec * egl, k2 * egl
        p_c = jnp.exp(g_last)
        for h in range(n_heads):
            sl = slice(h * hd, (h + 1) * hd)
            lhss.append(jnp.concatenate([kk_t[:, sl], r_t[:, sl]], axis=0))
            rhss.append(jnp.concatenate([b_h[:, sl], k_h[:, sl]], axis=0))
            bl_kls.append(jnp.concatenate([b_l[:, sl], k_l[:, sl]], axis=0))
            p_cs.append(p_c[:, sl])
            vhs.append(vfull[bb][:, sl])
    idx = range(nb * n_heads)
    ms = [_dot_nt(lhss[i], rhss[i], PREC_MIX) for i in idx]
    states = [state_ref[i] for i in idx]
    lss = [_dot_nt(lhss[i], states[i], PREC_MIX) for i in idx]
    yield
    a_bbs = [jnp.where(strict, m[0:CHUNK, 0:CHUNK], 0.0) for m in ms]
    a_bks = [jnp.where(strict, m[0:CHUNK, CHUNK:2 * CHUNK], 0.0) for m in ms]
    a_rs = [jnp.where(incl2, m[CHUNK:2 * CHUNK, :], 0.0) for m in ms]
    rhs_u = [lss[i][0:CHUNK] + _dot(a_bks[i], vhs[i], PREC_MIX) for i in idx]
    t_invs = yield from _tri_inverse_stages(a_bbs)
    uvs = [jnp.concatenate([-_dot(t_invs[i], rhs_u[i], PREC_MIX), vhs[i]], axis=0) for i in idx]
    yield
    for i in idx:
        bb, h = divmod(i, n_heads)
        obuf_ref[bb, :, h * hd:(h + 1) * hd] = lss[i][CHUNK:2 * CHUNK] + _dot(a_rs[i], uvs[i], PREC_MIX)
        state_ref[i] = states[i] * p_cs[i] + _dot_tn(uvs[i], bl_kls[i], PREC_MIX)
    yield
    inv_n = 1.0 / hd
    os_ = [obuf_ref[bb] for bb in range(nb)]
    means = [_dot(o, bd, PREC_AUX) * inv_n for o in os_]
    bonus = [_dot(rs[bb] * k2s[bb] * rk_ref[...], bd, PREC_AUX) * vfull[bb] for bb in range(nb)]
    yield
    cens = [o - mean for o, mean in zip(os_, means)]
    vars_ = [_dot(cen * cen, bd, PREC_AUX) * inv_n for cen in cens]
    yield
    for bb in range(nb):
        ln = cens[bb] * lax.rsqrt(vars_[bb] + GROUP_NORM_EPS) * lnw_ref[...] + lnb_ref[...]
        o_ref[bb, :, col0:col0 + width] = (ln + bonus[bb]) * gates[bb]


def _mixer_ab_kernel(n_gdn, n_rwkv, pa_ref, pdb_ref, pb_ref, *refs):
    gdn_params = refs[:n_gdn]
    rwkv_params = refs[n_gdn:n_gdn + n_rwkv]
    o_ref, xbuf_ref, gstate_ref, carry_ref, rstate_ref, obuf_ref = refs[n_gdn + n_rwkv:]
    nb = pa_ref.shape[0]

    @pl.when(pl.program_id(1) == 0)
    def _():
        xbuf_ref[:, 0:8, :] = jnp.zeros((nb, 8, xbuf_ref.shape[2]), F32)
        gstate_ref[...] = jnp.zeros_like(gstate_ref)
        carry_ref[...] = jnp.zeros_like(carry_ref)
        rstate_ref[...] = jnp.zeros_like(rstate_ref)

    a_width = pa_ref.shape[2] // 4
    _interleave([
        _gdn_stages(nb, pa_ref, pdb_ref, *gdn_params, o_ref, xbuf_ref, gstate_ref),
        _rwkv_stages(nb, a_width, pb_ref, *rwkv_params, o_ref, carry_ref, rstate_ref, obuf_ref),
    ])


def _mixer_ab(pa, pdb, pb, gdn_params, rwkv_params):
    b, t, a_cols = pa.shape
    b_cols = pb.shape[2]
    a_width = a_cols // 4
    b_width = (b_cols - 3 * LANES) // 3
    a_heads = a_width // A_HEAD_DIM
    b_heads = b_width // B_HEAD_DIM
    nb = MIXER_BATCH if b % MIXER_BATCH == 0 else 1
    params = list(gdn_params) + list(rwkv_params)
    return pl.pallas_call(
        functools.partial(_mixer_ab_kernel, len(gdn_params), len(rwkv_params)),
        grid=(b // nb, t // CHUNK),
        in_specs=[pl.BlockSpec((nb, CHUNK, a_cols), lambda i, c: (i, c, 0)),
                  pl.BlockSpec((nb, CHUNK, LANES), lambda i, c: (i, c, 0)),
                  pl.BlockSpec((nb, CHUNK, b_cols), lambda i, c: (i, c, 0))]
                 + [pl.BlockSpec(a.shape, lambda i, c: (0, 0)) for a in params],
        out_specs=pl.BlockSpec((nb, CHUNK, a_width + b_width), lambda i, c: (i, c, 0)),
        out_shape=jax.ShapeDtypeStruct((b, t, a_width + b_width), F32),
        scratch_shapes=[pltpu.VMEM((nb, 8 + CHUNK, 3 * a_width), F32),
                        pltpu.VMEM((nb * a_heads, A_HEAD_DIM, A_HEAD_DIM), F32),
                        pltpu.VMEM((nb, 8, b_cols), F32),
                        pltpu.VMEM((nb * b_heads, B_HEAD_DIM, B_HEAD_DIM), F32),
                        pltpu.VMEM((nb, CHUNK, b_width), F32)],
        compiler_params=_cparams(("parallel", "arbitrary")),
        name="mixer_ab",
    )(pa, pdb, pb, *params)


def _hgrn_tables():
    i = np.arange(CHUNK)[:, None]
    t = np.arange(CHUNK)[None, :]
    blocks, pair = [], []
    for ls in range(HGRN_LEVELS - 1, -1, -1):
        s = 1 << ls
        mid = (i >> (ls + 1) << (ls + 1)) + s
        q_rows, k_rows = _hgrn_level_rows(ls)
        blocks.append(((i >= mid) & (t >= mid) & (t <= i))[q_rows])
        blocks.append(((i < mid) & (t > i) & (t <= mid - 1))[k_rows])
        pair.append((((i >> ls) ^ (t >> ls)) == 1) & (i > t))
    blocks.append(t <= i)
    blocks.append(t > i)
    sums = np.concatenate(blocks, axis=0).astype(np.float32)
    return np.concatenate([sums] * 3, axis=1), np.stack(pair).astype(np.float32)


def _hgrn_level_rows(ls):
    rows = np.arange(CHUNK)
    if (1 << ls) < SUBLANES:
        return rows, rows
    upper = ((rows >> ls) & 1) == 1
    return rows[upper], rows[~upper]


def _hgrn_kernel(p_ref, loglb_ref, log1mlb_ref, onorm_ref, sums_ref, pair_ref, o_ref, state_ref):
    nb = p_ref.shape[0]
    n_heads = state_ref.shape[0] // nb
    hd = C_KEY_DIM
    kw = n_heads * hd

    @pl.when(pl.program_id(1) == 0)
    def _():
        state_ref[...] = jnp.zeros_like(state_ref)

    def take_rows(a, rows):
        runs, start = [], 0
        for n in range(1, len(rows) + 1):
            if n == len(rows) or rows[n] != rows[n - 1] + 1:
                runs.append(a[int(rows[start]):int(rows[n - 1]) + 1])
                start = n
        return runs[0] if len(runs) == 1 else jnp.concatenate(runs, axis=0)

    def spread_rows(a, rows):
        if len(rows) == CHUNK:
            return a
        pieces, pos, taken = [], 0, 0
        while pos < CHUNK:
            run = 1
            inside = pos in rows
            while pos + run < CHUNK and ((pos + run) in rows) == inside:
                run += 1
            if inside:
                pieces.append(a[taken:taken + run])
                taken += run
            else:
                pieces.append(jnp.zeros((run, a.shape[1]), a.dtype))
            pos += run
        return jnp.concatenate(pieces, axis=0)

    qs, ks, exps = [], [], []
    for bb in range(nb):
        qs.append(_silu(p_ref[bb, :, 0:kw]))
        f = p_ref[bb, :, kw:2 * kw]
        a1 = jnp.broadcast_to(loglb_ref[...], f.shape)
        b1 = log1mlb_ref[...] - _softplus(-f)
        log_f = jnp.maximum(a1, b1) + jnp.log(1.0 + jnp.exp(-jnp.abs(a1 - b1)))
        ks.append(1.0 - jnp.exp(log_f))
        f1 = log_f.astype(BF16)
        f2 = (log_f - f1.astype(F32)).astype(BF16)
        f3 = (log_f - f1.astype(F32) - f2.astype(F32)).astype(BF16)
        exps.append(_dot(sums_ref[...], jnp.concatenate([f1, f2, f3], axis=0)))

    idx = [(bb, h) for bb in range(nb) for h in range(n_heads)]
    states = [state_ref[bb * n_heads + h] for bb, h in idx]
    scores = [None] * len(idx)
    offset = 0
    for lvl in range(HGRN_LEVELS):
        q_rows, k_rows = _hgrn_level_rows(HGRN_LEVELS - 1 - lvl)
        q_rows, k_rows = [int(r) for r in q_rows], [int(r) for r in k_rows]
        e_q = [e[offset:offset + len(q_rows)] for e in exps]
        e_k = [e[offset + len(q_rows):offset + len(q_rows) + len(k_rows)] for e in exps]
        offset += len(q_rows) + len(k_rows)
        q_l = [(take_rows(qs[bb], q_rows) * jnp.exp(e_q[bb])).astype(BF16) for bb in range(nb)]
        k_l = [spread_rows(take_rows(ks[bb], k_rows) * jnp.exp(e_k[bb]), k_rows).astype(BF16) for bb in range(nb)]
        mask = pair_ref[lvl]
        for n, (bb, h) in enumerate(idx):
            sl = slice(h * hd, (h + 1) * hd)
            part = spread_rows(_dot_nt(q_l[bb][:, sl], k_l[bb][:, sl]), q_rows) * mask
            scores[n] = part if scores[n] is None else scores[n] + part
    g_cum = [e[offset:offset + CHUNK] for e in exps]
    q_g = [qs[bb] * jnp.exp(g_cum[bb]) for bb in range(nb)]
    k_r = [ks[bb] * jnp.exp(exps[bb][offset + CHUNK:offset + 2 * CHUNK]) for bb in range(nb)]
    diag = [qs[bb] * ks[bb] for bb in range(nb)]
    outs = []
    for n, (bb, h) in enumerate(idx):
        sl = slice(h * hd, (h + 1) * hd)
        v = p_ref[bb, :, 2 * kw + h * hd: 2 * kw + (h + 1) * hd]
        o = _dot_nt(q_g[bb][:, sl], states[n], PREC_MIX) + _dot(scores[n], v, PREC_MIX)
        outs.append(o + jnp.sum(diag[bb][:, sl], axis=-1, keepdims=True) * v)
        p_last = jnp.exp(g_cum[bb][CHUNK - 1:CHUNK, sl])
        state_ref[bb * n_heads + h] = states[n] * p_last + _dot_tn(v, k_r[bb][:, sl], PREC_MIX)
    for n, (bb, h) in enumerate(idx):
        gate = p_ref[bb, :, 3 * kw + h * hd: 3 * kw + (h + 1) * hd]
        o_ref[bb, :, h * hd:(h + 1) * hd] = _rms(outs[n], onorm_ref[...]) * _silu(gate)


def _hgrn(proj, log_lb, log1m_lb, onorm):
    b, t, cols = proj.shape
    kw = cols // 4
    n_heads = kw // C_KEY_DIM
    nb = MIXER_BATCH if b % MIXER_BATCH == 0 else 1
    sums, pair = _hgrn_tables()
    sums = jnp.asarray(sums, BF16)
    pair = jnp.asarray(pair, F32)
    return pl.pallas_call(
        _hgrn_kernel,
        grid=(b // nb, t // CHUNK),
        in_specs=[pl.BlockSpec((nb, CHUNK, cols), lambda i, c: (i, c, 0)),
                  pl.BlockSpec((1, kw), lambda i, c: (0, 0)),
                  pl.BlockSpec((1, kw), lambda i, c: (0, 0)),
                  pl.BlockSpec((1, C_KEY_DIM), lambda i, c: (0, 0)),
                  pl.BlockSpec(sums.shape, lambda i, c: (0, 0)),
                  pl.BlockSpec(pair.shape, lambda i, c: (0, 0, 0))],
        out_specs=pl.BlockSpec((nb, CHUNK, kw), lambda i, c: (i, c, 0)),
        out_shape=jax.ShapeDtypeStruct((b, t, kw), F32),
        scratch_shapes=[pltpu.VMEM((nb * n_heads, C_KEY_DIM, C_KEY_DIM), F32)],
        compiler_params=_cparams(("parallel", "arbitrary")),
        name="hgrn2",
    )(proj, log_lb, log1m_lb, onorm, sums, pair)


def _route(logits):
    lane = _iota2(logits.shape, 1)
    neg = -jnp.inf
    gmask = lane < N_GROUPS
    gl = jnp.where(gmask, logits, neg)
    gmax = jnp.max(gl, axis=-1, keepdims=True)
    gsel = jnp.min(jnp.where(gl == gmax, lane, LANES), axis=-1, keepdims=True)
    group_gate = 1.0 / jnp.sum(jnp.where(gmask, jnp.exp(gl - gmax), 0.0), axis=-1, keepdims=True)
    lo = N_GROUPS + EXPERTS_PER_GROUP * gsel
    emask = (lane >= lo) & (lane < lo + EXPERTS_PER_GROUP)
    el = jnp.where(emask, logits, neg)
    ee = jnp.where(emask, jnp.exp(el - jnp.max(el, axis=-1, keepdims=True)), 0.0)
    prob = ee / jnp.sum(ee, axis=-1, keepdims=True)
    p1 = jnp.max(jnp.where(emask, prob, -1.0), axis=-1, keepdims=True)
    i1 = jnp.min(jnp.where(emask & (prob == p1), lane, LANES), axis=-1, keepdims=True)
    rest = emask & (lane != i1)
    p2 = jnp.max(jnp.where(rest, prob, -1.0), axis=-1, keepdims=True)
    i2 = jnp.min(jnp.where(rest & (prob == p2), lane, LANES), axis=-1, keepdims=True)
    scale = group_gate / (p1 + p2)
    comb = jnp.where(lane == i1, p1 * scale, 0.0) + jnp.where(lane == i2, p2 * scale, 0.0)
    return comb, gsel


def _sorted_rows(tile_rows):
    return tile_rows + N_GROUPS * SEG_ALIGN


def _xattn_route_kernel(x_ref, a_ref, wmix_ref, gain_ref, wq_ref, k_ref, v_ref, wo_ref, gffn_ref, wr_ref, br_ref,
                        x2_ref, xs_ref, cs_ref, info_ref, seg_ref):
    n_sub = xs_ref.shape[0]
    tq = x_ref.shape[0] // n_sub
    d = x_ref.shape[1]
    hd = d // XA_HEADS
    rows = xs_ref.shape[1]

    def tile_stages(u):
        tok = slice(u * tq, (u + 1) * tq)
        x = x_ref[tok, :] + _dot(a_ref[tok, :].astype(BF16), wmix_ref[...])
        q = _dot(_rms(x, gain_ref[...]).astype(BF16), wq_ref[...]).astype(BF16)
        yield
        outs = []
        for h in range(XA_HEADS):
            sl = slice(h * hd, (h + 1) * hd)
            s = _dot_nt(q[:, sl], k_ref[:, sl]) * (hd ** -0.5)
            p = jnp.exp(s - jnp.max(s, axis=-1, keepdims=True))
            p = p / jnp.sum(p, axis=-1, keepdims=True)
            outs.append(_dot(p.astype(BF16), v_ref[:, sl]).astype(BF16))
        yield
        x2 = x + _dot(jnp.concatenate(outs, axis=-1), wo_ref[...])
        x2_ref[tok, :] = x2
        xn = _rms(x2, gffn_ref[...])
        xn_hi = xn.astype(BF16)
        xn_lo = (xn - xn_hi.astype(F32)).astype(BF16)
        hi_part = _dot(xn_hi, wr_ref[...])
        logits = hi_part[:, 0:LANES] + hi_part[:, LANES:2 * LANES] + _dot(xn_lo, wr_ref[:, 0:LANES])
        yield
        comb, gsel = _route(logits + br_ref[...])
        lane = _iota2((tq, LANES), 1)
        onehot = (lane == gsel).astype(F32)
        earlier = (_iota2((tq, tq), 0) > _iota2((tq, tq), 1)).astype(BF16)
        before = _dot(earlier, onehot.astype(BF16))
        yield
        rank = jnp.sum(onehot * before, axis=-1, keepdims=True)
        counts = before[tq - 1:tq, :] + onehot[tq - 1:tq, :]
        padded = jnp.floor((counts + (SEG_ALIGN - 1)) * (1.0 / SEG_ALIGN)) * SEG_ALIGN
        lane1 = _iota2((1, LANES), 1)
        starts = jnp.zeros((1, LANES), F32)
        for g in range(N_GROUPS - 1):
            starts = starts + jnp.where(lane1 > g, padded[:, g:g + 1], 0.0)
        dest = jnp.sum(onehot * starts, axis=-1, keepdims=True) + rank
        place = (_iota2((tq, rows), 1).astype(F32) == dest).astype(BF16)
        comb_hi = comb.astype(BF16)
        comb_lo = (comb - comb_hi.astype(F32)).astype(BF16)
        placed = _dot_tn(place, jnp.concatenate([xn_hi, comb_hi, comb_lo], axis=1))
        xs_ref[u] = placed[:, 0:d].astype(BF16)
        cs_ref[u] = placed[:, d:d + LANES] + placed[:, d + LANES:d + 2 * LANES]
        info_ref[tok, :] = jnp.where(lane == 0, dest, 0.0)
        seg_ref[u] = jnp.concatenate([starts, padded, jnp.zeros((6, LANES), F32)], axis=0)

    _interleave([tile_stages(u) for u in range(n_sub)])


def _xattn_route(x, act, w_mix, gain, wq, k, v, wo, gain_ffn, w_router, b_router, tq=512):
    b, t, d = x.shape
    m = k.shape[1]
    tq = min(tq, t)
    tiles_per_row = t // tq
    n_tiles = b * tiles_per_row
    rows = _sorted_rows(tq)
    n_sub = XATTN_TILES_PER_STEP if tiles_per_row % XATTN_TILES_PER_STEP == 0 else 1
    steps_per_row = tiles_per_row // n_sub

    def tile(i, j):
        return (i * steps_per_row + j, 0, 0)

    return pl.pallas_call(
        _xattn_route_kernel,
        grid=(b, steps_per_row),
        in_specs=[pl.BlockSpec((None, n_sub * tq, d), lambda i, j: (i, j, 0)),
                  pl.BlockSpec((None, n_sub * tq, act.shape[2]), lambda i, j: (i, j, 0)),
                  pl.BlockSpec(w_mix.shape, lambda i, j: (0, 0)),
                  pl.BlockSpec((1, d), lambda i, j: (0, 0)),
                  pl.BlockSpec((d, d), lambda i, j: (0, 0)),
                  pl.BlockSpec((None, m, d), lambda i, j: (i, 0, 0)),
                  pl.BlockSpec((None, m, d), lambda i, j: (i, 0, 0)),
                  pl.BlockSpec((d, d), lambda i, j: (0, 0)),
                  pl.BlockSpec((1, d), lambda i, j: (0, 0)),
                  pl.BlockSpec((d, 2 * LANES), lambda i, j: (0, 0)),
                  pl.BlockSpec((1, LANES), lambda i, j: (0, 0))],
        out_specs=[pl.BlockSpec((None, n_sub * tq, d), lambda i, j: (i, j, 0)),
                   pl.BlockSpec((n_sub, rows, d), tile),
                   pl.BlockSpec((n_sub, rows, LANES), tile),
                   pl.BlockSpec((None, n_sub * tq, LANES), lambda i, j: (i, j, 0)),
                   pl.BlockSpec((n_sub, 8, LANES), tile)],
        out_shape=[jax.ShapeDtypeStruct((b, t, d), F32),
                   jax.ShapeDtypeStruct((n_tiles, rows, d), BF16),
                   jax.ShapeDtypeStruct((n_tiles, rows, LANES), F32),
                   jax.ShapeDtypeStruct((b, t, LANES), F32),
                   jax.ShapeDtypeStruct((n_tiles, 8, LANES), F32)],
        compiler_params=_cparams(("parallel", "parallel")),
        name="xattn_route",
    )(x, act, w_mix, gain.reshape(1, d), wq, k, v, wo, gain_ffn.reshape(1, d), w_router, b_router)


def _moe_ffn_kernel(seg_ref, xs_ref, cs_ref, w1_ref, w3_ref, w2_ref, ys_ref, xbuf_ref, cbuf_ref, ybuf_ref):
    win = pl.program_id(0)
    grp = pl.program_id(1)
    n_win = xs_ref.shape[0]

    @pl.when((win == 0) & (grp == 0))
    def _():
        xbuf_ref[...] = jnp.zeros_like(xbuf_ref)
        cbuf_ref[...] = jnp.zeros_like(cbuf_ref)

    @pl.when(grp == 0)
    def _():
        ys_ref[...] = jnp.zeros_like(ys_ref)

    def segment(t):
        base = (win * n_win + t) * 8
        return seg_ref[base + grp], seg_ref[base + N_GROUPS + grp]

    def copy_segments(copy_piece):
        packed = jnp.int32(0)
        for t in range(n_win):
            start, size = segment(t)

            def body(j, carry, t=t, start=start, packed=packed):
                copy_piece(t, pl.multiple_of(start + j * SEG_ALIGN, SEG_ALIGN),
                           pl.multiple_of(packed + j * SEG_ALIGN, SEG_ALIGN))
                return carry

            lax.fori_loop(0, size // SEG_ALIGN, body, 0)
            packed = packed + size
        return packed

    def gather(t, src, dst):
        xbuf_ref[pl.ds(dst, SEG_ALIGN), :] = xs_ref[t, pl.ds(src, SEG_ALIGN), :]
        cbuf_ref[pl.ds(dst, SEG_ALIGN), :] = cs_ref[t, pl.ds(src, SEG_ALIGN), :]

    total = copy_segments(gather)

    def ffn_piece(r0, n_rows):
        xc = xbuf_ref[pl.ds(r0, n_rows), :]
        cb = cbuf_ref[pl.ds(r0, n_rows), :]
        lane = _iota2(cb.shape, 1)
        hids = []
        for e in range(EXPERTS_PER_GROUP):
            weight = jnp.sum(jnp.where(lane == N_GROUPS + EXPERTS_PER_GROUP * grp + e, cb, 0.0),
                             axis=-1, keepdims=True)
            hids.append((_silu(_dot(xc, w1_ref[e])) * _dot(xc, w3_ref[e]) * weight).astype(BF16))
        ybuf_ref[pl.ds(r0, n_rows), :] = _dot(jnp.concatenate(hids, axis=-1), w2_ref[...]).astype(ybuf_ref.dtype)

    def chunk(c, carry):
        ffn_piece(pl.multiple_of(c * MOE_CHUNK, MOE_CHUNK), MOE_CHUNK)
        return carry

    n_full = total // MOE_CHUNK
    lax.fori_loop(0, n_full, chunk, 0)
    rest = total - n_full * MOE_CHUNK
    rest_start = pl.multiple_of(n_full * MOE_CHUNK, MOE_CHUNK)

    @pl.when((rest > 0) & (rest <= MOE_CHUNK // 2))
    def _():
        ffn_piece(rest_start, MOE_CHUNK // 2)

    @pl.when(rest > MOE_CHUNK // 2)
    def _():
        ffn_piece(rest_start, MOE_CHUNK)

    def scatter(t, dst, src):
        ys_ref[t, pl.ds(dst, SEG_ALIGN), :] = ybuf_ref[pl.ds(src, SEG_ALIGN), :]

    copy_segments(scatter)


def _moe_ffn(seg, xs, cs, w1, w3, w2g):
    n_tiles, rows, d = xs.shape
    ff = w1.shape[2]
    n_win = next(w for w in (MOE_WINDOW, 2, 1) if n_tiles % w == 0)
    buf_rows = -(-(n_win * rows) // MOE_CHUNK) * MOE_CHUNK
    grid_spec = pltpu.PrefetchScalarGridSpec(
        num_scalar_prefetch=1,
        grid=(n_tiles // n_win, N_GROUPS),
        in_specs=[pl.BlockSpec((n_win, rows, d), lambda w, g, seg: (w, 0, 0)),
                  pl.BlockSpec((n_win, rows, LANES), lambda w, g, seg: (w, 0, 0)),
                  pl.BlockSpec((EXPERTS_PER_GROUP, d, ff), lambda w, g, seg: (g, 0, 0)),
                  pl.BlockSpec((EXPERTS_PER_GROUP, d, ff), lambda w, g, seg: (g, 0, 0)),
                  pl.BlockSpec((None, EXPERTS_PER_GROUP * ff, d), lambda w, g, seg: (g, 0, 0))],
        out_specs=pl.BlockSpec((n_win, rows, d), lambda w, g, seg: (w, 0, 0)),
        scratch_shapes=[pltpu.VMEM((buf_rows, d), BF16),
                        pltpu.VMEM((buf_rows, LANES), F32),
                        pltpu.VMEM((buf_rows, d), BF16)])
    return pl.pallas_call(
        _moe_ffn_kernel,
        grid_spec=grid_spec,
        out_shape=jax.ShapeDtypeStruct((n_tiles, rows, d), BF16),
        compiler_params=_cparams(("arbitrary", "arbitrary")),
        name="moe_ffn",
    )(seg, xs, cs, w1, w3, w2g)


def _moe_unsort_kernel(final_norm, x_ref, info_ref, ys_ref, gfin_ref, o_ref):
    info = info_ref[...]
    dest = jnp.sum(jnp.where(_iota2(info.shape, 1) == 0, info, 0.0), axis=-1, keepdims=True)
    rows = ys_ref.shape[0]
    place = (_iota2((info.shape[0], rows), 1).astype(F32) == dest).astype(BF16)
    y = x_ref[...] + _dot(place, ys_ref[...])
    if final_norm:
        y = _rms(y, gfin_ref[...])
    o_ref[...] = y


def _moe_unsort(x, info, ys, gain_final, final_norm):
    b, t, d = x.shape
    n_tiles, rows, _ = ys.shape
    tiles_per_row = n_tiles // b
    tq = t // tiles_per_row
    return pl.pallas_call(
        functools.partial(_moe_unsort_kernel, final_norm),
        grid=(b, tiles_per_row),
        in_specs=[pl.BlockSpec((None, tq, d), lambda i, j: (i, j, 0)),
                  pl.BlockSpec((None, tq, LANES), lambda i, j: (i, j, 0)),
                  pl.BlockSpec((None, rows, d), lambda i, j: (i * tiles_per_row + j, 0, 0)),
                  pl.BlockSpec((1, d), lambda i, j: (0, 0))],
        out_specs=pl.BlockSpec((None, tq, d), lambda i, j: (i, j, 0)),
        out_shape=jax.ShapeDtypeStruct((b, t, d), F32),
        compiler_params=_cparams(("parallel", "parallel")),
        name="moe_unsort",
    )(x, info, ys, gain_final.reshape(1, d))


def _pad_cols(w, n):
    return jnp.pad(w, ((0, 0), (0, n - w.shape[1])))


def _pad_rows(w, n):
    return jnp.pad(w, ((0, n - w.shape[0]), (0, 0)))


def _row(v, n=None):
    v = v.reshape(1, -1).astype(F32)
    return v if n is None else _pad_cols(v, n)


def kernel(x, mem, norm_mix, norm_xattn, norm_mem, norm_ffn, norm_final, ab_w_in, ab_conv, ab_a_log, ab_dt_bias, ab_onorm, ab_shift_mu, ab_w0, ab_w2, ab_a0, ab_a2, ab_g2, ab_k_k, ab_k_a, ab_r_k, ab_ln_w, ab_ln_b, ab_w_out, c_w_in, c_lb_param, c_onorm, c_w_out, xa_wq, xa_wkv, xa_wo, moe_w_group, moe_b_group, moe_w_expert, moe_b_expert, moe_w1, moe_w3, moe_w2):
    bsz, seq, d = x.shape
    n_mem = mem.shape[1]
    depth = norm_mix.shape[0]
    a_width = d // 2
    b_width = d // 2
    a_heads = a_width // A_HEAD_DIM
    a_cols = 4 * a_width + 2 * a_heads
    n_tok = bsz * seq

    lb_p = jax.nn.softmax(c_lb_param.astype(F32), axis=0)
    lower_bounds = jnp.cumsum(lb_p, axis=0) - lb_p[0]

    x2 = x.reshape(n_tok, d)
    mem2 = mem.reshape(bsz * n_mem, d)
    for layer in range(depth):
        if layer % 2 == 0:
            e = layer // 2
            w_in = ab_w_in[e]
            w_a = w_in[:, :4 * a_width].astype(BF16)
            w_db = _pad_cols(w_in[:, 4 * a_width:a_cols], LANES).astype(BF16)
            wb = w_in[:, a_cols:]
            o1, o2 = 3 * b_width, 3 * b_width + B_DECAY_LORA
            o3 = o2 + B_ICLR_LORA
            w_b = jnp.concatenate([wb[:, :o1], _pad_cols(wb[:, o1:o2], LANES), _pad_cols(wb[:, o2:o3], LANES),
                                   wb[:, o3:]], axis=1).astype(BF16)
            mu = ab_shift_mu[e].reshape(1, -1)
            mu_p = jnp.concatenate([mu[:, :o1], _pad_cols(mu[:, o1:o2], LANES), _pad_cols(mu[:, o2:o3], LANES),
                                    mu[:, o3:]], axis=1)
            pa, pdb, pb = _norm_proj(x2, norm_mix[layer], [w_a, w_db, w_b])
            head_id = jnp.arange(b_width) // B_HEAD_DIM
            bd = (head_id[:, None] == head_id[None, :]).astype(F32)
            gdn_params = [ab_conv[e], _row(ab_a_log[e], LANES), _row(ab_dt_bias[e], LANES), _row(ab_onorm[e])]
            rwkv_params = [mu_p, _row(ab_w0[e]), _pad_rows(ab_w2[e], LANES), _row(ab_a0[e]),
                           _pad_rows(ab_a2[e], LANES), ab_g2[e], _row(ab_k_k[e]), _row(ab_k_a[e]),
                           _row(ab_r_k[e]), _row(ab_ln_w[e]), _row(ab_ln_b[e]), bd]
            mix = _mixer_ab(pa.reshape(bsz, seq, -1), pdb.reshape(bsz, seq, LANES), pb.reshape(bsz, seq, -1),
                            gdn_params, rwkv_params)
            w_mix = ab_w_out[e].astype(BF16)
        else:
            o = layer // 2
            (proj,) = _norm_proj(x2, norm_mix[layer], [c_w_in[o].astype(BF16)])
            lb = lower_bounds[layer].reshape(1, -1)
            mix = _hgrn(proj.reshape(bsz, seq, -1), jnp.log(lb), jnp.log1p(-lb), _row(c_onorm[o]))
            w_mix = c_w_out[o].astype(BF16)

        wkv = xa_wkv[layer].astype(BF16)
        k_mem, v_mem = _norm_proj(mem2, norm_mem[layer], [wkv[:, :d], wkv[:, d:]], out_dtype=BF16)
        w_r = _pad_cols(jnp.concatenate([moe_w_group[layer], moe_w_expert[layer]], axis=1), LANES).astype(F32)
        w_r_hi = w_r.astype(BF16)
        w_r_lo = (w_r - w_r_hi.astype(F32)).astype(BF16)
        b_router = _row(jnp.concatenate([moe_b_group[layer], moe_b_expert[layer]]), LANES)
        x3, xs, cs, info, seg = _xattn_route(
            x2.reshape(bsz, seq, d), mix, w_mix, norm_xattn[layer], xa_wq[layer].astype(BF16),
            k_mem.reshape(bsz, n_mem, d), v_mem.reshape(bsz, n_mem, d), xa_wo[layer].astype(BF16),
            norm_ffn[layer], jnp.concatenate([w_r_hi, w_r_lo], axis=1), b_router)
        seg_i = jnp.concatenate([seg[:, 0, :N_GROUPS], seg[:, 1, :N_GROUPS]], axis=1).astype(jnp.int32).reshape(-1)
        ff = moe_w2.shape[2]
        ys = _moe_ffn(seg_i, xs, cs, moe_w1[layer].astype(BF16), moe_w3[layer].astype(BF16),
                      moe_w2[layer].reshape(N_GROUPS, EXPERTS_PER_GROUP * ff, d).astype(BF16))
        x2 = _moe_unsort(x3, info, ys, norm_final, final_norm=(layer == depth - 1)).reshape(n_tok, d)
    return x2.reshape(bsz, seq, d)
```

```python
import functools

import jax
import jax.numpy as jnp
import numpy as np
from jax import lax
from jax.experimental import pallas as pl
from jax.experimental.pallas import tpu as pltpu

F32 = jnp.float32
BF16 = jnp.bfloat16
HI = lax.Precision.HIGHEST
ONE_PASS = "bf16"

PREC_CUM = HI
PREC_INV = ONE_PASS
PREC_MIX = ONE_PASS
PREC_AUX = ONE_PASS

CHUNK = 64
HGRN_LEVELS = 6
NORM_EPS = 1e-6
GROUP_NORM_EPS = 64e-5
CONV_WIDTH = 4

A_HEAD_DIM = 128
B_HEAD_DIM = 64
B_DECAY_LORA = 64
B_ICLR_LORA = 64
B_GATE_LORA = 128
C_KEY_DIM = 128
XA_HEADS = 4
N_GROUPS = 4
EXPERTS_PER_GROUP = 4
N_EXPERTS = N_GROUPS * EXPERTS_PER_GROUP

LANES = 128
SUBLANES = 8
VMEM_LIMIT = 56 * 1024 * 1024
MIXER_BATCH = 4
SEG_ALIGN = 16
MOE_WINDOW = 4
MOE_CHUNK = 256
XATTN_TILES_PER_STEP = 2


def _cparams(sem):
    return pltpu.CompilerParams(dimension_semantics=sem, vmem_limit_bytes=VMEM_LIMIT)


def _dot_dims(a, b, dims, prec):
    if prec == ONE_PASS:
        a, b, prec = a.astype(BF16), b.astype(BF16), None
    return lax.dot_general(a, b, (dims, ((), ())), preferred_element_type=F32, precision=prec)


def _dot(a, b, prec=None):
    return _dot_dims(a, b, ((1,), (0,)), prec)


def _dot_nt(a, b, prec=None):
    return _dot_dims(a, b, ((1,), (1,)), prec)


def _dot_tn(a, b, prec=None):
    return _dot_dims(a, b, ((0,), (0,)), prec)


def _rms(x, gain):
    return x * lax.rsqrt(jnp.mean(x * x, axis=-1, keepdims=True) + NORM_EPS) * gain


def _sigmoid(x):
    return 1.0 / (1.0 + jnp.exp(-x))


def _silu(x):
    return x * _sigmoid(x)


def _softplus(x):
    return jnp.maximum(x, 0.0) + jnp.log(1.0 + jnp.exp(-jnp.abs(x)))


def _iota2(shape, dim):
    return lax.broadcasted_iota(jnp.int32, shape, dim)


def _interleave(stage_generators):
    live = list(stage_generators)
    while live:
        for gen in list(live):
            try:
                next(gen)
            except StopIteration:
                live.remove(gen)


def _tri_inverse_stages(mats):
    n = mats[0].shape[0]
    row = _iota2((n, n), 0)
    col = _iota2((n, n), 1)
    same16 = (row >> 4) == (col >> 4)
    same32 = (row >> 5) == (col >> 5)
    eye = (row == col).astype(F32)
    ads = [jnp.where(same16, a, 0.0) for a in mats]
    xs = [eye - ad for ad in ads]
    ps = [_dot(ad, ad, PREC_INV) for ad in ads]
    yield
    for stage in range(3):
        xs = [x + _dot(x, p, PREC_INV) for x, p in zip(xs, ps)]
        if stage < 2:
            ps = [_dot(p, p, PREC_INV) for p in ps]
        yield
    for lower in ([jnp.where(same32 & jnp.logical_not(same16), a, 0.0) for a in mats],
                  [jnp.where(same32, 0.0, a) for a in mats]):
        ts = [_dot(l, x, PREC_INV) for l, x in zip(lower, xs)]
        yield
        xs = [x - _dot(x, t, PREC_INV) for x, t in zip(xs, ts)]
        yield
    return xs


def _norm_proj_kernel(n_w, x_ref, g_ref, *refs):
    w_refs, o_refs = refs[:n_w], refs[n_w:]
    xn = _rms(x_ref[...], g_ref[...]).astype(BF16)
    for w_ref, o_ref in zip(w_refs, o_refs):
        n = w_ref.shape[1]
        for c0 in range(0, n, 512):
            c1 = min(n, c0 + 512)
            o_ref[:, c0:c1] = _dot(xn, w_ref[:, c0:c1]).astype(o_ref.dtype)


def _norm_proj(x, gain, ws, out_dtype=F32, tm=512):
    n, d = x.shape
    tm = min(tm, n)
    return pl.pallas_call(
        functools.partial(_norm_proj_kernel, len(ws)),
        grid=(n // tm,),
        in_specs=[pl.BlockSpec((tm, d), lambda i: (i, 0)),
                  pl.BlockSpec((1, d), lambda i: (0, 0))]
                 + [pl.BlockSpec(w.shape, lambda i: (0, 0)) for w in ws],
        out_specs=[pl.BlockSpec((tm, w.shape[1]), lambda i: (i, 0)) for w in ws],
        out_shape=[jax.ShapeDtypeStruct((n, w.shape[1]), out_dtype) for w in ws],
        compiler_params=_cparams(("parallel",)),
        name="norm_proj",
    )(x, gain.reshape(1, d), *ws)


def _log_forget(f, log_lb, log1m_lb):
    a1 = jnp.broadcast_to(log_lb, f.shape)
    b1 = log1m_lb - _softplus(-f)
    return jnp.maximum(a1, b1) + jnp.log(1.0 + jnp.exp(-jnp.abs(a1 - b1)))


def _gdn_stages(nb, pa_ref, pdb_ref, convw_ref, alog_ref, dtb_ref, onorm_ref, o_ref, xbuf_ref, state_ref):
    n_heads = state_ref.shape[0] // nb
    hd = A_HEAD_DIM
    width = n_heads * hd
    row = _iota2((CHUNK, CHUNK), 0)
    col = _iota2((CHUNK, CHUNK), 1)
    incl = row >= col
    strict = row > col
    tril = incl.astype(F32)
    triu = (row <= col).astype(F32)

    qs, ks, vs, betas, gs, g_lasts, decays = [], [], [], [], [], [], []
    for bb in range(nb):
        x = pa_ref[bb, :, 0:3 * width]
        xbuf_ref[bb, 8:8 + CHUNK, :] = x
        acc = convw_ref[0:1, :] * xbuf_ref[bb, 5:5 + CHUNK, :]
        for j in range(1, CONV_WIDTH):
            acc = acc + convw_ref[j:j + 1, :] * xbuf_ref[bb, 5 + j:5 + j + CHUNK, :]
        xbuf_ref[bb, 0:8, :] = x[CHUNK - 8:CHUNK, :]
        qkv = _silu(acc)
        pdb = pdb_ref[bb]
        log_alpha = -jnp.exp(alog_ref[...]) * _softplus(pdb + dtb_ref[...])
        beta_all = _sigmoid(pdb)
        g_all = _dot(tril, log_alpha, PREC_CUM)
        g_all_t = _dot_tn(log_alpha, triu, PREC_CUM)
        for h in range(n_heads):
            q = qkv[:, h * hd:(h + 1) * hd]
            k = qkv[:, width + h * hd: width + (h + 1) * hd]
            qs.append(q * lax.rsqrt(jnp.sum(q * q, axis=-1, keepdims=True) + 1e-6) * (hd ** -0.5))
            ks.append(k * lax.rsqrt(jnp.sum(k * k, axis=-1, keepdims=True) + 1e-6))
            vs.append(qkv[:, 2 * width + h * hd: 2 * width + (h + 1) * hd])
            betas.append(beta_all[:, n_heads + h:n_heads + h + 1])
            g = g_all[:, h:h + 1]
            gs.append(g)
            g_lasts.append(g_all[CHUNK - 1:CHUNK, h:h + 1])
            decays.append(jnp.where(incl, jnp.exp(jnp.where(incl, g - g_all_t[h:h + 1, :], 0.0)), 0.0))
    yield
    idx = range(nb * n_heads)
    k_betas = [ks[i] * betas[i] for i in idx]
    egs = [jnp.exp(g) for g in gs]
    a_lows = [jnp.where(strict, _dot_nt(k_betas[i], ks[i], PREC_MIX) * decays[i], 0.0) for i in idx]
    qks = [_dot_nt(qs[i], ks[i], PREC_MIX) * decays[i] for i in idx]
    states = [state_ref[i] for i in idx]
    o_inters = [_dot(qs[i] * egs[i], states[i], PREC_MIX) for i in idx]
    yield
    t_invs = yield from _tri_inverse_stages(a_lows)
    us = [_dot(t_invs[i], vs[i] * betas[i], PREC_MIX) for i in idx]
    ws = [_dot(t_invs[i], k_betas[i] * egs[i], PREC_MIX) for i in idx]
    yield
    v_news = [us[i] - _dot(ws[i], states[i], PREC_MIX) for i in idx]
    yield
    os_ = [o_inters[i] + _dot(qks[i], v_news[i], PREC_MIX) for i in idx]
    for i in idx:
        state_ref[i] = (states[i] * jnp.exp(g_lasts[i])
                        + _dot_tn(ks[i] * jnp.exp(g_lasts[i] - gs[i]), v_news[i], PREC_MIX))
    yield
    for i in idx:
        bb, h = divmod(i, n_heads)
        gate = pa_ref[bb, :, 3 * width + h * hd: 3 * width + (h + 1) * hd]
        o_ref[bb, :, h * hd:(h + 1) * hd] = _rms(os_[i], onorm_ref[...]) * _silu(gate)


def _rwkv_stages(nb, col0, pb_ref, mu_ref, w0_ref, w2_ref, a0_ref, a2_ref, g2_ref, kk_ref, ka_ref, rk_ref,
                 lnw_ref, lnb_ref, bd_ref, o_ref, carry_ref, state_ref, obuf_ref):
    n_heads = state_ref.shape[0] // nb
    hd = B_HEAD_DIM
    width = n_heads * hd
    row = _iota2((CHUNK, CHUNK), 0)
    col = _iota2((CHUNK, CHUNK), 1)
    strict = row > col
    tril = (row >= col).astype(F32)
    incl2 = _iota2((CHUNK, 2 * CHUNK), 0) >= (_iota2((CHUNK, 2 * CHUNK), 1) & (CHUNK - 1))
    bd = bd_ref[...]

    pss = []
    for bb in range(nb):
        p = pb_ref[bb]
        prev = jnp.where(_iota2(p.shape, 0) == 0, carry_ref[bb, 0:1, :], pltpu.roll(p, 1, 0))
        carry_ref[bb, 0:1, :] = p[CHUNK - 1:CHUNK, :]
        pss.append(p + mu_ref[...] * (prev - p))
    rs = [ps[:, 0:width] for ps in pss]
    kraw = [ps[:, width:2 * width] for ps in pss]
    vfull = [ps[:, 2 * width:3 * width] for ps in pss]
    w_raws = [w0_ref[...] + _dot(jnp.tanh(ps[:, 3 * width:3 * width + LANES]), w2_ref[...], PREC_AUX) for ps in pss]
    iclrs = [_sigmoid(a0_ref[...] + _dot(ps[:, 3 * width + LANES:3 * width + 2 * LANES], a2_ref[...], PREC_AUX))
             for ps in pss]
    gates = [_dot(_sigmoid(ps[:, 3 * width + 2 * LANES:3 * width + 3 * LANES]), g2_ref[...], PREC_AUX) for ps in pss]
    kks = [k * kk_ref[...] for k in kraw]
    kk_ss = [_dot(kk * kk, bd, PREC_AUX) for kk in kks]
    yield
    log_ws = [-jnp.exp(-_softplus(-w_raw) - 0.5) for w_raw in w_raws]
    g_cums = [_dot(tril, log_w, PREC_CUM) for log_w in log_ws]
    yield
    lhss, rhss, bl_kls, p_cs, vhs, k2s = [], [], [], [], [], []
    for bb in range(nb):
        kk = kks[bb] * lax.rsqrt(kk_ss[bb] + 1e-6)
        k2 = kraw[bb] * (1.0 + (iclrs[bb] - 1.0) * ka_ref[...])
        k2s.append(k2)
        bvec = kk * iclrs[bb]
        g = g_cums[bb]
        g_last = g[CHUNK - 1:CHUNK, :]
        eng = jnp.exp(-g)
        egl = jnp.exp(g_last - g)
        r_t = rs[bb] * jnp.exp(g)
        kk_t = kk * jnp.exp(g - log_ws[bb])
        b_h, k_h = bvec * eng, k2 * eng
        b_l, k_l = bvec * egl, k2 * egl
        p_c = jnp.exp(g_last)
        for h in range(n_heads):
            sl = slice(h * hd, (h + 1) * hd)
            lhss.append(jnp.concatenate([kk_t[:, sl], r_t[:, sl]], axis=0))
            rhss.append(jnp.concatenate([b_h[:, sl], k_h[:, sl]], axis=0))
            bl_kls.append(jnp.concatenate([b_l[:, sl], k_l[:, sl]], axis=0))
            p_cs.append(p_c[:, sl])
            vhs.append(vfull[bb][:, sl])
    idx = range(nb * n_heads)
    ms = [_dot_nt(lhss[i], rhss[i], PREC_MIX) for i in idx]
    states = [state_ref[i] for i in idx]
    lss = [_dot_nt(lhss[i], states[i], PREC_MIX) for i in idx]
    yield
    a_bbs = [jnp.where(strict, m[0:CHUNK, 0:CHUNK], 0.0) for m in ms]
    a_bks = [jnp.where(strict, m[0:CHUNK, CHUNK:2 * CHUNK], 0.0) for m in ms]
    a_rs = [jnp.where(incl2, m[CHUNK:2 * CHUNK, :], 0.0) for m in ms]
    rhs_u = [lss[i][0:CHUNK] + _dot(a_bks[i], vhs[i], PREC_MIX) for i in idx]
    t_invs = yield from _tri_inverse_stages(a_bbs)
    uvs = [jnp.concatenate([-_dot(t_invs[i], rhs_u[i], PREC_MIX), vhs[i]], axis=0) for i in idx]
    yield
    for i in idx:
        bb, h = divmod(i, n_heads)
        obuf_ref[bb, :, h * hd:(h + 1) * hd] = lss[i][CHUNK:2 * CHUNK] + _dot(a_rs[i], uvs[i], PREC_MIX)
        state_ref[i] = states[i] * p_cs[i] + _dot_tn(uvs[i], bl_kls[i], PREC_MIX)
    yield
    inv_n = 1.0 / hd
    os_ = [obuf_ref[bb] for bb in range(nb)]
    means = [_dot(o, bd, PREC_AUX) * inv_n for o in os_]
    bonus = [_dot(rs[bb] * k2s[bb] * rk_ref[...], bd, PREC_AUX) * vfull[bb] for bb in range(nb)]
    yield
    cens = [o - mean for o, mean in zip(os_, means)]
    vars_ = [_dot(cen * cen, bd, PREC_AUX) * inv_n for cen in cens]
    yield
    for bb in range(nb):
        ln = cens[bb] * lax.rsqrt(vars_[bb] + GROUP_NORM_EPS) * lnw_ref[...] + lnb_ref[...]
        o_ref[bb, :, col0:col0 + width] = (ln + bonus[bb]) * gates[bb]


def _proj_stages(x_rows, gain_ref, w_refs, out_refs):
    nb = len(x_rows)
    xn = _rms(jnp.concatenate(x_rows, axis=0), gain_ref[...]).astype(BF16)
    yield
    for w_ref, out_ref in zip(w_refs, out_refs):
        n = w_ref.shape[1]
        for c0 in range(0, n, 512):
            c1 = min(n, c0 + 512)
            r = _dot(xn, w_ref[:, c0:c1])
            for bb in range(nb):
                out_ref[bb, :, c0:c1] = r[bb * CHUNK:(bb + 1) * CHUNK]
            yield


def _mixer_ab_kernel(n_gdn, n_rwkv, xc_ref, xn_ref, gain_ref, wa_ref, wdb_ref, wb_ref, *refs):
    gdn_params = refs[:n_gdn]
    rwkv_params = refs[n_gdn:n_gdn + n_rwkv]
    o_ref, xbuf_ref, gstate_ref, carry_ref, rstate_ref, obuf_ref = refs[n_gdn + n_rwkv:n_gdn + n_rwkv + 6]
    bufs = refs[n_gdn + n_rwkv + 6:]
    proj = (bufs[0:3], bufs[3:6])
    w_refs = (wa_ref, wdb_ref, wb_ref)
    nb = xc_ref.shape[0]
    a_width = wa_ref.shape[1] // 4

    @pl.when(pl.program_id(1) == 0)
    def _():
        xbuf_ref[:, 0:8, :] = jnp.zeros((nb, 8, xbuf_ref.shape[2]), F32)
        gstate_ref[...] = jnp.zeros_like(gstate_ref)
        carry_ref[...] = jnp.zeros_like(carry_ref)
        rstate_ref[...] = jnp.zeros_like(rstate_ref)
        for _ in _proj_stages([xc_ref[bb, 0:CHUNK, :] for bb in range(nb)], gain_ref, w_refs, proj[0]):
            pass

    for half in range(2):
        pa_ref, pdb_ref, pb_ref = proj[half]
        if half == 0:
            x_next = [xc_ref[bb, CHUNK:2 * CHUNK, :] for bb in range(nb)]
        else:
            x_next = [xn_ref[bb, 0:CHUNK, :] for bb in range(nb)]
        out = o_ref.at[:, half * CHUNK:(half + 1) * CHUNK, :]
        _interleave([
            _gdn_stages(nb, pa_ref, pdb_ref, *gdn_params, out, xbuf_ref, gstate_ref),
            _rwkv_stages(nb, a_width, pb_ref, *rwkv_params, out, carry_ref, rstate_ref, obuf_ref),
            _proj_stages(x_next, gain_ref, w_refs, proj[1 - half]),
        ])


def _mixer_ab(x, gain, w_a, w_db, w_b, gdn_params, rwkv_params):
    b, t, d = x.shape
    a_cols, b_cols = w_a.shape[1], w_b.shape[1]
    a_width = a_cols // 4
    b_width = (b_cols - 3 * LANES) // 3
    a_heads = a_width // A_HEAD_DIM
    b_heads = b_width // B_HEAD_DIM
    nb = MIXER_BATCH if b % MIXER_BATCH == 0 else 1
    n_steps = t // (2 * CHUNK)
    params = list(gdn_params) + list(rwkv_params)
    proj_bufs = [pltpu.VMEM((nb, CHUNK, a_cols), F32), pltpu.VMEM((nb, CHUNK, LANES), F32),
                 pltpu.VMEM((nb, CHUNK, b_cols), F32)]
    return pl.pallas_call(
        functools.partial(_mixer_ab_kernel, len(gdn_params), len(rwkv_params)),
        grid=(b // nb, n_steps),
        in_specs=[pl.BlockSpec((nb, 2 * CHUNK, d), lambda i, s: (i, s, 0)),
                  pl.BlockSpec((nb, 2 * CHUNK, d), lambda i, s: (i, jnp.minimum(s + 1, n_steps - 1), 0)),
                  pl.BlockSpec((1, d), lambda i, s: (0, 0))]
                 + [pl.BlockSpec(a.shape, lambda i, s: (0, 0)) for a in [w_a, w_db, w_b] + params],
        out_specs=pl.BlockSpec((nb, 2 * CHUNK, a_width + b_width), lambda i, s: (i, s, 0)),
        out_shape=jax.ShapeDtypeStruct((b, t, a_width + b_width), F32),
        scratch_shapes=[pltpu.VMEM((nb, 8 + CHUNK, 3 * a_width), F32),
                        pltpu.VMEM((nb * a_heads, A_HEAD_DIM, A_HEAD_DIM), F32),
                        pltpu.VMEM((nb, 8, b_cols), F32),
                        pltpu.VMEM((nb * b_heads, B_HEAD_DIM, B_HEAD_DIM), F32),
                        pltpu.VMEM((nb, CHUNK, b_width), F32)] + proj_bufs + proj_bufs,
        compiler_params=_cparams(("parallel", "arbitrary")),
        name="mixer_ab",
    )(x, x, gain.reshape(1, d), w_a, w_db, w_b, *params)


def _hgrn_tables():
    i = np.arange(CHUNK)[:, None]
    t = np.arange(CHUNK)[None, :]
    blocks, pair = [], []
    for ls in range(HGRN_LEVELS - 1, -1, -1):
        s = 1 << ls
        mid = (i >> (ls + 1) << (ls + 1)) + s
        q_rows, k_rows = _hgrn_level_rows(ls)
        blocks.append(((i >= mid) & (t >= mid) & (t <= i))[q_rows])
        blocks.append(((i < mid) & (t > i) & (t <= mid - 1))[k_rows])
        pair.append((((i >> ls) ^ (t >> ls)) == 1) & (i > t))
    blocks.append(t <= i)
    blocks.append(t > i)
    sums = np.concatenate(blocks, axis=0).astype(np.float32)
    return np.concatenate([sums] * 3, axis=1), np.stack(pair).astype(np.float32)


def _hgrn_level_rows(ls):
    rows = np.arange(CHUNK)
    if (1 << ls) < SUBLANES:
        return rows, rows
    upper = ((rows >> ls) & 1) == 1
    return rows[upper], rows[~upper]


def _hgrn_stages(p_ref, loglb_ref, log1mlb_ref, onorm_ref, sums_ref, pair_ref, o_ref, state_ref):
    nb = p_ref.shape[0]
    n_heads = state_ref.shape[0] // nb
    hd = C_KEY_DIM
    kw = n_heads * hd

    def take_rows(a, rows):
        runs, start = [], 0
        for n in range(1, len(rows) + 1):
            if n == len(rows) or rows[n] != rows[n - 1] + 1:
                runs.append(a[int(rows[start]):int(rows[n - 1]) + 1])
                start = n
        return runs[0] if len(runs) == 1 else jnp.concatenate(runs, axis=0)

    def spread_rows(a, rows):
        if len(rows) == CHUNK:
            return a
        pieces, pos, taken = [], 0, 0
        while pos < CHUNK:
            run = 1
            inside = pos in rows
            while pos + run < CHUNK and ((pos + run) in rows) == inside:
                run += 1
            if inside:
                pieces.append(a[taken:taken + run])
                taken += run
            else:
                pieces.append(jnp.zeros((run, a.shape[1]), a.dtype))
            pos += run
        return jnp.concatenate(pieces, axis=0)

    qs, ks, exps = [], [], []
    for bb in range(nb):
        qs.append(_silu(p_ref[bb, :, 0:kw]))
        log_f = _log_forget(p_ref[bb, :, kw:2 * kw], loglb_ref[...], log1mlb_ref[...])
        ks.append(1.0 - jnp.exp(log_f))
        f1 = log_f.astype(BF16)
        f2 = (log_f - f1.astype(F32)).astype(BF16)
        f3 = (log_f - f1.astype(F32) - f2.astype(F32)).astype(BF16)
        exps.append(_dot(sums_ref[...], jnp.concatenate([f1, f2, f3], axis=0)))
        yield

    idx = [(bb, h) for bb in range(nb) for h in range(n_heads)]
    states = [state_ref[bb * n_heads + h] for bb, h in idx]
    scores = [None] * len(idx)
    offset = 0
    for lvl in range(HGRN_LEVELS):
        q_rows, k_rows = _hgrn_level_rows(HGRN_LEVELS - 1 - lvl)
        q_rows, k_rows = [int(r) for r in q_rows], [int(r) for r in k_rows]
        e_q = [e[offset:offset + len(q_rows)] for e in exps]
        e_k = [e[offset + len(q_rows):offset + len(q_rows) + len(k_rows)] for e in exps]
        offset += len(q_rows) + len(k_rows)
        q_l = [(take_rows(qs[bb], q_rows) * jnp.exp(e_q[bb])).astype(BF16) for bb in range(nb)]
        k_l = [spread_rows(take_rows(ks[bb], k_rows) * jnp.exp(e_k[bb]), k_rows).astype(BF16) for bb in range(nb)]
        mask = pair_ref[lvl]
        for n, (bb, h) in enumerate(idx):
            sl = slice(h * hd, (h + 1) * hd)
            part = spread_rows(_dot_nt(q_l[bb][:, sl], k_l[bb][:, sl]), q_rows) * mask
            scores[n] = part if scores[n] is None else scores[n] + part
        yield
    g_cum = [e[offset:offset + CHUNK] for e in exps]
    q_g = [qs[bb] * jnp.exp(g_cum[bb]) for bb in range(nb)]
    k_r = [ks[bb] * jnp.exp(exps[bb][offset + CHUNK:offset + 2 * CHUNK]) for bb in range(nb)]
    diag = [qs[bb] * ks[bb] for bb in range(nb)]
    outs = []
    for n, (bb, h) in enumerate(idx):
        sl = slice(h * hd, (h + 1) * hd)
        v = p_ref[bb, :, 2 * kw + h * hd: 2 * kw + (h + 1) * hd]
        o = _dot_nt(q_g[bb][:, sl], states[n], PREC_MIX) + _dot(scores[n], v, PREC_MIX)
        outs.append(o + jnp.sum(diag[bb][:, sl], axis=-1, keepdims=True) * v)
        p_last = jnp.exp(g_cum[bb][CHUNK - 1:CHUNK, sl])
        state_ref[bb * n_heads + h] = states[n] * p_last + _dot_tn(v, k_r[bb][:, sl], PREC_MIX)
    yield
    for n, (bb, h) in enumerate(idx):
        gate = p_ref[bb, :, 3 * kw + h * hd: 3 * kw + (h + 1) * hd]
        o_ref[bb, :, h * hd:(h + 1) * hd] = _rms(outs[n], onorm_ref[...]) * _silu(gate)


def _hgrn_kernel(xc_ref, xn_ref, gain_ref, w_ref, loglb_ref, log1mlb_ref, onorm_ref, sums_ref, pair_ref,
                 o_ref, state_ref, proj0_ref, proj1_ref):
    nb = xc_ref.shape[0]
    proj = (proj0_ref, proj1_ref)

    @pl.when(pl.program_id(1) == 0)
    def _():
        state_ref[...] = jnp.zeros_like(state_ref)
        for _ in _proj_stages([xc_ref[bb, 0:CHUNK, :] for bb in range(nb)], gain_ref, [w_ref], [proj[0]]):
            pass

    for half in range(2):
        if half == 0:
            x_next = [xc_ref[bb, CHUNK:2 * CHUNK, :] for bb in range(nb)]
        else:
            x_next = [xn_ref[bb, 0:CHUNK, :] for bb in range(nb)]
        _interleave([
            _hgrn_stages(proj[half], loglb_ref, log1mlb_ref, onorm_ref, sums_ref, pair_ref,
                         o_ref.at[:, half * CHUNK:(half + 1) * CHUNK, :], state_ref),
            _proj_stages(x_next, gain_ref, [w_ref], [proj[1 - half]]),
        ])


def _hgrn(x, gain, w_in, log_lb, log1m_lb, onorm):
    b, t, d = x.shape
    cols = w_in.shape[1]
    kw = cols // 4
    n_heads = kw // C_KEY_DIM
    nb = MIXER_BATCH if b % MIXER_BATCH == 0 else 1
    n_steps = t // (2 * CHUNK)
    sums, pair = _hgrn_tables()
    sums = jnp.asarray(sums, BF16)
    pair = jnp.asarray(pair, F32)
    return pl.pallas_call(
        _hgrn_kernel,
        grid=(b // nb, n_steps),
        in_specs=[pl.BlockSpec((nb, 2 * CHUNK, d), lambda i, s: (i, s, 0)),
                  pl.BlockSpec((nb, 2 * CHUNK, d), lambda i, s: (i, jnp.minimum(s + 1, n_steps - 1), 0)),
                  pl.BlockSpec((1, d), lambda i, s: (0, 0)),
                  pl.BlockSpec(w_in.shape, lambda i, s: (0, 0)),
                  pl.BlockSpec((1, kw), lambda i, s: (0, 0)),
                  pl.BlockSpec((1, kw), lambda i, s: (0, 0)),
                  pl.BlockSpec((1, C_KEY_DIM), lambda i, s: (0, 0)),
                  pl.BlockSpec(sums.shape, lambda i, s: (0, 0)),
                  pl.BlockSpec(pair.shape, lambda i, s: (0, 0, 0))],
        out_specs=pl.BlockSpec((nb, 2 * CHUNK, kw), lambda i, s: (i, s, 0)),
        out_shape=jax.ShapeDtypeStruct((b, t, kw), F32),
        scratch_shapes=[pltpu.VMEM((nb * n_heads, C_KEY_DIM, C_KEY_DIM), F32),
                        pltpu.VMEM((nb, CHUNK, cols), F32),
                        pltpu.VMEM((nb, CHUNK, cols), F32)],
        compiler_params=_cparams(("parallel", "arbitrary")),
        name="hgrn2",
    )(x, x, gain.reshape(1, d), w_in, log_lb, log1m_lb, onorm, sums, pair)


def _route(logits):
    lane = _iota2(logits.shape, 1)
    neg = -jnp.inf
    gmask = lane < N_GROUPS
    gl = jnp.where(gmask, logits, neg)
    gmax = jnp.max(gl, axis=-1, keepdims=True)
    gsel = jnp.min(jnp.where(gl == gmax, lane, LANES), axis=-1, keepdims=True)
    group_gate = 1.0 / jnp.sum(jnp.where(gmask, jnp.exp(gl - gmax), 0.0), axis=-1, keepdims=True)
    lo = N_GROUPS + EXPERTS_PER_GROUP * gsel
    emask = (lane >= lo) & (lane < lo + EXPERTS_PER_GROUP)
    el = jnp.where(emask, logits, neg)
    ee = jnp.where(emask, jnp.exp(el - jnp.max(el, axis=-1, keepdims=True)), 0.0)
    prob = ee / jnp.sum(ee, axis=-1, keepdims=True)
    p1 = jnp.max(jnp.where(emask, prob, -1.0), axis=-1, keepdims=True)
    i1 = jnp.min(jnp.where(emask & (prob == p1), lane, LANES), axis=-1, keepdims=True)
    rest = emask & (lane != i1)
    p2 = jnp.max(jnp.where(rest, prob, -1.0), axis=-1, keepdims=True)
    i2 = jnp.min(jnp.where(rest & (prob == p2), lane, LANES), axis=-1, keepdims=True)
    scale = group_gate / (p1 + p2)
    comb = jnp.where(lane == i1, p1 * scale, 0.0) + jnp.where(lane == i2, p2 * scale, 0.0)
    return comb, gsel


def _sorted_rows(tile_rows):
    return tile_rows + N_GROUPS * SEG_ALIGN


def _xattn_route_kernel(x_ref, a_ref, wmix_ref, gain_ref, wq_ref, k_ref, v_ref, wo_ref, gffn_ref, wr_ref, br_ref,
                        x2_ref, xs_ref, cs_ref, info_ref, seg_ref):
    n_sub = xs_ref.shape[0]
    tq = x_ref.shape[0] // n_sub
    d = x_ref.shape[1]
    hd = d // XA_HEADS
    rows = xs_ref.shape[1]

    def tile_stages(u):
        tok = slice(u * tq, (u + 1) * tq)
        x = x_ref[tok, :] + _dot(a_ref[tok, :].astype(BF16), wmix_ref[...])
        q = _dot(_rms(x, gain_ref[...]).astype(BF16), wq_ref[...]).astype(BF16)
        yield
        outs = []
        for h in range(XA_HEADS):
            sl = slice(h * hd, (h + 1) * hd)
            s = _dot_nt(q[:, sl], k_ref[:, sl]) * (hd ** -0.5)
            p = jnp.exp(s - jnp.max(s, axis=-1, keepdims=True))
            p = p / jnp.sum(p, axis=-1, keepdims=True)
            outs.append(_dot(p.astype(BF16), v_ref[:, sl]).astype(BF16))
        yield
        x2 = x + _dot(jnp.concatenate(outs, axis=-1), wo_ref[...])
        x2_ref[tok, :] = x2
        xn = _rms(x2, gffn_ref[...])
        xn_hi = xn.astype(BF16)
        xn_lo = (xn - xn_hi.astype(F32)).astype(BF16)
        hi_part = _dot(xn_hi, wr_ref[...])
        logits = hi_part[:, 0:LANES] + hi_part[:, LANES:2 * LANES] + _dot(xn_lo, wr_ref[:, 0:LANES])
        yield
        comb, gsel = _route(logits + br_ref[...])
        lane = _iota2((tq, LANES), 1)
        onehot = (lane == gsel).astype(F32)
        earlier = (_iota2((tq, tq), 0) > _iota2((tq, tq), 1)).astype(BF16)
        before = _dot(earlier, onehot.astype(BF16))
        yield
        rank = jnp.sum(onehot * before, axis=-1, keepdims=True)
        counts = before[tq - 1:tq, :] + onehot[tq - 1:tq, :]
        padded = jnp.floor((counts + (SEG_ALIGN - 1)) * (1.0 / SEG_ALIGN)) * SEG_ALIGN
        lane1 = _iota2((1, LANES), 1)
        starts = jnp.zeros((1, LANES), F32)
        for g in range(N_GROUPS - 1):
            starts = starts + jnp.where(lane1 > g, padded[:, g:g + 1], 0.0)
        dest = jnp.sum(onehot * starts, axis=-1, keepdims=True) + rank
        place = (_iota2((tq, rows), 1).astype(F32) == dest).astype(BF16)
        comb_hi = comb.astype(BF16)
        comb_lo = (comb - comb_hi.astype(F32)).astype(BF16)
        placed = _dot_tn(place, jnp.concatenate([xn_hi, comb_hi, comb_lo], axis=1))
        xs_ref[u] = placed[:, 0:d].astype(BF16)
        cs_ref[u] = placed[:, d:d + LANES] + placed[:, d + LANES:d + 2 * LANES]
        info_ref[tok, :] = jnp.where(lane == 0, dest, 0.0)
        seg_ref[u] = jnp.concatenate([starts, padded, jnp.zeros((6, LANES), F32)], axis=0)

    _interleave([tile_stages(u) for u in range(n_sub)])


def _xattn_route(x, act, w_mix, gain, wq, k, v, wo, gain_ffn, w_router, b_router, tq=512):
    b, t, d = x.shape
    m = k.shape[1]
    tq = min(tq, t)
    tiles_per_row = t // tq
    n_tiles = b * tiles_per_row
    rows = _sorted_rows(tq)
    n_sub = XATTN_TILES_PER_STEP if tiles_per_row % XATTN_TILES_PER_STEP == 0 else 1
    steps_per_row = tiles_per_row // n_sub

    def tile(i, j):
        return (i * steps_per_row + j, 0, 0)

    return pl.pallas_call(
        _xattn_route_kernel,
        grid=(b, steps_per_row),
        in_specs=[pl.BlockSpec((None, n_sub * tq, d), lambda i, j: (i, j, 0)),
                  pl.BlockSpec((None, n_sub * tq, act.shape[2]), lambda i, j: (i, j, 0)),
                  pl.BlockSpec(w_mix.shape, lambda i, j: (0, 0)),
                  pl.BlockSpec((1, d), lambda i, j: (0, 0)),
                  pl.BlockSpec((d, d), lambda i, j: (0, 0)),
                  pl.BlockSpec((None, m, d), lambda i, j: (i, 0, 0)),
                  pl.BlockSpec((None, m, d), lambda i, j: (i, 0, 0)),
                  pl.BlockSpec((d, d), lambda i, j: (0, 0)),
                  pl.BlockSpec((1, d), lambda i, j: (0, 0)),
                  pl.BlockSpec((d, 2 * LANES), lambda i, j: (0, 0)),
                  pl.BlockSpec((1, LANES), lambda i, j: (0, 0))],
        out_specs=[pl.BlockSpec((None, n_sub * tq, d), lambda i, j: (i, j, 0)),
                   pl.BlockSpec((n_sub, rows, d), tile),
                   pl.BlockSpec((n_sub, rows, LANES), tile),
                   pl.BlockSpec((None, n_sub * tq, LANES), lambda i, j: (i, j, 0)),
                   pl.BlockSpec((n_sub, 8, LANES), tile)],
        out_shape=[jax.ShapeDtypeStruct((b, t, d), F32),
                   jax.ShapeDtypeStruct((n_tiles, rows, d), BF16),
                   jax.ShapeDtypeStruct((n_tiles, rows, LANES), F32),
                   jax.ShapeDtypeStruct((b, t, LANES), F32),
                   jax.ShapeDtypeStruct((n_tiles, 8, LANES), F32)],
        compiler_params=_cparams(("parallel", "parallel")),
        name="xattn_route",
    )(x, act, w_mix, gain.reshape(1, d), wq, k, v, wo, gain_ffn.reshape(1, d), w_router, b_router)


def _moe_ffn_kernel(seg_ref, xs_ref, cs_ref, w1_ref, w3_ref, w2_ref, ys_ref, xbuf_ref, cbuf_ref, ybuf_ref):
    win = pl.program_id(0)
    grp = pl.program_id(1)
    n_win = xs_ref.shape[0]

    @pl.when((win == 0) & (grp == 0))
    def _():
        xbuf_ref[...] = jnp.zeros_like(xbuf_ref)
        cbuf_ref[...] = jnp.zeros_like(cbuf_ref)

    @pl.when(grp == 0)
    def _():
        ys_ref[...] = jnp.zeros_like(ys_ref)

    def segment(t):
        base = (win * n_win + t) * 8
        return seg_ref[base + grp], seg_ref[base + N_GROUPS + grp]

    def copy_segments(copy_piece):
        packed = jnp.int32(0)
        for t in range(n_win):
            start, size = segment(t)

            def body(j, carry, t=t, start=start, packed=packed):
                copy_piece(t, pl.multiple_of(start + j * SEG_ALIGN, SEG_ALIGN),
                           pl.multiple_of(packed + j * SEG_ALIGN, SEG_ALIGN))
                return carry

            lax.fori_loop(0, size // SEG_ALIGN, body, 0)
            packed = packed + size
        return packed

    def gather(t, src, dst):
        xbuf_ref[pl.ds(dst, SEG_ALIGN), :] = xs_ref[t, pl.ds(src, SEG_ALIGN), :]
        cbuf_ref[pl.ds(dst, SEG_ALIGN), :] = cs_ref[t, pl.ds(src, SEG_ALIGN), :]

    total = copy_segments(gather)

    def ffn_piece(r0, n_rows):
        xc = xbuf_ref[pl.ds(r0, n_rows), :]
        cb = cbuf_ref[pl.ds(r0, n_rows), :]
        lane = _iota2(cb.shape, 1)
        hids = []
        for e in range(EXPERTS_PER_GROUP):
            weight = jnp.sum(jnp.where(lane == N_GROUPS + EXPERTS_PER_GROUP * grp + e, cb, 0.0),
                             axis=-1, keepdims=True)
            hids.append((_silu(_dot(xc, w1_ref[e])) * _dot(xc, w3_ref[e]) * weight).astype(BF16))
        ybuf_ref[pl.ds(r0, n_rows), :] = _dot(jnp.concatenate(hids, axis=-1), w2_ref[...]).astype(ybuf_ref.dtype)

    def chunk(c, carry):
        ffn_piece(pl.multiple_of(c * MOE_CHUNK, MOE_CHUNK), MOE_CHUNK)
        return carry

    n_full = total // MOE_CHUNK
    lax.fori_loop(0, n_full, chunk, 0)
    rest = total - n_full * MOE_CHUNK
    rest_start = pl.multiple_of(n_full * MOE_CHUNK, MOE_CHUNK)

    @pl.when((rest > 0) & (rest <= MOE_CHUNK // 2))
    def _():
        ffn_piece(rest_start, MOE_CHUNK // 2)

    @pl.when(rest > MOE_CHUNK // 2)
    def _():
        ffn_piece(rest_start, MOE_CHUNK)

    def scatter(t, dst, src):
        ys_ref[t, pl.ds(dst, SEG_ALIGN), :] = ybuf_ref[pl.ds(src, SEG_ALIGN), :]

    copy_segments(scatter)


def _moe_ffn(seg, xs, cs, w1, w3, w2g):
    n_tiles, rows, d = xs.shape
    ff = w1.shape[2]
    n_win = next(w for w in (MOE_WINDOW, 2, 1) if n_tiles % w == 0)
    buf_rows = -(-(n_win * rows) // MOE_CHUNK) * MOE_CHUNK
    grid_spec = pltpu.PrefetchScalarGridSpec(
        num_scalar_prefetch=1,
        grid=(n_tiles // n_win, N_GROUPS),
        in_specs=[pl.BlockSpec((n_win, rows, d), lambda w, g, seg: (w, 0, 0)),
                  pl.BlockSpec((n_win, rows, LANES), lambda w, g, seg: (w, 0, 0)),
                  pl.BlockSpec((EXPERTS_PER_GROUP, d, ff), lambda w, g, seg: (g, 0, 0)),
                  pl.BlockSpec((EXPERTS_PER_GROUP, d, ff), lambda w, g, seg: (g, 0, 0)),
                  pl.BlockSpec((None, EXPERTS_PER_GROUP * ff, d), lambda w, g, seg: (g, 0, 0))],
        out_specs=pl.BlockSpec((n_win, rows, d), lambda w, g, seg: (w, 0, 0)),
        scratch_shapes=[pltpu.VMEM((buf_rows, d), BF16),
                        pltpu.VMEM((buf_rows, LANES), F32),
                        pltpu.VMEM((buf_rows, d), BF16)])
    return pl.pallas_call(
        _moe_ffn_kernel,
        grid_spec=grid_spec,
        out_shape=jax.ShapeDtypeStruct((n_tiles, rows, d), BF16),
        compiler_params=_cparams(("arbitrary", "arbitrary")),
        name="moe_ffn",
    )(seg, xs, cs, w1, w3, w2g)


def _moe_unsort_kernel(final_norm, x_ref, info_ref, ys_ref, gfin_ref, o_ref):
    info = info_ref[...]
    dest = jnp.sum(jnp.where(_iota2(info.shape, 1) == 0, info, 0.0), axis=-1, keepdims=True)
    rows = ys_ref.shape[0]
    place = (_iota2((info.shape[0], rows), 1).astype(F32) == dest).astype(BF16)
    y = x_ref[...] + _dot(place, ys_ref[...])
    if final_norm:
        y = _rms(y, gfin_ref[...])
    o_ref[...] = y


def _moe_unsort(x, info, ys, gain_final, final_norm):
    b, t, d = x.shape
    n_tiles, rows, _ = ys.shape
    tiles_per_row = n_tiles // b
    tq = t // tiles_per_row
    return pl.pallas_call(
        functools.partial(_moe_unsort_kernel, final_norm),
        grid=(b, tiles_per_row),
        in_specs=[pl.BlockSpec((None, tq, d), lambda i, j: (i, j, 0)),
                  pl.BlockSpec((None, tq, LANES), lambda i, j: (i, j, 0)),
                  pl.BlockSpec((None, rows, d), lambda i, j: (i * tiles_per_row + j, 0, 0)),
                  pl.BlockSpec((1, d), lambda i, j: (0, 0))],
        out_specs=pl.BlockSpec((None, tq, d), lambda i, j: (i, j, 0)),
        out_shape=jax.ShapeDtypeStruct((b, t, d), F32),
        compiler_params=_cparams(("parallel", "parallel")),
        name="moe_unsort",
    )(x, info, ys, gain_final.reshape(1, d))


def _pad_cols(w, n):
    return jnp.pad(w, ((0, 0), (0, n - w.shape[1])))


def _pad_rows(w, n):
    return jnp.pad(w, ((0, n - w.shape[0]), (0, 0)))


def _row(v, n=None):
    v = v.reshape(1, -1).astype(F32)
    return v if n is None else _pad_cols(v, n)


def kernel(x, mem, norm_mix, norm_xattn, norm_mem, norm_ffn, norm_final, ab_w_in, ab_conv, ab_a_log, ab_dt_bias, ab_onorm, ab_shift_mu, ab_w0, ab_w2, ab_a0, ab_a2, ab_g2, ab_k_k, ab_k_a, ab_r_k, ab_ln_w, ab_ln_b, ab_w_out, c_w_in, c_lb_param, c_onorm, c_w_out, xa_wq, xa_wkv, xa_wo, moe_w_group, moe_b_group, moe_w_expert, moe_b_expert, moe_w1, moe_w3, moe_w2):
    bsz, seq, d = x.shape
    n_mem = mem.shape[1]
    depth = norm_mix.shape[0]
    a_width = d // 2
    b_width = d // 2
    a_heads = a_width // A_HEAD_DIM
    a_cols = 4 * a_width + 2 * a_heads
    n_tok = bsz * seq

    lb_p = jax.nn.softmax(c_lb_param.astype(F32), axis=0)
    lower_bounds = jnp.cumsum(lb_p, axis=0) - lb_p[0]

    x2 = x.reshape(n_tok, d)
    mem2 = mem.reshape(bsz * n_mem, d)
    for layer in range(depth):
        if layer % 2 == 0:
            e = layer // 2
            w_in = ab_w_in[e]
            w_a = w_in[:, :4 * a_width].astype(BF16)
            w_db = _pad_cols(w_in[:, 4 * a_width:a_cols], LANES).astype(BF16)
            wb = w_in[:, a_cols:]
            o1, o2 = 3 * b_width, 3 * b_width + B_DECAY_LORA
            o3 = o2 + B_ICLR_LORA
            w_b = jnp.concatenate([wb[:, :o1], _pad_cols(wb[:, o1:o2], LANES), _pad_cols(wb[:, o2:o3], LANES),
                                   wb[:, o3:]], axis=1).astype(BF16)
            mu = ab_shift_mu[e].reshape(1, -1)
            mu_p = jnp.concatenate([mu[:, :o1], _pad_cols(mu[:, o1:o2], LANES), _pad_cols(mu[:, o2:o3], LANES),
                                    mu[:, o3:]], axis=1)
            head_id = jnp.arange(b_width) // B_HEAD_DIM
            bd = (head_id[:, None] == head_id[None, :]).astype(F32)
            gdn_params = [ab_conv[e], _row(ab_a_log[e], LANES), _row(ab_dt_bias[e], LANES), _row(ab_onorm[e])]
            rwkv_params = [mu_p, _row(ab_w0[e]), _pad_rows(ab_w2[e], LANES), _row(ab_a0[e]),
                           _pad_rows(ab_a2[e], LANES), ab_g2[e], _row(ab_k_k[e]), _row(ab_k_a[e]),
                           _row(ab_r_k[e]), _row(ab_ln_w[e]), _row(ab_ln_b[e]), bd]
            mix = _mixer_ab(x2.reshape(bsz, seq, d), norm_mix[layer], w_a, w_db, w_b, gdn_params, rwkv_params)
            w_mix = ab_w_out[e].astype(BF16)
        else:
            o = layer // 2
            lb = lower_bounds[layer].reshape(1, -1)
            mix = _hgrn(x2.reshape(bsz, seq, d), norm_mix[layer], c_w_in[o].astype(BF16),
                        jnp.log(lb), jnp.log1p(-lb), _row(c_onorm[o]))
            w_mix = c_w_out[o].astype(BF16)

        wkv = xa_wkv[layer].astype(BF16)
        k_mem, v_mem = _norm_proj(mem2, norm_mem[layer], [wkv[:, :d], wkv[:, d:]], out_dtype=BF16)
        w_r = _pad_cols(jnp.concatenate([moe_w_group[layer], moe_w_expert[layer]], axis=1), LANES).astype(F32)
        w_r_hi = w_r.astype(BF16)
        w_r_lo = (w_r - w_r_hi.astype(F32)).astype(BF16)
        b_router = _row(jnp.concatenate([moe_b_group[layer], moe_b_expert[layer]]), LANES)
        x3, xs, cs, info, seg = _xattn_route(
            x2.reshape(bsz, seq, d), mix, w_mix, norm_xattn[layer], xa_wq[layer].astype(BF16),
            k_mem.reshape(bsz, n_mem, d), v_mem.reshape(bsz, n_mem, d), xa_wo[layer].astype(BF16),
            norm_ffn[layer], jnp.concatenate([w_r_hi, w_r_lo], axis=1), b_router)
        seg_i = jnp.concatenate([seg[:, 0, :N_GROUPS], seg[:, 1, :N_GROUPS]], axis=1).astype(jnp.int32).reshape(-1)
        ff = moe_w2.shape[2]
        ys = _moe_ffn(seg_i, xs, cs, moe_w1[layer].astype(BF16), moe_w3[layer].astype(BF16),
                      moe_w2[layer].reshape(N_GROUPS, EXPERTS_PER_GROUP * ff, d).astype(BF16))
        x2 = _moe_unsort(x3, info, ys, norm_final, final_norm=(layer == depth - 1)).reshape(n_tok, d)
    return x2.reshape(bsz, seq, d)
```

```python
import functools

import jax
import jax.numpy as jnp
import numpy as np
from jax import lax
from jax.experimental import pallas as pl
from jax.experimental.pallas import tpu as pltpu

F32 = jnp.float32
BF16 = jnp.bfloat16
HI = lax.Precision.HIGHEST
ONE_PASS = "bf16"

PREC_CUM = HI
PREC_INV = ONE_PASS
PREC_MIX = ONE_PASS
PREC_AUX = ONE_PASS

CHUNK = 64
HGRN_LEVELS = CHUNK.bit_length() - 1
NORM_EPS = 1e-6
GROUP_NORM_EPS = 64e-5
CONV_WIDTH = 4

A_HEAD_DIM = 128
B_HEAD_DIM = 64
B_DECAY_LORA = 64
B_ICLR_LORA = 64
B_GATE_LORA = 128
C_KEY_DIM = 128
XA_HEADS = 4
N_GROUPS = 4
EXPERTS_PER_GROUP = 4
N_EXPERTS = N_GROUPS * EXPERTS_PER_GROUP

LANES = 128
SUBLANES = 8
VMEM_LIMIT = 56 * 1024 * 1024
MIXER_BATCH = 4
SEG_ALIGN = 16
MOE_WINDOW = 4
MOE_CHUNK = 256
XATTN_TILES_PER_STEP = 2


def _cparams(sem):
    return pltpu.CompilerParams(dimension_semantics=sem, vmem_limit_bytes=VMEM_LIMIT)


def _dot_dims(a, b, dims, prec):
    if prec == ONE_PASS:
        a, b, prec = a.astype(BF16), b.astype(BF16), None
    return lax.dot_general(a, b, (dims, ((), ())), preferred_element_type=F32, precision=prec)


def _dot(a, b, prec=None):
    return _dot_dims(a, b, ((1,), (0,)), prec)


def _dot_nt(a, b, prec=None):
    return _dot_dims(a, b, ((1,), (1,)), prec)


def _dot_tn(a, b, prec=None):
    return _dot_dims(a, b, ((0,), (0,)), prec)


def _rms(x, gain):
    return x * lax.rsqrt(jnp.mean(x * x, axis=-1, keepdims=True) + NORM_EPS) * gain


def _sigmoid(x):
    return 1.0 / (1.0 + jnp.exp(-x))


def _silu(x):
    return x * _sigmoid(x)


def _softplus(x):
    return jnp.maximum(x, 0.0) + jnp.log(1.0 + jnp.exp(-jnp.abs(x)))


def _iota2(shape, dim):
    return lax.broadcasted_iota(jnp.int32, shape, dim)


def _interleave(stage_generators):
    live = list(stage_generators)
    while live:
        for gen in list(live):
            try:
                next(gen)
            except StopIteration:
                live.remove(gen)


def _tri_inverse_stages(mats):
    n = mats[0].shape[0]
    row = _iota2((n, n), 0)
    col = _iota2((n, n), 1)
    eye = (row == col).astype(F32)
    xs = [eye - jnp.where((row >> 1) == (col >> 1), a, 0.0) for a in mats]
    for level in range(1, n.bit_length() - 1):
        joins = ((row >> (level + 1)) == (col >> (level + 1))) & ((row >> level) != (col >> level))
        ts = [_dot(jnp.where(joins, a, 0.0), x, PREC_INV) for a, x in zip(mats, xs)]
        yield
        xs = [x - _dot(x, t, PREC_INV) for x, t in zip(xs, ts)]
        yield
    return xs


def _norm_proj_kernel(n_w, x_ref, g_ref, *refs):
    w_refs, o_refs = refs[:n_w], refs[n_w:]
    xn = _rms(x_ref[...], g_ref[...]).astype(BF16)
    for w_ref, o_ref in zip(w_refs, o_refs):
        n = w_ref.shape[1]
        for c0 in range(0, n, 512):
            c1 = min(n, c0 + 512)
            o_ref[:, c0:c1] = _dot(xn, w_ref[:, c0:c1]).astype(o_ref.dtype)


def _norm_proj(x, gain, ws, out_dtype=F32, tm=512):
    n, d = x.shape
    tm = min(tm, n)
    return pl.pallas_call(
        functools.partial(_norm_proj_kernel, len(ws)),
        grid=(n // tm,),
        in_specs=[pl.BlockSpec((tm, d), lambda i: (i, 0)),
                  pl.BlockSpec((1, d), lambda i: (0, 0))]
                 + [pl.BlockSpec(w.shape, lambda i: (0, 0)) for w in ws],
        out_specs=[pl.BlockSpec((tm, w.shape[1]), lambda i: (i, 0)) for w in ws],
        out_shape=[jax.ShapeDtypeStruct((n, w.shape[1]), out_dtype) for w in ws],
        compiler_params=_cparams(("parallel",)),
        name="norm_proj",
    )(x, gain.reshape(1, d), *ws)


def _log_forget(f, log_lb, log1m_lb):
    a1 = jnp.broadcast_to(log_lb, f.shape)
    b1 = log1m_lb - _softplus(-f)
    return jnp.maximum(a1, b1) + jnp.log(1.0 + jnp.exp(-jnp.abs(a1 - b1)))


def _gdn_stages(nb, pa_ref, pdb_ref, convw_ref, alog_ref, dtb_ref, onorm_ref, o_ref, xbuf_ref, state_ref):
    n_heads = state_ref.shape[0] // nb
    hd = A_HEAD_DIM
    width = n_heads * hd
    row = _iota2((CHUNK, CHUNK), 0)
    col = _iota2((CHUNK, CHUNK), 1)
    incl = row >= col
    strict = row > col
    tril = incl.astype(F32)
    triu = (row <= col).astype(F32)

    qs, ks, vs, betas, gs, g_lasts, decays = [], [], [], [], [], [], []
    for bb in range(nb):
        x = pa_ref[bb, :, 0:3 * width]
        xbuf_ref[bb, 8:8 + CHUNK, :] = x
        acc = convw_ref[0:1, :] * xbuf_ref[bb, 5:5 + CHUNK, :]
        for j in range(1, CONV_WIDTH):
            acc = acc + convw_ref[j:j + 1, :] * xbuf_ref[bb, 5 + j:5 + j + CHUNK, :]
        xbuf_ref[bb, 0:8, :] = x[CHUNK - 8:CHUNK, :]
        qkv = _silu(acc)
        pdb = pdb_ref[bb]
        log_alpha = -jnp.exp(alog_ref[...]) * _softplus(pdb + dtb_ref[...])
        beta_all = _sigmoid(pdb)
        g_all = _dot(tril, log_alpha, PREC_CUM)
        g_all_t = _dot_tn(log_alpha, triu, PREC_CUM)
        for h in range(n_heads):
            q = qkv[:, h * hd:(h + 1) * hd]
            k = qkv[:, width + h * hd: width + (h + 1) * hd]
            qs.append(q * lax.rsqrt(jnp.sum(q * q, axis=-1, keepdims=True) + 1e-6) * (hd ** -0.5))
            ks.append(k * lax.rsqrt(jnp.sum(k * k, axis=-1, keepdims=True) + 1e-6))
            vs.append(qkv[:, 2 * width + h * hd: 2 * width + (h + 1) * hd])
            betas.append(beta_all[:, n_heads + h:n_heads + h + 1])
            g = g_all[:, h:h + 1]
            gs.append(g)
            g_lasts.append(g_all[CHUNK - 1:CHUNK, h:h + 1])
            decays.append(jnp.where(incl, jnp.exp(jnp.where(incl, g - g_all_t[h:h + 1, :], 0.0)), 0.0))
    yield
    idx = range(nb * n_heads)
    k_betas = [ks[i] * betas[i] for i in idx]
    egs = [jnp.exp(g) for g in gs]
    a_lows = [jnp.where(strict, _dot_nt(k_betas[i], ks[i], PREC_MIX) * decays[i], 0.0) for i in idx]
    qks = [_dot_nt(qs[i], ks[i], PREC_MIX) * decays[i] for i in idx]
    states = [state_ref[i] for i in idx]
    o_inters = [_dot(qs[i] * egs[i], states[i], PREC_MIX) for i in idx]
    yield
    t_invs = yield from _tri_inverse_stages(a_lows)
    us = [_dot(t_invs[i], vs[i] * betas[i], PREC_MIX) for i in idx]
    ws = [_dot(t_invs[i], k_betas[i] * egs[i], PREC_MIX) for i in idx]
    yield
    v_news = [us[i] - _dot(ws[i], states[i], PREC_MIX) for i in idx]
    yield
    os_ = [o_inters[i] + _dot(qks[i], v_news[i], PREC_MIX) for i in idx]
    for i in idx:
        state_ref[i] = (states[i] * jnp.exp(g_lasts[i])
                        + _dot_tn(ks[i] * jnp.exp(g_lasts[i] - gs[i]), v_news[i], PREC_MIX))
    yield
    for i in idx:
        bb, h = divmod(i, n_heads)
        gate = pa_ref[bb, :, 3 * width + h * hd: 3 * width + (h + 1) * hd]
        o_ref[bb, :, h * hd:(h + 1) * hd] = _rms(os_[i], onorm_ref[...]) * _silu(gate)


def _rwkv_stages(nb, col0, pb_ref, mu_ref, w0_ref, w2_ref, a0_ref, a2_ref, g2_ref, kk_ref, ka_ref, rk_ref,
                 lnw_ref, lnb_ref, bd_ref, o_ref, carry_ref, state_ref, obuf_ref):
    n_heads = state_ref.shape[0] // nb
    hd = B_HEAD_DIM
    width = n_heads * hd
    row = _iota2((CHUNK, CHUNK), 0)
    col = _iota2((CHUNK, CHUNK), 1)
    strict = row > col
    tril = (row >= col).astype(F32)
    incl2 = _iota2((CHUNK, 2 * CHUNK), 0) >= (_iota2((CHUNK, 2 * CHUNK), 1) & (CHUNK - 1))
    bd = bd_ref[...]

    pss = []
    for bb in range(nb):
        p = pb_ref[bb]
        prev = jnp.where(_iota2(p.shape, 0) == 0, carry_ref[bb, 0:1, :], pltpu.roll(p, 1, 0))
        carry_ref[bb, 0:1, :] = p[CHUNK - 1:CHUNK, :]
        pss.append(p + mu_ref[...] * (prev - p))
    rs = [ps[:, 0:width] for ps in pss]
    kraw = [ps[:, width:2 * width] for ps in pss]
    vfull = [ps[:, 2 * width:3 * width] for ps in pss]
    w_raws = [w0_ref[...] + _dot(jnp.tanh(ps[:, 3 * width:3 * width + LANES]), w2_ref[...], PREC_AUX) for ps in pss]
    iclrs = [_sigmoid(a0_ref[...] + _dot(ps[:, 3 * width + LANES:3 * width + 2 * LANES], a2_ref[...], PREC_AUX))
             for ps in pss]
    gates = [_dot(_sigmoid(ps[:, 3 * width + 2 * LANES:3 * width + 3 * LANES]), g2_ref[...], PREC_AUX) for ps in pss]
    kks = [k * kk_ref[...] for k in kraw]
    kk_ss = [_dot(kk * kk, bd, PREC_AUX) for kk in kks]
    yield
    log_ws = [-jnp.exp(-_softplus(-w_raw) - 0.5) for w_raw in w_raws]
    g_cums = [_dot(tril, log_w, PREC_CUM) for log_w in log_ws]
    yield
    lhss, rhss, bl_kls, p_cs, vhs, k2s = [], [], [], [], [], []
    for bb in range(nb):
        kk = kks[bb] * lax.rsqrt(kk_ss[bb] + 1e-6)
        k2 = kraw[bb] * (1.0 + (iclrs[bb] - 1.0) * ka_ref[...])
        k2s.append(k2)
        bvec = kk * iclrs[bb]
        g = g_cums[bb]
        g_last = g[CHUNK - 1:CHUNK, :]
        eng = jnp.exp(-g)
        egl = jnp.exp(g_last - g)
        r_t = rs[bb] * jnp.exp(g)
        kk_t = kk * jnp.exp(g - log_ws[bb])
        b_h, k_h = bvec * eng, k2 * eng
        b_l, k_l = bvec * egl, k2 * egl
        p_c = jnp.exp(g_last)
        for h in range(n_heads):
            sl = slice(h * hd, (h + 1) * hd)
            lhss.append(jnp.concatenate([kk_t[:, sl], r_t[:, sl]], axis=0))
            rhss.append(jnp.concatenate([b_h[:, sl], k_h[:, sl]], axis=0))
            bl_kls.append(jnp.concatenate([b_l[:, sl], k_l[:, sl]], axis=0))
            p_cs.append(p_c[:, sl])
            vhs.append(vfull[bb][:, sl])
    idx = range(nb * n_heads)
    ms = [_dot_nt(lhss[i], rhss[i], PREC_MIX) for i in idx]
    states = [state_ref[i] for i in idx]
    lss = [_dot_nt(lhss[i], states[i], PREC_MIX) for i in idx]
    yield
    a_bbs = [jnp.where(strict, m[0:CHUNK, 0:CHUNK], 0.0) for m in ms]
    a_bks = [jnp.where(strict, m[0:CHUNK, CHUNK:2 * CHUNK], 0.0) for m in ms]
    a_rs = [jnp.where(incl2, m[CHUNK:2 * CHUNK, :], 0.0) for m in ms]
    rhs_u = [lss[i][0:CHUNK] + _dot(a_bks[i], vhs[i], PREC_MIX) for i in idx]
    t_invs = yield from _tri_inverse_stages(a_bbs)
    uvs = [jnp.concatenate([-_dot(t_invs[i], rhs_u[i], PREC_MIX), vhs[i]], axis=0) for i in idx]
    yield
    for i in idx:
        bb, h = divmod(i, n_heads)
        obuf_ref[bb, :, h * hd:(h + 1) * hd] = lss[i][CHUNK:2 * CHUNK] + _dot(a_rs[i], uvs[i], PREC_MIX)
        state_ref[i] = states[i] * p_cs[i] + _dot_tn(uvs[i], bl_kls[i], PREC_MIX)
    yield
    inv_n = 1.0 / hd
    os_ = [obuf_ref[bb] for bb in range(nb)]
    means = [_dot(o, bd, PREC_AUX) * inv_n for o in os_]
    bonus = [_dot(rs[bb] * k2s[bb] * rk_ref[...], bd, PREC_AUX) * vfull[bb] for bb in range(nb)]
    yield
    cens = [o - mean for o, mean in zip(os_, means)]
    vars_ = [_dot(cen * cen, bd, PREC_AUX) * inv_n for cen in cens]
    yield
    for bb in range(nb):
        ln = cens[bb] * lax.rsqrt(vars_[bb] + GROUP_NORM_EPS) * lnw_ref[...] + lnb_ref[...]
        o_ref[bb, :, col0:col0 + width] = (ln + bonus[bb]) * gates[bb]


def _mixer_ab_kernel(n_gdn, n_rwkv, pa_ref, pdb_ref, pb_ref, *refs):
    gdn_params = refs[:n_gdn]
    rwkv_params = refs[n_gdn:n_gdn + n_rwkv]
    o_ref, xbuf_ref, gstate_ref, carry_ref, rstate_ref, obuf_ref = refs[n_gdn + n_rwkv:]
    nb = pa_ref.shape[0]

    @pl.when(pl.program_id(1) == 0)
    def _():
        xbuf_ref[:, 0:8, :] = jnp.zeros((nb, 8, xbuf_ref.shape[2]), F32)
        gstate_ref[...] = jnp.zeros_like(gstate_ref)
        carry_ref[...] = jnp.zeros_like(carry_ref)
        rstate_ref[...] = jnp.zeros_like(rstate_ref)

    a_width = pa_ref.shape[2] // 4
    _interleave([
        _gdn_stages(nb, pa_ref, pdb_ref, *gdn_params, o_ref, xbuf_ref, gstate_ref),
        _rwkv_stages(nb, a_width, pb_ref, *rwkv_params, o_ref, carry_ref, rstate_ref, obuf_ref),
    ])


def _mixer_ab(pa, pdb, pb, gdn_params, rwkv_params):
    b, t, a_cols = pa.shape
    b_cols = pb.shape[2]
    a_width = a_cols // 4
    b_width = (b_cols - 3 * LANES) // 3
    a_heads = a_width // A_HEAD_DIM
    b_heads = b_width // B_HEAD_DIM
    nb = MIXER_BATCH if b % MIXER_BATCH == 0 else 1
    params = list(gdn_params) + list(rwkv_params)
    return pl.pallas_call(
        functools.partial(_mixer_ab_kernel, len(gdn_params), len(rwkv_params)),
        grid=(b // nb, t // CHUNK),
        in_specs=[pl.BlockSpec((nb, CHUNK, a_cols), lambda i, c: (i, c, 0)),
                  pl.BlockSpec((nb, CHUNK, LANES), lambda i, c: (i, c, 0)),
                  pl.BlockSpec((nb, CHUNK, b_cols), lambda i, c: (i, c, 0))]
                 + [pl.BlockSpec(a.shape, lambda i, c: (0, 0)) for a in params],
        out_specs=pl.BlockSpec((nb, CHUNK, a_width + b_width), lambda i, c: (i, c, 0)),
        out_shape=jax.ShapeDtypeStruct((b, t, a_width + b_width), F32),
        scratch_shapes=[pltpu.VMEM((nb, 8 + CHUNK, 3 * a_width), F32),
                        pltpu.VMEM((nb * a_heads, A_HEAD_DIM, A_HEAD_DIM), F32),
                        pltpu.VMEM((nb, 8, b_cols), F32),
                        pltpu.VMEM((nb * b_heads, B_HEAD_DIM, B_HEAD_DIM), F32),
                        pltpu.VMEM((nb, CHUNK, b_width), F32)],
        compiler_params=_cparams(("parallel", "arbitrary")),
        name="mixer_ab",
    )(pa, pdb, pb, *params)


def _hgrn_tables():
    i = np.arange(CHUNK)[:, None]
    t = np.arange(CHUNK)[None, :]
    blocks, pair = [], []
    for ls in range(HGRN_LEVELS - 1, -1, -1):
        s = 1 << ls
        mid = (i >> (ls + 1) << (ls + 1)) + s
        q_rows, k_rows = _hgrn_level_rows(ls)
        blocks.append(((i >= mid) & (t >= mid) & (t <= i))[q_rows])
        blocks.append(((i < mid) & (t > i) & (t <= mid - 1))[k_rows])
        pair.append((((i >> ls) ^ (t >> ls)) == 1) & (i > t))
    blocks.append(t <= i)
    blocks.append(t > i)
    sums = np.concatenate(blocks, axis=0).astype(np.float32)
    return np.concatenate([sums] * 3, axis=1), np.stack(pair).astype(np.float32)


def _hgrn_level_rows(ls):
    rows = np.arange(CHUNK)
    if (1 << ls) < SUBLANES:
        return rows, rows
    upper = ((rows >> ls) & 1) == 1
    return rows[upper], rows[~upper]


def _hgrn_kernel(p_ref, loglb_ref, log1mlb_ref, onorm_ref, sums_ref, pair_ref, o_ref, state_ref):
    nb = p_ref.shape[0]
    n_heads = state_ref.shape[0] // nb
    hd = C_KEY_DIM
    kw = n_heads * hd

    @pl.when(pl.program_id(1) == 0)
    def _():
        state_ref[...] = jnp.zeros_like(state_ref)

    def take_rows(a, rows):
        runs, start = [], 0
        for n in range(1, len(rows) + 1):
            if n == len(rows) or rows[n] != rows[n - 1] + 1:
                runs.append(a[int(rows[start]):int(rows[n - 1]) + 1])
                start = n
        return runs[0] if len(runs) == 1 else jnp.concatenate(runs, axis=0)

    def spread_rows(a, rows):
        if len(rows) == CHUNK:
            return a
        pieces, pos, taken = [], 0, 0
        while pos < CHUNK:
            run = 1
            inside = pos in rows
            while pos + run < CHUNK and ((pos + run) in rows) == inside:
                run += 1
            if inside:
                pieces.append(a[taken:taken + run])
                taken += run
            else:
                pieces.append(jnp.zeros((run, a.shape[1]), a.dtype))
            pos += run
        return jnp.concatenate(pieces, axis=0)

    qs, ks, exps = [], [], []
    for bb in range(nb):
        qs.append(_silu(p_ref[bb, :, 0:kw]))
        log_f = _log_forget(p_ref[bb, :, kw:2 * kw], loglb_ref[...], log1mlb_ref[...])
        ks.append(1.0 - jnp.exp(log_f))
        f1 = log_f.astype(BF16)
        f2 = (log_f - f1.astype(F32)).astype(BF16)
        f3 = (log_f - f1.astype(F32) - f2.astype(F32)).astype(BF16)
        exps.append(_dot(sums_ref[...], jnp.concatenate([f1, f2, f3], axis=0)))

    idx = [(bb, h) for bb in range(nb) for h in range(n_heads)]
    states = [state_ref[bb * n_heads + h] for bb, h in idx]
    scores = [None] * len(idx)
    offset = 0
    for lvl in range(HGRN_LEVELS):
        q_rows, k_rows = _hgrn_level_rows(HGRN_LEVELS - 1 - lvl)
        q_rows, k_rows = [int(r) for r in q_rows], [int(r) for r in k_rows]
        e_q = [e[offset:offset + len(q_rows)] for e in exps]
        e_k = [e[offset + len(q_rows):offset + len(q_rows) + len(k_rows)] for e in exps]
        offset += len(q_rows) + len(k_rows)
        q_l = [(take_rows(qs[bb], q_rows) * jnp.exp(e_q[bb])).astype(BF16) for bb in range(nb)]
        k_l = [spread_rows(take_rows(ks[bb], k_rows) * jnp.exp(e_k[bb]), k_rows).astype(BF16) for bb in range(nb)]
        mask = pair_ref[lvl]
        for n, (bb, h) in enumerate(idx):
            sl = slice(h * hd, (h + 1) * hd)
            part = spread_rows(_dot_nt(q_l[bb][:, sl], k_l[bb][:, sl]), q_rows) * mask
            scores[n] = part if scores[n] is None else scores[n] + part
    g_cum = [e[offset:offset + CHUNK] for e in exps]
    q_g = [qs[bb] * jnp.exp(g_cum[bb]) for bb in range(nb)]
    k_r = [ks[bb] * jnp.exp(exps[bb][offset + CHUNK:offset + 2 * CHUNK]) for bb in range(nb)]
    diag = [qs[bb] * ks[bb] for bb in range(nb)]
    outs = []
    for n, (bb, h) in enumerate(idx):
        sl = slice(h * hd, (h + 1) * hd)
        v = p_ref[bb, :, 2 * kw + h * hd: 2 * kw + (h + 1) * hd]
        o = _dot_nt(q_g[bb][:, sl], states[n], PREC_MIX) + _dot(scores[n], v, PREC_MIX)
        outs.append(o + jnp.sum(diag[bb][:, sl], axis=-1, keepdims=True) * v)
        p_last = jnp.exp(g_cum[bb][CHUNK - 1:CHUNK, sl])
        state_ref[bb * n_heads + h] = states[n] * p_last + _dot_tn(v, k_r[bb][:, sl], PREC_MIX)
    for n, (bb, h) in enumerate(idx):
        gate = p_ref[bb, :, 3 * kw + h * hd: 3 * kw + (h + 1) * hd]
        o_ref[bb, :, h * hd:(h + 1) * hd] = _rms(outs[n], onorm_ref[...]) * _silu(gate)


def _hgrn(proj, log_lb, log1m_lb, onorm):
    b, t, cols = proj.shape
    kw = cols // 4
    n_heads = kw // C_KEY_DIM
    nb = MIXER_BATCH if b % MIXER_BATCH == 0 else 1
    sums, pair = _hgrn_tables()
    sums = jnp.asarray(sums, BF16)
    pair = jnp.asarray(pair, F32)
    return pl.pallas_call(
        _hgrn_kernel,
        grid=(b // nb, t // CHUNK),
        in_specs=[pl.BlockSpec((nb, CHUNK, cols), lambda i, c: (i, c, 0)),
                  pl.BlockSpec((1, kw), lambda i, c: (0, 0)),
                  pl.BlockSpec((1, kw), lambda i, c: (0, 0)),
                  pl.BlockSpec((1, C_KEY_DIM), lambda i, c: (0, 0)),
                  pl.BlockSpec(sums.shape, lambda i, c: (0, 0)),
                  pl.BlockSpec(pair.shape, lambda i, c: (0, 0, 0))],
        out_specs=pl.BlockSpec((nb, CHUNK, kw), lambda i, c: (i, c, 0)),
        out_shape=jax.ShapeDtypeStruct((b, t, kw), F32),
        scratch_shapes=[pltpu.VMEM((nb * n_heads, C_KEY_DIM, C_KEY_DIM), F32)],
        compiler_params=_cparams(("parallel", "arbitrary")),
        name="hgrn2",
    )(proj, log_lb, log1m_lb, onorm, sums, pair)


def _route(logits):
    lane = _iota2(logits.shape, 1)
    neg = -jnp.inf
    gmask = lane < N_GROUPS
    gl = jnp.where(gmask, logits, neg)
    gmax = jnp.max(gl, axis=-1, keepdims=True)
    gsel = jnp.min(jnp.where(gl == gmax, lane, LANES), axis=-1, keepdims=True)
    group_gate = 1.0 / jnp.sum(jnp.where(gmask, jnp.exp(gl - gmax), 0.0), axis=-1, keepdims=True)
    lo = N_GROUPS + EXPERTS_PER_GROUP * gsel
    emask = (lane >= lo) & (lane < lo + EXPERTS_PER_GROUP)
    el = jnp.where(emask, logits, neg)
    ee = jnp.where(emask, jnp.exp(el - jnp.max(el, axis=-1, keepdims=True)), 0.0)
    prob = ee / jnp.sum(ee, axis=-1, keepdims=True)
    p1 = jnp.max(jnp.where(emask, prob, -1.0), axis=-1, keepdims=True)
    i1 = jnp.min(jnp.where(emask & (prob == p1), lane, LANES), axis=-1, keepdims=True)
    rest = emask & (lane != i1)
    p2 = jnp.max(jnp.where(rest, prob, -1.0), axis=-1, keepdims=True)
    i2 = jnp.min(jnp.where(rest & (prob == p2), lane, LANES), axis=-1, keepdims=True)
    scale = group_gate / (p1 + p2)
    comb = jnp.where(lane == i1, p1 * scale, 0.0) + jnp.where(lane == i2, p2 * scale, 0.0)
    return comb, gsel


def _sorted_rows(tile_rows):
    return tile_rows + N_GROUPS * SEG_ALIGN


def _xattn_route_kernel(x_ref, a_ref, wmix_ref, gain_ref, wq_ref, k_ref, v_ref, wo_ref, gffn_ref, wr_ref, br_ref,
                        x2_ref, xs_ref, cs_ref, info_ref, seg_ref):
    n_sub = xs_ref.shape[0]
    tq = x_ref.shape[0] // n_sub
    d = x_ref.shape[1]
    hd = d // XA_HEADS
    rows = xs_ref.shape[1]

    def tile_stages(u):
        tok = slice(u * tq, (u + 1) * tq)
        x = x_ref[tok, :] + _dot(a_ref[tok, :].astype(BF16), wmix_ref[...])
        q = _dot(_rms(x, gain_ref[...]).astype(BF16), wq_ref[...]).astype(BF16)
        yield
        outs = []
        for h in range(XA_HEADS):
            sl = slice(h * hd, (h + 1) * hd)
            s = _dot_nt(q[:, sl], k_ref[:, sl]) * (hd ** -0.5)
            p = jnp.exp(s - jnp.max(s, axis=-1, keepdims=True))
            p = p / jnp.sum(p, axis=-1, keepdims=True)
            outs.append(_dot(p.astype(BF16), v_ref[:, sl]).astype(BF16))
        yield
        x2 = x + _dot(jnp.concatenate(outs, axis=-1), wo_ref[...])
        x2_ref[tok, :] = x2
        xn = _rms(x2, gffn_ref[...])
        xn_hi = xn.astype(BF16)
        xn_lo = (xn - xn_hi.astype(F32)).astype(BF16)
        hi_part = _dot(xn_hi, wr_ref[...])
        logits = hi_part[:, 0:LANES] + hi_part[:, LANES:2 * LANES] + _dot(xn_lo, wr_ref[:, 0:LANES])
        yield
        comb, gsel = _route(logits + br_ref[...])
        lane = _iota2((tq, LANES), 1)
        onehot = (lane == gsel).astype(F32)
        earlier = (_iota2((tq, tq), 0) > _iota2((tq, tq), 1)).astype(BF16)
        before = _dot(earlier, onehot.astype(BF16))
        yield
        rank = jnp.sum(onehot * before, axis=-1, keepdims=True)
        counts = before[tq - 1:tq, :] + onehot[tq - 1:tq, :]
        padded = jnp.floor((counts + (SEG_ALIGN - 1)) * (1.0 / SEG_ALIGN)) * SEG_ALIGN
        lane1 = _iota2((1, LANES), 1)
        starts = jnp.zeros((1, LANES), F32)
        for g in range(N_GROUPS - 1):
            starts = starts + jnp.where(lane1 > g, padded[:, g:g + 1], 0.0)
        dest = jnp.sum(onehot * starts, axis=-1, keepdims=True) + rank
        place = (_iota2((tq, rows), 1).astype(F32) == dest).astype(BF16)
        comb_hi = comb.astype(BF16)
        comb_lo = (comb - comb_hi.astype(F32)).astype(BF16)
        placed = _dot_tn(place, jnp.concatenate([xn_hi, comb_hi, comb_lo], axis=1))
        xs_ref[u] = placed[:, 0:d].astype(BF16)
        cs_ref[u] = placed[:, d:d + LANES] + placed[:, d + LANES:d + 2 * LANES]
        info_ref[tok, :] = jnp.where(lane == 0, dest, 0.0)
        seg_ref[u] = jnp.concatenate([starts, padded, jnp.zeros((6, LANES), F32)], axis=0)

    _interleave([tile_stages(u) for u in range(n_sub)])


def _xattn_route(x, act, w_mix, gain, wq, k, v, wo, gain_ffn, w_router, b_router, tq=512):
    b, t, d = x.shape
    m = k.shape[1]
    tq = min(tq, t)
    tiles_per_row = t // tq
    n_tiles = b * tiles_per_row
    rows = _sorted_rows(tq)
    n_sub = XATTN_TILES_PER_STEP if tiles_per_row % XATTN_TILES_PER_STEP == 0 else 1
    steps_per_row = tiles_per_row // n_sub

    def tile(i, j):
        return (i * steps_per_row + j, 0, 0)

    return pl.pallas_call(
        _xattn_route_kernel,
        grid=(b, steps_per_row),
        in_specs=[pl.BlockSpec((None, n_sub * tq, d), lambda i, j: (i, j, 0)),
                  pl.BlockSpec((None, n_sub * tq, act.shape[2]), lambda i, j: (i, j, 0)),
                  pl.BlockSpec(w_mix.shape, lambda i, j: (0, 0)),
                  pl.BlockSpec((1, d), lambda i, j: (0, 0)),
                  pl.BlockSpec((d, d), lambda i, j: (0, 0)),
                  pl.BlockSpec((None, m, d), lambda i, j: (i, 0, 0)),
                  pl.BlockSpec((None, m, d), lambda i, j: (i, 0, 0)),
                  pl.BlockSpec((d, d), lambda i, j: (0, 0)),
                  pl.BlockSpec((1, d), lambda i, j: (0, 0)),
                  pl.BlockSpec((d, 2 * LANES), lambda i, j: (0, 0)),
                  pl.BlockSpec((1, LANES), lambda i, j: (0, 0))],
        out_specs=[pl.BlockSpec((None, n_sub * tq, d), lambda i, j: (i, j, 0)),
                   pl.BlockSpec((n_sub, rows, d), tile),
                   pl.BlockSpec((n_sub, rows, LANES), tile),
                   pl.BlockSpec((None, n_sub * tq, LANES), lambda i, j: (i, j, 0)),
                   pl.BlockSpec((n_sub, 8, LANES), tile)],
        out_shape=[jax.ShapeDtypeStruct((b, t, d), F32),
                   jax.ShapeDtypeStruct((n_tiles, rows, d), BF16),
                   jax.ShapeDtypeStruct((n_tiles, rows, LANES), F32),
                   jax.ShapeDtypeStruct((b, t, LANES), F32),
                   jax.ShapeDtypeStruct((n_tiles, 8, LANES), F32)],
        compiler_params=_cparams(("parallel", "parallel")),
        name="xattn_route",
    )(x, act, w_mix, gain.reshape(1, d), wq, k, v, wo, gain_ffn.reshape(1, d), w_router, b_router)


def _moe_ffn_kernel(seg_ref, xs_ref, cs_ref, w1_ref, w3_ref, w2_ref, ys_ref, xbuf_ref, cbuf_ref, ybuf_ref):
    win = pl.program_id(0)
    grp = pl.program_id(1)
    n_win = xs_ref.shape[0]

    @pl.when((win == 0) & (grp == 0))
    def _():
        xbuf_ref[...] = jnp.zeros_like(xbuf_ref)
        cbuf_ref[...] = jnp.zeros_like(cbuf_ref)

    @pl.when(grp == 0)
    def _():
        ys_ref[...] = jnp.zeros_like(ys_ref)

    def segment(t):
        base = (win * n_win + t) * 8
        return seg_ref[base + grp], seg_ref[base + N_GROUPS + grp]

    def copy_segments(copy_piece):
        packed = jnp.int32(0)
        for t in range(n_win):
            start, size = segment(t)

            def body(j, carry, t=t, start=start, packed=packed):
                copy_piece(t, pl.multiple_of(start + j * SEG_ALIGN, SEG_ALIGN),
                           pl.multiple_of(packed + j * SEG_ALIGN, SEG_ALIGN))
                return carry

            lax.fori_loop(0, size // SEG_ALIGN, body, 0)
            packed = packed + size
        return packed

    def gather(t, src, dst):
        xbuf_ref[pl.ds(dst, SEG_ALIGN), :] = xs_ref[t, pl.ds(src, SEG_ALIGN), :]
        cbuf_ref[pl.ds(dst, SEG_ALIGN), :] = cs_ref[t, pl.ds(src, SEG_ALIGN), :]

    total = copy_segments(gather)

    def ffn_piece(r0, n_rows):
        xc = xbuf_ref[pl.ds(r0, n_rows), :]
        cb = cbuf_ref[pl.ds(r0, n_rows), :]
        lane = _iota2(cb.shape, 1)
        hids = []
        for e in range(EXPERTS_PER_GROUP):
            weight = jnp.sum(jnp.where(lane == N_GROUPS + EXPERTS_PER_GROUP * grp + e, cb, 0.0),
                             axis=-1, keepdims=True)
            hids.append((_silu(_dot(xc, w1_ref[e])) * _dot(xc, w3_ref[e]) * weight).astype(BF16))
        ybuf_ref[pl.ds(r0, n_rows), :] = _dot(jnp.concatenate(hids, axis=-1), w2_ref[...]).astype(ybuf_ref.dtype)

    def chunk(c, carry):
        ffn_piece(pl.multiple_of(c * MOE_CHUNK, MOE_CHUNK), MOE_CHUNK)
        return carry

    n_full = total // MOE_CHUNK
    lax.fori_loop(0, n_full, chunk, 0)
    rest = total - n_full * MOE_CHUNK
    rest_start = pl.multiple_of(n_full * MOE_CHUNK, MOE_CHUNK)

    @pl.when((rest > 0) & (rest <= MOE_CHUNK // 2))
    def _():
        ffn_piece(rest_start, MOE_CHUNK // 2)

    @pl.when(rest > MOE_CHUNK // 2)
    def _():
        ffn_piece(rest_start, MOE_CHUNK)

    def scatter(t, dst, src):
        ys_ref[t, pl.ds(dst, SEG_ALIGN), :] = ybuf_ref[pl.ds(src, SEG_ALIGN), :]

    copy_segments(scatter)


def _moe_ffn(seg, xs, cs, w1, w3, w2g):
    n_tiles, rows, d = xs.shape
    ff = w1.shape[2]
    n_win = next(w for w in (MOE_WINDOW, 2, 1) if n_tiles % w == 0)
    buf_rows = -(-(n_win * rows) // MOE_CHUNK) * MOE_CHUNK
    grid_spec = pltpu.PrefetchScalarGridSpec(
        num_scalar_prefetch=1,
        grid=(n_tiles // n_win, N_GROUPS),
        in_specs=[pl.BlockSpec((n_win, rows, d), lambda w, g, seg: (w, 0, 0)),
                  pl.BlockSpec((n_win, rows, LANES), lambda w, g, seg: (w, 0, 0)),
                  pl.BlockSpec((EXPERTS_PER_GROUP, d, ff), lambda w, g, seg: (g, 0, 0)),
                  pl.BlockSpec((EXPERTS_PER_GROUP, d, ff), lambda w, g, seg: (g, 0, 0)),
                  pl.BlockSpec((None, EXPERTS_PER_GROUP * ff, d), lambda w, g, seg: (g, 0, 0))],
        out_specs=pl.BlockSpec((n_win, rows, d), lambda w, g, seg: (w, 0, 0)),
        scratch_shapes=[pltpu.VMEM((buf_rows, d), BF16),
                        pltpu.VMEM((buf_rows, LANES), F32),
                        pltpu.VMEM((buf_rows, d), BF16)])
    return pl.pallas_call(
        _moe_ffn_kernel,
        grid_spec=grid_spec,
        out_shape=jax.ShapeDtypeStruct((n_tiles, rows, d), BF16),
        compiler_params=_cparams(("arbitrary", "arbitrary")),
        name="moe_ffn",
    )(seg, xs, cs, w1, w3, w2g)


def _moe_unsort_kernel(final_norm, x_ref, info_ref, ys_ref, gfin_ref, o_ref):
    info = info_ref[...]
    dest = jnp.sum(jnp.where(_iota2(info.shape, 1) == 0, info, 0.0), axis=-1, keepdims=True)
    rows = ys_ref.shape[0]
    place = (_iota2((info.shape[0], rows), 1).astype(F32) == dest).astype(BF16)
    y = x_ref[...] + _dot(place, ys_ref[...])
    if final_norm:
        y = _rms(y, gfin_ref[...])
    o_ref[...] = y


def _moe_unsort(x, info, ys, gain_final, final_norm):
    b, t, d = x.shape
    n_tiles, rows, _ = ys.shape
    tiles_per_row = n_tiles // b
    tq = t // tiles_per_row
    return pl.pallas_call(
        functools.partial(_moe_unsort_kernel, final_norm),
        grid=(b, tiles_per_row),
        in_specs=[pl.BlockSpec((None, tq, d), lambda i, j: (i, j, 0)),
                  pl.BlockSpec((None, tq, LANES), lambda i, j: (i, j, 0)),
                  pl.BlockSpec((None, rows, d), lambda i, j: (i * tiles_per_row + j, 0, 0)),
                  pl.BlockSpec((1, d), lambda i, j: (0, 0))],
        out_specs=pl.BlockSpec((None, tq, d), lambda i, j: (i, j, 0)),
        out_shape=jax.ShapeDtypeStruct((b, t, d), F32),
        compiler_params=_cparams(("parallel", "parallel")),
        name="moe_unsort",
    )(x, info, ys, gain_final.reshape(1, d))


def _pad_cols(w, n):
    return jnp.pad(w, ((0, 0), (0, n - w.shape[1])))


def _pad_rows(w, n):
    return jnp.pad(w, ((0, n - w.shape[0]), (0, 0)))


def _row(v, n=None):
    v = v.reshape(1, -1).astype(F32)
    return v if n is None else _pad_cols(v, n)


def kernel(x, mem, norm_mix, norm_xattn, norm_mem, norm_ffn, norm_final, ab_w_in, ab_conv, ab_a_log, ab_dt_bias, ab_onorm, ab_shift_mu, ab_w0, ab_w2, ab_a0, ab_a2, ab_g2, ab_k_k, ab_k_a, ab_r_k, ab_ln_w, ab_ln_b, ab_w_out, c_w_in, c_lb_param, c_onorm, c_w_out, xa_wq, xa_wkv, xa_wo, moe_w_group, moe_b_group, moe_w_expert, moe_b_expert, moe_w1, moe_w3, moe_w2):
    bsz, seq, d = x.shape
    n_mem = mem.shape[1]
    depth = norm_mix.shape[0]
    a_width = d // 2
    b_width = d // 2
    a_heads = a_width // A_HEAD_DIM
    a_cols = 4 * a_width + 2 * a_heads
    n_tok = bsz * seq

    lb_p = jax.nn.softmax(c_lb_param.astype(F32), axis=0)
    lower_bounds = jnp.cumsum(lb_p, axis=0) - lb_p[0]

    x2 = x.reshape(n_tok, d)
    mem2 = mem.reshape(bsz * n_mem, d)
    for layer in range(depth):
        if layer % 2 == 0:
            e = layer // 2
            w_in = ab_w_in[e]
            w_a = w_in[:, :4 * a_width].astype(BF16)
            w_db = _pad_cols(w_in[:, 4 * a_width:a_cols], LANES).astype(BF16)
            wb = w_in[:, a_cols:]
            o1, o2 = 3 * b_width, 3 * b_width + B_DECAY_LORA
            o3 = o2 + B_ICLR_LORA
            w_b = jnp.concatenate([wb[:, :o1], _pad_cols(wb[:, o1:o2], LANES), _pad_cols(wb[:, o2:o3], LANES),
                                   wb[:, o3:]], axis=1).astype(BF16)
            mu = ab_shift_mu[e].reshape(1, -1)
            mu_p = jnp.concatenate([mu[:, :o1], _pad_cols(mu[:, o1:o2], LANES), _pad_cols(mu[:, o2:o3], LANES),
                                    mu[:, o3:]], axis=1)
            head_id = jnp.arange(b_width) // B_HEAD_DIM
            bd = (head_id[:, None] == head_id[None, :]).astype(F32)
            gdn_params = [ab_conv[e], _row(ab_a_log[e], LANES), _row(ab_dt_bias[e], LANES), _row(ab_onorm[e])]
            rwkv_params = [mu_p, _row(ab_w0[e]), _pad_rows(ab_w2[e], LANES), _row(ab_a0[e]),
                           _pad_rows(ab_a2[e], LANES), ab_g2[e], _row(ab_k_k[e]), _row(ab_k_a[e]),
                           _row(ab_r_k[e]), _row(ab_ln_w[e]), _row(ab_ln_b[e]), bd]
            pa, pdb, pb = _norm_proj(x2, norm_mix[layer], [w_a, w_db, w_b])
            mix = _mixer_ab(pa.reshape(bsz, seq, -1), pdb.reshape(bsz, seq, LANES), pb.reshape(bsz, seq, -1),
                            gdn_params, rwkv_params)
            w_mix = ab_w_out[e].astype(BF16)
        else:
            o = layer // 2
            (proj,) = _norm_proj(x2, norm_mix[layer], [c_w_in[o].astype(BF16)])
            lb = lower_bounds[layer].reshape(1, -1)
            mix = _hgrn(proj.reshape(bsz, seq, -1), jnp.log(lb), jnp.log1p(-lb), _row(c_onorm[o]))
            w_mix = c_w_out[o].astype(BF16)

        wkv = xa_wkv[layer].astype(BF16)
        k_mem, v_mem = _norm_proj(mem2, norm_mem[layer], [wkv[:, :d], wkv[:, d:]], out_dtype=BF16)
        w_r = _pad_cols(jnp.concatenate([moe_w_group[layer], moe_w_expert[layer]], axis=1), LANES).astype(F32)
        w_r_hi = w_r.astype(BF16)
        w_r_lo = (w_r - w_r_hi.astype(F32)).astype(BF16)
        b_router = _row(jnp.concatenate([moe_b_group[layer], moe_b_expert[layer]]), LANES)
        x3, xs, cs, info, seg = _xattn_route(
            x2.reshape(bsz, seq, d), mix, w_mix, norm_xattn[layer], xa_wq[layer].astype(BF16),
            k_mem.reshape(bsz, n_mem, d), v_mem.reshape(bsz, n_mem, d), xa_wo[layer].astype(BF16),
            norm_ffn[layer], jnp.concatenate([w_r_hi, w_r_lo], axis=1), b_router)
        seg_i = jnp.concatenate([seg[:, 0, :N_GROUPS], seg[:, 1, :N_GROUPS]], axis=1).astype(jnp.int32).reshape(-1)
        ff = moe_w2.shape[2]
        ys = _moe_ffn(seg_i, xs, cs, moe_w1[layer].astype(BF16), moe_w3[layer].astype(BF16),
                      moe_w2[layer].reshape(N_GROUPS, EXPERTS_PER_GROUP * ff, d).astype(BF16))
        x2 = _moe_unsort(x3, info, ys, norm_final, final_norm=(layer == depth - 1)).reshape(n_tok, d)
    return x2.reshape(bsz, seq, d)
```

```python
import functools

import jax
import jax.numpy as jnp
import numpy as np
from jax import lax
from jax.experimental import pallas as pl
from jax.experimental.pallas import tpu as pltpu

F32 = jnp.float32
BF16 = jnp.bfloat16
HI = lax.Precision.HIGHEST
ONE_PASS = "bf16"

PREC_INV = ONE_PASS
PREC_MIX = ONE_PASS
PREC_AUX = ONE_PASS

CHUNK = 64
HGRN_LEVELS = CHUNK.bit_length() - 1
NORM_EPS = 1e-6
GROUP_NORM_EPS = 64e-5
CONV_WIDTH = 4

A_HEAD_DIM = 128
B_HEAD_DIM = 64
B_DECAY_LORA = 64
B_ICLR_LORA = 64
B_GATE_LORA = 128
C_KEY_DIM = 128
XA_HEADS = 4
N_GROUPS = 4
EXPERTS_PER_GROUP = 4
N_EXPERTS = N_GROUPS * EXPERTS_PER_GROUP

LANES = 128
SUBLANES = 8
VMEM_LIMIT = 56 * 1024 * 1024
MIXER_BATCH = 4
SEG_ALIGN = 16
MOE_WINDOW = 4
MOE_CHUNK = 256
XATTN_TILES_PER_STEP = 2


def _cparams(sem):
    return pltpu.CompilerParams(dimension_semantics=sem, vmem_limit_bytes=VMEM_LIMIT)


def _dot_dims(a, b, dims, prec):
    if prec == ONE_PASS:
        a, b, prec = a.astype(BF16), b.astype(BF16), None
    return lax.dot_general(a, b, (dims, ((), ())), preferred_element_type=F32, precision=prec)


def _dot(a, b, prec=None):
    return _dot_dims(a, b, ((1,), (0,)), prec)


def _dot_nt(a, b, prec=None):
    return _dot_dims(a, b, ((1,), (1,)), prec)


def _dot_tn(a, b, prec=None):
    return _dot_dims(a, b, ((0,), (0,)), prec)


def _rms(x, gain):
    return x * lax.rsqrt(jnp.mean(x * x, axis=-1, keepdims=True) + NORM_EPS) * gain


def _sigmoid(x):
    return 1.0 / (1.0 + jnp.exp(-x))


def _silu(x):
    return x * _sigmoid(x)


def _softplus(x):
    return jnp.maximum(x, 0.0) + jnp.log(1.0 + jnp.exp(-jnp.abs(x)))


def _iota2(shape, dim):
    return lax.broadcasted_iota(jnp.int32, shape, dim)


def _split3(v):
    v1 = v.astype(BF16)
    v2 = (v - v1.astype(F32)).astype(BF16)
    v3 = (v - v1.astype(F32) - v2.astype(F32)).astype(BF16)
    return v1, v2, v3


def _table_dot(table, v):
    return _dot(jnp.concatenate([table.astype(BF16)] * 3, axis=1), jnp.concatenate(_split3(v), axis=0))


def _table_dot_tn(v, table):
    return _dot_tn(jnp.concatenate(_split3(v), axis=0), jnp.concatenate([table.astype(BF16)] * 3, axis=0))


def _interleave(stage_generators):
    live = list(stage_generators)
    while live:
        for gen in list(live):
            try:
                next(gen)
            except StopIteration:
                live.remove(gen)


def _tri_inverse_stages(mats):
    n = mats[0].shape[0]
    row = _iota2((n, n), 0)
    col = _iota2((n, n), 1)
    eye = (row == col).astype(F32)
    xs = [eye - jnp.where((row >> 1) == (col >> 1), a, 0.0) for a in mats]
    for level in range(1, n.bit_length() - 1):
        joins = ((row >> (level + 1)) == (col >> (level + 1))) & ((row >> level) != (col >> level))
        ts = [_dot(jnp.where(joins, a, 0.0), x, PREC_INV) for a, x in zip(mats, xs)]
        yield
        xs = [x - _dot(x, t, PREC_INV) for x, t in zip(xs, ts)]
        yield
    return xs


def _norm_proj_kernel(n_w, x_ref, g_ref, *refs):
    w_refs, o_refs = refs[:n_w], refs[n_w:]
    xn = _rms(x_ref[...], g_ref[...]).astype(BF16)
    for w_ref, o_ref in zip(w_refs, o_refs):
        n = w_ref.shape[1]
        for c0 in range(0, n, 512):
            c1 = min(n, c0 + 512)
            o_ref[:, c0:c1] = _dot(xn, w_ref[:, c0:c1]).astype(o_ref.dtype)


def _norm_proj(x, gain, ws, out_dtype=F32, tm=512):
    n, d = x.shape
    tm = min(tm, n)
    return pl.pallas_call(
        functools.partial(_norm_proj_kernel, len(ws)),
        grid=(n // tm,),
        in_specs=[pl.BlockSpec((tm, d), lambda i: (i, 0)),
                  pl.BlockSpec((1, d), lambda i: (0, 0))]
                 + [pl.BlockSpec(w.shape, lambda i: (0, 0)) for w in ws],
        out_specs=[pl.BlockSpec((tm, w.shape[1]), lambda i: (i, 0)) for w in ws],
        out_shape=[jax.ShapeDtypeStruct((n, w.shape[1]), out_dtype) for w in ws],
        compiler_params=_cparams(("parallel",)),
        name="norm_proj",
    )(x, gain.reshape(1, d), *ws)


def _log_forget(f, log_lb, log1m_lb):
    a1 = jnp.broadcast_to(log_lb, f.shape)
    b1 = log1m_lb - _softplus(-f)
    return jnp.maximum(a1, b1) + jnp.log(1.0 + jnp.exp(-jnp.abs(a1 - b1)))


def _gdn_stages(nb, pa_ref, pdb_ref, convw_ref, alog_ref, dtb_ref, onorm_ref, o_ref, xbuf_ref, state_ref):
    n_heads = state_ref.shape[0] // nb
    hd = A_HEAD_DIM
    width = n_heads * hd
    row = _iota2((CHUNK, CHUNK), 0)
    col = _iota2((CHUNK, CHUNK), 1)
    incl = row >= col
    strict = row > col
    tril = incl.astype(F32)
    triu = (row <= col).astype(F32)

    qs, ks, vs, betas, gs, g_lasts, decays = [], [], [], [], [], [], []
    for bb in range(nb):
        x = pa_ref[bb, :, 0:3 * width]
        xbuf_ref[bb, 8:8 + CHUNK, :] = x
        acc = convw_ref[0:1, :] * xbuf_ref[bb, 5:5 + CHUNK, :]
        for j in range(1, CONV_WIDTH):
            acc = acc + convw_ref[j:j + 1, :] * xbuf_ref[bb, 5 + j:5 + j + CHUNK, :]
        xbuf_ref[bb, 0:8, :] = x[CHUNK - 8:CHUNK, :]
        qkv = _silu(acc)
        pdb = pdb_ref[bb]
        log_alpha = -jnp.exp(alog_ref[...]) * _softplus(pdb + dtb_ref[...])
        beta_all = _sigmoid(pdb)
        g_all = _table_dot(tril, log_alpha)
        g_all_t = _table_dot_tn(log_alpha, triu)
        for h in range(n_heads):
            q = qkv[:, h * hd:(h + 1) * hd]
            k = qkv[:, width + h * hd: width + (h + 1) * hd]
            qs.append(q * lax.rsqrt(jnp.sum(q * q, axis=-1, keepdims=True) + 1e-6) * (hd ** -0.5))
            ks.append(k * lax.rsqrt(jnp.sum(k * k, axis=-1, keepdims=True) + 1e-6))
            vs.append(qkv[:, 2 * width + h * hd: 2 * width + (h + 1) * hd])
            betas.append(beta_all[:, n_heads + h:n_heads + h + 1])
            g = g_all[:, h:h + 1]
            gs.append(g)
            g_lasts.append(g_all[CHUNK - 1:CHUNK, h:h + 1])
            decays.append(jnp.where(incl, jnp.exp(jnp.where(incl, g - g_all_t[h:h + 1, :], 0.0)), 0.0))
    yield
    idx = range(nb * n_heads)
    k_betas = [ks[i] * betas[i] for i in idx]
    egs = [jnp.exp(g) for g in gs]
    a_lows = [jnp.where(strict, _dot_nt(k_betas[i], ks[i], PREC_MIX) * decays[i], 0.0) for i in idx]
    qks = [_dot_nt(qs[i], ks[i], PREC_MIX) * decays[i] for i in idx]
    states = [state_ref[i] for i in idx]
    o_inters = [_dot(qs[i] * egs[i], states[i], PREC_MIX) for i in idx]
    yield
    t_invs = yield from _tri_inverse_stages(a_lows)
    us = [_dot(t_invs[i], vs[i] * betas[i], PREC_MIX) for i in idx]
    ws = [_dot(t_invs[i], k_betas[i] * egs[i], PREC_MIX) for i in idx]
    yield
    v_news = [us[i] - _dot(ws[i], states[i], PREC_MIX) for i in idx]
    yield
    os_ = [o_inters[i] + _dot(qks[i], v_news[i], PREC_MIX) for i in idx]
    for i in idx:
        state_ref[i] = (states[i] * jnp.exp(g_lasts[i])
                        + _dot_tn(ks[i] * jnp.exp(g_lasts[i] - gs[i]), v_news[i], PREC_MIX))
    yield
    for i in idx:
        bb, h = divmod(i, n_heads)
        gate = pa_ref[bb, :, 3 * width + h * hd: 3 * width + (h + 1) * hd]
        o_ref[bb, :, h * hd:(h + 1) * hd] = _rms(os_[i], onorm_ref[...]) * _silu(gate)


def _rwkv_stages(nb, col0, pb_ref, mu_ref, w0_ref, w2_ref, a0_ref, a2_ref, g2_ref, kk_ref, ka_ref, rk_ref,
                 lnw_ref, lnb_ref, bd_ref, o_ref, carry_ref, state_ref, obuf_ref):
    n_heads = state_ref.shape[0] // nb
    hd = B_HEAD_DIM
    width = n_heads * hd
    row = _iota2((CHUNK, CHUNK), 0)
    col = _iota2((CHUNK, CHUNK), 1)
    strict = row > col
    tril = (row >= col).astype(F32)
    incl2 = _iota2((CHUNK, 2 * CHUNK), 0) >= (_iota2((CHUNK, 2 * CHUNK), 1) & (CHUNK - 1))
    bd = bd_ref[...]

    pss = []
    for bb in range(nb):
        p = pb_ref[bb]
        prev = jnp.where(_iota2(p.shape, 0) == 0, carry_ref[bb, 0:1, :], pltpu.roll(p, 1, 0))
        carry_ref[bb, 0:1, :] = p[CHUNK - 1:CHUNK, :]
        pss.append(p + mu_ref[...] * (prev - p))
    rs = [ps[:, 0:width] for ps in pss]
    kraw = [ps[:, width:2 * width] for ps in pss]
    vfull = [ps[:, 2 * width:3 * width] for ps in pss]
    w_raws = [w0_ref[...] + _dot(jnp.tanh(ps[:, 3 * width:3 * width + LANES]), w2_ref[...], PREC_AUX) for ps in pss]
    iclrs = [_sigmoid(a0_ref[...] + _dot(ps[:, 3 * width + LANES:3 * width + 2 * LANES], a2_ref[...], PREC_AUX))
             for ps in pss]
    gates = [_dot(_sigmoid(ps[:, 3 * width + 2 * LANES:3 * width + 3 * LANES]), g2_ref[...], PREC_AUX) for ps in pss]
    kks = [k * kk_ref[...] for k in kraw]
    kk_ss = [_dot(kk * kk, bd, PREC_AUX) for kk in kks]
    yield
    log_ws = [-jnp.exp(-_softplus(-w_raw) - 0.5) for w_raw in w_raws]
    g_cums = [_table_dot(tril, log_w) for log_w in log_ws]
    yield
    lhss, rhss, bl_kls, p_cs, vhs, k2s = [], [], [], [], [], []
    for bb in range(nb):
        kk = kks[bb] * lax.rsqrt(kk_ss[bb] + 1e-6)
        k2 = kraw[bb] * (1.0 + (iclrs[bb] - 1.0) * ka_ref[...])
        k2s.append(k2)
        bvec = kk * iclrs[bb]
        g = g_cums[bb]
        g_last = g[CHUNK - 1:CHUNK, :]
        eng = jnp.exp(-g)
        egl = jnp.exp(g_last - g)
        r_t = rs[bb] * jnp.exp(g)
        kk_t = kk * jnp.exp(g - log_ws[bb])
        b_h, k_h = bvec * eng, k2 * eng
        b_l, k_l = bvec * egl, k2 * egl
        p_c = jnp.exp(g_last)
        for h in range(n_heads):
            sl = slice(h * hd, (h + 1) * hd)
            lhss.append(jnp.concatenate([kk_t[:, sl], r_t[:, sl]], axis=0))
            rhss.append(jnp.concatenate([b_h[:, sl], k_h[:, sl]], axis=0))
            bl_kls.append(jnp.concatenate([b_l[:, sl], k_l[:, sl]], axis=0))
            p_cs.append(p_c[:, sl])
            vhs.append(vfull[bb][:, sl])
    idx = range(nb * n_heads)
    ms = [_dot_nt(lhss[i], rhss[i], PREC_MIX) for i in idx]
    states = [state_ref[i] for i in idx]
    lss = [_dot_nt(lhss[i], states[i], PREC_MIX) for i in idx]
    yield
    a_bbs = [jnp.where(strict, m[0:CHUNK, 0:CHUNK], 0.0) for m in ms]
    a_bks = [jnp.where(strict, m[0:CHUNK, CHUNK:2 * CHUNK], 0.0) for m in ms]
    a_rs = [jnp.where(incl2, m[CHUNK:2 * CHUNK, :], 0.0) for m in ms]
    rhs_u = [lss[i][0:CHUNK] + _dot(a_bks[i], vhs[i], PREC_MIX) for i in idx]
    t_invs = yield from _tri_inverse_stages(a_bbs)
    uvs = [jnp.concatenate([-_dot(t_invs[i], rhs_u[i], PREC_MIX), vhs[i]], axis=0) for i in idx]
    yield
    for i in idx:
        bb, h = divmod(i, n_heads)
        obuf_ref[bb, :, h * hd:(h + 1) * hd] = lss[i][CHUNK:2 * CHUNK] + _dot(a_rs[i], uvs[i], PREC_MIX)
        state_ref[i] = states[i] * p_cs[i] + _dot_tn(uvs[i], bl_kls[i], PREC_MIX)
    yield
    inv_n = 1.0 / hd
    os_ = [obuf_ref[bb] for bb in range(nb)]
    means = [_dot(o, bd, PREC_AUX) * inv_n for o in os_]
    bonus = [_dot(rs[bb] * k2s[bb] * rk_ref[...], bd, PREC_AUX) * vfull[bb] for bb in range(nb)]
    yield
    cens = [o - mean for o, mean in zip(os_, means)]
    vars_ = [_dot(cen * cen, bd, PREC_AUX) * inv_n for cen in cens]
    yield
    for bb in range(nb):
        ln = cens[bb] * lax.rsqrt(vars_[bb] + GROUP_NORM_EPS) * lnw_ref[...] + lnb_ref[...]
        o_ref[bb, :, col0:col0 + width] = (ln + bonus[bb]) * gates[bb]


def _mixer_ab_kernel(n_gdn, n_rwkv, pa_ref, pdb_ref, pb_ref, *refs):
    gdn_params = refs[:n_gdn]
    rwkv_params = refs[n_gdn:n_gdn + n_rwkv]
    o_ref, xbuf_ref, gstate_ref, carry_ref, rstate_ref, obuf_ref = refs[n_gdn + n_rwkv:]
    nb = pa_ref.shape[0]

    @pl.when(pl.program_id(1) == 0)
    def _():
        xbuf_ref[:, 0:8, :] = jnp.zeros((nb, 8, xbuf_ref.shape[2]), F32)
        gstate_ref[...] = jnp.zeros_like(gstate_ref)
        carry_ref[...] = jnp.zeros_like(carry_ref)
        rstate_ref[...] = jnp.zeros_like(rstate_ref)

    a_width = pa_ref.shape[2] // 4
    _interleave([
        _gdn_stages(nb, pa_ref, pdb_ref, *gdn_params, o_ref, xbuf_ref, gstate_ref),
        _rwkv_stages(nb, a_width, pb_ref, *rwkv_params, o_ref, carry_ref, rstate_ref, obuf_ref),
    ])


def _mixer_ab(pa, pdb, pb, gdn_params, rwkv_params):
    b, t, a_cols = pa.shape
    b_cols = pb.shape[2]
    a_width = a_cols // 4
    b_width = (b_cols - 3 * LANES) // 3
    a_heads = a_width // A_HEAD_DIM
    b_heads = b_width // B_HEAD_DIM
    nb = MIXER_BATCH if b % MIXER_BATCH == 0 else 1
    params = list(gdn_params) + list(rwkv_params)
    return pl.pallas_call(
        functools.partial(_mixer_ab_kernel, len(gdn_params), len(rwkv_params)),
        grid=(b // nb, t // CHUNK),
        in_specs=[pl.BlockSpec((nb, CHUNK, a_cols), lambda i, c: (i, c, 0)),
                  pl.BlockSpec((nb, CHUNK, LANES), lambda i, c: (i, c, 0)),
                  pl.BlockSpec((nb, CHUNK, b_cols), lambda i, c: (i, c, 0))]
                 + [pl.BlockSpec(a.shape, lambda i, c: (0, 0)) for a in params],
        out_specs=pl.BlockSpec((nb, CHUNK, a_width + b_width), lambda i, c: (i, c, 0)),
        out_shape=jax.ShapeDtypeStruct((b, t, a_width + b_width), F32),
        scratch_shapes=[pltpu.VMEM((nb, 8 + CHUNK, 3 * a_width), F32),
                        pltpu.VMEM((nb * a_heads, A_HEAD_DIM, A_HEAD_DIM), F32),
                        pltpu.VMEM((nb, 8, b_cols), F32),
                        pltpu.VMEM((nb * b_heads, B_HEAD_DIM, B_HEAD_DIM), F32),
                        pltpu.VMEM((nb, CHUNK, b_width), F32)],
        compiler_params=_cparams(("parallel", "arbitrary")),
        name="mixer_ab",
    )(pa, pdb, pb, *params)


def _hgrn_tables():
    i = np.arange(CHUNK)[:, None]
    t = np.arange(CHUNK)[None, :]
    blocks, pair = [], []
    for ls in range(HGRN_LEVELS - 1, -1, -1):
        s = 1 << ls
        mid = (i >> (ls + 1) << (ls + 1)) + s
        q_rows, k_rows = _hgrn_level_rows(ls)
        blocks.append(((i >= mid) & (t >= mid) & (t <= i))[q_rows])
        blocks.append(((i < mid) & (t > i) & (t <= mid - 1))[k_rows])
        pair.append((((i >> ls) ^ (t >> ls)) == 1) & (i > t))
    blocks.append(t <= i)
    blocks.append(t > i)
    sums = np.concatenate(blocks, axis=0).astype(np.float32)
    return np.concatenate([sums] * 3, axis=1), np.stack(pair).astype(np.float32)


def _hgrn_level_rows(ls):
    rows = np.arange(CHUNK)
    if (1 << ls) < SUBLANES:
        return rows, rows
    upper = ((rows >> ls) & 1) == 1
    return rows[upper], rows[~upper]


def _hgrn_kernel(p_ref, loglb_ref, log1mlb_ref, onorm_ref, sums_ref, pair_ref, o_ref, state_ref):
    nb = p_ref.shape[0]
    n_heads = state_ref.shape[0] // nb
    hd = C_KEY_DIM
    kw = n_heads * hd

    @pl.when(pl.program_id(1) == 0)
    def _():
        state_ref[...] = jnp.zeros_like(state_ref)

    def take_rows(a, rows):
        runs, start = [], 0
        for n in range(1, len(rows) + 1):
            if n == len(rows) or rows[n] != rows[n - 1] + 1:
                runs.append(a[int(rows[start]):int(rows[n - 1]) + 1])
                start = n
        return runs[0] if len(runs) == 1 else jnp.concatenate(runs, axis=0)

    def spread_rows(a, rows):
        if len(rows) == CHUNK:
            return a
        pieces, pos, taken = [], 0, 0
        while pos < CHUNK:
            run = 1
            inside = pos in rows
            while pos + run < CHUNK and ((pos + run) in rows) == inside:
                run += 1
            if inside:
                pieces.append(a[taken:taken + run])
                taken += run
            else:
                pieces.append(jnp.zeros((run, a.shape[1]), a.dtype))
            pos += run
        return jnp.concatenate(pieces, axis=0)

    qs, ks, exps = [], [], []
    for bb in range(nb):
        qs.append(_silu(p_ref[bb, :, 0:kw]))
        log_f = _log_forget(p_ref[bb, :, kw:2 * kw], loglb_ref[...], log1mlb_ref[...])
        ks.append(1.0 - jnp.exp(log_f))
        exps.append(_dot(sums_ref[...], jnp.concatenate(_split3(log_f), axis=0)))

    idx = [(bb, h) for bb in range(nb) for h in range(n_heads)]
    states = [state_ref[bb * n_heads + h] for bb, h in idx]
    scores = [None] * len(idx)
    offset = 0
    for lvl in range(HGRN_LEVELS):
        q_rows, k_rows = _hgrn_level_rows(HGRN_LEVELS - 1 - lvl)
        q_rows, k_rows = [int(r) for r in q_rows], [int(r) for r in k_rows]
        e_q = [e[offset:offset + len(q_rows)] for e in exps]
        e_k = [e[offset + len(q_rows):offset + len(q_rows) + len(k_rows)] for e in exps]
        offset += len(q_rows) + len(k_rows)
        q_l = [(take_rows(qs[bb], q_rows) * jnp.exp(e_q[bb])).astype(BF16) for bb in range(nb)]
        k_l = [spread_rows(take_rows(ks[bb], k_rows) * jnp.exp(e_k[bb]), k_rows).astype(BF16) for bb in range(nb)]
        mask = None if (lvl == 0 and len(q_rows) < CHUNK) else pair_ref[lvl]
        for n, (bb, h) in enumerate(idx):
            sl = slice(h * hd, (h + 1) * hd)
            part = spread_rows(_dot_nt(q_l[bb][:, sl], k_l[bb][:, sl]), q_rows)
            if mask is not None:
                part = part * mask
            scores[n] = part if scores[n] is None else scores[n] + part
    g_cum = [e[offset:offset + CHUNK] for e in exps]
    q_g = [qs[bb] * jnp.exp(g_cum[bb]) for bb in range(nb)]
    k_r = [ks[bb] * jnp.exp(exps[bb][offset + CHUNK:offset + 2 * CHUNK]) for bb in range(nb)]
    diag = [qs[bb] * ks[bb] for bb in range(nb)]
    outs = []
    for n, (bb, h) in enumerate(idx):
        sl = slice(h * hd, (h + 1) * hd)
        v = p_ref[bb, :, 2 * kw + h * hd: 2 * kw + (h + 1) * hd]
        o = _dot_nt(q_g[bb][:, sl], states[n], PREC_MIX) + _dot(scores[n], v, PREC_MIX)
        outs.append(o + jnp.sum(diag[bb][:, sl], axis=-1, keepdims=True) * v)
        p_last = jnp.exp(g_cum[bb][CHUNK - 1:CHUNK, sl])
        state_ref[bb * n_heads + h] = states[n] * p_last + _dot_tn(v, k_r[bb][:, sl], PREC_MIX)
    for n, (bb, h) in enumerate(idx):
        gate = p_ref[bb, :, 3 * kw + h * hd: 3 * kw + (h + 1) * hd]
        o_ref[bb, :, h * hd:(h + 1) * hd] = _rms(outs[n], onorm_ref[...]) * _silu(gate)


def _hgrn(proj, log_lb, log1m_lb, onorm):
    b, t, cols = proj.shape
    kw = cols // 4
    n_heads = kw // C_KEY_DIM
    nb = MIXER_BATCH if b % MIXER_BATCH == 0 else 1
    sums, pair = _hgrn_tables()
    sums = jnp.asarray(sums, BF16)
    pair = jnp.asarray(pair, F32)
    return pl.pallas_call(
        _hgrn_kernel,
        grid=(b // nb, t // CHUNK),
        in_specs=[pl.BlockSpec((nb, CHUNK, cols), lambda i, c: (i, c, 0)),
                  pl.BlockSpec((1, kw), lambda i, c: (0, 0)),
                  pl.BlockSpec((1, kw), lambda i, c: (0, 0)),
                  pl.BlockSpec((1, C_KEY_DIM), lambda i, c: (0, 0)),
                  pl.BlockSpec(sums.shape, lambda i, c: (0, 0)),
                  pl.BlockSpec(pair.shape, lambda i, c: (0, 0, 0))],
        out_specs=pl.BlockSpec((nb, CHUNK, kw), lambda i, c: (i, c, 0)),
        out_shape=jax.ShapeDtypeStruct((b, t, kw), F32),
        scratch_shapes=[pltpu.VMEM((nb * n_heads, C_KEY_DIM, C_KEY_DIM), F32)],
        compiler_params=_cparams(("parallel", "arbitrary")),
        name="hgrn2",
    )(proj, log_lb, log1m_lb, onorm, sums, pair)


def _route(logits):
    lane = _iota2(logits.shape, 1)
    neg = -jnp.inf
    gmask = lane < N_GROUPS
    gl = jnp.where(gmask, logits, neg)
    gmax = jnp.max(gl, axis=-1, keepdims=True)
    gsel = jnp.min(jnp.where(gl == gmax, lane, LANES), axis=-1, keepdims=True)
    group_gate = 1.0 / jnp.sum(jnp.where(gmask, jnp.exp(gl - gmax), 0.0), axis=-1, keepdims=True)
    lo = N_GROUPS + EXPERTS_PER_GROUP * gsel
    emask = (lane >= lo) & (lane < lo + EXPERTS_PER_GROUP)
    el = jnp.where(emask, logits, neg)
    ee = jnp.where(emask, jnp.exp(el - jnp.max(el, axis=-1, keepdims=True)), 0.0)
    prob = ee / jnp.sum(ee, axis=-1, keepdims=True)
    p1 = jnp.max(jnp.where(emask, prob, -1.0), axis=-1, keepdims=True)
    i1 = jnp.min(jnp.where(emask & (prob == p1), lane, LANES), axis=-1, keepdims=True)
    rest = emask & (lane != i1)
    p2 = jnp.max(jnp.where(rest, prob, -1.0), axis=-1, keepdims=True)
    i2 = jnp.min(jnp.where(rest & (prob == p2), lane, LANES), axis=-1, keepdims=True)
    scale = group_gate / (p1 + p2)
    comb = jnp.where(lane == i1, p1 * scale, 0.0) + jnp.where(lane == i2, p2 * scale, 0.0)
    return comb, gsel


def _sorted_rows(tile_rows):
    return tile_rows + N_GROUPS * SEG_ALIGN


def _xattn_route_kernel(x_ref, a_ref, wmix_ref, gain_ref, wq_ref, k_ref, v_ref, wo_ref, gffn_ref, wr_ref, br_ref,
                        x2_ref, xs_ref, cs_ref, info_ref, seg_ref):
    n_sub = xs_ref.shape[0]
    tq = x_ref.shape[0] // n_sub
    d = x_ref.shape[1]
    hd = d // XA_HEADS
    rows = xs_ref.shape[1]

    def tile_stages(u):
        tok = slice(u * tq, (u + 1) * tq)
        x = x_ref[tok, :] + _dot(a_ref[tok, :].astype(BF16), wmix_ref[...])
        q = _dot(_rms(x, gain_ref[...]).astype(BF16), wq_ref[...]).astype(BF16)
        yield
        outs = []
        for h in range(XA_HEADS):
            sl = slice(h * hd, (h + 1) * hd)
            s = _dot_nt(q[:, sl], k_ref[:, sl]) * (hd ** -0.5)
            p = jnp.exp(s - jnp.max(s, axis=-1, keepdims=True))
            p = p / jnp.sum(p, axis=-1, keepdims=True)
            outs.append(_dot(p.astype(BF16), v_ref[:, sl]).astype(BF16))
        yield
        x2 = x + _dot(jnp.concatenate(outs, axis=-1), wo_ref[...])
        x2_ref[tok, :] = x2
        xn = _rms(x2, gffn_ref[...])
        xn_hi = xn.astype(BF16)
        xn_lo = (xn - xn_hi.astype(F32)).astype(BF16)
        hi_part = _dot(xn_hi, wr_ref[...])
        logits = hi_part[:, 0:LANES] + hi_part[:, LANES:2 * LANES] + _dot(xn_lo, wr_ref[:, 0:LANES])
        yield
        comb, gsel = _route(logits + br_ref[...])
        lane = _iota2((tq, LANES), 1)
        onehot = (lane == gsel).astype(F32)
        earlier = (_iota2((tq, tq), 0) > _iota2((tq, tq), 1)).astype(BF16)
        before = _dot(earlier, onehot.astype(BF16))
        yield
        rank = jnp.sum(onehot * before, axis=-1, keepdims=True)
        counts = before[tq - 1:tq, :] + onehot[tq - 1:tq, :]
        padded = jnp.floor((counts + (SEG_ALIGN - 1)) * (1.0 / SEG_ALIGN)) * SEG_ALIGN
        lane1 = _iota2((1, LANES), 1)
        starts = jnp.zeros((1, LANES), F32)
        for g in range(N_GROUPS - 1):
            starts = starts + jnp.where(lane1 > g, padded[:, g:g + 1], 0.0)
        dest = jnp.sum(onehot * starts, axis=-1, keepdims=True) + rank
        place = (_iota2((tq, rows), 1).astype(F32) == dest).astype(BF16)
        comb_hi = comb.astype(BF16)
        comb_lo = (comb - comb_hi.astype(F32)).astype(BF16)
        placed = _dot_tn(place, jnp.concatenate([xn_hi, comb_hi, comb_lo], axis=1))
        xs_ref[u] = placed[:, 0:d].astype(BF16)
        cs_ref[u] = placed[:, d:d + LANES] + placed[:, d + LANES:d + 2 * LANES]
        info_ref[tok, :] = jnp.where(lane == 0, dest, 0.0)
        seg_ref[u] = jnp.concatenate([starts, padded, jnp.zeros((6, LANES), F32)], axis=0)

    _interleave([tile_stages(u) for u in range(n_sub)])


def _xattn_route(x, act, w_mix, gain, wq, k, v, wo, gain_ffn, w_router, b_router, tq=512):
    b, t, d = x.shape
    m = k.shape[1]
    tq = min(tq, t)
    tiles_per_row = t // tq
    n_tiles = b * tiles_per_row
    rows = _sorted_rows(tq)
    n_sub = XATTN_TILES_PER_STEP if tiles_per_row % XATTN_TILES_PER_STEP == 0 else 1
    steps_per_row = tiles_per_row // n_sub

    def tile(i, j):
        return (i * steps_per_row + j, 0, 0)

    return pl.pallas_call(
        _xattn_route_kernel,
        grid=(b, steps_per_row),
        in_specs=[pl.BlockSpec((None, n_sub * tq, d), lambda i, j: (i, j, 0)),
                  pl.BlockSpec((None, n_sub * tq, act.shape[2]), lambda i, j: (i, j, 0)),
                  pl.BlockSpec(w_mix.shape, lambda i, j: (0, 0)),
                  pl.BlockSpec((1, d), lambda i, j: (0, 0)),
                  pl.BlockSpec((d, d), lambda i, j: (0, 0)),
                  pl.BlockSpec((None, m, d), lambda i, j: (i, 0, 0)),
                  pl.BlockSpec((None, m, d), lambda i, j: (i, 0, 0)),
                  pl.BlockSpec((d, d), lambda i, j: (0, 0)),
                  pl.BlockSpec((1, d), lambda i, j: (0, 0)),
                  pl.BlockSpec((d, 2 * LANES), lambda i, j: (0, 0)),
                  pl.BlockSpec((1, LANES), lambda i, j: (0, 0))],
        out_specs=[pl.BlockSpec((None, n_sub * tq, d), lambda i, j: (i, j, 0)),
                   pl.BlockSpec((n_sub, rows, d), tile),
                   pl.BlockSpec((n_sub, rows, LANES), tile),
                   pl.BlockSpec((None, n_sub * tq, LANES), lambda i, j: (i, j, 0)),
                   pl.BlockSpec((n_sub, 8, LANES), tile)],
        out_shape=[jax.ShapeDtypeStruct((b, t, d), F32),
                   jax.ShapeDtypeStruct((n_tiles, rows, d), BF16),
                   jax.ShapeDtypeStruct((n_tiles, rows, LANES), F32),
                   jax.ShapeDtypeStruct((b, t, LANES), F32),
                   jax.ShapeDtypeStruct((n_tiles, 8, LANES), F32)],
        compiler_params=_cparams(("parallel", "parallel")),
        name="xattn_route",
    )(x, act, w_mix, gain.reshape(1, d), wq, k, v, wo, gain_ffn.reshape(1, d), w_router, b_router)


def _moe_ffn_kernel(seg_ref, xs_ref, cs_ref, w1_ref, w3_ref, w2_ref, ys_ref, xbuf_ref, cbuf_ref, ybuf_ref):
    win = pl.program_id(0)
    grp = pl.program_id(1)
    n_win = xs_ref.shape[0]

    @pl.when((win == 0) & (grp == 0))
    def _():
        xbuf_ref[...] = jnp.zeros_like(xbuf_ref)
        cbuf_ref[...] = jnp.zeros_like(cbuf_ref)

    @pl.when(grp == 0)
    def _():
        ys_ref[...] = jnp.zeros_like(ys_ref)

    def segment(t):
        base = (win * n_win + t) * 8
        return seg_ref[base + grp], seg_ref[base + N_GROUPS + grp]

    def copy_segments(copy_piece):
        packed = jnp.int32(0)
        for t in range(n_win):
            start, size = segment(t)

            def body(j, carry, t=t, start=start, packed=packed):
                copy_piece(t, pl.multiple_of(start + j * SEG_ALIGN, SEG_ALIGN),
                           pl.multiple_of(packed + j * SEG_ALIGN, SEG_ALIGN))
                return carry

            lax.fori_loop(0, size // SEG_ALIGN, body, 0)
            packed = packed + size
        return packed

    def gather(t, src, dst):
        xbuf_ref[pl.ds(dst, SEG_ALIGN), :] = xs_ref[t, pl.ds(src, SEG_ALIGN), :]
        cbuf_ref[pl.ds(dst, SEG_ALIGN), :] = cs_ref[t, pl.ds(src, SEG_ALIGN), :]

    total = copy_segments(gather)

    def ffn_piece(r0, n_rows):
        xc = xbuf_ref[pl.ds(r0, n_rows), :]
        cb = cbuf_ref[pl.ds(r0, n_rows), :]
        lane = _iota2(cb.shape, 1)
        hids = []
        for e in range(EXPERTS_PER_GROUP):
            weight = jnp.sum(jnp.where(lane == N_GROUPS + EXPERTS_PER_GROUP * grp + e, cb, 0.0),
                             axis=-1, keepdims=True)
            hids.append((_silu(_dot(xc, w1_ref[e])) * _dot(xc, w3_ref[e]) * weight).astype(BF16))
        ybuf_ref[pl.ds(r0, n_rows), :] = _dot(jnp.concatenate(hids, axis=-1), w2_ref[...]).astype(ybuf_ref.dtype)

    def chunk(c, carry):
        ffn_piece(pl.multiple_of(c * MOE_CHUNK, MOE_CHUNK), MOE_CHUNK)
        return carry

    n_full = total // MOE_CHUNK
    lax.fori_loop(0, n_full, chunk, 0)
    rest = total - n_full * MOE_CHUNK
    rest_start = pl.multiple_of(n_full * MOE_CHUNK, MOE_CHUNK)
    lower = 0
    for n_rows in (MOE_CHUNK // 4, MOE_CHUNK // 2, MOE_CHUNK):
        @pl.when((rest > lower) & (rest <= n_rows))
        def _(n_rows=n_rows):
            ffn_piece(rest_start, n_rows)
        lower = n_rows

    def scatter(t, dst, src):
        ys_ref[t, pl.ds(dst, SEG_ALIGN), :] = ybuf_ref[pl.ds(src, SEG_ALIGN), :]

    copy_segments(scatter)


def _moe_ffn(seg, xs, cs, w1, w3, w2g):
    n_tiles, rows, d = xs.shape
    ff = w1.shape[2]
    n_win = next(w for w in (MOE_WINDOW, 2, 1) if n_tiles % w == 0)
    buf_rows = -(-(n_win * rows) // MOE_CHUNK) * MOE_CHUNK
    grid_spec = pltpu.PrefetchScalarGridSpec(
        num_scalar_prefetch=1,
        grid=(n_tiles // n_win, N_GROUPS),
        in_specs=[pl.BlockSpec((n_win, rows, d), lambda w, g, seg: (w, 0, 0)),
                  pl.BlockSpec((n_win, rows, LANES), lambda w, g, seg: (w, 0, 0)),
                  pl.BlockSpec((EXPERTS_PER_GROUP, d, ff), lambda w, g, seg: (g, 0, 0)),
                  pl.BlockSpec((EXPERTS_PER_GROUP, d, ff), lambda w, g, seg: (g, 0, 0)),
                  pl.BlockSpec((None, EXPERTS_PER_GROUP * ff, d), lambda w, g, seg: (g, 0, 0))],
        out_specs=pl.BlockSpec((n_win, rows, d), lambda w, g, seg: (w, 0, 0)),
        scratch_shapes=[pltpu.VMEM((buf_rows, d), BF16),
                        pltpu.VMEM((buf_rows, LANES), F32),
                        pltpu.VMEM((buf_rows, d), BF16)])
    return pl.pallas_call(
        _moe_ffn_kernel,
        grid_spec=grid_spec,
        out_shape=jax.ShapeDtypeStruct((n_tiles, rows, d), BF16),
        compiler_params=_cparams(("arbitrary", "arbitrary")),
        name="moe_ffn",
    )(seg, xs, cs, w1, w3, w2g)


def _unsorted_sum(x_ref, info_ref, ys_ref):
    info = info_ref[...]
    dest = jnp.sum(jnp.where(_iota2(info.shape, 1) == 0, info, 0.0), axis=-1, keepdims=True)
    rows = ys_ref.shape[0]
    place = (_iota2((info.shape[0], rows), 1).astype(F32) == dest).astype(BF16)
    return x_ref[...] + _dot(place, ys_ref[...])


def _moe_unsort_kernel(n_w, x_ref, info_ref, ys_ref, gain_ref, *refs):
    w_refs, o_refs = refs[:n_w], refs[n_w:]
    y = _unsorted_sum(x_ref, info_ref, ys_ref)
    yn = _rms(y, gain_ref[...])
    if n_w == 0:
        o_refs[0][...] = yn
        return
    o_refs[0][...] = y
    yn = yn.astype(BF16)
    for w_ref, o_ref in zip(w_refs, o_refs[1:]):
        n = w_ref.shape[1]
        for c0 in range(0, n, 512):
            c1 = min(n, c0 + 512)
            o_ref[:, c0:c1] = _dot(yn, w_ref[:, c0:c1])


def _moe_unsort(x, info, ys, gain, ws=()):
    b, t, d = x.shape
    n_tiles, rows, _ = ys.shape
    tiles_per_row = n_tiles // b
    tq = t // tiles_per_row

    def token_block(cols):
        return pl.BlockSpec((None, tq, cols), lambda i, j: (i, j, 0))

    out_cols = [d] + [w.shape[1] for w in ws]
    outs = pl.pallas_call(
        functools.partial(_moe_unsort_kernel, len(ws)),
        grid=(b, tiles_per_row),
        in_specs=[token_block(d), token_block(LANES),
                  pl.BlockSpec((None, rows, d), lambda i, j: (i * tiles_per_row + j, 0, 0)),
                  pl.BlockSpec((1, d), lambda i, j: (0, 0))]
                 + [pl.BlockSpec(w.shape, lambda i, j: (0, 0)) for w in ws],
        out_specs=[token_block(c) for c in out_cols],
        out_shape=[jax.ShapeDtypeStruct((b, t, c), F32) for c in out_cols],
        compiler_params=_cparams(("parallel", "parallel")),
        name="moe_unsort",
    )(x, info, ys, gain.reshape(1, d), *ws)
    return outs


def _pad_cols(w, n):
    return jnp.pad(w, ((0, 0), (0, n - w.shape[1])))


def _pad_rows(w, n):
    return jnp.pad(w, ((0, n - w.shape[0]), (0, 0)))


def _row(v, n=None):
    v = v.reshape(1, -1).astype(F32)
    return v if n is None else _pad_cols(v, n)


def kernel(x, mem, norm_mix, norm_xattn, norm_mem, norm_ffn, norm_final, ab_w_in, ab_conv, ab_a_log, ab_dt_bias, ab_onorm, ab_shift_mu, ab_w0, ab_w2, ab_a0, ab_a2, ab_g2, ab_k_k, ab_k_a, ab_r_k, ab_ln_w, ab_ln_b, ab_w_out, c_w_in, c_lb_param, c_onorm, c_w_out, xa_wq, xa_wkv, xa_wo, moe_w_group, moe_b_group, moe_w_expert, moe_b_expert, moe_w1, moe_w3, moe_w2):
    bsz, seq, d = x.shape
    n_mem = mem.shape[1]
    depth = norm_mix.shape[0]
    a_width = d // 2
    b_width = d // 2
    a_heads = a_width // A_HEAD_DIM
    a_cols = 4 * a_width + 2 * a_heads
    n_tok = bsz * seq

    lb_p = jax.nn.softmax(c_lb_param.astype(F32), axis=0)
    lower_bounds = jnp.cumsum(lb_p, axis=0) - lb_p[0]

    def project(x_cur, moe_pending, gain, ws):
        if moe_pending is None:
            projs = _norm_proj(x_cur.reshape(n_tok, d), gain, ws)
            return x_cur, [p.reshape(bsz, seq, -1) for p in projs]
        x_new, *projs = _moe_unsort(*moe_pending, gain, ws)
        return x_new, projs

    x2 = x
    moe_pending = None
    mem2 = mem.reshape(bsz * n_mem, d)
    for layer in range(depth):
        if layer % 2 == 0:
            e = layer // 2
            w_in = ab_w_in[e]
            w_a = w_in[:, :4 * a_width].astype(BF16)
            w_db = _pad_cols(w_in[:, 4 * a_width:a_cols], LANES).astype(BF16)
            wb = w_in[:, a_cols:]
            o1, o2 = 3 * b_width, 3 * b_width + B_DECAY_LORA
            o3 = o2 + B_ICLR_LORA
            w_b = jnp.concatenate([wb[:, :o1], _pad_cols(wb[:, o1:o2], LANES), _pad_cols(wb[:, o2:o3], LANES),
                                   wb[:, o3:]], axis=1).astype(BF16)
            mu = ab_shift_mu[e].reshape(1, -1)
            mu_p = jnp.concatenate([mu[:, :o1], _pad_cols(mu[:, o1:o2], LANES), _pad_cols(mu[:, o2:o3], LANES),
                                    mu[:, o3:]], axis=1)
            head_id = jnp.arange(b_width) // B_HEAD_DIM
            bd = (head_id[:, None] == head_id[None, :]).astype(F32)
            gdn_params = [ab_conv[e], _row(ab_a_log[e], LANES), _row(ab_dt_bias[e], LANES), _row(ab_onorm[e])]
            rwkv_params = [mu_p, _row(ab_w0[e]), _pad_rows(ab_w2[e], LANES), _row(ab_a0[e]),
                           _pad_rows(ab_a2[e], LANES), ab_g2[e], _row(ab_k_k[e]), _row(ab_k_a[e]),
                           _row(ab_r_k[e]), _row(ab_ln_w[e]), _row(ab_ln_b[e]), bd]
            x2, (pa, pdb, pb) = project(x2, moe_pending, norm_mix[layer], [w_a, w_db, w_b])
            mix = _mixer_ab(pa, pdb, pb, gdn_params, rwkv_params)
            w_mix = ab_w_out[e].astype(BF16)
        else:
            o = layer // 2
            x2, (proj,) = project(x2, moe_pending, norm_mix[layer], [c_w_in[o].astype(BF16)])
            lb = lower_bounds[layer].reshape(1, -1)
            mix = _hgrn(proj, jnp.log(lb), jnp.log1p(-lb), _row(c_onorm[o]))
            w_mix = c_w_out[o].astype(BF16)

        wkv = xa_wkv[layer].astype(BF16)
        k_mem, v_mem = _norm_proj(mem2, norm_mem[layer], [wkv[:, :d], wkv[:, d:]], out_dtype=BF16)
        w_r = _pad_cols(jnp.concatenate([moe_w_group[layer], moe_w_expert[layer]], axis=1), LANES).astype(F32)
        w_r_hi = w_r.astype(BF16)
        w_r_lo = (w_r - w_r_hi.astype(F32)).astype(BF16)
        b_router = _row(jnp.concatenate([moe_b_group[layer], moe_b_expert[layer]]), LANES)
        x3, xs, cs, info, seg = _xattn_route(
            x2, mix, w_mix, norm_xattn[layer], xa_wq[layer].astype(BF16),
            k_mem.reshape(bsz, n_mem, d), v_mem.reshape(bsz, n_mem, d), xa_wo[layer].astype(BF16),
            norm_ffn[layer], jnp.concatenate([w_r_hi, w_r_lo], axis=1), b_router)
        seg_i = jnp.concatenate([seg[:, 0, :N_GROUPS], seg[:, 1, :N_GROUPS]], axis=1).astype(jnp.int32).reshape(-1)
        ff = moe_w2.shape[2]
        ys = _moe_ffn(seg_i, xs, cs, moe_w1[layer].astype(BF16), moe_w3[layer].astype(BF16),
                      moe_w2[layer].reshape(N_GROUPS, EXPERTS_PER_GROUP * ff, d).astype(BF16))
        moe_pending = (x3, info, ys)
    (out,) = _moe_unsort(*moe_pending, norm_final)
    return out
```

```python
import functools

import jax
import jax.numpy as jnp
import numpy as np
from jax import lax
from jax.experimental import pallas as pl
from jax.experimental.pallas import tpu as pltpu

F32 = jnp.float32
BF16 = jnp.bfloat16
HI = lax.Precision.HIGHEST
ONE_PASS = "bf16"

PREC_INV = ONE_PASS
PREC_MIX = ONE_PASS
PREC_AUX = ONE_PASS

CHUNK = 64
HGRN_LEVELS = CHUNK.bit_length() - 1
NORM_EPS = 1e-6
GROUP_NORM_EPS = 64e-5
CONV_WIDTH = 4

A_HEAD_DIM = 128
B_HEAD_DIM = 64
B_DECAY_LORA = 64
B_ICLR_LORA = 64
B_GATE_LORA = 128
C_KEY_DIM = 128
XA_HEADS = 4
N_GROUPS = 4
EXPERTS_PER_GROUP = 4
N_EXPERTS = N_GROUPS * EXPERTS_PER_GROUP

LANES = 128
SUBLANES = 8
VMEM_LIMIT = 56 * 1024 * 1024
MIXER_BATCH = 4
SEG_ALIGN = 16
MOE_WINDOW = 4
MOE_CHUNK = 288
XATTN_TILES_PER_STEP = 2


def _cparams(sem):
    return pltpu.CompilerParams(dimension_semantics=sem, vmem_limit_bytes=VMEM_LIMIT)


def _dot_dims(a, b, dims, prec):
    if prec == ONE_PASS:
        a, b, prec = a.astype(BF16), b.astype(BF16), None
    return lax.dot_general(a, b, (dims, ((), ())), preferred_element_type=F32, precision=prec)


def _dot(a, b, prec=None):
    return _dot_dims(a, b, ((1,), (0,)), prec)


def _dot_nt(a, b, prec=None):
    return _dot_dims(a, b, ((1,), (1,)), prec)


def _dot_tn(a, b, prec=None):
    return _dot_dims(a, b, ((0,), (0,)), prec)


def _rms(x, gain):
    return x * lax.rsqrt(jnp.mean(x * x, axis=-1, keepdims=True) + NORM_EPS) * gain


def _sigmoid(x):
    return 1.0 / (1.0 + jnp.exp(-x))


def _silu(x):
    return x * _sigmoid(x)


def _softplus(x):
    return jnp.maximum(x, 0.0) + jnp.log(1.0 + jnp.exp(-jnp.abs(x)))


def _iota2(shape, dim):
    return lax.broadcasted_iota(jnp.int32, shape, dim)


def _split3(v):
    v1 = v.astype(BF16)
    v2 = (v - v1.astype(F32)).astype(BF16)
    v3 = (v - v1.astype(F32) - v2.astype(F32)).astype(BF16)
    return v1, v2, v3


def _table_dot(table, v):
    return _dot(jnp.concatenate([table.astype(BF16)] * 3, axis=1), jnp.concatenate(_split3(v), axis=0))


def _table_dot_tn(v, table):
    return _dot_tn(jnp.concatenate(_split3(v), axis=0), jnp.concatenate([table.astype(BF16)] * 3, axis=0))


def _interleave(stage_generators):
    live = list(stage_generators)
    while live:
        for gen in list(live):
            try:
                next(gen)
            except StopIteration:
                live.remove(gen)


def _tri_inverse_stages(mats):
    n = mats[0].shape[0]
    row = _iota2((n, n), 0)
    col = _iota2((n, n), 1)
    eye = (row == col).astype(F32)
    xs = [eye - jnp.where((row >> 1) == (col >> 1), a, 0.0) for a in mats]
    for level in range(1, n.bit_length() - 1):
        joins = ((row >> (level + 1)) == (col >> (level + 1))) & ((row >> level) != (col >> level))
        ts = [_dot(jnp.where(joins, a, 0.0), x, PREC_INV) for a, x in zip(mats, xs)]
        yield
        xs = [x - _dot(x, t, PREC_INV) for x, t in zip(xs, ts)]
        yield
    return xs


def _norm_proj_kernel(n_w, x_ref, g_ref, *refs):
    w_refs, o_refs = refs[:n_w], refs[n_w:]
    xn = _rms(x_ref[...], g_ref[...]).astype(BF16)
    for w_ref, o_ref in zip(w_refs, o_refs):
        n = w_ref.shape[1]
        for c0 in range(0, n, 512):
            c1 = min(n, c0 + 512)
            o_ref[:, c0:c1] = _dot(xn, w_ref[:, c0:c1]).astype(o_ref.dtype)


def _norm_proj(x, gain, ws, out_dtype=F32, tm=512):
    n, d = x.shape
    tm = min(tm, n)
    return pl.pallas_call(
        functools.partial(_norm_proj_kernel, len(ws)),
        grid=(n // tm,),
        in_specs=[pl.BlockSpec((tm, d), lambda i: (i, 0)),
                  pl.BlockSpec((1, d), lambda i: (0, 0))]
                 + [pl.BlockSpec(w.shape, lambda i: (0, 0)) for w in ws],
        out_specs=[pl.BlockSpec((tm, w.shape[1]), lambda i: (i, 0)) for w in ws],
        out_shape=[jax.ShapeDtypeStruct((n, w.shape[1]), out_dtype) for w in ws],
        compiler_params=_cparams(("parallel",)),
        name="norm_proj",
    )(x, gain.reshape(1, d), *ws)


def _log_forget(f, log_lb, log1m_lb):
    a1 = jnp.broadcast_to(log_lb, f.shape)
    b1 = log1m_lb - _softplus(-f)
    return jnp.maximum(a1, b1) + jnp.log(1.0 + jnp.exp(-jnp.abs(a1 - b1)))


def _gdn_stages(nb, pa_ref, pdb_ref, convw_ref, alog_ref, dtb_ref, onorm_ref, o_ref, xbuf_ref, state_ref):
    n_heads = state_ref.shape[0] // nb
    hd = A_HEAD_DIM
    width = n_heads * hd
    row = _iota2((CHUNK, CHUNK), 0)
    col = _iota2((CHUNK, CHUNK), 1)
    incl = row >= col
    strict = row > col
    tril = incl.astype(F32)
    triu = (row <= col).astype(F32)

    qs, ks, vs, betas, gs, g_lasts, decays = [], [], [], [], [], [], []
    for bb in range(nb):
        x = pa_ref[bb, :, 0:3 * width]
        xbuf_ref[bb, 8:8 + CHUNK, :] = x
        acc = convw_ref[0:1, :] * xbuf_ref[bb, 5:5 + CHUNK, :]
        for j in range(1, CONV_WIDTH):
            acc = acc + convw_ref[j:j + 1, :] * xbuf_ref[bb, 5 + j:5 + j + CHUNK, :]
        xbuf_ref[bb, 0:8, :] = x[CHUNK - 8:CHUNK, :]
        qkv = _silu(acc)
        pdb = pdb_ref[bb]
        log_alpha = -jnp.exp(alog_ref[...]) * _softplus(pdb + dtb_ref[...])
        beta_all = _sigmoid(pdb)
        g_all = _table_dot(tril, log_alpha)
        g_all_t = _table_dot_tn(log_alpha, triu)
        for h in range(n_heads):
            q = qkv[:, h * hd:(h + 1) * hd]
            k = qkv[:, width + h * hd: width + (h + 1) * hd]
            qs.append(q * lax.rsqrt(jnp.sum(q * q, axis=-1, keepdims=True) + 1e-6) * (hd ** -0.5))
            ks.append(k * lax.rsqrt(jnp.sum(k * k, axis=-1, keepdims=True) + 1e-6))
            vs.append(qkv[:, 2 * width + h * hd: 2 * width + (h + 1) * hd])
            betas.append(beta_all[:, n_heads + h:n_heads + h + 1])
            g = g_all[:, h:h + 1]
            gs.append(g)
            g_lasts.append(g_all[CHUNK - 1:CHUNK, h:h + 1])
            decays.append(jnp.where(incl, jnp.exp(jnp.where(incl, g - g_all_t[h:h + 1, :], 0.0)), 0.0))
    yield
    idx = range(nb * n_heads)
    k_betas = [ks[i] * betas[i] for i in idx]
    egs = [jnp.exp(g) for g in gs]
    a_lows = [jnp.where(strict, _dot_nt(k_betas[i], ks[i], PREC_MIX) * decays[i], 0.0) for i in idx]
    qks = [_dot_nt(qs[i], ks[i], PREC_MIX) * decays[i] for i in idx]
    states = [state_ref[i] for i in idx]
    o_inters = [_dot(qs[i] * egs[i], states[i], PREC_MIX) for i in idx]
    yield
    t_invs = yield from _tri_inverse_stages(a_lows)
    us = [_dot(t_invs[i], vs[i] * betas[i], PREC_MIX) for i in idx]
    ws = [_dot(t_invs[i], k_betas[i] * egs[i], PREC_MIX) for i in idx]
    yield
    v_news = [us[i] - _dot(ws[i], states[i], PREC_MIX) for i in idx]
    yield
    os_ = [o_inters[i] + _dot(qks[i], v_news[i], PREC_MIX) for i in idx]
    for i in idx:
        state_ref[i] = (states[i] * jnp.exp(g_lasts[i])
                        + _dot_tn(ks[i] * jnp.exp(g_lasts[i] - gs[i]), v_news[i], PREC_MIX))
    yield
    for i in idx:
        bb, h = divmod(i, n_heads)
        gate = pa_ref[bb, :, 3 * width + h * hd: 3 * width + (h + 1) * hd]
        o_ref[bb, :, h * hd:(h + 1) * hd] = _rms(os_[i], onorm_ref[...]) * _silu(gate)


def _rwkv_stages(nb, col0, pb_ref, mu_ref, w0_ref, w2_ref, a0_ref, a2_ref, g2_ref, kk_ref, ka_ref, rk_ref,
                 lnw_ref, lnb_ref, bd_ref, o_ref, carry_ref, state_ref, obuf_ref):
    n_heads = state_ref.shape[0] // nb
    hd = B_HEAD_DIM
    width = n_heads * hd
    row = _iota2((CHUNK, CHUNK), 0)
    col = _iota2((CHUNK, CHUNK), 1)
    strict = row > col
    tril = (row >= col).astype(F32)
    incl2 = _iota2((CHUNK, 2 * CHUNK), 0) >= (_iota2((CHUNK, 2 * CHUNK), 1) & (CHUNK - 1))
    bd = bd_ref[...]

    pss = []
    for bb in range(nb):
        p = pb_ref[bb]
        prev = jnp.where(_iota2(p.shape, 0) == 0, carry_ref[bb, 0:1, :], pltpu.roll(p, 1, 0))
        carry_ref[bb, 0:1, :] = p[CHUNK - 1:CHUNK, :]
        pss.append(p + mu_ref[...] * (prev - p))
    rs = [ps[:, 0:width] for ps in pss]
    kraw = [ps[:, width:2 * width] for ps in pss]
    vfull = [ps[:, 2 * width:3 * width] for ps in pss]
    w_raws = [w0_ref[...] + _dot(jnp.tanh(ps[:, 3 * width:3 * width + LANES]), w2_ref[...], PREC_AUX) for ps in pss]
    iclrs = [_sigmoid(a0_ref[...] + _dot(ps[:, 3 * width + LANES:3 * width + 2 * LANES], a2_ref[...], PREC_AUX))
             for ps in pss]
    gates = [_dot(_sigmoid(ps[:, 3 * width + 2 * LANES:3 * width + 3 * LANES]), g2_ref[...], PREC_AUX) for ps in pss]
    kks = [k * kk_ref[...] for k in kraw]
    kk_ss = [_dot(kk * kk, bd, PREC_AUX) for kk in kks]
    yield
    log_ws = [-jnp.exp(-_softplus(-w_raw) - 0.5) for w_raw in w_raws]
    g_cums = [_table_dot(tril, log_w) for log_w in log_ws]
    yield
    lhss, rhss, bl_kls, p_cs, vhs, k2s = [], [], [], [], [], []
    for bb in range(nb):
        kk = kks[bb] * lax.rsqrt(kk_ss[bb] + 1e-6)
        k2 = kraw[bb] * (1.0 + (iclrs[bb] - 1.0) * ka_ref[...])
        k2s.append(k2)
        bvec = kk * iclrs[bb]
        g = g_cums[bb]
        g_last = g[CHUNK - 1:CHUNK, :]
        eng = jnp.exp(-g)
        egl = jnp.exp(g_last - g)
        r_t = rs[bb] * jnp.exp(g)
        kk_t = kk * jnp.exp(g - log_ws[bb])
        b_h, k_h = bvec * eng, k2 * eng
        b_l, k_l = bvec * egl, k2 * egl
        p_c = jnp.exp(g_last)
        for h in range(n_heads):
            sl = slice(h * hd, (h + 1) * hd)
            lhss.append(jnp.concatenate([kk_t[:, sl], r_t[:, sl]], axis=0))
            rhss.append(jnp.concatenate([b_h[:, sl], k_h[:, sl]], axis=0))
            bl_kls.append(jnp.concatenate([b_l[:, sl], k_l[:, sl]], axis=0))
            p_cs.append(p_c[:, sl])
            vhs.append(vfull[bb][:, sl])
    idx = range(nb * n_heads)
    ms = [_dot_nt(lhss[i], rhss[i], PREC_MIX) for i in idx]
    states = [state_ref[i] for i in idx]
    lss = [_dot_nt(lhss[i], states[i], PREC_MIX) for i in idx]
    yield
    a_bbs = [jnp.where(strict, m[0:CHUNK, 0:CHUNK], 0.0) for m in ms]
    a_bks = [jnp.where(strict, m[0:CHUNK, CHUNK:2 * CHUNK], 0.0) for m in ms]
    a_rs = [jnp.where(incl2, m[CHUNK:2 * CHUNK, :], 0.0) for m in ms]
    rhs_u = [lss[i][0:CHUNK] + _dot(a_bks[i], vhs[i], PREC_MIX) for i in idx]
    t_invs = yield from _tri_inverse_stages(a_bbs)
    uvs = [jnp.concatenate([-_dot(t_invs[i], rhs_u[i], PREC_MIX), vhs[i]], axis=0) for i in idx]
    yield
    for i in idx:
        bb, h = divmod(i, n_heads)
        obuf_ref[bb, :, h * hd:(h + 1) * hd] = lss[i][CHUNK:2 * CHUNK] + _dot(a_rs[i], uvs[i], PREC_MIX)
        state_ref[i] = states[i] * p_cs[i] + _dot_tn(uvs[i], bl_kls[i], PREC_MIX)
    yield
    inv_n = 1.0 / hd
    os_ = [obuf_ref[bb] for bb in range(nb)]
    means = [_dot(o, bd, PREC_AUX) * inv_n for o in os_]
    bonus = [_dot(rs[bb] * k2s[bb] * rk_ref[...], bd, PREC_AUX) * vfull[bb] for bb in range(nb)]
    yield
    cens = [o - mean for o, mean in zip(os_, means)]
    vars_ = [_dot(cen * cen, bd, PREC_AUX) * inv_n for cen in cens]
    yield
    for bb in range(nb):
        ln = cens[bb] * lax.rsqrt(vars_[bb] + GROUP_NORM_EPS) * lnw_ref[...] + lnb_ref[...]
        o_ref[bb, :, col0:col0 + width] = (ln + bonus[bb]) * gates[bb]


def _mixer_ab_kernel(n_gdn, n_rwkv, pa_ref, pdb_ref, pb_ref, *refs):
    gdn_params = refs[:n_gdn]
    rwkv_params = refs[n_gdn:n_gdn + n_rwkv]
    o_ref, xbuf_ref, gstate_ref, carry_ref, rstate_ref, obuf_ref = refs[n_gdn + n_rwkv:]
    nb = pa_ref.shape[0]

    @pl.when(pl.program_id(1) == 0)
    def _():
        xbuf_ref[:, 0:8, :] = jnp.zeros((nb, 8, xbuf_ref.shape[2]), F32)
        gstate_ref[...] = jnp.zeros_like(gstate_ref)
        carry_ref[...] = jnp.zeros_like(carry_ref)
        rstate_ref[...] = jnp.zeros_like(rstate_ref)

    a_width = pa_ref.shape[2] // 4
    _interleave([
        _gdn_stages(nb, pa_ref, pdb_ref, *gdn_params, o_ref, xbuf_ref, gstate_ref),
        _rwkv_stages(nb, a_width, pb_ref, *rwkv_params, o_ref, carry_ref, rstate_ref, obuf_ref),
    ])


def _mixer_ab(pa, pdb, pb, gdn_params, rwkv_params):
    b, t, a_cols = pa.shape
    b_cols = pb.shape[2]
    a_width = a_cols // 4
    b_width = (b_cols - 3 * LANES) // 3
    a_heads = a_width // A_HEAD_DIM
    b_heads = b_width // B_HEAD_DIM
    nb = MIXER_BATCH if b % MIXER_BATCH == 0 else 1
    params = list(gdn_params) + list(rwkv_params)
    return pl.pallas_call(
        functools.partial(_mixer_ab_kernel, len(gdn_params), len(rwkv_params)),
        grid=(b // nb, t // CHUNK),
        in_specs=[pl.BlockSpec((nb, CHUNK, a_cols), lambda i, c: (i, c, 0)),
                  pl.BlockSpec((nb, CHUNK, LANES), lambda i, c: (i, c, 0)),
                  pl.BlockSpec((nb, CHUNK, b_cols), lambda i, c: (i, c, 0))]
                 + [pl.BlockSpec(a.shape, lambda i, c: (0, 0)) for a in params],
        out_specs=pl.BlockSpec((nb, CHUNK, a_width + b_width), lambda i, c: (i, c, 0)),
        out_shape=jax.ShapeDtypeStruct((b, t, a_width + b_width), F32),
        scratch_shapes=[pltpu.VMEM((nb, 8 + CHUNK, 3 * a_width), F32),
                        pltpu.VMEM((nb * a_heads, A_HEAD_DIM, A_HEAD_DIM), F32),
                        pltpu.VMEM((nb, 8, b_cols), F32),
                        pltpu.VMEM((nb * b_heads, B_HEAD_DIM, B_HEAD_DIM), F32),
                        pltpu.VMEM((nb, CHUNK, b_width), F32)],
        compiler_params=_cparams(("parallel", "arbitrary")),
        name="mixer_ab",
    )(pa, pdb, pb, *params)


def _hgrn_tables():
    i = np.arange(CHUNK)[:, None]
    t = np.arange(CHUNK)[None, :]
    blocks, pair = [], []
    for ls in range(HGRN_LEVELS - 1, -1, -1):
        s = 1 << ls
        mid = (i >> (ls + 1) << (ls + 1)) + s
        q_rows, k_rows = _hgrn_level_rows(ls)
        blocks.append(((i >= mid) & (t >= mid) & (t <= i))[q_rows])
        blocks.append(((i < mid) & (t > i) & (t <= mid - 1))[k_rows])
        pair.append((((i >> ls) ^ (t >> ls)) == 1) & (i > t))
    blocks.append(t <= i)
    blocks.append(t > i)
    sums = np.concatenate(blocks, axis=0).astype(np.float32)
    return np.concatenate([sums] * 3, axis=1), np.stack(pair).astype(np.float32)


def _hgrn_level_rows(ls):
    rows = np.arange(CHUNK)
    if (1 << ls) < SUBLANES:
        return rows, rows
    upper = ((rows >> ls) & 1) == 1
    return rows[upper], rows[~upper]


def _hgrn_kernel(p_ref, loglb_ref, log1mlb_ref, onorm_ref, sums_ref, pair_ref, o_ref, state_ref):
    nb = p_ref.shape[0]
    n_heads = state_ref.shape[0] // nb
    hd = C_KEY_DIM
    kw = n_heads * hd

    @pl.when(pl.program_id(1) == 0)
    def _():
        state_ref[...] = jnp.zeros_like(state_ref)

    def take_rows(a, rows):
        runs, start = [], 0
        for n in range(1, len(rows) + 1):
            if n == len(rows) or rows[n] != rows[n - 1] + 1:
                runs.append(a[int(rows[start]):int(rows[n - 1]) + 1])
                start = n
        return runs[0] if len(runs) == 1 else jnp.concatenate(runs, axis=0)

    def spread_rows(a, rows):
        if len(rows) == CHUNK:
            return a
        pieces, pos, taken = [], 0, 0
        while pos < CHUNK:
            run = 1
            inside = pos in rows
            while pos + run < CHUNK and ((pos + run) in rows) == inside:
                run += 1
            if inside:
                pieces.append(a[taken:taken + run])
                taken += run
            else:
                pieces.append(jnp.zeros((run, a.shape[1]), a.dtype))
            pos += run
        return jnp.concatenate(pieces, axis=0)

    qs, ks, exps = [], [], []
    for bb in range(nb):
        qs.append(_silu(p_ref[bb, :, 0:kw]))
        log_f = _log_forget(p_ref[bb, :, kw:2 * kw], loglb_ref[...], log1mlb_ref[...])
        ks.append(1.0 - jnp.exp(log_f))
        exps.append(_dot(sums_ref[...], jnp.concatenate(_split3(log_f), axis=0)))

    idx = [(bb, h) for bb in range(nb) for h in range(n_heads)]
    states = [state_ref[bb * n_heads + h] for bb, h in idx]
    scores = [None] * len(idx)
    offset = 0
    for lvl in range(HGRN_LEVELS):
        q_rows, k_rows = _hgrn_level_rows(HGRN_LEVELS - 1 - lvl)
        q_rows, k_rows = [int(r) for r in q_rows], [int(r) for r in k_rows]
        e_q = [e[offset:offset + len(q_rows)] for e in exps]
        e_k = [e[offset + len(q_rows):offset + len(q_rows) + len(k_rows)] for e in exps]
        offset += len(q_rows) + len(k_rows)
        q_l = [(take_rows(qs[bb], q_rows) * jnp.exp(e_q[bb])).astype(BF16) for bb in range(nb)]
        k_l = [spread_rows(take_rows(ks[bb], k_rows) * jnp.exp(e_k[bb]), k_rows).astype(BF16) for bb in range(nb)]
        mask = None if (lvl == 0 and len(q_rows) < CHUNK) else pair_ref[lvl]
        for n, (bb, h) in enumerate(idx):
            sl = slice(h * hd, (h + 1) * hd)
            part = spread_rows(_dot_nt(q_l[bb][:, sl], k_l[bb][:, sl]), q_rows)
            if mask is not None:
                part = part * mask
            scores[n] = part if scores[n] is None else scores[n] + part
    g_cum = [e[offset:offset + CHUNK] for e in exps]
    q_g = [qs[bb] * jnp.exp(g_cum[bb]) for bb in range(nb)]
    k_r = [ks[bb] * jnp.exp(exps[bb][offset + CHUNK:offset + 2 * CHUNK]) for bb in range(nb)]
    diag = [qs[bb] * ks[bb] for bb in range(nb)]
    outs = []
    for n, (bb, h) in enumerate(idx):
        sl = slice(h * hd, (h + 1) * hd)
        v = p_ref[bb, :, 2 * kw + h * hd: 2 * kw + (h + 1) * hd]
        o = _dot_nt(q_g[bb][:, sl], states[n], PREC_MIX) + _dot(scores[n], v, PREC_MIX)
        outs.append(o + jnp.sum(diag[bb][:, sl], axis=-1, keepdims=True) * v)
        p_last = jnp.exp(g_cum[bb][CHUNK - 1:CHUNK, sl])
        state_ref[bb * n_heads + h] = states[n] * p_last + _dot_tn(v, k_r[bb][:, sl], PREC_MIX)
    for n, (bb, h) in enumerate(idx):
        gate = p_ref[bb, :, 3 * kw + h * hd: 3 * kw + (h + 1) * hd]
        o_ref[bb, :, h * hd:(h + 1) * hd] = _rms(outs[n], onorm_ref[...]) * _silu(gate)


def _hgrn(proj, log_lb, log1m_lb, onorm):
    b, t, cols = proj.shape
    kw = cols // 4
    n_heads = kw // C_KEY_DIM
    nb = MIXER_BATCH if b % MIXER_BATCH == 0 else 1
    sums, pair = _hgrn_tables()
    sums = jnp.asarray(sums, BF16)
    pair = jnp.asarray(pair, F32)
    return pl.pallas_call(
        _hgrn_kernel,
        grid=(b // nb, t // CHUNK),
        in_specs=[pl.BlockSpec((nb, CHUNK, cols), lambda i, c: (i, c, 0)),
                  pl.BlockSpec((1, kw), lambda i, c: (0, 0)),
                  pl.BlockSpec((1, kw), lambda i, c: (0, 0)),
                  pl.BlockSpec((1, C_KEY_DIM), lambda i, c: (0, 0)),
                  pl.BlockSpec(sums.shape, lambda i, c: (0, 0)),
                  pl.BlockSpec(pair.shape, lambda i, c: (0, 0, 0))],
        out_specs=pl.BlockSpec((nb, CHUNK, kw), lambda i, c: (i, c, 0)),
        out_shape=jax.ShapeDtypeStruct((b, t, kw), F32),
        scratch_shapes=[pltpu.VMEM((nb * n_heads, C_KEY_DIM, C_KEY_DIM), F32)],
        compiler_params=_cparams(("parallel", "arbitrary")),
        name="hgrn2",
    )(proj, log_lb, log1m_lb, onorm, sums, pair)


def _route(logits):
    lane = _iota2(logits.shape, 1)
    neg = -jnp.inf
    gmask = lane < N_GROUPS
    gl = jnp.where(gmask, logits, neg)
    gmax = jnp.max(gl, axis=-1, keepdims=True)
    gsel = jnp.min(jnp.where(gl == gmax, lane, LANES), axis=-1, keepdims=True)
    group_gate = 1.0 / jnp.sum(jnp.where(gmask, jnp.exp(gl - gmax), 0.0), axis=-1, keepdims=True)
    lo = N_GROUPS + EXPERTS_PER_GROUP * gsel
    emask = (lane >= lo) & (lane < lo + EXPERTS_PER_GROUP)
    el = jnp.where(emask, logits, neg)
    ee = jnp.where(emask, jnp.exp(el - jnp.max(el, axis=-1, keepdims=True)), 0.0)
    prob = ee / jnp.sum(ee, axis=-1, keepdims=True)
    p1 = jnp.max(jnp.where(emask, prob, -1.0), axis=-1, keepdims=True)
    i1 = jnp.min(jnp.where(emask & (prob == p1), lane, LANES), axis=-1, keepdims=True)
    rest = emask & (lane != i1)
    p2 = jnp.max(jnp.where(rest, prob, -1.0), axis=-1, keepdims=True)
    i2 = jnp.min(jnp.where(rest & (prob == p2), lane, LANES), axis=-1, keepdims=True)
    scale = group_gate / (p1 + p2)
    comb = jnp.where(lane == i1, p1 * scale, 0.0) + jnp.where(lane == i2, p2 * scale, 0.0)
    return comb, gsel


def _sorted_rows(tile_rows):
    return tile_rows + N_GROUPS * SEG_ALIGN


def _xattn_route_kernel(x_ref, a_ref, wmix_ref, gain_ref, wq_ref, k_ref, v_ref, wo_ref, gffn_ref, wr_ref, br_ref,
                        x2_ref, xs_ref, cs_ref, info_ref, seg_ref):
    n_sub = xs_ref.shape[0]
    tq = x_ref.shape[0] // n_sub
    d = x_ref.shape[1]
    hd = d // XA_HEADS
    rows = xs_ref.shape[1]

    def tile_stages(u):
        tok = slice(u * tq, (u + 1) * tq)
        x = x_ref[tok, :] + _dot(a_ref[tok, :].astype(BF16), wmix_ref[...])
        q = _dot(_rms(x, gain_ref[...]).astype(BF16), wq_ref[...]).astype(BF16)
        yield
        outs = []
        for h in range(XA_HEADS):
            sl = slice(h * hd, (h + 1) * hd)
            s = _dot_nt(q[:, sl], k_ref[:, sl]) * (hd ** -0.5)
            p = jnp.exp(s - jnp.max(s, axis=-1, keepdims=True))
            p = p / jnp.sum(p, axis=-1, keepdims=True)
            outs.append(_dot(p.astype(BF16), v_ref[:, sl]).astype(BF16))
        yield
        x2 = x + _dot(jnp.concatenate(outs, axis=-1), wo_ref[...])
        x2_ref[tok, :] = x2
        xn = _rms(x2, gffn_ref[...])
        xn_hi = xn.astype(BF16)
        xn_lo = (xn - xn_hi.astype(F32)).astype(BF16)
        hi_part = _dot(xn_hi, wr_ref[...])
        logits = hi_part[:, 0:LANES] + hi_part[:, LANES:2 * LANES] + _dot(xn_lo, wr_ref[:, 0:LANES])
        yield
        comb, gsel = _route(logits + br_ref[...])
        lane = _iota2((tq, LANES), 1)
        onehot = (lane == gsel).astype(F32)
        earlier = (_iota2((tq, tq), 0) > _iota2((tq, tq), 1)).astype(BF16)
        before = _dot(earlier, onehot.astype(BF16))
        yield
        rank = jnp.sum(onehot * before, axis=-1, keepdims=True)
        counts = before[tq - 1:tq, :] + onehot[tq - 1:tq, :]
        padded = jnp.floor((counts + (SEG_ALIGN - 1)) * (1.0 / SEG_ALIGN)) * SEG_ALIGN
        lane1 = _iota2((1, LANES), 1)
        starts = jnp.zeros((1, LANES), F32)
        for g in range(N_GROUPS - 1):
            starts = starts + jnp.where(lane1 > g, padded[:, g:g + 1], 0.0)
        dest = jnp.sum(onehot * starts, axis=-1, keepdims=True) + rank
        place = (_iota2((tq, rows), 1).astype(F32) == dest).astype(BF16)
        comb_hi = comb.astype(BF16)
        comb_lo = (comb - comb_hi.astype(F32)).astype(BF16)
        placed = _dot_tn(place, jnp.concatenate([xn_hi, comb_hi, comb_lo], axis=1))
        xs_ref[u] = placed[:, 0:d].astype(BF16)
        cs_ref[u] = placed[:, d:d + LANES] + placed[:, d + LANES:d + 2 * LANES]
        info_ref[tok, :] = jnp.where(lane == 0, dest, 0.0)
        seg_ref[u] = jnp.concatenate([starts, padded, jnp.zeros((6, LANES), F32)], axis=0)

    _interleave([tile_stages(u) for u in range(n_sub)])


def _xattn_route(x, act, w_mix, gain, wq, k, v, wo, gain_ffn, w_router, b_router, tq=512):
    b, t, d = x.shape
    m = k.shape[1]
    tq = min(tq, t)
    tiles_per_row = t // tq
    n_tiles = b * tiles_per_row
    rows = _sorted_rows(tq)
    n_sub = XATTN_TILES_PER_STEP if tiles_per_row % XATTN_TILES_PER_STEP == 0 else 1
    steps_per_row = tiles_per_row // n_sub

    def tile(i, j):
        return (i * steps_per_row + j, 0, 0)

    return pl.pallas_call(
        _xattn_route_kernel,
        grid=(b, steps_per_row),
        in_specs=[pl.BlockSpec((None, n_sub * tq, d), lambda i, j: (i, j, 0)),
                  pl.BlockSpec((None, n_sub * tq, act.shape[2]), lambda i, j: (i, j, 0)),
                  pl.BlockSpec(w_mix.shape, lambda i, j: (0, 0)),
                  pl.BlockSpec((1, d), lambda i, j: (0, 0)),
                  pl.BlockSpec((d, d), lambda i, j: (0, 0)),
                  pl.BlockSpec((None, m, d), lambda i, j: (i, 0, 0)),
                  pl.BlockSpec((None, m, d), lambda i, j: (i, 0, 0)),
                  pl.BlockSpec((d, d), lambda i, j: (0, 0)),
                  pl.BlockSpec((1, d), lambda i, j: (0, 0)),
                  pl.BlockSpec((d, 2 * LANES), lambda i, j: (0, 0)),
                  pl.BlockSpec((1, LANES), lambda i, j: (0, 0))],
        out_specs=[pl.BlockSpec((None, n_sub * tq, d), lambda i, j: (i, j, 0)),
                   pl.BlockSpec((n_sub, rows, d), tile),
                   pl.BlockSpec((n_sub, rows, LANES), tile),
                   pl.BlockSpec((None, n_sub * tq, LANES), lambda i, j: (i, j, 0)),
                   pl.BlockSpec((n_sub, 8, LANES), tile)],
        out_shape=[jax.ShapeDtypeStruct((b, t, d), F32),
                   jax.ShapeDtypeStruct((n_tiles, rows, d), BF16),
                   jax.ShapeDtypeStruct((n_tiles, rows, LANES), F32),
                   jax.ShapeDtypeStruct((b, t, LANES), F32),
                   jax.ShapeDtypeStruct((n_tiles, 8, LANES), F32)],
        compiler_params=_cparams(("parallel", "parallel")),
        name="xattn_route",
    )(x, act, w_mix, gain.reshape(1, d), wq, k, v, wo, gain_ffn.reshape(1, d), w_router, b_router)


def _moe_ffn_kernel(seg_ref, xs_ref, cs_ref, w1_ref, w3_ref, w2_ref, ys_ref, xbuf_ref, cbuf_ref, ybuf_ref):
    win = pl.program_id(0)
    grp = pl.program_id(1)
    n_win = xs_ref.shape[0]

    @pl.when((win == 0) & (grp == 0))
    def _():
        xbuf_ref[...] = jnp.zeros_like(xbuf_ref)
        cbuf_ref[...] = jnp.zeros_like(cbuf_ref)

    @pl.when(grp == 0)
    def _():
        ys_ref[...] = jnp.zeros_like(ys_ref)

    def segment(t):
        base = (win * n_win + t) * 8
        return seg_ref[base + grp], seg_ref[base + N_GROUPS + grp]

    def copy_segments(copy_piece):
        packed = jnp.int32(0)
        for t in range(n_win):
            start, size = segment(t)

            def body(j, carry, t=t, start=start, packed=packed):
                copy_piece(t, pl.multiple_of(start + j * SEG_ALIGN, SEG_ALIGN),
                           pl.multiple_of(packed + j * SEG_ALIGN, SEG_ALIGN))
                return carry

            lax.fori_loop(0, size // SEG_ALIGN, body, 0)
            packed = packed + size
        return packed

    def gather(t, src, dst):
        xbuf_ref[pl.ds(dst, SEG_ALIGN), :] = xs_ref[t, pl.ds(src, SEG_ALIGN), :]
        cbuf_ref[pl.ds(dst, SEG_ALIGN), :] = cs_ref[t, pl.ds(src, SEG_ALIGN), :]

    total = copy_segments(gather)

    def ffn_piece(r0, n_rows):
        xc = xbuf_ref[pl.ds(r0, n_rows), :]
        cb = cbuf_ref[pl.ds(r0, n_rows), :]
        lane = _iota2(cb.shape, 1)
        hids = []
        for e in range(EXPERTS_PER_GROUP):
            weight = jnp.sum(jnp.where(lane == N_GROUPS + EXPERTS_PER_GROUP * grp + e, cb, 0.0),
                             axis=-1, keepdims=True)
            hids.append((_silu(_dot(xc, w1_ref[e])) * _dot(xc, w3_ref[e]) * weight).astype(BF16))
        ybuf_ref[pl.ds(r0, n_rows), :] = _dot(jnp.concatenate(hids, axis=-1), w2_ref[...]).astype(ybuf_ref.dtype)

    def chunk(c, carry):
        ffn_piece(pl.multiple_of(c * MOE_CHUNK, MOE_CHUNK), MOE_CHUNK)
        return carry

    n_full = total // MOE_CHUNK
    lax.fori_loop(0, n_full, chunk, 0)
    rest = total - n_full * MOE_CHUNK
    rest_start = pl.multiple_of(n_full * MOE_CHUNK, MOE_CHUNK)
    lower = 0
    for n_rows in (MOE_CHUNK // 2, MOE_CHUNK):
        @pl.when((rest > lower) & (rest <= n_rows))
        def _(n_rows=n_rows):
            ffn_piece(rest_start, n_rows)
        lower = n_rows

    def scatter(t, dst, src):
        ys_ref[t, pl.ds(dst, SEG_ALIGN), :] = ybuf_ref[pl.ds(src, SEG_ALIGN), :]

    copy_segments(scatter)


def _moe_ffn(seg, xs, cs, w1, w3, w2g):
    n_tiles, rows, d = xs.shape
    ff = w1.shape[2]
    n_win = next(w for w in (MOE_WINDOW, 2, 1) if n_tiles % w == 0)
    buf_rows = -(-(n_win * rows) // MOE_CHUNK) * MOE_CHUNK
    grid_spec = pltpu.PrefetchScalarGridSpec(
        num_scalar_prefetch=1,
        grid=(n_tiles // n_win, N_GROUPS),
        in_specs=[pl.BlockSpec((n_win, rows, d), lambda w, g, seg: (w, 0, 0)),
                  pl.BlockSpec((n_win, rows, LANES), lambda w, g, seg: (w, 0, 0)),
                  pl.BlockSpec((EXPERTS_PER_GROUP, d, ff), lambda w, g, seg: (g, 0, 0)),
                  pl.BlockSpec((EXPERTS_PER_GROUP, d, ff), lambda w, g, seg: (g, 0, 0)),
                  pl.BlockSpec((None, EXPERTS_PER_GROUP * ff, d), lambda w, g, seg: (g, 0, 0))],
        out_specs=pl.BlockSpec((n_win, rows, d), lambda w, g, seg: (w, 0, 0)),
        scratch_shapes=[pltpu.VMEM((buf_rows, d), BF16),
                        pltpu.VMEM((buf_rows, LANES), F32),
                        pltpu.VMEM((buf_rows, d), BF16)])
    return pl.pallas_call(
        _moe_ffn_kernel,
        grid_spec=grid_spec,
        out_shape=jax.ShapeDtypeStruct((n_tiles, rows, d), BF16),
        compiler_params=_cparams(("arbitrary", "arbitrary")),
        name="moe_ffn",
    )(seg, xs, cs, w1, w3, w2g)


def _unsorted_sum(x_ref, info_ref, ys_ref):
    info = info_ref[...]
    dest = jnp.sum(jnp.where(_iota2(info.shape, 1) == 0, info, 0.0), axis=-1, keepdims=True)
    rows = ys_ref.shape[0]
    place = (_iota2((info.shape[0], rows), 1).astype(F32) == dest).astype(BF16)
    return x_ref[...] + _dot(place, ys_ref[...])


def _moe_unsort_kernel(n_w, x_ref, info_ref, ys_ref, gain_ref, *refs):
    w_refs, o_refs = refs[:n_w], refs[n_w:]
    y = _unsorted_sum(x_ref, info_ref, ys_ref)
    yn = _rms(y, gain_ref[...])
    if n_w == 0:
        o_refs[0][...] = yn
        return
    o_refs[0][...] = y
    yn = yn.astype(BF16)
    for w_ref, o_ref in zip(w_refs, o_refs[1:]):
        n = w_ref.shape[1]
        for c0 in range(0, n, 512):
            c1 = min(n, c0 + 512)
            o_ref[:, c0:c1] = _dot(yn, w_ref[:, c0:c1])


def _moe_unsort(x, info, ys, gain, ws=()):
    b, t, d = x.shape
    n_tiles, rows, _ = ys.shape
    tiles_per_row = n_tiles // b
    tq = t // tiles_per_row

    def token_block(cols):
        return pl.BlockSpec((None, tq, cols), lambda i, j: (i, j, 0))

    out_cols = [d] + [w.shape[1] for w in ws]
    outs = pl.pallas_call(
        functools.partial(_moe_unsort_kernel, len(ws)),
        grid=(b, tiles_per_row),
        in_specs=[token_block(d), token_block(LANES),
                  pl.BlockSpec((None, rows, d), lambda i, j: (i * tiles_per_row + j, 0, 0)),
                  pl.BlockSpec((1, d), lambda i, j: (0, 0))]
                 + [pl.BlockSpec(w.shape, lambda i, j: (0, 0)) for w in ws],
        out_specs=[token_block(c) for c in out_cols],
        out_shape=[jax.ShapeDtypeStruct((b, t, c), F32) for c in out_cols],
        compiler_params=_cparams(("parallel", "parallel")),
        name="moe_unsort",
    )(x, info, ys, gain.reshape(1, d), *ws)
    return outs


def _pad_cols(w, n):
    return jnp.pad(w, ((0, 0), (0, n - w.shape[1])))


def _pad_rows(w, n):
    return jnp.pad(w, ((0, n - w.shape[0]), (0, 0)))


def _row(v, n=None):
    v = v.reshape(1, -1).astype(F32)
    return v if n is None else _pad_cols(v, n)


def kernel(x, mem, norm_mix, norm_xattn, norm_mem, norm_ffn, norm_final, ab_w_in, ab_conv, ab_a_log, ab_dt_bias, ab_onorm, ab_shift_mu, ab_w0, ab_w2, ab_a0, ab_a2, ab_g2, ab_k_k, ab_k_a, ab_r_k, ab_ln_w, ab_ln_b, ab_w_out, c_w_in, c_lb_param, c_onorm, c_w_out, xa_wq, xa_wkv, xa_wo, moe_w_group, moe_b_group, moe_w_expert, moe_b_expert, moe_w1, moe_w3, moe_w2):
    bsz, seq, d = x.shape
    n_mem = mem.shape[1]
    depth = norm_mix.shape[0]
    a_width = d // 2
    b_width = d // 2
    a_heads = a_width // A_HEAD_DIM
    a_cols = 4 * a_width + 2 * a_heads
    n_tok = bsz * seq

    lb_p = jax.nn.softmax(c_lb_param.astype(F32), axis=0)
    lower_bounds = jnp.cumsum(lb_p, axis=0) - lb_p[0]

    def project(x_cur, moe_pending, gain, ws):
        if moe_pending is None:
            projs = _norm_proj(x_cur.reshape(n_tok, d), gain, ws)
            return x_cur, [p.reshape(bsz, seq, -1) for p in projs]
        x_new, *projs = _moe_unsort(*moe_pending, gain, ws)
        return x_new, projs

    x2 = x
    moe_pending = None
    mem2 = mem.reshape(bsz * n_mem, d)
    for layer in range(depth):
        if layer % 2 == 0:
            e = layer // 2
            w_in = ab_w_in[e]
            w_a = w_in[:, :4 * a_width].astype(BF16)
            w_db = _pad_cols(w_in[:, 4 * a_width:a_cols], LANES).astype(BF16)
            wb = w_in[:, a_cols:]
            o1, o2 = 3 * b_width, 3 * b_width + B_DECAY_LORA
            o3 = o2 + B_ICLR_LORA
            w_b = jnp.concatenate([wb[:, :o1], _pad_cols(wb[:, o1:o2], LANES), _pad_cols(wb[:, o2:o3], LANES),
                                   wb[:, o3:]], axis=1).astype(BF16)
            mu = ab_shift_mu[e].reshape(1, -1)
            mu_p = jnp.concatenate([mu[:, :o1], _pad_cols(mu[:, o1:o2], LANES), _pad_cols(mu[:, o2:o3], LANES),
                                    mu[:, o3:]], axis=1)
            head_id = jnp.arange(b_width) // B_HEAD_DIM
            bd = (head_id[:, None] == head_id[None, :]).astype(F32)
            gdn_params = [ab_conv[e], _row(ab_a_log[e], LANES), _row(ab_dt_bias[e], LANES), _row(ab_onorm[e])]
            rwkv_params = [mu_p, _row(ab_w0[e]), _pad_rows(ab_w2[e], LANES), _row(ab_a0[e]),
                           _pad_rows(ab_a2[e], LANES), ab_g2[e], _row(ab_k_k[e]), _row(ab_k_a[e]),
                           _row(ab_r_k[e]), _row(ab_ln_w[e]), _row(ab_ln_b[e]), bd]
            x2, (pa, pdb, pb) = project(x2, moe_pending, norm_mix[layer], [w_a, w_db, w_b])
            mix = _mixer_ab(pa, pdb, pb, gdn_params, rwkv_params)
            w_mix = ab_w_out[e].astype(BF16)
        else:
            o = layer // 2
            x2, (proj,) = project(x2, moe_pending, norm_mix[layer], [c_w_in[o].astype(BF16)])
            lb = lower_bounds[layer].reshape(1, -1)
            mix = _hgrn(proj, jnp.log(lb), jnp.log1p(-lb), _row(c_onorm[o]))
            w_mix = c_w_out[o].astype(BF16)

        wkv = xa_wkv[layer].astype(BF16)
        k_mem, v_mem = _norm_proj(mem2, norm_mem[layer], [wkv[:, :d], wkv[:, d:]], out_dtype=BF16)
        w_r = _pad_cols(jnp.concatenate([moe_w_group[layer], moe_w_expert[layer]], axis=1), LANES).astype(F32)
        w_r_hi = w_r.astype(BF16)
        w_r_lo = (w_r - w_r_hi.astype(F32)).astype(BF16)
        b_router = _row(jnp.concatenate([moe_b_group[layer], moe_b_expert[layer]]), LANES)
        x3, xs, cs, info, seg = _xattn_route(
            x2, mix, w_mix, norm_xattn[layer], xa_wq[layer].astype(BF16),
            k_mem.reshape(bsz, n_mem, d), v_mem.reshape(bsz, n_mem, d), xa_wo[layer].astype(BF16),
            norm_ffn[layer], jnp.concatenate([w_r_hi, w_r_lo], axis=1), b_router)
        seg_i = jnp.concatenate([seg[:, 0, :N_GROUPS], seg[:, 1, :N_GROUPS]], axis=1).astype(jnp.int32).reshape(-1)
        ff = moe_w2.shape[2]
        ys = _moe_ffn(seg_i, xs, cs, moe_w1[layer].astype(BF16), moe_w3[layer].astype(BF16),
                      moe_w2[layer].reshape(N_GROUPS, EXPERTS_PER_GROUP * ff, d).astype(BF16))
        moe_pending = (x3, info, ys)
    (out,) = _moe_unsort(*moe_pending, norm_final)
    return out
```

```python
import functools

import jax
import jax.numpy as jnp
import numpy as np
from jax import lax
from jax.experimental import pallas as pl
from jax.experimental.pallas import tpu as pltpu

F32 = jnp.float32
BF16 = jnp.bfloat16

CHUNK = 64
HGRN_LEVELS = CHUNK.bit_length() - 1
NORM_EPS = 1e-6
GROUP_NORM_EPS = 64e-5
CONV_WIDTH = 4

A_HEAD_DIM = 128
B_HEAD_DIM = 64
B_DECAY_LORA = 64
B_ICLR_LORA = 64
B_GATE_LORA = 128
C_KEY_DIM = 128
XA_HEADS = 4
N_GROUPS = 4
EXPERTS_PER_GROUP = 4
N_EXPERTS = N_GROUPS * EXPERTS_PER_GROUP

LANES = 128
SUBLANES = 8
VMEM_LIMIT = 56 * 1024 * 1024
MIXER_BATCH = 4
SEG_ALIGN = 16
MOE_WINDOW = 4
MOE_CHUNK = 288
XATTN_TILES_PER_STEP = 2


def _cparams(sem):
    return pltpu.CompilerParams(dimension_semantics=sem, vmem_limit_bytes=VMEM_LIMIT)


def _dot_dims(a, b, dims):
    return lax.dot_general(a.astype(BF16), b.astype(BF16), (dims, ((), ())), preferred_element_type=F32)


def _dot(a, b):
    return _dot_dims(a, b, ((1,), (0,)))


def _dot_nt(a, b):
    return _dot_dims(a, b, ((1,), (1,)))


def _dot_tn(a, b):
    return _dot_dims(a, b, ((0,), (0,)))


def _rms(x, gain):
    return x * lax.rsqrt(jnp.mean(x * x, axis=-1, keepdims=True) + NORM_EPS) * gain


def _sigmoid(x):
    return 1.0 / (1.0 + jnp.exp(-x))


def _silu(x):
    return x * _sigmoid(x)


def _softplus(x):
    return jnp.maximum(x, 0.0) + jnp.log(1.0 + jnp.exp(-jnp.abs(x)))


def _iota2(shape, dim):
    return lax.broadcasted_iota(jnp.int32, shape, dim)


def _split3(v):
    v1 = v.astype(BF16)
    v2 = (v - v1.astype(F32)).astype(BF16)
    v3 = (v - v1.astype(F32) - v2.astype(F32)).astype(BF16)
    return v1, v2, v3


def _table_dot(table, v):
    return _dot(jnp.concatenate([table.astype(BF16)] * 3, axis=1), jnp.concatenate(_split3(v), axis=0))


def _table_dot_tn(v, table):
    return _dot_tn(jnp.concatenate(_split3(v), axis=0), jnp.concatenate([table.astype(BF16)] * 3, axis=0))


def _interleave(stage_generators):
    live = list(stage_generators)
    while live:
        for gen in list(live):
            try:
                next(gen)
            except StopIteration:
                live.remove(gen)


def _tri_inverse_stages(mats):
    n = mats[0].shape[0]
    row = _iota2((n, n), 0)
    col = _iota2((n, n), 1)
    eye = (row == col).astype(F32)
    xs = [eye - jnp.where((row >> 1) == (col >> 1), a, 0.0) for a in mats]
    for level in range(1, n.bit_length() - 1):
        joins = ((row >> (level + 1)) == (col >> (level + 1))) & ((row >> level) != (col >> level))
        ts = [_dot(jnp.where(joins, a, 0.0), x) for a, x in zip(mats, xs)]
        yield
        xs = [x - _dot(x, t) for x, t in zip(xs, ts)]
        yield
    return xs


def _norm_proj_kernel(n_w, x_ref, g_ref, *refs):
    w_refs, o_refs = refs[:n_w], refs[n_w:]
    xn = _rms(x_ref[...], g_ref[...]).astype(BF16)
    for w_ref, o_ref in zip(w_refs, o_refs):
        n = w_ref.shape[1]
        for c0 in range(0, n, 512):
            c1 = min(n, c0 + 512)
            o_ref[:, c0:c1] = _dot(xn, w_ref[:, c0:c1]).astype(o_ref.dtype)


def _norm_proj(x, gain, ws, out_dtype=F32, tm=512):
    n, d = x.shape
    tm = min(tm, n)
    return pl.pallas_call(
        functools.partial(_norm_proj_kernel, len(ws)),
        grid=(n // tm,),
        in_specs=[pl.BlockSpec((tm, d), lambda i: (i, 0)),
                  pl.BlockSpec((1, d), lambda i: (0, 0))]
                 + [pl.BlockSpec(w.shape, lambda i: (0, 0)) for w in ws],
        out_specs=[pl.BlockSpec((tm, w.shape[1]), lambda i: (i, 0)) for w in ws],
        out_shape=[jax.ShapeDtypeStruct((n, w.shape[1]), out_dtype) for w in ws],
        compiler_params=_cparams(("parallel",)),
        name="norm_proj",
    )(x, gain.reshape(1, d), *ws)


def _log_forget(f, log_lb, log1m_lb):
    a1 = jnp.broadcast_to(log_lb, f.shape)
    b1 = log1m_lb - _softplus(-f)
    return jnp.maximum(a1, b1) + jnp.log(1.0 + jnp.exp(-jnp.abs(a1 - b1)))


def _gdn_stages(nb, pa_ref, pdb_ref, convw_ref, alog_ref, dtb_ref, onorm_ref, o_ref, xbuf_ref, state_ref):
    n_heads = state_ref.shape[0] // nb
    hd = A_HEAD_DIM
    width = n_heads * hd
    row = _iota2((CHUNK, CHUNK), 0)
    col = _iota2((CHUNK, CHUNK), 1)
    incl = row >= col
    strict = row > col
    tril = incl.astype(F32)
    triu = (row <= col).astype(F32)

    qs, ks, vs, betas, gs, g_lasts, decays = [], [], [], [], [], [], []
    for bb in range(nb):
        x = pa_ref[bb, :, 0:3 * width]
        xbuf_ref[bb, 8:8 + CHUNK, :] = x
        acc = convw_ref[0:1, :] * xbuf_ref[bb, 5:5 + CHUNK, :]
        for j in range(1, CONV_WIDTH):
            acc = acc + convw_ref[j:j + 1, :] * xbuf_ref[bb, 5 + j:5 + j + CHUNK, :]
        xbuf_ref[bb, 0:8, :] = x[CHUNK - 8:CHUNK, :]
        qkv = _silu(acc)
        pdb = pdb_ref[bb]
        log_alpha = -jnp.exp(alog_ref[...]) * _softplus(pdb + dtb_ref[...])
        beta_all = _sigmoid(pdb)
        g_all = _table_dot(tril, log_alpha)
        g_all_t = _table_dot_tn(log_alpha, triu)
        for h in range(n_heads):
            q = qkv[:, h * hd:(h + 1) * hd]
            k = qkv[:, width + h * hd: width + (h + 1) * hd]
            qs.append(q * lax.rsqrt(jnp.sum(q * q, axis=-1, keepdims=True) + 1e-6) * (hd ** -0.5))
            ks.append(k * lax.rsqrt(jnp.sum(k * k, axis=-1, keepdims=True) + 1e-6))
            vs.append(qkv[:, 2 * width + h * hd: 2 * width + (h + 1) * hd])
            betas.append(beta_all[:, n_heads + h:n_heads + h + 1])
            g = g_all[:, h:h + 1]
            gs.append(g)
            g_lasts.append(g_all[CHUNK - 1:CHUNK, h:h + 1])
            decays.append(jnp.where(incl, jnp.exp(jnp.where(incl, g - g_all_t[h:h + 1, :], 0.0)), 0.0))
    yield
    idx = range(nb * n_heads)
    k_betas = [ks[i] * betas[i] for i in idx]
    egs = [jnp.exp(g) for g in gs]
    a_lows = [jnp.where(strict, _dot_nt(k_betas[i], ks[i]) * decays[i], 0.0) for i in idx]
    qks = [_dot_nt(qs[i], ks[i]) * decays[i] for i in idx]
    states = [state_ref[i] for i in idx]
    o_inters = [_dot(qs[i] * egs[i], states[i]) for i in idx]
    yield
    t_invs = yield from _tri_inverse_stages(a_lows)
    us = [_dot(t_invs[i], vs[i] * betas[i]) for i in idx]
    ws = [_dot(t_invs[i], k_betas[i] * egs[i]) for i in idx]
    yield
    v_news = [us[i] - _dot(ws[i], states[i]) for i in idx]
    yield
    os_ = [o_inters[i] + _dot(qks[i], v_news[i]) for i in idx]
    for i in idx:
        state_ref[i] = (states[i] * jnp.exp(g_lasts[i])
                        + _dot_tn(ks[i] * jnp.exp(g_lasts[i] - gs[i]), v_news[i]))
    yield
    for i in idx:
        bb, h = divmod(i, n_heads)
        gate = pa_ref[bb, :, 3 * width + h * hd: 3 * width + (h + 1) * hd]
        o_ref[bb, :, h * hd:(h + 1) * hd] = _rms(os_[i], onorm_ref[...]) * _silu(gate)


def _rwkv_stages(nb, col0, pb_ref, mu_ref, w0_ref, w2_ref, a0_ref, a2_ref, g2_ref, kk_ref, ka_ref, rk_ref,
                 lnw_ref, lnb_ref, bd_ref, o_ref, carry_ref, state_ref, obuf_ref):
    n_heads = state_ref.shape[0] // nb
    hd = B_HEAD_DIM
    width = n_heads * hd
    row = _iota2((CHUNK, CHUNK), 0)
    col = _iota2((CHUNK, CHUNK), 1)
    strict = row > col
    tril = (row >= col).astype(F32)
    incl2 = _iota2((CHUNK, 2 * CHUNK), 0) >= (_iota2((CHUNK, 2 * CHUNK), 1) & (CHUNK - 1))
    bd = bd_ref[...]

    pss = []
    for bb in range(nb):
        p = pb_ref[bb]
        prev = jnp.where(_iota2(p.shape, 0) == 0, carry_ref[bb, 0:1, :], pltpu.roll(p, 1, 0))
        carry_ref[bb, 0:1, :] = p[CHUNK - 1:CHUNK, :]
        pss.append(p + mu_ref[...] * (prev - p))
    rs = [ps[:, 0:width] for ps in pss]
    kraw = [ps[:, width:2 * width] for ps in pss]
    vfull = [ps[:, 2 * width:3 * width] for ps in pss]
    w_raws = [w0_ref[...] + _dot(jnp.tanh(ps[:, 3 * width:3 * width + LANES]), w2_ref[...]) for ps in pss]
    iclrs = [_sigmoid(a0_ref[...] + _dot(ps[:, 3 * width + LANES:3 * width + 2 * LANES], a2_ref[...]))
             for ps in pss]
    gates = [_dot(_sigmoid(ps[:, 3 * width + 2 * LANES:3 * width + 3 * LANES]), g2_ref[...]) for ps in pss]
    kks = [k * kk_ref[...] for k in kraw]
    kk_ss = [_dot(kk * kk, bd) for kk in kks]
    yield
    log_ws = [-jnp.exp(-_softplus(-w_raw) - 0.5) for w_raw in w_raws]
    g_cums = [_table_dot(tril, log_w) for log_w in log_ws]
    yield
    lhss, rhss, bl_kls, p_cs, vhs, k2s = [], [], [], [], [], []
    for bb in range(nb):
        kk = kks[bb] * lax.rsqrt(kk_ss[bb] + 1e-6)
        k2 = kraw[bb] * (1.0 + (iclrs[bb] - 1.0) * ka_ref[...])
        k2s.append(k2)
        bvec = kk * iclrs[bb]
        g = g_cums[bb]
        g_last = g[CHUNK - 1:CHUNK, :]
        eng = jnp.exp(-g)
        egl = jnp.exp(g_last - g)
        r_t = rs[bb] * jnp.exp(g)
        kk_t = kk * jnp.exp(g - log_ws[bb])
        b_h, k_h = bvec * eng, k2 * eng
        b_l, k_l = bvec * egl, k2 * egl
        p_c = jnp.exp(g_last)
        for h in range(n_heads):
            sl = slice(h * hd, (h + 1) * hd)
            lhss.append(jnp.concatenate([kk_t[:, sl], r_t[:, sl]], axis=0))
            rhss.append(jnp.concatenate([b_h[:, sl], k_h[:, sl]], axis=0))
            bl_kls.append(jnp.concatenate([b_l[:, sl], k_l[:, sl]], axis=0))
            p_cs.append(p_c[:, sl])
            vhs.append(vfull[bb][:, sl])
    idx = range(nb * n_heads)
    ms = [_dot_nt(lhss[i], rhss[i]) for i in idx]
    states = [state_ref[i] for i in idx]
    lss = [_dot_nt(lhss[i], states[i]) for i in idx]
    yield
    a_bbs = [jnp.where(strict, m[0:CHUNK, 0:CHUNK], 0.0) for m in ms]
    a_bks = [jnp.where(strict, m[0:CHUNK, CHUNK:2 * CHUNK], 0.0) for m in ms]
    a_rs = [jnp.where(incl2, m[CHUNK:2 * CHUNK, :], 0.0) for m in ms]
    rhs_u = [lss[i][0:CHUNK] + _dot(a_bks[i], vhs[i]) for i in idx]
    t_invs = yield from _tri_inverse_stages(a_bbs)
    uvs = [jnp.concatenate([-_dot(t_invs[i], rhs_u[i]), vhs[i]], axis=0) for i in idx]
    yield
    for i in idx:
        bb, h = divmod(i, n_heads)
        obuf_ref[bb, :, h * hd:(h + 1) * hd] = lss[i][CHUNK:2 * CHUNK] + _dot(a_rs[i], uvs[i])
        state_ref[i] = states[i] * p_cs[i] + _dot_tn(uvs[i], bl_kls[i])
    yield
    inv_n = 1.0 / hd
    os_ = [obuf_ref[bb] for bb in range(nb)]
    means = [_dot(o, bd) * inv_n for o in os_]
    bonus = [_dot(rs[bb] * k2s[bb] * rk_ref[...], bd) * vfull[bb] for bb in range(nb)]
    yield
    cens = [o - mean for o, mean in zip(os_, means)]
    vars_ = [_dot(cen * cen, bd) * inv_n for cen in cens]
    yield
    for bb in range(nb):
        ln = cens[bb] * lax.rsqrt(vars_[bb] + GROUP_NORM_EPS) * lnw_ref[...] + lnb_ref[...]
        o_ref[bb, :, col0:col0 + width] = (ln + bonus[bb]) * gates[bb]


def _mixer_ab_kernel(n_gdn, n_rwkv, pa_ref, pdb_ref, pb_ref, *refs):
    gdn_params = refs[:n_gdn]
    rwkv_params = refs[n_gdn:n_gdn + n_rwkv]
    o_ref, xbuf_ref, gstate_ref, carry_ref, rstate_ref, obuf_ref = refs[n_gdn + n_rwkv:]
    nb = pa_ref.shape[0]

    @pl.when(pl.program_id(1) == 0)
    def _():
        xbuf_ref[:, 0:8, :] = jnp.zeros((nb, 8, xbuf_ref.shape[2]), F32)
        gstate_ref[...] = jnp.zeros_like(gstate_ref)
        carry_ref[...] = jnp.zeros_like(carry_ref)
        rstate_ref[...] = jnp.zeros_like(rstate_ref)

    a_width = pa_ref.shape[2] // 4
    _interleave([
        _gdn_stages(nb, pa_ref, pdb_ref, *gdn_params, o_ref, xbuf_ref, gstate_ref),
        _rwkv_stages(nb, a_width, pb_ref, *rwkv_params, o_ref, carry_ref, rstate_ref, obuf_ref),
    ])


def _mixer_ab(pa, pdb, pb, gdn_params, rwkv_params):
    b, t, a_cols = pa.shape
    b_cols = pb.shape[2]
    a_width = a_cols // 4
    b_width = (b_cols - 3 * LANES) // 3
    a_heads = a_width // A_HEAD_DIM
    b_heads = b_width // B_HEAD_DIM
    nb = MIXER_BATCH if b % MIXER_BATCH == 0 else 1
    params = list(gdn_params) + list(rwkv_params)
    return pl.pallas_call(
        functools.partial(_mixer_ab_kernel, len(gdn_params), len(rwkv_params)),
        grid=(b // nb, t // CHUNK),
        in_specs=[pl.BlockSpec((nb, CHUNK, a_cols), lambda i, c: (i, c, 0)),
                  pl.BlockSpec((nb, CHUNK, LANES), lambda i, c: (i, c, 0)),
                  pl.BlockSpec((nb, CHUNK, b_cols), lambda i, c: (i, c, 0))]
                 + [pl.BlockSpec(a.shape, lambda i, c: (0, 0)) for a in params],
        out_specs=pl.BlockSpec((nb, CHUNK, a_width + b_width), lambda i, c: (i, c, 0)),
        out_shape=jax.ShapeDtypeStruct((b, t, a_width + b_width), F32),
        scratch_shapes=[pltpu.VMEM((nb, 8 + CHUNK, 3 * a_width), F32),
                        pltpu.VMEM((nb * a_heads, A_HEAD_DIM, A_HEAD_DIM), F32),
                        pltpu.VMEM((nb, 8, b_cols), F32),
                        pltpu.VMEM((nb * b_heads, B_HEAD_DIM, B_HEAD_DIM), F32),
                        pltpu.VMEM((nb, CHUNK, b_width), F32)],
        compiler_params=_cparams(("parallel", "arbitrary")),
        name="mixer_ab",
    )(pa, pdb, pb, *params)


def _hgrn_tables():
    i = np.arange(CHUNK)[:, None]
    t = np.arange(CHUNK)[None, :]
    blocks, pair = [], []
    for ls in range(HGRN_LEVELS - 1, -1, -1):
        s = 1 << ls
        mid = (i >> (ls + 1) << (ls + 1)) + s
        q_rows, k_rows = _hgrn_level_rows(ls)
        blocks.append(((i >= mid) & (t >= mid) & (t <= i))[q_rows])
        blocks.append(((i < mid) & (t > i) & (t <= mid - 1))[k_rows])
        pair.append((((i >> ls) ^ (t >> ls)) == 1) & (i > t))
    blocks.append(t <= i)
    blocks.append(t > i)
    sums = np.concatenate(blocks, axis=0).astype(np.float32)
    return np.concatenate([sums] * 3, axis=1), np.stack(pair).astype(np.float32)


def _hgrn_level_rows(ls):
    rows = np.arange(CHUNK)
    if (1 << ls) < SUBLANES:
        return rows, rows
    upper = ((rows >> ls) & 1) == 1
    return rows[upper], rows[~upper]


def _hgrn_kernel(p_ref, loglb_ref, log1mlb_ref, onorm_ref, sums_ref, pair_ref, o_ref, state_ref):
    nb = p_ref.shape[0]
    n_heads = state_ref.shape[0] // nb
    hd = C_KEY_DIM
    kw = n_heads * hd

    @pl.when(pl.program_id(1) == 0)
    def _():
        state_ref[...] = jnp.zeros_like(state_ref)

    def take_rows(a, rows):
        runs, start = [], 0
        for n in range(1, len(rows) + 1):
            if n == len(rows) or rows[n] != rows[n - 1] + 1:
                runs.append(a[int(rows[start]):int(rows[n - 1]) + 1])
                start = n
        return runs[0] if len(runs) == 1 else jnp.concatenate(runs, axis=0)

    def spread_rows(a, rows):
        if len(rows) == CHUNK:
            return a
        pieces, pos, taken = [], 0, 0
        while pos < CHUNK:
            run = 1
            inside = pos in rows
            while pos + run < CHUNK and ((pos + run) in rows) == inside:
                run += 1
            if inside:
                pieces.append(a[taken:taken + run])
                taken += run
            else:
                pieces.append(jnp.zeros((run, a.shape[1]), a.dtype))
            pos += run
        return jnp.concatenate(pieces, axis=0)

    qs, ks, exps = [], [], []
    for bb in range(nb):
        qs.append(_silu(p_ref[bb, :, 0:kw]))
        log_f = _log_forget(p_ref[bb, :, kw:2 * kw], loglb_ref[...], log1mlb_ref[...])
        ks.append(1.0 - jnp.exp(log_f))
        exps.append(_dot(sums_ref[...], jnp.concatenate(_split3(log_f), axis=0)))

    idx = [(bb, h) for bb in range(nb) for h in range(n_heads)]
    states = [state_ref[bb * n_heads + h] for bb, h in idx]
    scores = [None] * len(idx)
    offset = 0
    for lvl in range(HGRN_LEVELS):
        q_rows, k_rows = _hgrn_level_rows(HGRN_LEVELS - 1 - lvl)
        q_rows, k_rows = [int(r) for r in q_rows], [int(r) for r in k_rows]
        e_q = [e[offset:offset + len(q_rows)] for e in exps]
        e_k = [e[offset + len(q_rows):offset + len(q_rows) + len(k_rows)] for e in exps]
        offset += len(q_rows) + len(k_rows)
        q_l = [(take_rows(qs[bb], q_rows) * jnp.exp(e_q[bb])).astype(BF16) for bb in range(nb)]
        k_l = [spread_rows(take_rows(ks[bb], k_rows) * jnp.exp(e_k[bb]), k_rows).astype(BF16) for bb in range(nb)]
        mask = None if (lvl == 0 and len(q_rows) < CHUNK) else pair_ref[lvl]
        for n, (bb, h) in enumerate(idx):
            sl = slice(h * hd, (h + 1) * hd)
            part = spread_rows(_dot_nt(q_l[bb][:, sl], k_l[bb][:, sl]), q_rows)
            if mask is not None:
                part = part * mask
            scores[n] = part if scores[n] is None else scores[n] + part
    g_cum = [e[offset:offset + CHUNK] for e in exps]
    q_g = [qs[bb] * jnp.exp(g_cum[bb]) for bb in range(nb)]
    k_r = [ks[bb] * jnp.exp(exps[bb][offset + CHUNK:offset + 2 * CHUNK]) for bb in range(nb)]
    diag = [qs[bb] * ks[bb] for bb in range(nb)]
    outs = []
    for n, (bb, h) in enumerate(idx):
        sl = slice(h * hd, (h + 1) * hd)
        v = p_ref[bb, :, 2 * kw + h * hd: 2 * kw + (h + 1) * hd]
        o = _dot_nt(q_g[bb][:, sl], states[n]) + _dot(scores[n], v)
        outs.append(o + jnp.sum(diag[bb][:, sl], axis=-1, keepdims=True) * v)
        p_last = jnp.exp(g_cum[bb][CHUNK - 1:CHUNK, sl])
        state_ref[bb * n_heads + h] = states[n] * p_last + _dot_tn(v, k_r[bb][:, sl])
    for n, (bb, h) in enumerate(idx):
        gate = p_ref[bb, :, 3 * kw + h * hd: 3 * kw + (h + 1) * hd]
        o_ref[bb, :, h * hd:(h + 1) * hd] = _rms(outs[n], onorm_ref[...]) * _silu(gate)


def _hgrn(proj, log_lb, log1m_lb, onorm):
    b, t, cols = proj.shape
    kw = cols // 4
    n_heads = kw // C_KEY_DIM
    nb = MIXER_BATCH if b % MIXER_BATCH == 0 else 1
    sums, pair = _hgrn_tables()
    sums = jnp.asarray(sums, BF16)
    pair = jnp.asarray(pair, F32)
    return pl.pallas_call(
        _hgrn_kernel,
        grid=(b // nb, t // CHUNK),
        in_specs=[pl.BlockSpec((nb, CHUNK, cols), lambda i, c: (i, c, 0)),
                  pl.BlockSpec((1, kw), lambda i, c: (0, 0)),
                  pl.BlockSpec((1, kw), lambda i, c: (0, 0)),
                  pl.BlockSpec((1, C_KEY_DIM), lambda i, c: (0, 0)),
                  pl.BlockSpec(sums.shape, lambda i, c: (0, 0)),
                  pl.BlockSpec(pair.shape, lambda i, c: (0, 0, 0))],
        out_specs=pl.BlockSpec((nb, CHUNK, kw), lambda i, c: (i, c, 0)),
        out_shape=jax.ShapeDtypeStruct((b, t, kw), F32),
        scratch_shapes=[pltpu.VMEM((nb * n_heads, C_KEY_DIM, C_KEY_DIM), F32)],
        compiler_params=_cparams(("parallel", "arbitrary")),
        name="hgrn2",
    )(proj, log_lb, log1m_lb, onorm, sums, pair)


def _route(logits):
    lane = _iota2(logits.shape, 1)
    neg = -jnp.inf
    gmask = lane < N_GROUPS
    gl = jnp.where(gmask, logits, neg)
    gmax = jnp.max(gl, axis=-1, keepdims=True)
    gsel = jnp.min(jnp.where(gl == gmax, lane, LANES), axis=-1, keepdims=True)
    group_gate = 1.0 / jnp.sum(jnp.where(gmask, jnp.exp(gl - gmax), 0.0), axis=-1, keepdims=True)
    lo = N_GROUPS + EXPERTS_PER_GROUP * gsel
    emask = (lane >= lo) & (lane < lo + EXPERTS_PER_GROUP)
    el = jnp.where(emask, logits, neg)
    ee = jnp.where(emask, jnp.exp(el - jnp.max(el, axis=-1, keepdims=True)), 0.0)
    prob = ee / jnp.sum(ee, axis=-1, keepdims=True)
    p1 = jnp.max(jnp.where(emask, prob, -1.0), axis=-1, keepdims=True)
    i1 = jnp.min(jnp.where(emask & (prob == p1), lane, LANES), axis=-1, keepdims=True)
    rest = emask & (lane != i1)
    p2 = jnp.max(jnp.where(rest, prob, -1.0), axis=-1, keepdims=True)
    i2 = jnp.min(jnp.where(rest & (prob == p2), lane, LANES), axis=-1, keepdims=True)
    scale = group_gate / (p1 + p2)
    comb = jnp.where(lane == i1, p1 * scale, 0.0) + jnp.where(lane == i2, p2 * scale, 0.0)
    return comb, gsel


def _sorted_rows(tile_rows):
    return tile_rows + N_GROUPS * SEG_ALIGN


def _xattn_route_kernel(x_ref, a_ref, wmix_ref, gain_ref, wq_ref, k_ref, v_ref, wo_ref, gffn_ref, wr_ref, br_ref,
                        x2_ref, xs_ref, cs_ref, info_ref, seg_ref):
    n_sub = xs_ref.shape[0]
    tq = x_ref.shape[0] // n_sub
    d = x_ref.shape[1]
    hd = d // XA_HEADS
    rows = xs_ref.shape[1]

    def tile_stages(u):
        tok = slice(u * tq, (u + 1) * tq)
        x = x_ref[tok, :] + _dot(a_ref[tok, :].astype(BF16), wmix_ref[...])
        q = _dot(_rms(x, gain_ref[...]).astype(BF16), wq_ref[...]).astype(BF16)
        yield
        outs = []
        for h in range(XA_HEADS):
            sl = slice(h * hd, (h + 1) * hd)
            s = _dot_nt(q[:, sl], k_ref[:, sl]) * (hd ** -0.5)
            p = jnp.exp(s - jnp.max(s, axis=-1, keepdims=True))
            p = p / jnp.sum(p, axis=-1, keepdims=True)
            outs.append(_dot(p.astype(BF16), v_ref[:, sl]).astype(BF16))
        yield
        x2 = x + _dot(jnp.concatenate(outs, axis=-1), wo_ref[...])
        x2_ref[tok, :] = x2
        xn = _rms(x2, gffn_ref[...])
        xn_hi = xn.astype(BF16)
        xn_lo = (xn - xn_hi.astype(F32)).astype(BF16)
        hi_part = _dot(xn_hi, wr_ref[...])
        logits = hi_part[:, 0:LANES] + hi_part[:, LANES:2 * LANES] + _dot(xn_lo, wr_ref[:, 0:LANES])
        yield
        comb, gsel = _route(logits + br_ref[...])
        lane = _iota2((tq, LANES), 1)
        onehot = (lane == gsel).astype(F32)
        earlier = (_iota2((tq, tq), 0) > _iota2((tq, tq), 1)).astype(BF16)
        before = _dot(earlier, onehot.astype(BF16))
        yield
        rank = jnp.sum(onehot * before, axis=-1, keepdims=True)
        counts = before[tq - 1:tq, :] + onehot[tq - 1:tq, :]
        padded = jnp.floor((counts + (SEG_ALIGN - 1)) * (1.0 / SEG_ALIGN)) * SEG_ALIGN
        lane1 = _iota2((1, LANES), 1)
        starts = jnp.zeros((1, LANES), F32)
        for g in range(N_GROUPS - 1):
            starts = starts + jnp.where(lane1 > g, padded[:, g:g + 1], 0.0)
        dest = jnp.sum(onehot * starts, axis=-1, keepdims=True) + rank
        place = (_iota2((tq, rows), 1).astype(F32) == dest).astype(BF16)
        comb_hi = comb.astype(BF16)
        comb_lo = (comb - comb_hi.astype(F32)).astype(BF16)
        placed = _dot_tn(place, jnp.concatenate([xn_hi, comb_hi, comb_lo], axis=1))
        xs_ref[u] = placed[:, 0:d].astype(BF16)
        cs_ref[u] = placed[:, d:d + LANES] + placed[:, d + LANES:d + 2 * LANES]
        info_ref[tok, :] = jnp.where(lane == 0, dest, 0.0)
        seg_ref[u] = jnp.concatenate([starts, padded, jnp.zeros((6, LANES), F32)], axis=0)

    _interleave([tile_stages(u) for u in range(n_sub)])


def _xattn_route(x, act, w_mix, gain, wq, k, v, wo, gain_ffn, w_router, b_router, tq=512):
    b, t, d = x.shape
    m = k.shape[1]
    tq = min(tq, t)
    tiles_per_row = t // tq
    n_tiles = b * tiles_per_row
    rows = _sorted_rows(tq)
    n_sub = XATTN_TILES_PER_STEP if tiles_per_row % XATTN_TILES_PER_STEP == 0 else 1
    steps_per_row = tiles_per_row // n_sub

    def tile(i, j):
        return (i * steps_per_row + j, 0, 0)

    return pl.pallas_call(
        _xattn_route_kernel,
        grid=(b, steps_per_row),
        in_specs=[pl.BlockSpec((None, n_sub * tq, d), lambda i, j: (i, j, 0)),
                  pl.BlockSpec((None, n_sub * tq, act.shape[2]), lambda i, j: (i, j, 0)),
                  pl.BlockSpec(w_mix.shape, lambda i, j: (0, 0)),
                  pl.BlockSpec((1, d), lambda i, j: (0, 0)),
                  pl.BlockSpec((d, d), lambda i, j: (0, 0)),
                  pl.BlockSpec((None, m, d), lambda i, j: (i, 0, 0)),
                  pl.BlockSpec((None, m, d), lambda i, j: (i, 0, 0)),
                  pl.BlockSpec((d, d), lambda i, j: (0, 0)),
                  pl.BlockSpec((1, d), lambda i, j: (0, 0)),
                  pl.BlockSpec((d, 2 * LANES), lambda i, j: (0, 0)),
                  pl.BlockSpec((1, LANES), lambda i, j: (0, 0))],
        out_specs=[pl.BlockSpec((None, n_sub * tq, d), lambda i, j: (i, j, 0)),
                   pl.BlockSpec((n_sub, rows, d), tile),
                   pl.BlockSpec((n_sub, rows, LANES), tile),
                   pl.BlockSpec((None, n_sub * tq, LANES), lambda i, j: (i, j, 0)),
                   pl.BlockSpec((n_sub, 8, LANES), tile)],
        out_shape=[jax.ShapeDtypeStruct((b, t, d), F32),
                   jax.ShapeDtypeStruct((n_tiles, rows, d), BF16),
                   jax.ShapeDtypeStruct((n_tiles, rows, LANES), F32),
                   jax.ShapeDtypeStruct((b, t, LANES), F32),
                   jax.ShapeDtypeStruct((n_tiles, 8, LANES), F32)],
        compiler_params=_cparams(("parallel", "parallel")),
        name="xattn_route",
    )(x, act, w_mix, gain.reshape(1, d), wq, k, v, wo, gain_ffn.reshape(1, d), w_router, b_router)


def _moe_ffn_kernel(seg_ref, xs_ref, cs_ref, w1_ref, w3_ref, w2_ref, ys_ref, xbuf_ref, cbuf_ref, ybuf_ref):
    win = pl.program_id(0)
    grp = pl.program_id(1)
    n_win = xs_ref.shape[0]

    @pl.when((win == 0) & (grp == 0))
    def _():
        xbuf_ref[...] = jnp.zeros_like(xbuf_ref)
        cbuf_ref[...] = jnp.zeros_like(cbuf_ref)

    @pl.when(grp == 0)
    def _():
        ys_ref[...] = jnp.zeros_like(ys_ref)

    def segment(t):
        base = (win * n_win + t) * 8
        return seg_ref[base + grp], seg_ref[base + N_GROUPS + grp]

    def copy_segments(copy_piece):
        packed = jnp.int32(0)
        for t in range(n_win):
            start, size = segment(t)

            def body(j, carry, t=t, start=start, packed=packed):
                copy_piece(t, pl.multiple_of(start + j * SEG_ALIGN, SEG_ALIGN),
                           pl.multiple_of(packed + j * SEG_ALIGN, SEG_ALIGN))
                return carry

            lax.fori_loop(0, size // SEG_ALIGN, body, 0)
            packed = packed + size
        return packed

    def gather(t, src, dst):
        xbuf_ref[pl.ds(dst, SEG_ALIGN), :] = xs_ref[t, pl.ds(src, SEG_ALIGN), :]
        cbuf_ref[pl.ds(dst, SEG_ALIGN), :] = cs_ref[t, pl.ds(src, SEG_ALIGN), :]

    total = copy_segments(gather)

    def ffn_piece(r0, n_rows):
        xc = xbuf_ref[pl.ds(r0, n_rows), :]
        cb = cbuf_ref[pl.ds(r0, n_rows), :]
        lane = _iota2(cb.shape, 1)
        hids = []
        for e in range(EXPERTS_PER_GROUP):
            weight = jnp.sum(jnp.where(lane == N_GROUPS + EXPERTS_PER_GROUP * grp + e, cb, 0.0),
                             axis=-1, keepdims=True)
            hids.append((_silu(_dot(xc, w1_ref[e])) * _dot(xc, w3_ref[e]) * weight).astype(BF16))
        ybuf_ref[pl.ds(r0, n_rows), :] = _dot(jnp.concatenate(hids, axis=-1), w2_ref[...]).astype(ybuf_ref.dtype)

    def chunk(c, carry):
        ffn_piece(pl.multiple_of(c * MOE_CHUNK, MOE_CHUNK), MOE_CHUNK)
        return carry

    n_full = total // MOE_CHUNK
    lax.fori_loop(0, n_full, chunk, 0)
    rest = total - n_full * MOE_CHUNK
    rest_start = pl.multiple_of(n_full * MOE_CHUNK, MOE_CHUNK)
    lower = 0
    for n_rows in (MOE_CHUNK // 2, MOE_CHUNK):
        @pl.when((rest > lower) & (rest <= n_rows))
        def _(n_rows=n_rows):
            ffn_piece(rest_start, n_rows)
        lower = n_rows

    def scatter(t, dst, src):
        ys_ref[t, pl.ds(dst, SEG_ALIGN), :] = ybuf_ref[pl.ds(src, SEG_ALIGN), :]

    copy_segments(scatter)


def _moe_ffn(seg, xs, cs, w1, w3, w2g):
    n_tiles, rows, d = xs.shape
    ff = w1.shape[2]
    n_win = next(w for w in (MOE_WINDOW, 2, 1) if n_tiles % w == 0)
    buf_rows = -(-(n_win * rows) // MOE_CHUNK) * MOE_CHUNK
    grid_spec = pltpu.PrefetchScalarGridSpec(
        num_scalar_prefetch=1,
        grid=(n_tiles // n_win, N_GROUPS),
        in_specs=[pl.BlockSpec((n_win, rows, d), lambda w, g, seg: (w, 0, 0)),
                  pl.BlockSpec((n_win, rows, LANES), lambda w, g, seg: (w, 0, 0)),
                  pl.BlockSpec((EXPERTS_PER_GROUP, d, ff), lambda w, g, seg: (g, 0, 0)),
                  pl.BlockSpec((EXPERTS_PER_GROUP, d, ff), lambda w, g, seg: (g, 0, 0)),
                  pl.BlockSpec((None, EXPERTS_PER_GROUP * ff, d), lambda w, g, seg: (g, 0, 0))],
        out_specs=pl.BlockSpec((n_win, rows, d), lambda w, g, seg: (w, 0, 0)),
        scratch_shapes=[pltpu.VMEM((buf_rows, d), BF16),
                        pltpu.VMEM((buf_rows, LANES), F32),
                        pltpu.VMEM((buf_rows, d), BF16)])
    return pl.pallas_call(
        _moe_ffn_kernel,
        grid_spec=grid_spec,
        out_shape=jax.ShapeDtypeStruct((n_tiles, rows, d), BF16),
        compiler_params=_cparams(("arbitrary", "arbitrary")),
        name="moe_ffn",
    )(seg, xs, cs, w1, w3, w2g)


def _unsorted_sum(x_ref, info_ref, ys_ref):
    info = info_ref[...]
    dest = jnp.sum(jnp.where(_iota2(info.shape, 1) == 0, info, 0.0), axis=-1, keepdims=True)
    rows = ys_ref.shape[0]
    place = (_iota2((info.shape[0], rows), 1).astype(F32) == dest).astype(BF16)
    return x_ref[...] + _dot(place, ys_ref[...])


def _moe_unsort_kernel(n_w, x_ref, info_ref, ys_ref, gain_ref, *refs):
    w_refs, o_refs = refs[:n_w], refs[n_w:]
    y = _unsorted_sum(x_ref, info_ref, ys_ref)
    yn = _rms(y, gain_ref[...])
    if n_w == 0:
        o_refs[0][...] = yn
        return
    o_refs[0][...] = y
    yn = yn.astype(BF16)
    for w_ref, o_ref in zip(w_refs, o_refs[1:]):
        n = w_ref.shape[1]
        for c0 in range(0, n, 512):
            c1 = min(n, c0 + 512)
            o_ref[:, c0:c1] = _dot(yn, w_ref[:, c0:c1])


def _moe_unsort(x, info, ys, gain, ws=()):
    b, t, d = x.shape
    n_tiles, rows, _ = ys.shape
    tiles_per_row = n_tiles // b
    tq = t // tiles_per_row

    def token_block(cols):
        return pl.BlockSpec((None, tq, cols), lambda i, j: (i, j, 0))

    out_cols = [d] + [w.shape[1] for w in ws]
    outs = pl.pallas_call(
        functools.partial(_moe_unsort_kernel, len(ws)),
        grid=(b, tiles_per_row),
        in_specs=[token_block(d), token_block(LANES),
                  pl.BlockSpec((None, rows, d), lambda i, j: (i * tiles_per_row + j, 0, 0)),
                  pl.BlockSpec((1, d), lambda i, j: (0, 0))]
                 + [pl.BlockSpec(w.shape, lambda i, j: (0, 0)) for w in ws],
        out_specs=[token_block(c) for c in out_cols],
        out_shape=[jax.ShapeDtypeStruct((b, t, c), F32) for c in out_cols],
        compiler_params=_cparams(("parallel", "parallel")),
        name="moe_unsort",
    )(x, info, ys, gain.reshape(1, d), *ws)
    return outs


def _pad_cols(w, n):
    return jnp.pad(w, ((0, 0), (0, n - w.shape[1])))


def _pad_rows(w, n):
    return jnp.pad(w, ((0, n - w.shape[0]), (0, 0)))


def _row(v, n=None):
    v = v.reshape(1, -1).astype(F32)
    return v if n is None else _pad_cols(v, n)


def kernel(x, mem, norm_mix, norm_xattn, norm_mem, norm_ffn, norm_final, ab_w_in, ab_conv, ab_a_log, ab_dt_bias, ab_onorm, ab_shift_mu, ab_w0, ab_w2, ab_a0, ab_a2, ab_g2, ab_k_k, ab_k_a, ab_r_k, ab_ln_w, ab_ln_b, ab_w_out, c_w_in, c_lb_param, c_onorm, c_w_out, xa_wq, xa_wkv, xa_wo, moe_w_group, moe_b_group, moe_w_expert, moe_b_expert, moe_w1, moe_w3, moe_w2):
    bsz, seq, d = x.shape
    n_mem = mem.shape[1]
    depth = norm_mix.shape[0]
    a_width = d // 2
    b_width = d // 2
    a_heads = a_width // A_HEAD_DIM
    a_cols = 4 * a_width + 2 * a_heads
    n_tok = bsz * seq

    lb_p = jax.nn.softmax(c_lb_param.astype(F32), axis=0)
    lower_bounds = jnp.cumsum(lb_p, axis=0) - lb_p[0]

    def project(x_cur, moe_pending, gain, ws):
        if moe_pending is None:
            projs = _norm_proj(x_cur.reshape(n_tok, d), gain, ws)
            return x_cur, [p.reshape(bsz, seq, -1) for p in projs]
        x_new, *projs = _moe_unsort(*moe_pending, gain, ws)
        return x_new, projs

    x2 = x
    moe_pending = None
    mem2 = mem.reshape(bsz * n_mem, d)
    for layer in range(depth):
        if layer % 2 == 0:
            e = layer // 2
            w_in = ab_w_in[e]
            w_a = w_in[:, :4 * a_width].astype(BF16)
            w_db = _pad_cols(w_in[:, 4 * a_width:a_cols], LANES).astype(BF16)
            wb = w_in[:, a_cols:]
            o1, o2 = 3 * b_width, 3 * b_width + B_DECAY_LORA
            o3 = o2 + B_ICLR_LORA
            w_b = jnp.concatenate([wb[:, :o1], _pad_cols(wb[:, o1:o2], LANES), _pad_cols(wb[:, o2:o3], LANES),
                                   wb[:, o3:]], axis=1).astype(BF16)
            mu = ab_shift_mu[e].reshape(1, -1)
            mu_p = jnp.concatenate([mu[:, :o1], _pad_cols(mu[:, o1:o2], LANES), _pad_cols(mu[:, o2:o3], LANES),
                                    mu[:, o3:]], axis=1)
            head_id = jnp.arange(b_width) // B_HEAD_DIM
            bd = (head_id[:, None] == head_id[None, :]).astype(F32)
            gdn_params = [ab_conv[e], _row(ab_a_log[e], LANES), _row(ab_dt_bias[e], LANES), _row(ab_onorm[e])]
            rwkv_params = [mu_p, _row(ab_w0[e]), _pad_rows(ab_w2[e], LANES), _row(ab_a0[e]),
                           _pad_rows(ab_a2[e], LANES), ab_g2[e], _row(ab_k_k[e]), _row(ab_k_a[e]),
                           _row(ab_r_k[e]), _row(ab_ln_w[e]), _row(ab_ln_b[e]), bd]
            x2, (pa, pdb, pb) = project(x2, moe_pending, norm_mix[layer], [w_a, w_db, w_b])
            mix = _mixer_ab(pa, pdb, pb, gdn_params, rwkv_params)
            w_mix = ab_w_out[e].astype(BF16)
        else:
            o = layer // 2
            x2, (proj,) = project(x2, moe_pending, norm_mix[layer], [c_w_in[o].astype(BF16)])
            lb = lower_bounds[layer].reshape(1, -1)
            mix = _hgrn(proj, jnp.log(lb), jnp.log1p(-lb), _row(c_onorm[o]))
            w_mix = c_w_out[o].astype(BF16)

        wkv = xa_wkv[layer].astype(BF16)
        k_mem, v_mem = _norm_proj(mem2, norm_mem[layer], [wkv[:, :d], wkv[:, d:]], out_dtype=BF16)
        w_r = _pad_cols(jnp.concatenate([moe_w_group[layer], moe_w_expert[layer]], axis=1), LANES).astype(F32)
        w_r_hi = w_r.astype(BF16)
        w_r_lo = (w_r - w_r_hi.astype(F32)).astype(BF16)
        b_router = _row(jnp.concatenate([moe_b_group[layer], moe_b_expert[layer]]), LANES)
        x3, xs, cs, info, seg = _xattn_route(
            x2, mix, w_mix, norm_xattn[layer], xa_wq[layer].astype(BF16),
            k_mem.reshape(bsz, n_mem, d), v_mem.reshape(bsz, n_mem, d), xa_wo[layer].astype(BF16),
            norm_ffn[layer], jnp.concatenate([w_r_hi, w_r_lo], axis=1), b_router)
        seg_i = jnp.concatenate([seg[:, 0, :N_GROUPS], seg[:, 1, :N_GROUPS]], axis=1).astype(jnp.int32).reshape(-1)
        ff = moe_w2.shape[2]
        ys = _moe_ffn(seg_i, xs, cs, moe_w1[layer].astype(BF16), moe_w3[layer].astype(BF16),
                      moe_w2[layer].reshape(N_GROUPS, EXPERTS_PER_GROUP * ff, d).astype(BF16))
        moe_pending = (x3, info, ys)
    (out,) = _moe_unsort(*moe_pending, norm_final)
    return out
```

```python
import functools

import jax
import jax.numpy as jnp
import numpy as np
from jax import lax
from jax.experimental import pallas as pl
from jax.experimental.pallas import tpu as pltpu

F32 = jnp.float32
BF16 = jnp.bfloat16

CHUNK = 64
HGRN_LEVELS = CHUNK.bit_length() - 1
NORM_EPS = 1e-6
GROUP_NORM_EPS = 64e-5
CONV_WIDTH = 4

A_HEAD_DIM = 128
B_HEAD_DIM = 64
B_DECAY_LORA = 64
B_ICLR_LORA = 64
B_GATE_LORA = 128
C_KEY_DIM = 128
XA_HEADS = 4
N_GROUPS = 4
EXPERTS_PER_GROUP = 4
N_EXPERTS = N_GROUPS * EXPERTS_PER_GROUP

LANES = 128
SUBLANES = 8
VMEM_LIMIT = 56 * 1024 * 1024
MIXER_BATCH = 4
SEG_ALIGN = 16
MOE_WINDOW = 4
MOE_CHUNK = 288
XATTN_TILES_PER_STEP = 2


def _cparams(sem):
    return pltpu.CompilerParams(dimension_semantics=sem, vmem_limit_bytes=VMEM_LIMIT)


def _dot_dims(a, b, dims):
    return lax.dot_general(a.astype(BF16), b.astype(BF16), (dims, ((), ())), preferred_element_type=F32)


def _dot(a, b):
    return _dot_dims(a, b, ((1,), (0,)))


def _dot_nt(a, b):
    return _dot_dims(a, b, ((1,), (1,)))


def _dot_tn(a, b):
    return _dot_dims(a, b, ((0,), (0,)))


def _rms(x, gain):
    return x * lax.rsqrt(jnp.mean(x * x, axis=-1, keepdims=True) + NORM_EPS) * gain


def _sigmoid(x):
    return 1.0 / (1.0 + jnp.exp(-x))


def _silu(x):
    return x * _sigmoid(x)


def _softplus(x):
    return jnp.maximum(x, 0.0) + jnp.log(1.0 + jnp.exp(-jnp.abs(x)))


def _iota2(shape, dim):
    return lax.broadcasted_iota(jnp.int32, shape, dim)


def _split3(v):
    v1 = v.astype(BF16)
    v2 = (v - v1.astype(F32)).astype(BF16)
    v3 = (v - v1.astype(F32) - v2.astype(F32)).astype(BF16)
    return v1, v2, v3


def _table_dot(table, v):
    return _dot(jnp.concatenate([table.astype(BF16)] * 3, axis=1), jnp.concatenate(_split3(v), axis=0))


def _table_dot_tn(v, table):
    return _dot_tn(jnp.concatenate(_split3(v), axis=0), jnp.concatenate([table.astype(BF16)] * 3, axis=0))


def _interleave(stage_generators):
    live = list(stage_generators)
    while live:
        for gen in list(live):
            try:
                next(gen)
            except StopIteration:
                live.remove(gen)


def _tri_inverse_stages(mats):
    n = mats[0].shape[0]
    row = _iota2((n, n), 0)
    col = _iota2((n, n), 1)
    eye = (row == col).astype(F32)
    xs = [eye - jnp.where((row >> 1) == (col >> 1), a, 0.0) for a in mats]
    for level in range(1, n.bit_length() - 1):
        joins = ((row >> (level + 1)) == (col >> (level + 1))) & ((row >> level) != (col >> level))
        ts = [_dot(jnp.where(joins, a, 0.0), x) for a, x in zip(mats, xs)]
        yield
        xs = [x - _dot(x, t) for x, t in zip(xs, ts)]
        yield
    return xs


def _norm_proj_kernel(n_w, x_ref, g_ref, *refs):
    w_refs, o_refs = refs[:n_w], refs[n_w:]
    xn = _rms(x_ref[...], g_ref[...]).astype(BF16)
    for w_ref, o_ref in zip(w_refs, o_refs):
        n = w_ref.shape[1]
        for c0 in range(0, n, 512):
            c1 = min(n, c0 + 512)
            o_ref[:, c0:c1] = _dot(xn, w_ref[:, c0:c1]).astype(o_ref.dtype)


def _norm_proj(x, gain, ws, out_dtype=F32, tm=512):
    n, d = x.shape
    tm = min(tm, n)
    return pl.pallas_call(
        functools.partial(_norm_proj_kernel, len(ws)),
        grid=(n // tm,),
        in_specs=[pl.BlockSpec((tm, d), lambda i: (i, 0)),
                  pl.BlockSpec((1, d), lambda i: (0, 0))]
                 + [pl.BlockSpec(w.shape, lambda i: (0, 0)) for w in ws],
        out_specs=[pl.BlockSpec((tm, w.shape[1]), lambda i: (i, 0)) for w in ws],
        out_shape=[jax.ShapeDtypeStruct((n, w.shape[1]), out_dtype) for w in ws],
        compiler_params=_cparams(("parallel",)),
        name="norm_proj",
    )(x, gain.reshape(1, d), *ws)


def _log_forget(f, log_lb, log1m_lb):
    a1 = jnp.broadcast_to(log_lb, f.shape)
    b1 = log1m_lb - _softplus(-f)
    return jnp.maximum(a1, b1) + jnp.log(1.0 + jnp.exp(-jnp.abs(a1 - b1)))


def _gdn_stages(nb, pa_ref, pdb_ref, convw_ref, alog_ref, dtb_ref, onorm_ref, o_ref, xbuf_ref, state_ref):
    n_heads = state_ref.shape[0] // nb
    hd = A_HEAD_DIM
    width = n_heads * hd
    row = _iota2((CHUNK, CHUNK), 0)
    col = _iota2((CHUNK, CHUNK), 1)
    incl = row >= col
    strict = row > col
    tril = incl.astype(F32)
    triu = (row <= col).astype(F32)

    qs, ks, vs, betas, gs, g_lasts, decays = [], [], [], [], [], [], []
    for bb in range(nb):
        x = pa_ref[bb, :, 0:3 * width]
        xbuf_ref[bb, 8:8 + CHUNK, :] = x
        acc = convw_ref[0:1, :] * xbuf_ref[bb, 5:5 + CHUNK, :]
        for j in range(1, CONV_WIDTH):
            acc = acc + convw_ref[j:j + 1, :] * xbuf_ref[bb, 5 + j:5 + j + CHUNK, :]
        xbuf_ref[bb, 0:8, :] = x[CHUNK - 8:CHUNK, :]
        qkv = _silu(acc)
        pdb = pdb_ref[bb]
        log_alpha = -jnp.exp(alog_ref[...]) * _softplus(pdb + dtb_ref[...])
        beta_all = _sigmoid(pdb)
        g_all = _table_dot(tril, log_alpha)
        g_all_t = _table_dot_tn(log_alpha, triu)
        for h in range(n_heads):
            q = qkv[:, h * hd:(h + 1) * hd]
            k = qkv[:, width + h * hd: width + (h + 1) * hd]
            qs.append(q * lax.rsqrt(jnp.sum(q * q, axis=-1, keepdims=True) + 1e-6) * (hd ** -0.5))
            ks.append(k * lax.rsqrt(jnp.sum(k * k, axis=-1, keepdims=True) + 1e-6))
            vs.append(qkv[:, 2 * width + h * hd: 2 * width + (h + 1) * hd])
            betas.append(beta_all[:, n_heads + h:n_heads + h + 1])
            g = g_all[:, h:h + 1]
            gs.append(g)
            g_lasts.append(g_all[CHUNK - 1:CHUNK, h:h + 1])
            decays.append(jnp.where(incl, jnp.exp(jnp.where(incl, g - g_all_t[h:h + 1, :], 0.0)), 0.0))
    yield
    idx = range(nb * n_heads)
    k_betas = [ks[i] * betas[i] for i in idx]
    egs = [jnp.exp(g) for g in gs]
    a_lows = [jnp.where(strict, _dot_nt(k_betas[i], ks[i]) * decays[i], 0.0) for i in idx]
    qks = [_dot_nt(qs[i], ks[i]) * decays[i] for i in idx]
    states = [state_ref[i] for i in idx]
    o_inters = [_dot(qs[i] * egs[i], states[i]) for i in idx]
    yield
    t_invs = yield from _tri_inverse_stages(a_lows)
    us = [_dot(t_invs[i], vs[i] * betas[i]) for i in idx]
    ws = [_dot(t_invs[i], k_betas[i] * egs[i]) for i in idx]
    yield
    v_news = [us[i] - _dot(ws[i], states[i]) for i in idx]
    yield
    os_ = [o_inters[i] + _dot(qks[i], v_news[i]) for i in idx]
    for i in idx:
        state_ref[i] = (states[i] * jnp.exp(g_lasts[i])
                        + _dot_tn(ks[i] * jnp.exp(g_lasts[i] - gs[i]), v_news[i]))
    yield
    for i in idx:
        bb, h = divmod(i, n_heads)
        gate = pa_ref[bb, :, 3 * width + h * hd: 3 * width + (h + 1) * hd]
        o_ref[bb, :, h * hd:(h + 1) * hd] = (_rms(os_[i], onorm_ref[...]) * _silu(gate)).astype(o_ref.dtype)


def _rwkv_stages(nb, col0, pb_ref, mu_ref, w0_ref, w2_ref, a0_ref, a2_ref, g2_ref, kk_ref, ka_ref, rk_ref,
                 lnw_ref, lnb_ref, bd_ref, o_ref, carry_ref, state_ref, obuf_ref):
    n_heads = state_ref.shape[0] // nb
    hd = B_HEAD_DIM
    width = n_heads * hd
    row = _iota2((CHUNK, CHUNK), 0)
    col = _iota2((CHUNK, CHUNK), 1)
    strict = row > col
    tril = (row >= col).astype(F32)
    incl2 = _iota2((CHUNK, 2 * CHUNK), 0) >= (_iota2((CHUNK, 2 * CHUNK), 1) & (CHUNK - 1))
    bd = bd_ref[...]

    pss = []
    for bb in range(nb):
        p = pb_ref[bb]
        prev = jnp.where(_iota2(p.shape, 0) == 0, carry_ref[bb, 0:1, :], pltpu.roll(p, 1, 0))
        carry_ref[bb, 0:1, :] = p[CHUNK - 1:CHUNK, :]
        pss.append(p + mu_ref[...] * (prev - p))
    rs = [ps[:, 0:width] for ps in pss]
    kraw = [ps[:, width:2 * width] for ps in pss]
    vfull = [ps[:, 2 * width:3 * width] for ps in pss]
    w_raws = [w0_ref[...] + _dot(jnp.tanh(ps[:, 3 * width:3 * width + LANES]), w2_ref[...]) for ps in pss]
    iclrs = [_sigmoid(a0_ref[...] + _dot(ps[:, 3 * width + LANES:3 * width + 2 * LANES], a2_ref[...]))
             for ps in pss]
    gates = [_dot(_sigmoid(ps[:, 3 * width + 2 * LANES:3 * width + 3 * LANES]), g2_ref[...]) for ps in pss]
    kks = [k * kk_ref[...] for k in kraw]
    kk_ss = [_dot(kk * kk, bd) for kk in kks]
    yield
    log_ws = [-jnp.exp(-_softplus(-w_raw) - 0.5) for w_raw in w_raws]
    g_cums = [_table_dot(tril, log_w) for log_w in log_ws]
    yield
    lhss, rhss, bl_kls, p_cs, vhs, k2s = [], [], [], [], [], []
    for bb in range(nb):
        kk = kks[bb] * lax.rsqrt(kk_ss[bb] + 1e-6)
        k2 = kraw[bb] * (1.0 + (iclrs[bb] - 1.0) * ka_ref[...])
        k2s.append(k2)
        bvec = kk * iclrs[bb]
        g = g_cums[bb]
        g_last = g[CHUNK - 1:CHUNK, :]
        eng = jnp.exp(-g)
        egl = jnp.exp(g_last - g)
        r_t = rs[bb] * jnp.exp(g)
        kk_t = kk * jnp.exp(g - log_ws[bb])
        b_h, k_h = bvec * eng, k2 * eng
        b_l, k_l = bvec * egl, k2 * egl
        p_c = jnp.exp(g_last)
        for h in range(n_heads):
            sl = slice(h * hd, (h + 1) * hd)
            lhss.append(jnp.concatenate([kk_t[:, sl], r_t[:, sl]], axis=0))
            rhss.append(jnp.concatenate([b_h[:, sl], k_h[:, sl]], axis=0))
            bl_kls.append(jnp.concatenate([b_l[:, sl], k_l[:, sl]], axis=0))
            p_cs.append(p_c[:, sl])
            vhs.append(vfull[bb][:, sl])
    idx = range(nb * n_heads)
    ms = [_dot_nt(lhss[i], rhss[i]) for i in idx]
    states = [state_ref[i] for i in idx]
    lss = [_dot_nt(lhss[i], states[i]) for i in idx]
    yield
    a_bbs = [jnp.where(strict, m[0:CHUNK, 0:CHUNK], 0.0) for m in ms]
    a_bks = [jnp.where(strict, m[0:CHUNK, CHUNK:2 * CHUNK], 0.0) for m in ms]
    a_rs = [jnp.where(incl2, m[CHUNK:2 * CHUNK, :], 0.0) for m in ms]
    rhs_u = [lss[i][0:CHUNK] + _dot(a_bks[i], vhs[i]) for i in idx]
    t_invs = yield from _tri_inverse_stages(a_bbs)
    uvs = [jnp.concatenate([-_dot(t_invs[i], rhs_u[i]), vhs[i]], axis=0) for i in idx]
    yield
    for i in idx:
        bb, h = divmod(i, n_heads)
        obuf_ref[bb, :, h * hd:(h + 1) * hd] = lss[i][CHUNK:2 * CHUNK] + _dot(a_rs[i], uvs[i])
        state_ref[i] = states[i] * p_cs[i] + _dot_tn(uvs[i], bl_kls[i])
    yield
    inv_n = 1.0 / hd
    os_ = [obuf_ref[bb] for bb in range(nb)]
    means = [_dot(o, bd) * inv_n for o in os_]
    bonus = [_dot(rs[bb] * k2s[bb] * rk_ref[...], bd) * vfull[bb] for bb in range(nb)]
    yield
    cens = [o - mean for o, mean in zip(os_, means)]
    vars_ = [_dot(cen * cen, bd) * inv_n for cen in cens]
    yield
    for bb in range(nb):
        ln = cens[bb] * lax.rsqrt(vars_[bb] + GROUP_NORM_EPS) * lnw_ref[...] + lnb_ref[...]
        o_ref[bb, :, col0:col0 + width] = ((ln + bonus[bb]) * gates[bb]).astype(o_ref.dtype)


def _mixer_ab_kernel(n_gdn, n_rwkv, pa_ref, pdb_ref, pb_ref, *refs):
    gdn_params = refs[:n_gdn]
    rwkv_params = refs[n_gdn:n_gdn + n_rwkv]
    o_ref, xbuf_ref, gstate_ref, carry_ref, rstate_ref, obuf_ref = refs[n_gdn + n_rwkv:]
    nb = pa_ref.shape[0]

    @pl.when(pl.program_id(1) == 0)
    def _():
        xbuf_ref[:, 0:8, :] = jnp.zeros((nb, 8, xbuf_ref.shape[2]), F32)
        gstate_ref[...] = jnp.zeros_like(gstate_ref)
        carry_ref[...] = jnp.zeros_like(carry_ref)
        rstate_ref[...] = jnp.zeros_like(rstate_ref)

    a_width = pa_ref.shape[2] // 4
    _interleave([
        _gdn_stages(nb, pa_ref, pdb_ref, *gdn_params, o_ref, xbuf_ref, gstate_ref),
        _rwkv_stages(nb, a_width, pb_ref, *rwkv_params, o_ref, carry_ref, rstate_ref, obuf_ref),
    ])


def _mixer_ab(pa, pdb, pb, gdn_params, rwkv_params):
    b, t, a_cols = pa.shape
    b_cols = pb.shape[2]
    a_width = a_cols // 4
    b_width = (b_cols - 3 * LANES) // 3
    a_heads = a_width // A_HEAD_DIM
    b_heads = b_width // B_HEAD_DIM
    nb = MIXER_BATCH if b % MIXER_BATCH == 0 else 1
    params = list(gdn_params) + list(rwkv_params)
    return pl.pallas_call(
        functools.partial(_mixer_ab_kernel, len(gdn_params), len(rwkv_params)),
        grid=(b // nb, t // CHUNK),
        in_specs=[pl.BlockSpec((nb, CHUNK, a_cols), lambda i, c: (i, c, 0)),
                  pl.BlockSpec((nb, CHUNK, LANES), lambda i, c: (i, c, 0)),
                  pl.BlockSpec((nb, CHUNK, b_cols), lambda i, c: (i, c, 0))]
                 + [pl.BlockSpec(a.shape, lambda i, c: (0, 0)) for a in params],
        out_specs=pl.BlockSpec((nb, CHUNK, a_width + b_width), lambda i, c: (i, c, 0)),
        out_shape=jax.ShapeDtypeStruct((b, t, a_width + b_width), BF16),
        scratch_shapes=[pltpu.VMEM((nb, 8 + CHUNK, 3 * a_width), F32),
                        pltpu.VMEM((nb * a_heads, A_HEAD_DIM, A_HEAD_DIM), F32),
                        pltpu.VMEM((nb, 8, b_cols), F32),
                        pltpu.VMEM((nb * b_heads, B_HEAD_DIM, B_HEAD_DIM), F32),
                        pltpu.VMEM((nb, CHUNK, b_width), F32)],
        compiler_params=_cparams(("parallel", "arbitrary")),
        name="mixer_ab",
    )(pa, pdb, pb, *params)


def _hgrn_tables():
    i = np.arange(CHUNK)[:, None]
    t = np.arange(CHUNK)[None, :]
    blocks, pair = [], []
    for ls in range(HGRN_LEVELS - 1, -1, -1):
        s = 1 << ls
        mid = (i >> (ls + 1) << (ls + 1)) + s
        q_rows, k_rows = _hgrn_level_rows(ls)
        blocks.append(((i >= mid) & (t >= mid) & (t <= i))[q_rows])
        blocks.append(((i < mid) & (t > i) & (t <= mid - 1))[k_rows])
        pair.append((((i >> ls) ^ (t >> ls)) == 1) & (i > t))
    blocks.append(t <= i)
    blocks.append(t > i)
    sums = np.concatenate(blocks, axis=0).astype(np.float32)
    return np.concatenate([sums] * 3, axis=1), np.stack(pair).astype(np.float32)


def _hgrn_level_rows(ls):
    rows = np.arange(CHUNK)
    if (1 << ls) < SUBLANES:
        return rows, rows
    upper = ((rows >> ls) & 1) == 1
    return rows[upper], rows[~upper]


def _hgrn_kernel(p_ref, loglb_ref, log1mlb_ref, onorm_ref, sums_ref, pair_ref, o_ref, state_ref):
    nb = p_ref.shape[0]
    n_heads = state_ref.shape[0] // nb
    hd = C_KEY_DIM
    kw = n_heads * hd

    @pl.when(pl.program_id(1) == 0)
    def _():
        state_ref[...] = jnp.zeros_like(state_ref)

    def take_rows(a, rows):
        runs, start = [], 0
        for n in range(1, len(rows) + 1):
            if n == len(rows) or rows[n] != rows[n - 1] + 1:
                runs.append(a[int(rows[start]):int(rows[n - 1]) + 1])
                start = n
        return runs[0] if len(runs) == 1 else jnp.concatenate(runs, axis=0)

    def spread_rows(a, rows):
        if len(rows) == CHUNK:
            return a
        pieces, pos, taken = [], 0, 0
        while pos < CHUNK:
            run = 1
            inside = pos in rows
            while pos + run < CHUNK and ((pos + run) in rows) == inside:
                run += 1
            if inside:
                pieces.append(a[taken:taken + run])
                taken += run
            else:
                pieces.append(jnp.zeros((run, a.shape[1]), a.dtype))
            pos += run
        return jnp.concatenate(pieces, axis=0)

    qs, ks, exps = [], [], []
    for bb in range(nb):
        qs.append(_silu(p_ref[bb, :, 0:kw]))
        log_f = _log_forget(p_ref[bb, :, kw:2 * kw], loglb_ref[...], log1mlb_ref[...])
        ks.append(1.0 - jnp.exp(log_f))
        exps.append(_dot(sums_ref[...], jnp.concatenate(_split3(log_f), axis=0)))

    idx = [(bb, h) for bb in range(nb) for h in range(n_heads)]
    states = [state_ref[bb * n_heads + h] for bb, h in idx]
    scores = [None] * len(idx)
    offset = 0
    for lvl in range(HGRN_LEVELS):
        q_rows, k_rows = _hgrn_level_rows(HGRN_LEVELS - 1 - lvl)
        q_rows, k_rows = [int(r) for r in q_rows], [int(r) for r in k_rows]
        e_q = [e[offset:offset + len(q_rows)] for e in exps]
        e_k = [e[offset + len(q_rows):offset + len(q_rows) + len(k_rows)] for e in exps]
        offset += len(q_rows) + len(k_rows)
        q_l = [(take_rows(qs[bb], q_rows) * jnp.exp(e_q[bb])).astype(BF16) for bb in range(nb)]
        k_l = [spread_rows(take_rows(ks[bb], k_rows) * jnp.exp(e_k[bb]), k_rows).astype(BF16) for bb in range(nb)]
        mask = None if (lvl == 0 and len(q_rows) < CHUNK) else pair_ref[lvl]
        for n, (bb, h) in enumerate(idx):
            sl = slice(h * hd, (h + 1) * hd)
            part = spread_rows(_dot_nt(q_l[bb][:, sl], k_l[bb][:, sl]), q_rows)
            if mask is not None:
                part = part * mask
            scores[n] = part if scores[n] is None else scores[n] + part
    g_cum = [e[offset:offset + CHUNK] for e in exps]
    q_g = [qs[bb] * jnp.exp(g_cum[bb]) for bb in range(nb)]
    k_r = [ks[bb] * jnp.exp(exps[bb][offset + CHUNK:offset + 2 * CHUNK]) for bb in range(nb)]
    diag = [qs[bb] * ks[bb] for bb in range(nb)]
    outs = []
    for n, (bb, h) in enumerate(idx):
        sl = slice(h * hd, (h + 1) * hd)
        v = p_ref[bb, :, 2 * kw + h * hd: 2 * kw + (h + 1) * hd]
        o = _dot_nt(q_g[bb][:, sl], states[n]) + _dot(scores[n], v)
        outs.append(o + jnp.sum(diag[bb][:, sl], axis=-1, keepdims=True) * v)
        p_last = jnp.exp(g_cum[bb][CHUNK - 1:CHUNK, sl])
        state_ref[bb * n_heads + h] = states[n] * p_last + _dot_tn(v, k_r[bb][:, sl])
    for n, (bb, h) in enumerate(idx):
        gate = p_ref[bb, :, 3 * kw + h * hd: 3 * kw + (h + 1) * hd]
        o_ref[bb, :, h * hd:(h + 1) * hd] = (_rms(outs[n], onorm_ref[...]) * _silu(gate)).astype(o_ref.dtype)


def _hgrn(proj, log_lb, log1m_lb, onorm):
    b, t, cols = proj.shape
    kw = cols // 4
    n_heads = kw // C_KEY_DIM
    nb = MIXER_BATCH if b % MIXER_BATCH == 0 else 1
    sums, pair = _hgrn_tables()
    sums = jnp.asarray(sums, BF16)
    pair = jnp.asarray(pair, F32)
    return pl.pallas_call(
        _hgrn_kernel,
        grid=(b // nb, t // CHUNK),
        in_specs=[pl.BlockSpec((nb, CHUNK, cols), lambda i, c: (i, c, 0)),
                  pl.BlockSpec((1, kw), lambda i, c: (0, 0)),
                  pl.BlockSpec((1, kw), lambda i, c: (0, 0)),
                  pl.BlockSpec((1, C_KEY_DIM), lambda i, c: (0, 0)),
                  pl.BlockSpec(sums.shape, lambda i, c: (0, 0)),
                  pl.BlockSpec(pair.shape, lambda i, c: (0, 0, 0))],
        out_specs=pl.BlockSpec((nb, CHUNK, kw), lambda i, c: (i, c, 0)),
        out_shape=jax.ShapeDtypeStruct((b, t, kw), BF16),
        scratch_shapes=[pltpu.VMEM((nb * n_heads, C_KEY_DIM, C_KEY_DIM), F32)],
        compiler_params=_cparams(("parallel", "arbitrary")),
        name="hgrn2",
    )(proj, log_lb, log1m_lb, onorm, sums, pair)


def _route(logits):
    lane = _iota2(logits.shape, 1)
    neg = -jnp.inf
    gmask = lane < N_GROUPS
    gl = jnp.where(gmask, logits, neg)
    gmax = jnp.max(gl, axis=-1, keepdims=True)
    gsel = jnp.min(jnp.where(gl == gmax, lane, LANES), axis=-1, keepdims=True)
    group_gate = 1.0 / jnp.sum(jnp.where(gmask, jnp.exp(gl - gmax), 0.0), axis=-1, keepdims=True)
    lo = N_GROUPS + EXPERTS_PER_GROUP * gsel
    emask = (lane >= lo) & (lane < lo + EXPERTS_PER_GROUP)
    el = jnp.where(emask, logits, neg)
    ee = jnp.where(emask, jnp.exp(el - jnp.max(el, axis=-1, keepdims=True)), 0.0)
    prob = ee / jnp.sum(ee, axis=-1, keepdims=True)
    p1 = jnp.max(jnp.where(emask, prob, -1.0), axis=-1, keepdims=True)
    i1 = jnp.min(jnp.where(emask & (prob == p1), lane, LANES), axis=-1, keepdims=True)
    rest = emask & (lane != i1)
    p2 = jnp.max(jnp.where(rest, prob, -1.0), axis=-1, keepdims=True)
    i2 = jnp.min(jnp.where(rest & (prob == p2), lane, LANES), axis=-1, keepdims=True)
    scale = group_gate / (p1 + p2)
    comb = jnp.where(lane == i1, p1 * scale, 0.0) + jnp.where(lane == i2, p2 * scale, 0.0)
    return comb, gsel


def _sorted_rows(tile_rows):
    return tile_rows + N_GROUPS * SEG_ALIGN


def _xattn_route_kernel(x_ref, a_ref, wmix_ref, gain_ref, wq_ref, k_ref, v_ref, wo_ref, gffn_ref, wr_ref, br_ref,
                        x2_ref, xs_ref, cs_ref, info_ref, seg_ref):
    n_sub = xs_ref.shape[0]
    tq = x_ref.shape[0] // n_sub
    d = x_ref.shape[1]
    hd = d // XA_HEADS
    rows = xs_ref.shape[1]

    def tile_stages(u):
        tok = slice(u * tq, (u + 1) * tq)
        x = x_ref[tok, :] + _dot(a_ref[tok, :], wmix_ref[...])
        q = _dot(_rms(x, gain_ref[...]).astype(BF16), wq_ref[...]).astype(BF16)
        yield
        outs = []
        for h in range(XA_HEADS):
            sl = slice(h * hd, (h + 1) * hd)
            s = _dot_nt(q[:, sl], k_ref[:, sl]) * (hd ** -0.5)
            p = jnp.exp(s - jnp.max(s, axis=-1, keepdims=True))
            p = p / jnp.sum(p, axis=-1, keepdims=True)
            outs.append(_dot(p.astype(BF16), v_ref[:, sl]).astype(BF16))
        yield
        x2 = x + _dot(jnp.concatenate(outs, axis=-1), wo_ref[...])
        x2_ref[tok, :] = x2
        xn = _rms(x2, gffn_ref[...])
        xn_hi = xn.astype(BF16)
        xn_lo = (xn - xn_hi.astype(F32)).astype(BF16)
        hi_part = _dot(xn_hi, wr_ref[...])
        logits = hi_part[:, 0:LANES] + hi_part[:, LANES:2 * LANES] + _dot(xn_lo, wr_ref[:, 0:LANES])
        yield
        comb, gsel = _route(logits + br_ref[...])
        lane = _iota2((tq, LANES), 1)
        onehot = (lane == gsel).astype(F32)
        earlier = (_iota2((tq, tq), 0) > _iota2((tq, tq), 1)).astype(BF16)
        before = _dot(earlier, onehot.astype(BF16))
        yield
        rank = jnp.sum(onehot * before, axis=-1, keepdims=True)
        counts = before[tq - 1:tq, :] + onehot[tq - 1:tq, :]
        padded = jnp.floor((counts + (SEG_ALIGN - 1)) * (1.0 / SEG_ALIGN)) * SEG_ALIGN
        lane1 = _iota2((1, LANES), 1)
        starts = jnp.zeros((1, LANES), F32)
        for g in range(N_GROUPS - 1):
            starts = starts + jnp.where(lane1 > g, padded[:, g:g + 1], 0.0)
        dest = jnp.sum(onehot * starts, axis=-1, keepdims=True) + rank
        place = (_iota2((tq, rows), 1).astype(F32) == dest).astype(BF16)
        comb_hi = comb.astype(BF16)
        comb_lo = (comb - comb_hi.astype(F32)).astype(BF16)
        placed = _dot_tn(place, jnp.concatenate([xn_hi, comb_hi, comb_lo], axis=1))
        xs_ref[u] = placed[:, 0:d].astype(BF16)
        cs_ref[u] = placed[:, d:d + LANES] + placed[:, d + LANES:d + 2 * LANES]
        info_ref[tok, :] = jnp.where(lane == 0, dest, 0.0)
        seg_ref[u] = jnp.concatenate([starts, padded, jnp.zeros((6, LANES), F32)], axis=0)

    _interleave([tile_stages(u) for u in range(n_sub)])


def _xattn_route(x, act, w_mix, gain, wq, k, v, wo, gain_ffn, w_router, b_router, tq=512):
    b, t, d = x.shape
    m = k.shape[1]
    tq = min(tq, t)
    tiles_per_row = t // tq
    n_tiles = b * tiles_per_row
    rows = _sorted_rows(tq)
    n_sub = XATTN_TILES_PER_STEP if tiles_per_row % XATTN_TILES_PER_STEP == 0 else 1
    steps_per_row = tiles_per_row // n_sub

    def tile(i, j):
        return (i * steps_per_row + j, 0, 0)

    return pl.pallas_call(
        _xattn_route_kernel,
        grid=(b, steps_per_row),
        in_specs=[pl.BlockSpec((None, n_sub * tq, d), lambda i, j: (i, j, 0)),
                  pl.BlockSpec((None, n_sub * tq, act.shape[2]), lambda i, j: (i, j, 0)),
                  pl.BlockSpec(w_mix.shape, lambda i, j: (0, 0)),
                  pl.BlockSpec((1, d), lambda i, j: (0, 0)),
                  pl.BlockSpec((d, d), lambda i, j: (0, 0)),
                  pl.BlockSpec((None, m, d), lambda i, j: (i, 0, 0)),
                  pl.BlockSpec((None, m, d), lambda i, j: (i, 0, 0)),
                  pl.BlockSpec((d, d), lambda i, j: (0, 0)),
                  pl.BlockSpec((1, d), lambda i, j: (0, 0)),
                  pl.BlockSpec((d, 2 * LANES), lambda i, j: (0, 0)),
                  pl.BlockSpec((1, LANES), lambda i, j: (0, 0))],
        out_specs=[pl.BlockSpec((None, n_sub * tq, d), lambda i, j: (i, j, 0)),
                   pl.BlockSpec((n_sub, rows, d), tile),
                   pl.BlockSpec((n_sub, rows, LANES), tile),
                   pl.BlockSpec((None, n_sub * tq, LANES), lambda i, j: (i, j, 0)),
                   pl.BlockSpec((n_sub, 8, LANES), tile)],
        out_shape=[jax.ShapeDtypeStruct((b, t, d), F32),
                   jax.ShapeDtypeStruct((n_tiles, rows, d), BF16),
                   jax.ShapeDtypeStruct((n_tiles, rows, LANES), F32),
                   jax.ShapeDtypeStruct((b, t, LANES), F32),
                   jax.ShapeDtypeStruct((n_tiles, 8, LANES), F32)],
        compiler_params=_cparams(("parallel", "parallel")),
        name="xattn_route",
    )(x, act, w_mix, gain.reshape(1, d), wq, k, v, wo, gain_ffn.reshape(1, d), w_router, b_router)


def _moe_ffn_kernel(seg_ref, xs_ref, cs_ref, w1_ref, w3_ref, w2_ref, ys_ref, xbuf_ref, cbuf_ref, ybuf_ref):
    win = pl.program_id(0)
    grp = pl.program_id(1)
    n_win = xs_ref.shape[0]

    @pl.when((win == 0) & (grp == 0))
    def _():
        xbuf_ref[...] = jnp.zeros_like(xbuf_ref)
        cbuf_ref[...] = jnp.zeros_like(cbuf_ref)

    @pl.when(grp == 0)
    def _():
        ys_ref[...] = jnp.zeros_like(ys_ref)

    def segment(t):
        base = (win * n_win + t) * 8
        return seg_ref[base + grp], seg_ref[base + N_GROUPS + grp]

    def copy_segments(copy_piece):
        packed = jnp.int32(0)
        for t in range(n_win):
            start, size = segment(t)

            def body(j, carry, t=t, start=start, packed=packed):
                copy_piece(t, pl.multiple_of(start + j * SEG_ALIGN, SEG_ALIGN),
                           pl.multiple_of(packed + j * SEG_ALIGN, SEG_ALIGN))
                return carry

            lax.fori_loop(0, size // SEG_ALIGN, body, 0)
            packed = packed + size
        return packed

    def gather(t, src, dst):
        xbuf_ref[pl.ds(dst, SEG_ALIGN), :] = xs_ref[t, pl.ds(src, SEG_ALIGN), :]
        cbuf_ref[pl.ds(dst, SEG_ALIGN), :] = cs_ref[t, pl.ds(src, SEG_ALIGN), :]

    total = copy_segments(gather)

    def ffn_piece(r0, n_rows):
        xc = xbuf_ref[pl.ds(r0, n_rows), :]
        cb = cbuf_ref[pl.ds(r0, n_rows), :]
        lane = _iota2(cb.shape, 1)
        hids = []
        for e in range(EXPERTS_PER_GROUP):
            weight = jnp.sum(jnp.where(lane == N_GROUPS + EXPERTS_PER_GROUP * grp + e, cb, 0.0),
                             axis=-1, keepdims=True)
            hids.append((_silu(_dot(xc, w1_ref[e])) * _dot(xc, w3_ref[e]) * weight).astype(BF16))
        ybuf_ref[pl.ds(r0, n_rows), :] = _dot(jnp.concatenate(hids, axis=-1), w2_ref[...]).astype(ybuf_ref.dtype)

    def chunk(c, carry):
        ffn_piece(pl.multiple_of(c * MOE_CHUNK, MOE_CHUNK), MOE_CHUNK)
        return carry

    n_full = total // MOE_CHUNK
    lax.fori_loop(0, n_full, chunk, 0)
    rest = total - n_full * MOE_CHUNK
    rest_start = pl.multiple_of(n_full * MOE_CHUNK, MOE_CHUNK)
    lower = 0
    for n_rows in (MOE_CHUNK // 2, MOE_CHUNK):
        @pl.when((rest > lower) & (rest <= n_rows))
        def _(n_rows=n_rows):
            ffn_piece(rest_start, n_rows)
        lower = n_rows

    def scatter(t, dst, src):
        ys_ref[t, pl.ds(dst, SEG_ALIGN), :] = ybuf_ref[pl.ds(src, SEG_ALIGN), :]

    copy_segments(scatter)


def _moe_ffn(seg, xs, cs, w1, w3, w2g):
    n_tiles, rows, d = xs.shape
    ff = w1.shape[2]
    n_win = next(w for w in (MOE_WINDOW, 2, 1) if n_tiles % w == 0)
    buf_rows = -(-(n_win * rows) // MOE_CHUNK) * MOE_CHUNK
    grid_spec = pltpu.PrefetchScalarGridSpec(
        num_scalar_prefetch=1,
        grid=(n_tiles // n_win, N_GROUPS),
        in_specs=[pl.BlockSpec((n_win, rows, d), lambda w, g, seg: (w, 0, 0)),
                  pl.BlockSpec((n_win, rows, LANES), lambda w, g, seg: (w, 0, 0)),
                  pl.BlockSpec((EXPERTS_PER_GROUP, d, ff), lambda w, g, seg: (g, 0, 0)),
                  pl.BlockSpec((EXPERTS_PER_GROUP, d, ff), lambda w, g, seg: (g, 0, 0)),
                  pl.BlockSpec((None, EXPERTS_PER_GROUP * ff, d), lambda w, g, seg: (g, 0, 0))],
        out_specs=pl.BlockSpec((n_win, rows, d), lambda w, g, seg: (w, 0, 0)),
        scratch_shapes=[pltpu.VMEM((buf_rows, d), BF16),
                        pltpu.VMEM((buf_rows, LANES), F32),
                        pltpu.VMEM((buf_rows, d), BF16)])
    return pl.pallas_call(
        _moe_ffn_kernel,
        grid_spec=grid_spec,
        out_shape=jax.ShapeDtypeStruct((n_tiles, rows, d), BF16),
        compiler_params=_cparams(("arbitrary", "arbitrary")),
        name="moe_ffn",
    )(seg, xs, cs, w1, w3, w2g)


def _unsorted_sum(x_ref, info_ref, ys_ref):
    info = info_ref[...]
    dest = jnp.sum(jnp.where(_iota2(info.shape, 1) == 0, info, 0.0), axis=-1, keepdims=True)
    rows = ys_ref.shape[0]
    place = (_iota2((info.shape[0], rows), 1).astype(F32) == dest).astype(BF16)
    return x_ref[...] + _dot(place, ys_ref[...])


def _moe_unsort_kernel(n_w, x_ref, info_ref, ys_ref, gain_ref, *refs):
    w_refs, o_refs = refs[:n_w], refs[n_w:]
    if n_w == 0:
        tq = x_ref.shape[0] // ys_ref.shape[0]
        for u in range(ys_ref.shape[0]):
            tok = slice(u * tq, (u + 1) * tq)
            y = _unsorted_sum(x_ref.at[tok, :], info_ref.at[tok, :], ys_ref.at[u])
            o_refs[0][tok, :] = _rms(y, gain_ref[...])
        return
    y = _unsorted_sum(x_ref, info_ref, ys_ref.at[0])
    yn = _rms(y, gain_ref[...])
    o_refs[0][...] = y
    yn = yn.astype(BF16)
    for w_ref, o_ref in zip(w_refs, o_refs[1:]):
        n = w_ref.shape[1]
        for c0 in range(0, n, 512):
            c1 = min(n, c0 + 512)
            o_ref[:, c0:c1] = _dot(yn, w_ref[:, c0:c1])


def _moe_unsort(x, info, ys, gain, ws=()):
    b, t, d = x.shape
    n_tiles, rows, _ = ys.shape
    tiles_per_row = n_tiles // b
    tq = t // tiles_per_row
    n_sub = 2 if (not ws and tiles_per_row % 2 == 0) else 1
    steps_per_row = tiles_per_row // n_sub

    def token_block(cols):
        return pl.BlockSpec((None, n_sub * tq, cols), lambda i, j: (i, j, 0))

    out_cols = [d] + [w.shape[1] for w in ws]
    outs = pl.pallas_call(
        functools.partial(_moe_unsort_kernel, len(ws)),
        grid=(b, steps_per_row),
        in_specs=[token_block(d), token_block(LANES),
                  pl.BlockSpec((n_sub, rows, d), lambda i, j: (i * steps_per_row + j, 0, 0)),
                  pl.BlockSpec((1, d), lambda i, j: (0, 0))]
                 + [pl.BlockSpec(w.shape, lambda i, j: (0, 0)) for w in ws],
        out_specs=[token_block(c) for c in out_cols],
        out_shape=[jax.ShapeDtypeStruct((b, t, c), F32) for c in out_cols],
        compiler_params=_cparams(("parallel", "parallel")),
        name="moe_unsort",
    )(x, info, ys, gain.reshape(1, d), *ws)
    return outs


def _pad_cols(w, n):
    return jnp.pad(w, ((0, 0), (0, n - w.shape[1])))


def _pad_rows(w, n):
    return jnp.pad(w, ((0, n - w.shape[0]), (0, 0)))


def _row(v, n=None):
    v = v.reshape(1, -1).astype(F32)
    return v if n is None else _pad_cols(v, n)


def kernel(x, mem, norm_mix, norm_xattn, norm_mem, norm_ffn, norm_final, ab_w_in, ab_conv, ab_a_log, ab_dt_bias, ab_onorm, ab_shift_mu, ab_w0, ab_w2, ab_a0, ab_a2, ab_g2, ab_k_k, ab_k_a, ab_r_k, ab_ln_w, ab_ln_b, ab_w_out, c_w_in, c_lb_param, c_onorm, c_w_out, xa_wq, xa_wkv, xa_wo, moe_w_group, moe_b_group, moe_w_expert, moe_b_expert, moe_w1, moe_w3, moe_w2):
    bsz, seq, d = x.shape
    n_mem = mem.shape[1]
    depth = norm_mix.shape[0]
    a_width = d // 2
    b_width = d // 2
    a_heads = a_width // A_HEAD_DIM
    a_cols = 4 * a_width + 2 * a_heads
    n_tok = bsz * seq

    lb_p = jax.nn.softmax(c_lb_param.astype(F32), axis=0)
    lower_bounds = jnp.cumsum(lb_p, axis=0) - lb_p[0]

    def project(x_cur, moe_pending, gain, ws):
        if moe_pending is None:
            projs = _norm_proj(x_cur.reshape(n_tok, d), gain, ws)
            return x_cur, [p.reshape(bsz, seq, -1) for p in projs]
        x_new, *projs = _moe_unsort(*moe_pending, gain, ws)
        return x_new, projs

    x2 = x
    moe_pending = None
    mem2 = mem.reshape(bsz * n_mem, d)
    for layer in range(depth):
        if layer % 2 == 0:
            e = layer // 2
            w_in = ab_w_in[e]
            w_a = w_in[:, :4 * a_width].astype(BF16)
            w_db = _pad_cols(w_in[:, 4 * a_width:a_cols], LANES).astype(BF16)
            wb = w_in[:, a_cols:]
            o1, o2 = 3 * b_width, 3 * b_width + B_DECAY_LORA
            o3 = o2 + B_ICLR_LORA
            w_b = jnp.concatenate([wb[:, :o1], _pad_cols(wb[:, o1:o2], LANES), _pad_cols(wb[:, o2:o3], LANES),
                                   wb[:, o3:]], axis=1).astype(BF16)
            mu = ab_shift_mu[e].reshape(1, -1)
            mu_p = jnp.concatenate([mu[:, :o1], _pad_cols(mu[:, o1:o2], LANES), _pad_cols(mu[:, o2:o3], LANES),
                                    mu[:, o3:]], axis=1)
            head_id = jnp.arange(b_width) // B_HEAD_DIM
            bd = (head_id[:, None] == head_id[None, :]).astype(F32)
            gdn_params = [ab_conv[e], _row(ab_a_log[e], LANES), _row(ab_dt_bias[e], LANES), _row(ab_onorm[e])]
            rwkv_params = [mu_p, _row(ab_w0[e]), _pad_rows(ab_w2[e], LANES), _row(ab_a0[e]),
                           _pad_rows(ab_a2[e], LANES), ab_g2[e], _row(ab_k_k[e]), _row(ab_k_a[e]),
                           _row(ab_r_k[e]), _row(ab_ln_w[e]), _row(ab_ln_b[e]), bd]
            x2, (pa, pdb, pb) = project(x2, moe_pending, norm_mix[layer], [w_a, w_db, w_b])
            mix = _mixer_ab(pa, pdb, pb, gdn_params, rwkv_params)
            w_mix = ab_w_out[e].astype(BF16)
        else:
            o = layer // 2
            x2, (proj,) = project(x2, moe_pending, norm_mix[layer], [c_w_in[o].astype(BF16)])
            lb = lower_bounds[layer].reshape(1, -1)
            mix = _hgrn(proj, jnp.log(lb), jnp.log1p(-lb), _row(c_onorm[o]))
            w_mix = c_w_out[o].astype(BF16)

        wkv = xa_wkv[layer].astype(BF16)
        k_mem, v_mem = _norm_proj(mem2, norm_mem[layer], [wkv[:, :d], wkv[:, d:]], out_dtype=BF16)
        w_r = _pad_cols(jnp.concatenate([moe_w_group[layer], moe_w_expert[layer]], axis=1), LANES).astype(F32)
        w_r_hi = w_r.astype(BF16)
        w_r_lo = (w_r - w_r_hi.astype(F32)).astype(BF16)
        b_router = _row(jnp.concatenate([moe_b_group[layer], moe_b_expert[layer]]), LANES)
        x3, xs, cs, info, seg = _xattn_route(
            x2, mix, w_mix, norm_xattn[layer], xa_wq[layer].astype(BF16),
            k_mem.reshape(bsz, n_mem, d), v_mem.reshape(bsz, n_mem, d), xa_wo[layer].astype(BF16),
            norm_ffn[layer], jnp.concatenate([w_r_hi, w_r_lo], axis=1), b_router)
        seg_i = jnp.concatenate([seg[:, 0, :N_GROUPS], seg[:, 1, :N_GROUPS]], axis=1).astype(jnp.int32).reshape(-1)
        ff = moe_w2.shape[2]
        ys = _moe_ffn(seg_i, xs, cs, moe_w1[layer].astype(BF16), moe_w3[layer].astype(BF16),
                      moe_w2[layer].reshape(N_GROUPS, EXPERTS_PER_GROUP * ff, d).astype(BF16))
        moe_pending = (x3, info, ys)
    (out,) = _moe_unsort(*moe_pending, norm_final)
    return out
```

```python
import functools

import jax
import jax.numpy as jnp
import numpy as np
from jax import lax
from jax.experimental import pallas as pl
from jax.experimental.pallas import tpu as pltpu

F32 = jnp.float32
BF16 = jnp.bfloat16

CHUNK = 64
HGRN_LEVELS = CHUNK.bit_length() - 1
NORM_EPS = 1e-6
GROUP_NORM_EPS = 64e-5
CONV_WIDTH = 4

A_HEAD_DIM = 128
B_HEAD_DIM = 64
B_DECAY_LORA = 64
B_ICLR_LORA = 64
B_GATE_LORA = 128
C_KEY_DIM = 128
XA_HEADS = 4
N_GROUPS = 4
EXPERTS_PER_GROUP = 4
N_EXPERTS = N_GROUPS * EXPERTS_PER_GROUP

LANES = 128
SUBLANES = 8
VMEM_LIMIT = 56 * 1024 * 1024
MIXER_BATCH = 4
SEG_ALIGN = 16
MOE_WINDOW = 4
MOE_CHUNK = 288
XATTN_TILES_PER_STEP = 2


def _cparams(sem):
    return pltpu.CompilerParams(dimension_semantics=sem, vmem_limit_bytes=VMEM_LIMIT)


def _dot_dims(a, b, dims):
    return lax.dot_general(a.astype(BF16), b.astype(BF16), (dims, ((), ())), preferred_element_type=F32)


def _dot(a, b):
    return _dot_dims(a, b, ((1,), (0,)))


def _dot_nt(a, b):
    return _dot_dims(a, b, ((1,), (1,)))


def _dot_tn(a, b):
    return _dot_dims(a, b, ((0,), (0,)))


def _rms(x, gain):
    return x * lax.rsqrt(jnp.mean(x * x, axis=-1, keepdims=True) + NORM_EPS) * gain


def _sigmoid(x):
    return 1.0 / (1.0 + jnp.exp(-x))


def _silu(x):
    return x * _sigmoid(x)


def _softplus(x):
    return jnp.maximum(x, 0.0) + jnp.log(1.0 + jnp.exp(-jnp.abs(x)))


def _iota2(shape, dim):
    return lax.broadcasted_iota(jnp.int32, shape, dim)


def _split3(v):
    v1 = v.astype(BF16)
    v2 = (v - v1.astype(F32)).astype(BF16)
    v3 = (v - v1.astype(F32) - v2.astype(F32)).astype(BF16)
    return v1, v2, v3


def _table_dot(table, v):
    return _dot(jnp.concatenate([table.astype(BF16)] * 3, axis=1), jnp.concatenate(_split3(v), axis=0))


def _table_dot_tn(v, table):
    return _dot_tn(jnp.concatenate(_split3(v), axis=0), jnp.concatenate([table.astype(BF16)] * 3, axis=0))


def _interleave(stage_generators):
    live = list(stage_generators)
    while live:
        for gen in list(live):
            try:
                next(gen)
            except StopIteration:
                live.remove(gen)


def _tri_inverse_stages(mats):
    n = mats[0].shape[0]
    row = _iota2((n, n), 0)
    col = _iota2((n, n), 1)
    eye = (row == col).astype(F32)
    xs = [eye - jnp.where((row >> 1) == (col >> 1), a, 0.0) for a in mats]
    for level in range(1, n.bit_length() - 1):
        joins = ((row >> (level + 1)) == (col >> (level + 1))) & ((row >> level) != (col >> level))
        ts = [_dot(jnp.where(joins, a, 0.0), x) for a, x in zip(mats, xs)]
        yield
        xs = [x - _dot(x, t) for x, t in zip(xs, ts)]
        yield
    return xs


def _norm_proj_kernel(n_w, x_ref, g_ref, *refs):
    w_refs, o_refs = refs[:n_w], refs[n_w:]
    xn = _rms(x_ref[...], g_ref[...]).astype(BF16)
    for w_ref, o_ref in zip(w_refs, o_refs):
        n = w_ref.shape[1]
        for c0 in range(0, n, 512):
            c1 = min(n, c0 + 512)
            o_ref[:, c0:c1] = _dot(xn, w_ref[:, c0:c1]).astype(o_ref.dtype)


def _norm_proj(x, gain, ws, out_dtype=F32, tm=512):
    n, d = x.shape
    tm = min(tm, n)
    return pl.pallas_call(
        functools.partial(_norm_proj_kernel, len(ws)),
        grid=(n // tm,),
        in_specs=[pl.BlockSpec((tm, d), lambda i: (i, 0)),
                  pl.BlockSpec((1, d), lambda i: (0, 0))]
                 + [pl.BlockSpec(w.shape, lambda i: (0, 0)) for w in ws],
        out_specs=[pl.BlockSpec((tm, w.shape[1]), lambda i: (i, 0)) for w in ws],
        out_shape=[jax.ShapeDtypeStruct((n, w.shape[1]), out_dtype) for w in ws],
        compiler_params=_cparams(("parallel",)),
        name="norm_proj",
    )(x, gain.reshape(1, d), *ws)


def _log_forget(f, log_lb, log1m_lb):
    a1 = jnp.broadcast_to(log_lb, f.shape)
    b1 = log1m_lb - _softplus(-f)
    return jnp.maximum(a1, b1) + jnp.log(1.0 + jnp.exp(-jnp.abs(a1 - b1)))


def _gdn_stages(nb, pa_ref, pdb_ref, convw_ref, alog_ref, dtb_ref, onorm_ref, o_ref, xbuf_ref, state_ref):
    n_heads = state_ref.shape[0] // nb
    hd = A_HEAD_DIM
    width = n_heads * hd
    row = _iota2((CHUNK, CHUNK), 0)
    col = _iota2((CHUNK, CHUNK), 1)
    incl = row >= col
    strict = row > col
    tril = incl.astype(F32)
    triu = (row <= col).astype(F32)

    qs, ks, vs, betas, gs, g_lasts, decays = [], [], [], [], [], [], []
    for bb in range(nb):
        x = pa_ref[bb, :, 0:3 * width]
        xbuf_ref[bb, 8:8 + CHUNK, :] = x
        acc = convw_ref[0:1, :] * xbuf_ref[bb, 5:5 + CHUNK, :]
        for j in range(1, CONV_WIDTH):
            acc = acc + convw_ref[j:j + 1, :] * xbuf_ref[bb, 5 + j:5 + j + CHUNK, :]
        xbuf_ref[bb, 0:8, :] = x[CHUNK - 8:CHUNK, :]
        qkv = _silu(acc)
        pdb = pdb_ref[bb]
        log_alpha = -jnp.exp(alog_ref[...]) * _softplus(pdb + dtb_ref[...])
        beta_all = _sigmoid(pdb)
        g_all = _table_dot(tril, log_alpha)
        g_all_t = _table_dot_tn(log_alpha, triu)
        for h in range(n_heads):
            q = qkv[:, h * hd:(h + 1) * hd]
            k = qkv[:, width + h * hd: width + (h + 1) * hd]
            qs.append(q * lax.rsqrt(jnp.sum(q * q, axis=-1, keepdims=True) + 1e-6) * (hd ** -0.5))
            ks.append(k * lax.rsqrt(jnp.sum(k * k, axis=-1, keepdims=True) + 1e-6))
            vs.append(qkv[:, 2 * width + h * hd: 2 * width + (h + 1) * hd])
            betas.append(beta_all[:, n_heads + h:n_heads + h + 1])
            g = g_all[:, h:h + 1]
            gs.append(g)
            g_lasts.append(g_all[CHUNK - 1:CHUNK, h:h + 1])
            decays.append(jnp.where(incl, jnp.exp(jnp.where(incl, g - g_all_t[h:h + 1, :], 0.0)), 0.0))
    yield
    idx = range(nb * n_heads)
    k_betas = [ks[i] * betas[i] for i in idx]
    egs = [jnp.exp(g) for g in gs]
    a_lows = [jnp.where(strict, _dot_nt(k_betas[i], ks[i]) * decays[i], 0.0) for i in idx]
    qks = [_dot_nt(qs[i], ks[i]) * decays[i] for i in idx]
    states = [state_ref[i] for i in idx]
    o_inters = [_dot(qs[i] * egs[i], states[i]) for i in idx]
    yield
    t_invs = yield from _tri_inverse_stages(a_lows)
    us = [_dot(t_invs[i], vs[i] * betas[i]) for i in idx]
    ws = [_dot(t_invs[i], k_betas[i] * egs[i]) for i in idx]
    yield
    v_news = [us[i] - _dot(ws[i], states[i]) for i in idx]
    yield
    os_ = [o_inters[i] + _dot(qks[i], v_news[i]) for i in idx]
    for i in idx:
        state_ref[i] = (states[i] * jnp.exp(g_lasts[i])
                        + _dot_tn(ks[i] * jnp.exp(g_lasts[i] - gs[i]), v_news[i]))
    yield
    for i in idx:
        bb, h = divmod(i, n_heads)
        gate = pa_ref[bb, :, 3 * width + h * hd: 3 * width + (h + 1) * hd]
        o_ref[bb, :, h * hd:(h + 1) * hd] = (_rms(os_[i], onorm_ref[...]) * _silu(gate)).astype(o_ref.dtype)


def _rwkv_stages(nb, col0, pb_ref, mu_ref, w0_ref, w2_ref, a0_ref, a2_ref, g2_ref, kk_ref, ka_ref, rk_ref,
                 lnw_ref, lnb_ref, bd_ref, o_ref, carry_ref, state_ref, obuf_ref):
    n_heads = state_ref.shape[0] // nb
    hd = B_HEAD_DIM
    width = n_heads * hd
    row = _iota2((CHUNK, CHUNK), 0)
    col = _iota2((CHUNK, CHUNK), 1)
    strict = row > col
    tril = (row >= col).astype(F32)
    incl2 = _iota2((CHUNK, 2 * CHUNK), 0) >= (_iota2((CHUNK, 2 * CHUNK), 1) & (CHUNK - 1))
    bd = bd_ref[...]

    pss = []
    for bb in range(nb):
        p = pb_ref[bb]
        prev = jnp.where(_iota2(p.shape, 0) == 0, carry_ref[bb, 0:1, :], pltpu.roll(p, 1, 0))
        carry_ref[bb, 0:1, :] = p[CHUNK - 1:CHUNK, :]
        pss.append(p + mu_ref[...] * (prev - p))
    rs = [ps[:, 0:width] for ps in pss]
    kraw = [ps[:, width:2 * width] for ps in pss]
    vfull = [ps[:, 2 * width:3 * width] for ps in pss]
    w_raws = [w0_ref[...] + _dot(jnp.tanh(ps[:, 3 * width:3 * width + LANES]), w2_ref[...]) for ps in pss]
    iclrs = [_sigmoid(a0_ref[...] + _dot(ps[:, 3 * width + LANES:3 * width + 2 * LANES], a2_ref[...]))
             for ps in pss]
    gates = [_dot(_sigmoid(ps[:, 3 * width + 2 * LANES:3 * width + 3 * LANES]), g2_ref[...]) for ps in pss]
    kks = [k * kk_ref[...] for k in kraw]
    kk_ss = [_dot(kk * kk, bd) for kk in kks]
    yield
    log_ws = [-jnp.exp(-_softplus(-w_raw) - 0.5) for w_raw in w_raws]
    g_cums = [_table_dot(tril, log_w) for log_w in log_ws]
    yield
    lhss, rhss, bl_kls, p_cs, vhs, k2s = [], [], [], [], [], []
    for bb in range(nb):
        kk = kks[bb] * lax.rsqrt(kk_ss[bb] + 1e-6)
        k2 = kraw[bb] * (1.0 + (iclrs[bb] - 1.0) * ka_ref[...])
        k2s.append(k2)
        bvec = kk * iclrs[bb]
        g = g_cums[bb]
        g_last = g[CHUNK - 1:CHUNK, :]
        eng = jnp.exp(-g)
        egl = jnp.exp(g_last - g)
        r_t = rs[bb] * jnp.exp(g)
        kk_t = kk * jnp.exp(g - log_ws[bb])
        b_h, k_h = bvec * eng, k2 * eng
        b_l, k_l = bvec * egl, k2 * egl
        p_c = jnp.exp(g_last)
        for h in range(n_heads):
            sl = slice(h * hd, (h + 1) * hd)
            lhss.append(jnp.concatenate([kk_t[:, sl], r_t[:, sl]], axis=0))
            rhss.append(jnp.concatenate([b_h[:, sl], k_h[:, sl]], axis=0))
            bl_kls.append(jnp.concatenate([b_l[:, sl], k_l[:, sl]], axis=0))
            p_cs.append(p_c[:, sl])
            vhs.append(vfull[bb][:, sl])
    idx = range(nb * n_heads)
    ms = [_dot_nt(lhss[i], rhss[i]) for i in idx]
    states = [state_ref[i] for i in idx]
    lss = [_dot_nt(lhss[i], states[i]) for i in idx]
    yield
    a_bbs = [jnp.where(strict, m[0:CHUNK, 0:CHUNK], 0.0) for m in ms]
    a_bks = [jnp.where(strict, m[0:CHUNK, CHUNK:2 * CHUNK], 0.0) for m in ms]
    a_rs = [jnp.where(incl2, m[CHUNK:2 * CHUNK, :], 0.0) for m in ms]
    rhs_u = [lss[i][0:CHUNK] + _dot(a_bks[i], vhs[i]) for i in idx]
    t_invs = yield from _tri_inverse_stages(a_bbs)
    uvs = [jnp.concatenate([-_dot(t_invs[i], rhs_u[i]), vhs[i]], axis=0) for i in idx]
    yield
    for i in idx:
        bb, h = divmod(i, n_heads)
        obuf_ref[bb, :, h * hd:(h + 1) * hd] = lss[i][CHUNK:2 * CHUNK] + _dot(a_rs[i], uvs[i])
        state_ref[i] = states[i] * p_cs[i] + _dot_tn(uvs[i], bl_kls[i])
    yield
    inv_n = 1.0 / hd
    os_ = [obuf_ref[bb] for bb in range(nb)]
    means = [_dot(o, bd) * inv_n for o in os_]
    bonus = [_dot(rs[bb] * k2s[bb] * rk_ref[...], bd) * vfull[bb] for bb in range(nb)]
    yield
    cens = [o - mean for o, mean in zip(os_, means)]
    vars_ = [_dot(cen * cen, bd) * inv_n for cen in cens]
    yield
    for bb in range(nb):
        ln = cens[bb] * lax.rsqrt(vars_[bb] + GROUP_NORM_EPS) * lnw_ref[...] + lnb_ref[...]
        o_ref[bb, :, col0:col0 + width] = ((ln + bonus[bb]) * gates[bb]).astype(o_ref.dtype)


def _mixer_ab_kernel(n_gdn, n_rwkv, pa_ref, pdb_ref, pb_ref, *refs):
    gdn_params = refs[:n_gdn]
    rwkv_params = refs[n_gdn:n_gdn + n_rwkv]
    o_ref, xbuf_ref, gstate_ref, carry_ref, rstate_ref, obuf_ref = refs[n_gdn + n_rwkv:]
    nb = pa_ref.shape[0]

    @pl.when(pl.program_id(1) == 0)
    def _():
        xbuf_ref[:, 0:8, :] = jnp.zeros((nb, 8, xbuf_ref.shape[2]), F32)
        gstate_ref[...] = jnp.zeros_like(gstate_ref)
        carry_ref[...] = jnp.zeros_like(carry_ref)
        rstate_ref[...] = jnp.zeros_like(rstate_ref)

    a_width = pa_ref.shape[2] // 4
    _interleave([
        _gdn_stages(nb, pa_ref, pdb_ref, *gdn_params, o_ref, xbuf_ref, gstate_ref),
        _rwkv_stages(nb, a_width, pb_ref, *rwkv_params, o_ref, carry_ref, rstate_ref, obuf_ref),
    ])


def _mixer_ab(pa, pdb, pb, gdn_params, rwkv_params):
    b, t, a_cols = pa.shape
    b_cols = pb.shape[2]
    a_width = a_cols // 4
    b_width = (b_cols - 3 * LANES) // 3
    a_heads = a_width // A_HEAD_DIM
    b_heads = b_width // B_HEAD_DIM
    nb = MIXER_BATCH if b % MIXER_BATCH == 0 else 1
    params = list(gdn_params) + list(rwkv_params)
    return pl.pallas_call(
        functools.partial(_mixer_ab_kernel, len(gdn_params), len(rwkv_params)),
        grid=(b // nb, t // CHUNK),
        in_specs=[pl.BlockSpec((nb, CHUNK, a_cols), lambda i, c: (i, c, 0)),
                  pl.BlockSpec((nb, CHUNK, LANES), lambda i, c: (i, c, 0)),
                  pl.BlockSpec((nb, CHUNK, b_cols), lambda i, c: (i, c, 0))]
                 + [pl.BlockSpec(a.shape, lambda i, c: (0, 0)) for a in params],
        out_specs=pl.BlockSpec((nb, CHUNK, a_width + b_width), lambda i, c: (i, c, 0)),
        out_shape=jax.ShapeDtypeStruct((b, t, a_width + b_width), BF16),
        scratch_shapes=[pltpu.VMEM((nb, 8 + CHUNK, 3 * a_width), F32),
                        pltpu.VMEM((nb * a_heads, A_HEAD_DIM, A_HEAD_DIM), F32),
                        pltpu.VMEM((nb, 8, b_cols), F32),
                        pltpu.VMEM((nb * b_heads, B_HEAD_DIM, B_HEAD_DIM), F32),
                        pltpu.VMEM((nb, CHUNK, b_width), F32)],
        compiler_params=_cparams(("parallel", "arbitrary")),
        name="mixer_ab",
    )(pa, pdb, pb, *params)


def _hgrn_tables():
    i = np.arange(CHUNK)[:, None]
    t = np.arange(CHUNK)[None, :]
    blocks, pair = [], []
    for ls in range(HGRN_LEVELS - 1, -1, -1):
        s = 1 << ls
        mid = (i >> (ls + 1) << (ls + 1)) + s
        q_rows, k_rows = _hgrn_level_rows(ls)
        blocks.append(((i >= mid) & (t >= mid) & (t <= i))[q_rows])
        blocks.append(((i < mid) & (t > i) & (t <= mid - 1))[k_rows])
        pair.append((((i >> ls) ^ (t >> ls)) == 1) & (i > t))
    blocks.append(t <= i)
    blocks.append(t > i)
    sums = np.concatenate(blocks, axis=0).astype(np.float32)
    return np.concatenate([sums] * 3, axis=1), np.stack(pair).astype(np.float32)


def _hgrn_level_rows(ls):
    rows = np.arange(CHUNK)
    if (1 << ls) < SUBLANES:
        return rows, rows
    upper = ((rows >> ls) & 1) == 1
    return rows[upper], rows[~upper]


def _hgrn_kernel(p_ref, loglb_ref, log1mlb_ref, onorm_ref, sums_ref, pair_ref, o_ref, state_ref):
    nb = p_ref.shape[0]
    n_heads = state_ref.shape[0] // nb
    hd = C_KEY_DIM
    kw = n_heads * hd

    @pl.when(pl.program_id(1) == 0)
    def _():
        state_ref[...] = jnp.zeros_like(state_ref)

    def take_rows(a, rows):
        runs, start = [], 0
        for n in range(1, len(rows) + 1):
            if n == len(rows) or rows[n] != rows[n - 1] + 1:
                runs.append(a[int(rows[start]):int(rows[n - 1]) + 1])
                start = n
        return runs[0] if len(runs) == 1 else jnp.concatenate(runs, axis=0)

    def spread_rows(a, rows):
        if len(rows) == CHUNK:
            return a
        pieces, pos, taken = [], 0, 0
        while pos < CHUNK:
            run = 1
            inside = pos in rows
            while pos + run < CHUNK and ((pos + run) in rows) == inside:
                run += 1
            if inside:
                pieces.append(a[taken:taken + run])
                taken += run
            else:
                pieces.append(jnp.zeros((run, a.shape[1]), a.dtype))
            pos += run
        return jnp.concatenate(pieces, axis=0)

    qs, ks, exps = [], [], []
    for bb in range(nb):
        qs.append(_silu(p_ref[bb, :, 0:kw]))
        log_f = _log_forget(p_ref[bb, :, kw:2 * kw], loglb_ref[...], log1mlb_ref[...])
        ks.append(1.0 - jnp.exp(log_f))
        exps.append(_dot(sums_ref[...], jnp.concatenate(_split3(log_f), axis=0)))

    idx = [(bb, h) for bb in range(nb) for h in range(n_heads)]
    states = [state_ref[bb * n_heads + h] for bb, h in idx]
    scores = [None] * len(idx)
    offset = 0
    for lvl in range(HGRN_LEVELS):
        q_rows, k_rows = _hgrn_level_rows(HGRN_LEVELS - 1 - lvl)
        q_rows, k_rows = [int(r) for r in q_rows], [int(r) for r in k_rows]
        e_q = [e[offset:offset + len(q_rows)] for e in exps]
        e_k = [e[offset + len(q_rows):offset + len(q_rows) + len(k_rows)] for e in exps]
        offset += len(q_rows) + len(k_rows)
        q_l = [(take_rows(qs[bb], q_rows) * jnp.exp(e_q[bb])).astype(BF16) for bb in range(nb)]
        k_l = [spread_rows(take_rows(ks[bb], k_rows) * jnp.exp(e_k[bb]), k_rows).astype(BF16) for bb in range(nb)]
        mask = None if (lvl == 0 and len(q_rows) < CHUNK) else pair_ref[lvl]
        for n, (bb, h) in enumerate(idx):
            sl = slice(h * hd, (h + 1) * hd)
            part = spread_rows(_dot_nt(q_l[bb][:, sl], k_l[bb][:, sl]), q_rows)
            if mask is not None:
                part = part * mask
            scores[n] = part if scores[n] is None else scores[n] + part
    g_cum = [e[offset:offset + CHUNK] for e in exps]
    q_g = [qs[bb] * jnp.exp(g_cum[bb]) for bb in range(nb)]
    k_r = [ks[bb] * jnp.exp(exps[bb][offset + CHUNK:offset + 2 * CHUNK]) for bb in range(nb)]
    diag = [qs[bb] * ks[bb] for bb in range(nb)]
    outs = []
    for n, (bb, h) in enumerate(idx):
        sl = slice(h * hd, (h + 1) * hd)
        v = p_ref[bb, :, 2 * kw + h * hd: 2 * kw + (h + 1) * hd]
        o = _dot_nt(q_g[bb][:, sl], states[n]) + _dot(scores[n], v)
        outs.append(o + jnp.sum(diag[bb][:, sl], axis=-1, keepdims=True) * v)
        p_last = jnp.exp(g_cum[bb][CHUNK - 1:CHUNK, sl])
        state_ref[bb * n_heads + h] = states[n] * p_last + _dot_tn(v, k_r[bb][:, sl])
    for n, (bb, h) in enumerate(idx):
        gate = p_ref[bb, :, 3 * kw + h * hd: 3 * kw + (h + 1) * hd]
        o_ref[bb, :, h * hd:(h + 1) * hd] = (_rms(outs[n], onorm_ref[...]) * _silu(gate)).astype(o_ref.dtype)


def _hgrn(proj, log_lb, log1m_lb, onorm):
    b, t, cols = proj.shape
    kw = cols // 4
    n_heads = kw // C_KEY_DIM
    nb = MIXER_BATCH if b % MIXER_BATCH == 0 else 1
    sums, pair = _hgrn_tables()
    sums = jnp.asarray(sums, BF16)
    pair = jnp.asarray(pair, F32)
    return pl.pallas_call(
        _hgrn_kernel,
        grid=(b // nb, t // CHUNK),
        in_specs=[pl.BlockSpec((nb, CHUNK, cols), lambda i, c: (i, c, 0)),
                  pl.BlockSpec((1, kw), lambda i, c: (0, 0)),
                  pl.BlockSpec((1, kw), lambda i, c: (0, 0)),
                  pl.BlockSpec((1, C_KEY_DIM), lambda i, c: (0, 0)),
                  pl.BlockSpec(sums.shape, lambda i, c: (0, 0)),
                  pl.BlockSpec(pair.shape, lambda i, c: (0, 0, 0))],
        out_specs=pl.BlockSpec((nb, CHUNK, kw), lambda i, c: (i, c, 0)),
        out_shape=jax.ShapeDtypeStruct((b, t, kw), BF16),
        scratch_shapes=[pltpu.VMEM((nb * n_heads, C_KEY_DIM, C_KEY_DIM), F32)],
        compiler_params=_cparams(("parallel", "arbitrary")),
        name="hgrn2",
    )(proj, log_lb, log1m_lb, onorm, sums, pair)


def _route(logits):
    lane = _iota2(logits.shape, 1)
    neg = -jnp.inf
    gmask = lane < N_GROUPS
    gl = jnp.where(gmask, logits, neg)
    gmax = jnp.max(gl, axis=-1, keepdims=True)
    gsel = jnp.min(jnp.where(gl == gmax, lane, LANES), axis=-1, keepdims=True)
    group_gate = 1.0 / jnp.sum(jnp.where(gmask, jnp.exp(gl - gmax), 0.0), axis=-1, keepdims=True)
    lo = N_GROUPS + EXPERTS_PER_GROUP * gsel
    emask = (lane >= lo) & (lane < lo + EXPERTS_PER_GROUP)
    el = jnp.where(emask, logits, neg)
    ee = jnp.where(emask, jnp.exp(el - jnp.max(el, axis=-1, keepdims=True)), 0.0)
    prob = ee / jnp.sum(ee, axis=-1, keepdims=True)
    p1 = jnp.max(jnp.where(emask, prob, -1.0), axis=-1, keepdims=True)
    i1 = jnp.min(jnp.where(emask & (prob == p1), lane, LANES), axis=-1, keepdims=True)
    rest = emask & (lane != i1)
    p2 = jnp.max(jnp.where(rest, prob, -1.0), axis=-1, keepdims=True)
    i2 = jnp.min(jnp.where(rest & (prob == p2), lane, LANES), axis=-1, keepdims=True)
    scale = group_gate / (p1 + p2)
    comb = jnp.where(lane == i1, p1 * scale, 0.0) + jnp.where(lane == i2, p2 * scale, 0.0)
    return comb, gsel


def _sorted_rows(tile_rows):
    return tile_rows + N_GROUPS * SEG_ALIGN


def _xattn_route_kernel(x_ref, a_ref, wmix_ref, gain_ref, wq_ref, k_ref, v_ref, wo_ref, gffn_ref, wr_ref, br_ref,
                        x2_ref, xs_ref, cs_ref, info_ref, seg_ref):
    n_sub = xs_ref.shape[0]
    tq = x_ref.shape[0] // n_sub
    d = x_ref.shape[1]
    hd = d // XA_HEADS
    rows = xs_ref.shape[1]

    def tile_stages(u):
        tok = slice(u * tq, (u + 1) * tq)
        x = x_ref[tok, :] + _dot(a_ref[tok, :], wmix_ref[...])
        q = _dot(_rms(x, gain_ref[...]).astype(BF16), wq_ref[...]).astype(BF16)
        yield
        outs = []
        for h in range(XA_HEADS):
            sl = slice(h * hd, (h + 1) * hd)
            s = _dot_nt(q[:, sl], k_ref[:, sl]) * (hd ** -0.5)
            p = jnp.exp(s - jnp.max(s, axis=-1, keepdims=True))
            p = p / jnp.sum(p, axis=-1, keepdims=True)
            outs.append(_dot(p.astype(BF16), v_ref[:, sl]).astype(BF16))
        yield
        x2 = x + _dot(jnp.concatenate(outs, axis=-1), wo_ref[...])
        x2_ref[tok, :] = x2
        xn = _rms(x2, gffn_ref[...])
        xn_hi = xn.astype(BF16)
        xn_lo = (xn - xn_hi.astype(F32)).astype(BF16)
        hi_part = _dot(xn_hi, wr_ref[...])
        logits = hi_part[:, 0:LANES] + hi_part[:, LANES:2 * LANES] + _dot(xn_lo, wr_ref[:, 0:LANES])
        yield
        comb, gsel = _route(logits + br_ref[...])
        lane = _iota2((tq, LANES), 1)
        onehot = (lane == gsel).astype(F32)
        earlier = (_iota2((tq, tq), 0) > _iota2((tq, tq), 1)).astype(BF16)
        before = _dot(earlier, onehot.astype(BF16))
        yield
        rank = jnp.sum(onehot * before, axis=-1, keepdims=True)
        counts = before[tq - 1:tq, :] + onehot[tq - 1:tq, :]
        padded = jnp.floor((counts + (SEG_ALIGN - 1)) * (1.0 / SEG_ALIGN)) * SEG_ALIGN
        lane1 = _iota2((1, LANES), 1)
        starts = jnp.zeros((1, LANES), F32)
        for g in range(N_GROUPS - 1):
            starts = starts + jnp.where(lane1 > g, padded[:, g:g + 1], 0.0)
        dest = jnp.sum(onehot * starts, axis=-1, keepdims=True) + rank
        place = (_iota2((tq, rows), 1).astype(F32) == dest).astype(BF16)
        comb_hi = comb.astype(BF16)
        comb_lo = (comb - comb_hi.astype(F32)).astype(BF16)
        placed = _dot_tn(place, jnp.concatenate([xn_hi, comb_hi, comb_lo], axis=1))
        xs_ref[u] = placed[:, 0:d].astype(BF16)
        cs_ref[u] = placed[:, d:d + LANES] + placed[:, d + LANES:d + 2 * LANES]
        info_ref[tok, :] = jnp.where(lane == 0, dest, 0.0)
        seg_ref[u] = jnp.concatenate([starts, padded, jnp.zeros((6, LANES), F32)], axis=0)

    _interleave([tile_stages(u) for u in range(n_sub)])


def _xattn_route(x, act, w_mix, gain, wq, k, v, wo, gain_ffn, w_router, b_router, tq=512):
    b, t, d = x.shape
    m = k.shape[1]
    tq = min(tq, t)
    tiles_per_row = t // tq
    n_tiles = b * tiles_per_row
    rows = _sorted_rows(tq)
    n_sub = XATTN_TILES_PER_STEP if tiles_per_row % XATTN_TILES_PER_STEP == 0 else 1
    steps_per_row = tiles_per_row // n_sub

    def tile(i, j):
        return (i * steps_per_row + j, 0, 0)

    return pl.pallas_call(
        _xattn_route_kernel,
        grid=(b, steps_per_row),
        in_specs=[pl.BlockSpec((None, n_sub * tq, d), lambda i, j: (i, j, 0)),
                  pl.BlockSpec((None, n_sub * tq, act.shape[2]), lambda i, j: (i, j, 0)),
                  pl.BlockSpec(w_mix.shape, lambda i, j: (0, 0)),
                  pl.BlockSpec((1, d), lambda i, j: (0, 0)),
                  pl.BlockSpec((d, d), lambda i, j: (0, 0)),
                  pl.BlockSpec((None, m, d), lambda i, j: (i, 0, 0)),
                  pl.BlockSpec((None, m, d), lambda i, j: (i, 0, 0)),
                  pl.BlockSpec((d, d), lambda i, j: (0, 0)),
                  pl.BlockSpec((1, d), lambda i, j: (0, 0)),
                  pl.BlockSpec((d, 2 * LANES), lambda i, j: (0, 0)),
                  pl.BlockSpec((1, LANES), lambda i, j: (0, 0))],
        out_specs=[pl.BlockSpec((None, n_sub * tq, d), lambda i, j: (i, j, 0)),
                   pl.BlockSpec((n_sub, rows, d), tile),
                   pl.BlockSpec((n_sub, rows, LANES), tile),
                   pl.BlockSpec((None, n_sub * tq, LANES), lambda i, j: (i, j, 0)),
                   pl.BlockSpec((n_sub, 8, LANES), tile)],
        out_shape=[jax.ShapeDtypeStruct((b, t, d), F32),
                   jax.ShapeDtypeStruct((n_tiles, rows, d), BF16),
                   jax.ShapeDtypeStruct((n_tiles, rows, LANES), F32),
                   jax.ShapeDtypeStruct((b, t, LANES), F32),
                   jax.ShapeDtypeStruct((n_tiles, 8, LANES), F32)],
        compiler_params=_cparams(("parallel", "parallel")),
        name="xattn_route",
    )(x, act, w_mix, gain.reshape(1, d), wq, k, v, wo, gain_ffn.reshape(1, d), w_router, b_router)


def _moe_ffn_kernel(seg_ref, xs_ref, cs_ref, w1_ref, w3_ref, w2_ref, ys_ref, xbuf_ref, cbuf_ref, ybuf_ref):
    win = pl.program_id(0)
    grp = pl.program_id(1)
    n_win = xs_ref.shape[0]

    @pl.when((win == 0) & (grp == 0))
    def _():
        xbuf_ref[...] = jnp.zeros_like(xbuf_ref)
        cbuf_ref[...] = jnp.zeros_like(cbuf_ref)

    @pl.when(grp == 0)
    def _():
        ys_ref[...] = jnp.zeros_like(ys_ref)

    def segment(t):
        base = (win * n_win + t) * 8
        return seg_ref[base + grp], seg_ref[base + N_GROUPS + grp]

    def copy_segments(copy_piece):
        packed = jnp.int32(0)
        for t in range(n_win):
            start, size = segment(t)

            def body(j, carry, t=t, start=start, packed=packed):
                copy_piece(t, pl.multiple_of(start + j * SEG_ALIGN, SEG_ALIGN),
                           pl.multiple_of(packed + j * SEG_ALIGN, SEG_ALIGN))
                return carry

            lax.fori_loop(0, size // SEG_ALIGN, body, 0)
            packed = packed + size
        return packed

    def gather(t, src, dst):
        xbuf_ref[pl.ds(dst, SEG_ALIGN), :] = xs_ref[t, pl.ds(src, SEG_ALIGN), :]
        cbuf_ref[pl.ds(dst, SEG_ALIGN), :] = cs_ref[t, pl.ds(src, SEG_ALIGN), :]

    total = copy_segments(gather)

    def ffn_piece(r0, n_rows):
        xc = xbuf_ref[pl.ds(r0, n_rows), :]
        cb = cbuf_ref[pl.ds(r0, n_rows), :]
        lane = _iota2(cb.shape, 1)
        hids = []
        for e in range(EXPERTS_PER_GROUP):
            weight = jnp.sum(jnp.where(lane == N_GROUPS + EXPERTS_PER_GROUP * grp + e, cb, 0.0),
                             axis=-1, keepdims=True)
            hids.append((_silu(_dot(xc, w1_ref[e])) * _dot(xc, w3_ref[e]) * weight).astype(BF16))
        ybuf_ref[pl.ds(r0, n_rows), :] = _dot(jnp.concatenate(hids, axis=-1), w2_ref[...]).astype(ybuf_ref.dtype)

    def chunk(c, carry):
        ffn_piece(pl.multiple_of(c * MOE_CHUNK, MOE_CHUNK), MOE_CHUNK)
        return carry

    n_full = total // MOE_CHUNK
    lax.fori_loop(0, n_full, chunk, 0)
    rest = total - n_full * MOE_CHUNK
    rest_start = pl.multiple_of(n_full * MOE_CHUNK, MOE_CHUNK)
    lower = 0
    for n_rows in (MOE_CHUNK // 2, MOE_CHUNK):
        @pl.when((rest > lower) & (rest <= n_rows))
        def _(n_rows=n_rows):
            ffn_piece(rest_start, n_rows)
        lower = n_rows

    def scatter(t, dst, src):
        ys_ref[t, pl.ds(dst, SEG_ALIGN), :] = ybuf_ref[pl.ds(src, SEG_ALIGN), :]

    copy_segments(scatter)


def _moe_ffn(seg, xs, cs, w1, w3, w2g, layer):
    n_tiles, rows, d = xs.shape
    ff = w1.shape[3]
    n_win = next(w for w in (MOE_WINDOW, 2, 1) if n_tiles % w == 0)
    buf_rows = -(-(n_win * rows) // MOE_CHUNK) * MOE_CHUNK
    grid_spec = pltpu.PrefetchScalarGridSpec(
        num_scalar_prefetch=1,
        grid=(n_tiles // n_win, N_GROUPS),
        in_specs=[pl.BlockSpec((n_win, rows, d), lambda w, g, seg: (w, 0, 0)),
                  pl.BlockSpec((n_win, rows, LANES), lambda w, g, seg: (w, 0, 0)),
                  pl.BlockSpec((None, EXPERTS_PER_GROUP, d, ff), lambda w, g, seg: (layer, g, 0, 0)),
                  pl.BlockSpec((None, EXPERTS_PER_GROUP, d, ff), lambda w, g, seg: (layer, g, 0, 0)),
                  pl.BlockSpec((None, None, EXPERTS_PER_GROUP * ff, d), lambda w, g, seg: (layer, g, 0, 0))],
        out_specs=pl.BlockSpec((n_win, rows, d), lambda w, g, seg: (w, 0, 0)),
        scratch_shapes=[pltpu.VMEM((buf_rows, d), BF16),
                        pltpu.VMEM((buf_rows, LANES), F32),
                        pltpu.VMEM((buf_rows, d), BF16)])
    return pl.pallas_call(
        _moe_ffn_kernel,
        grid_spec=grid_spec,
        out_shape=jax.ShapeDtypeStruct((n_tiles, rows, d), BF16),
        compiler_params=_cparams(("arbitrary", "arbitrary")),
        name="moe_ffn",
    )(seg, xs, cs, w1, w3, w2g)


def _unsorted_sum(x_ref, info_ref, ys_ref):
    info = info_ref[...]
    dest = jnp.sum(jnp.where(_iota2(info.shape, 1) == 0, info, 0.0), axis=-1, keepdims=True)
    rows = ys_ref.shape[0]
    place = (_iota2((info.shape[0], rows), 1).astype(F32) == dest).astype(BF16)
    return x_ref[...] + _dot(place, ys_ref[...])


def _moe_unsort_kernel(n_w, x_ref, info_ref, ys_ref, gain_ref, *refs):
    w_refs, o_refs = refs[:n_w], refs[n_w:]
    if n_w == 0:
        tq = x_ref.shape[0] // ys_ref.shape[0]
        for u in range(ys_ref.shape[0]):
            tok = slice(u * tq, (u + 1) * tq)
            y = _unsorted_sum(x_ref.at[tok, :], info_ref.at[tok, :], ys_ref.at[u])
            o_refs[0][tok, :] = _rms(y, gain_ref[...])
        return
    y = _unsorted_sum(x_ref, info_ref, ys_ref.at[0])
    yn = _rms(y, gain_ref[...])
    o_refs[0][...] = y
    yn = yn.astype(BF16)
    for w_ref, o_ref in zip(w_refs, o_refs[1:]):
        n = w_ref.shape[1]
        for c0 in range(0, n, 512):
            c1 = min(n, c0 + 512)
            o_ref[:, c0:c1] = _dot(yn, w_ref[:, c0:c1])


def _moe_unsort(x, info, ys, gain, ws=()):
    b, t, d = x.shape
    n_tiles, rows, _ = ys.shape
    tiles_per_row = n_tiles // b
    tq = t // tiles_per_row
    n_sub = 2 if (not ws and tiles_per_row % 2 == 0) else 1
    steps_per_row = tiles_per_row // n_sub

    def token_block(cols):
        return pl.BlockSpec((None, n_sub * tq, cols), lambda i, j: (i, j, 0))

    out_cols = [d] + [w.shape[1] for w in ws]
    outs = pl.pallas_call(
        functools.partial(_moe_unsort_kernel, len(ws)),
        grid=(b, steps_per_row),
        in_specs=[token_block(d), token_block(LANES),
                  pl.BlockSpec((n_sub, rows, d), lambda i, j: (i * steps_per_row + j, 0, 0)),
                  pl.BlockSpec((1, d), lambda i, j: (0, 0))]
                 + [pl.BlockSpec(w.shape, lambda i, j: (0, 0)) for w in ws],
        out_specs=[token_block(c) for c in out_cols],
        out_shape=[jax.ShapeDtypeStruct((b, t, c), F32) for c in out_cols],
        compiler_params=_cparams(("parallel", "parallel")),
        name="moe_unsort",
    )(x, info, ys, gain.reshape(1, d), *ws)
    return outs


def _pad_cols(w, n):
    return jnp.pad(w, ((0, 0), (0, n - w.shape[1])))


def _pad_rows(w, n):
    return jnp.pad(w, ((0, n - w.shape[0]), (0, 0)))


def _row(v, n=None):
    v = v.reshape(1, -1).astype(F32)
    return v if n is None else _pad_cols(v, n)


def kernel(x, mem, norm_mix, norm_xattn, norm_mem, norm_ffn, norm_final, ab_w_in, ab_conv, ab_a_log, ab_dt_bias, ab_onorm, ab_shift_mu, ab_w0, ab_w2, ab_a0, ab_a2, ab_g2, ab_k_k, ab_k_a, ab_r_k, ab_ln_w, ab_ln_b, ab_w_out, c_w_in, c_lb_param, c_onorm, c_w_out, xa_wq, xa_wkv, xa_wo, moe_w_group, moe_b_group, moe_w_expert, moe_b_expert, moe_w1, moe_w3, moe_w2):
    bsz, seq, d = x.shape
    n_mem = mem.shape[1]
    depth = norm_mix.shape[0]
    a_width = d // 2
    b_width = d // 2
    a_heads = a_width // A_HEAD_DIM
    a_cols = 4 * a_width + 2 * a_heads
    n_tok = bsz * seq

    lb_p = jax.nn.softmax(c_lb_param.astype(F32), axis=0)
    lower_bounds = jnp.cumsum(lb_p, axis=0) - lb_p[0]

    def project(x_cur, moe_pending, gain, ws):
        if moe_pending is None:
            projs = _norm_proj(x_cur.reshape(n_tok, d), gain, ws)
            return x_cur, [p.reshape(bsz, seq, -1) for p in projs]
        x_new, *projs = _moe_unsort(*moe_pending, gain, ws)
        return x_new, projs

    ff = moe_w2.shape[2]
    moe_w1_bf = moe_w1.astype(BF16)
    moe_w3_bf = moe_w3.astype(BF16)
    moe_w2_bf = moe_w2.astype(BF16).reshape(depth, N_GROUPS, EXPERTS_PER_GROUP * ff, d)

    x2 = x
    moe_pending = None
    mem2 = mem.reshape(bsz * n_mem, d)
    for layer in range(depth):
        if layer % 2 == 0:
            e = layer // 2
            w_in = ab_w_in[e]
            w_a = w_in[:, :4 * a_width].astype(BF16)
            w_db = _pad_cols(w_in[:, 4 * a_width:a_cols], LANES).astype(BF16)
            wb = w_in[:, a_cols:]
            o1, o2 = 3 * b_width, 3 * b_width + B_DECAY_LORA
            o3 = o2 + B_ICLR_LORA
            w_b = jnp.concatenate([wb[:, :o1], _pad_cols(wb[:, o1:o2], LANES), _pad_cols(wb[:, o2:o3], LANES),
                                   wb[:, o3:]], axis=1).astype(BF16)
            mu = ab_shift_mu[e].reshape(1, -1)
            mu_p = jnp.concatenate([mu[:, :o1], _pad_cols(mu[:, o1:o2], LANES), _pad_cols(mu[:, o2:o3], LANES),
                                    mu[:, o3:]], axis=1)
            head_id = jnp.arange(b_width) // B_HEAD_DIM
            bd = (head_id[:, None] == head_id[None, :]).astype(F32)
            gdn_params = [ab_conv[e], _row(ab_a_log[e], LANES), _row(ab_dt_bias[e], LANES), _row(ab_onorm[e])]
            rwkv_params = [mu_p, _row(ab_w0[e]), _pad_rows(ab_w2[e], LANES), _row(ab_a0[e]),
                           _pad_rows(ab_a2[e], LANES), ab_g2[e], _row(ab_k_k[e]), _row(ab_k_a[e]),
                           _row(ab_r_k[e]), _row(ab_ln_w[e]), _row(ab_ln_b[e]), bd]
            x2, (pa, pdb, pb) = project(x2, moe_pending, norm_mix[layer], [w_a, w_db, w_b])
            mix = _mixer_ab(pa, pdb, pb, gdn_params, rwkv_params)
            w_mix = ab_w_out[e].astype(BF16)
        else:
            o = layer // 2
            x2, (proj,) = project(x2, moe_pending, norm_mix[layer], [c_w_in[o].astype(BF16)])
            lb = lower_bounds[layer].reshape(1, -1)
            mix = _hgrn(proj, jnp.log(lb), jnp.log1p(-lb), _row(c_onorm[o]))
            w_mix = c_w_out[o].astype(BF16)

        wkv = xa_wkv[layer].astype(BF16)
        k_mem, v_mem = _norm_proj(mem2, norm_mem[layer], [wkv[:, :d], wkv[:, d:]], out_dtype=BF16)
        w_r = _pad_cols(jnp.concatenate([moe_w_group[layer], moe_w_expert[layer]], axis=1), LANES).astype(F32)
        w_r_hi = w_r.astype(BF16)
        w_r_lo = (w_r - w_r_hi.astype(F32)).astype(BF16)
        b_router = _row(jnp.concatenate([moe_b_group[layer], moe_b_expert[layer]]), LANES)
        x3, xs, cs, info, seg = _xattn_route(
            x2, mix, w_mix, norm_xattn[layer], xa_wq[layer].astype(BF16),
            k_mem.reshape(bsz, n_mem, d), v_mem.reshape(bsz, n_mem, d), xa_wo[layer].astype(BF16),
            norm_ffn[layer], jnp.concatenate([w_r_hi, w_r_lo], axis=1), b_router)
        seg_i = jnp.concatenate([seg[:, 0, :N_GROUPS], seg[:, 1, :N_GROUPS]], axis=1).astype(jnp.int32).reshape(-1)
        ys = _moe_ffn(seg_i, xs, cs, moe_w1_bf, moe_w3_bf, moe_w2_bf, layer)
        moe_pending = (x3, info, ys)
    (out,) = _moe_unsort(*moe_pending, norm_final)
    return out
```

```python
import functools

import jax
import jax.numpy as jnp
import numpy as np
from jax import lax
from jax.experimental import pallas as pl
from jax.experimental.pallas import tpu as pltpu

F32 = jnp.float32
BF16 = jnp.bfloat16

CHUNK = 64
HGRN_LEVELS = CHUNK.bit_length() - 1
NORM_EPS = 1e-6
GROUP_NORM_EPS = 64e-5
CONV_WIDTH = 4

A_HEAD_DIM = 128
B_HEAD_DIM = 64
B_DECAY_LORA = 64
B_ICLR_LORA = 64
B_GATE_LORA = 128
C_KEY_DIM = 128
XA_HEADS = 4
N_GROUPS = 4
EXPERTS_PER_GROUP = 4
N_EXPERTS = N_GROUPS * EXPERTS_PER_GROUP

LANES = 128
SUBLANES = 8
VMEM_LIMIT = 56 * 1024 * 1024
MIXER_BATCH = 4
SEG_ALIGN = 16
MOE_WINDOW = 4
MOE_CHUNK = 288
XATTN_TILES_PER_STEP = 2


def _cparams(sem):
    return pltpu.CompilerParams(dimension_semantics=sem, vmem_limit_bytes=VMEM_LIMIT)


def _dot_dims(a, b, dims):
    return lax.dot_general(a.astype(BF16), b.astype(BF16), (dims, ((), ())), preferred_element_type=F32)


def _dot(a, b):
    return _dot_dims(a, b, ((1,), (0,)))


def _dot_nt(a, b):
    return _dot_dims(a, b, ((1,), (1,)))


def _dot_tn(a, b):
    return _dot_dims(a, b, ((0,), (0,)))


def _rms(x, gain):
    return x * lax.rsqrt(jnp.mean(x * x, axis=-1, keepdims=True) + NORM_EPS) * gain


def _sigmoid(x):
    return 1.0 / (1.0 + jnp.exp(-x))


def _silu(x):
    return x * _sigmoid(x)


def _softplus(x):
    return jnp.maximum(x, 0.0) + jnp.log(1.0 + jnp.exp(-jnp.abs(x)))


def _iota2(shape, dim):
    return lax.broadcasted_iota(jnp.int32, shape, dim)


def _split3(v):
    v1 = v.astype(BF16)
    v2 = (v - v1.astype(F32)).astype(BF16)
    v3 = (v - v1.astype(F32) - v2.astype(F32)).astype(BF16)
    return v1, v2, v3


def _table_dot(table, v):
    return _dot(jnp.concatenate([table.astype(BF16)] * 3, axis=1), jnp.concatenate(_split3(v), axis=0))


def _table_dot_tn(v, table):
    return _dot_tn(jnp.concatenate(_split3(v), axis=0), jnp.concatenate([table.astype(BF16)] * 3, axis=0))


def _interleave(stage_generators):
    live = list(stage_generators)
    while live:
        for gen in list(live):
            try:
                next(gen)
            except StopIteration:
                live.remove(gen)


def _tri_inverse_stages(mats):
    n = mats[0].shape[0]
    row = _iota2((n, n), 0)
    col = _iota2((n, n), 1)
    eye = (row == col).astype(F32)
    xs = [eye - jnp.where((row >> 1) == (col >> 1), a, 0.0) for a in mats]
    for level in range(1, n.bit_length() - 1):
        joins = ((row >> (level + 1)) == (col >> (level + 1))) & ((row >> level) != (col >> level))
        ts = [_dot(jnp.where(joins, a, 0.0), x) for a, x in zip(mats, xs)]
        yield
        xs = [x - _dot(x, t) for x, t in zip(xs, ts)]
        yield
    return xs


def _norm_proj_kernel(n_w, x_ref, g_ref, *refs):
    w_refs, o_refs = refs[:n_w], refs[n_w:]
    xn = _rms(x_ref[...], g_ref[...]).astype(BF16)
    for w_ref, o_ref in zip(w_refs, o_refs):
        n = w_ref.shape[1]
        for c0 in range(0, n, 512):
            c1 = min(n, c0 + 512)
            o_ref[:, c0:c1] = _dot(xn, w_ref[:, c0:c1]).astype(o_ref.dtype)


def _norm_proj(x, gain, ws, out_dtype=F32, tm=512):
    n, d = x.shape
    tm = min(tm, n)
    return pl.pallas_call(
        functools.partial(_norm_proj_kernel, len(ws)),
        grid=(n // tm,),
        in_specs=[pl.BlockSpec((tm, d), lambda i: (i, 0)),
                  pl.BlockSpec((1, d), lambda i: (0, 0))]
                 + [pl.BlockSpec(w.shape, lambda i: (0, 0)) for w in ws],
        out_specs=[pl.BlockSpec((tm, w.shape[1]), lambda i: (i, 0)) for w in ws],
        out_shape=[jax.ShapeDtypeStruct((n, w.shape[1]), out_dtype) for w in ws],
        compiler_params=_cparams(("parallel",)),
        name="norm_proj",
    )(x, gain.reshape(1, d), *ws)


def _log_forget(f, log_lb, log1m_lb):
    a1 = jnp.broadcast_to(log_lb, f.shape)
    b1 = log1m_lb - _softplus(-f)
    return jnp.maximum(a1, b1) + jnp.log(1.0 + jnp.exp(-jnp.abs(a1 - b1)))


def _gdn_stages(nb, pa_ref, pdb_ref, convw_ref, alog_ref, dtb_ref, onorm_ref, o_ref, xbuf_ref, state_ref):
    n_heads = state_ref.shape[0] // nb
    hd = A_HEAD_DIM
    width = n_heads * hd
    row = _iota2((CHUNK, CHUNK), 0)
    col = _iota2((CHUNK, CHUNK), 1)
    incl = row >= col
    strict = row > col
    tril = incl.astype(F32)
    triu = (row <= col).astype(F32)

    qs, ks, vs, betas, gs, g_lasts, decays = [], [], [], [], [], [], []
    for bb in range(nb):
        x = pa_ref[bb, :, 0:3 * width]
        xbuf_ref[bb, 8:8 + CHUNK, :] = x
        acc = convw_ref[0:1, :] * xbuf_ref[bb, 5:5 + CHUNK, :]
        for j in range(1, CONV_WIDTH):
            acc = acc + convw_ref[j:j + 1, :] * xbuf_ref[bb, 5 + j:5 + j + CHUNK, :]
        xbuf_ref[bb, 0:8, :] = x[CHUNK - 8:CHUNK, :]
        qkv = _silu(acc)
        pdb = pdb_ref[bb]
        log_alpha = -jnp.exp(alog_ref[...]) * _softplus(pdb + dtb_ref[...])
        beta_all = _sigmoid(pdb)
        g_all = _table_dot(tril, log_alpha)
        g_all_t = _table_dot_tn(log_alpha, triu)
        for h in range(n_heads):
            q = qkv[:, h * hd:(h + 1) * hd]
            k = qkv[:, width + h * hd: width + (h + 1) * hd]
            qs.append(q * lax.rsqrt(jnp.sum(q * q, axis=-1, keepdims=True) + 1e-6) * (hd ** -0.5))
            ks.append(k * lax.rsqrt(jnp.sum(k * k, axis=-1, keepdims=True) + 1e-6))
            vs.append(qkv[:, 2 * width + h * hd: 2 * width + (h + 1) * hd])
            betas.append(beta_all[:, n_heads + h:n_heads + h + 1])
            g = g_all[:, h:h + 1]
            gs.append(g)
            g_lasts.append(g_all[CHUNK - 1:CHUNK, h:h + 1])
            decays.append(jnp.where(incl, jnp.exp(jnp.where(incl, g - g_all_t[h:h + 1, :], 0.0)), 0.0))
    yield
    idx = range(nb * n_heads)
    k_betas = [ks[i] * betas[i] for i in idx]
    egs = [jnp.exp(g) for g in gs]
    a_lows = [jnp.where(strict, _dot_nt(k_betas[i], ks[i]) * decays[i], 0.0) for i in idx]
    qks = [_dot_nt(qs[i], ks[i]) * decays[i] for i in idx]
    states = [state_ref[i] for i in idx]
    o_inters = [_dot(qs[i] * egs[i], states[i]) for i in idx]
    yield
    t_invs = yield from _tri_inverse_stages(a_lows)
    us = [_dot(t_invs[i], vs[i] * betas[i]) for i in idx]
    ws = [_dot(t_invs[i], k_betas[i] * egs[i]) for i in idx]
    yield
    v_news = [us[i] - _dot(ws[i], states[i]) for i in idx]
    yield
    os_ = [o_inters[i] + _dot(qks[i], v_news[i]) for i in idx]
    for i in idx:
        state_ref[i] = (states[i] * jnp.exp(g_lasts[i])
                        + _dot_tn(ks[i] * jnp.exp(g_lasts[i] - gs[i]), v_news[i]))
    yield
    for i in idx:
        bb, h = divmod(i, n_heads)
        gate = pa_ref[bb, :, 3 * width + h * hd: 3 * width + (h + 1) * hd]
        o_ref[bb, :, h * hd:(h + 1) * hd] = (_rms(os_[i], onorm_ref[...]) * _silu(gate)).astype(o_ref.dtype)


def _rwkv_stages(nb, col0, pb_ref, mu_ref, w0_ref, w2_ref, a0_ref, a2_ref, g2_ref, kk_ref, ka_ref, rk_ref,
                 lnw_ref, lnb_ref, bd_ref, o_ref, carry_ref, state_ref, obuf_ref):
    n_heads = state_ref.shape[0] // nb
    hd = B_HEAD_DIM
    width = n_heads * hd
    row = _iota2((CHUNK, CHUNK), 0)
    col = _iota2((CHUNK, CHUNK), 1)
    strict = row > col
    tril = (row >= col).astype(F32)
    incl2 = _iota2((CHUNK, 2 * CHUNK), 0) >= (_iota2((CHUNK, 2 * CHUNK), 1) & (CHUNK - 1))
    bd = bd_ref[...]

    pss = []
    for bb in range(nb):
        p = pb_ref[bb]
        prev = jnp.where(_iota2(p.shape, 0) == 0, carry_ref[bb, 0:1, :], pltpu.roll(p, 1, 0))
        carry_ref[bb, 0:1, :] = p[CHUNK - 1:CHUNK, :]
        pss.append(p + mu_ref[...] * (prev - p))
    rs = [ps[:, 0:width] for ps in pss]
    kraw = [ps[:, width:2 * width] for ps in pss]
    vfull = [ps[:, 2 * width:3 * width] for ps in pss]
    w_raws = [w0_ref[...] + _dot(jnp.tanh(ps[:, 3 * width:3 * width + LANES]), w2_ref[...]) for ps in pss]
    iclrs = [_sigmoid(a0_ref[...] + _dot(ps[:, 3 * width + LANES:3 * width + 2 * LANES], a2_ref[...]))
             for ps in pss]
    gates = [_dot(_sigmoid(ps[:, 3 * width + 2 * LANES:3 * width + 3 * LANES]), g2_ref[...]) for ps in pss]
    kks = [k * kk_ref[...] for k in kraw]
    kk_ss = [_dot(kk * kk, bd) for kk in kks]
    yield
    log_ws = [-jnp.exp(-_softplus(-w_raw) - 0.5) for w_raw in w_raws]
    g_cums = [_table_dot(tril, log_w) for log_w in log_ws]
    yield
    lhss, rhss, bl_kls, p_cs, vhs, k2s = [], [], [], [], [], []
    for bb in range(nb):
        kk = kks[bb] * lax.rsqrt(kk_ss[bb] + 1e-6)
        k2 = kraw[bb] * (1.0 + (iclrs[bb] - 1.0) * ka_ref[...])
        k2s.append(k2)
        bvec = kk * iclrs[bb]
        g = g_cums[bb]
        g_last = g[CHUNK - 1:CHUNK, :]
        eng = jnp.exp(-g)
        egl = jnp.exp(g_last - g)
        r_t = rs[bb] * jnp.exp(g)
        kk_t = kk * jnp.exp(g - log_ws[bb])
        b_h, k_h = bvec * eng, k2 * eng
        b_l, k_l = bvec * egl, k2 * egl
        p_c = jnp.exp(g_last)
        for h in range(n_heads):
            sl = slice(h * hd, (h + 1) * hd)
            lhss.append(jnp.concatenate([kk_t[:, sl], r_t[:, sl]], axis=0))
            rhss.append(jnp.concatenate([b_h[:, sl], k_h[:, sl]], axis=0))
            bl_kls.append(jnp.concatenate([b_l[:, sl], k_l[:, sl]], axis=0))
            p_cs.append(p_c[:, sl])
            vhs.append(vfull[bb][:, sl])
    idx = range(nb * n_heads)
    ms = [_dot_nt(lhss[i], rhss[i]) for i in idx]
    states = [state_ref[i] for i in idx]
    lss = [_dot_nt(lhss[i], states[i]) for i in idx]
    yield
    a_bbs = [jnp.where(strict, m[0:CHUNK, 0:CHUNK], 0.0) for m in ms]
    a_bks = [jnp.where(strict, m[0:CHUNK, CHUNK:2 * CHUNK], 0.0) for m in ms]
    a_rs = [jnp.where(incl2, m[CHUNK:2 * CHUNK, :], 0.0) for m in ms]
    rhs_u = [lss[i][0:CHUNK] + _dot(a_bks[i], vhs[i]) for i in idx]
    t_invs = yield from _tri_inverse_stages(a_bbs)
    uvs = [jnp.concatenate([-_dot(t_invs[i], rhs_u[i]), vhs[i]], axis=0) for i in idx]
    yield
    for i in idx:
        bb, h = divmod(i, n_heads)
        obuf_ref[bb, :, h * hd:(h + 1) * hd] = lss[i][CHUNK:2 * CHUNK] + _dot(a_rs[i], uvs[i])
        state_ref[i] = states[i] * p_cs[i] + _dot_tn(uvs[i], bl_kls[i])
    yield
    inv_n = 1.0 / hd
    os_ = [obuf_ref[bb] for bb in range(nb)]
    means = [_dot(o, bd) * inv_n for o in os_]
    bonus = [_dot(rs[bb] * k2s[bb] * rk_ref[...], bd) * vfull[bb] for bb in range(nb)]
    yield
    cens = [o - mean for o, mean in zip(os_, means)]
    vars_ = [_dot(cen * cen, bd) * inv_n for cen in cens]
    yield
    for bb in range(nb):
        ln = cens[bb] * lax.rsqrt(vars_[bb] + GROUP_NORM_EPS) * lnw_ref[...] + lnb_ref[...]
        o_ref[bb, :, col0:col0 + width] = ((ln + bonus[bb]) * gates[bb]).astype(o_ref.dtype)


def _mixer_ab_kernel(n_gdn, n_rwkv, pa_ref, pdb_ref, pb_ref, *refs):
    gdn_params = refs[:n_gdn]
    rwkv_params = refs[n_gdn:n_gdn + n_rwkv]
    o_ref, xbuf_ref, gstate_ref, carry_ref, rstate_ref, obuf_ref = refs[n_gdn + n_rwkv:]
    nb = pa_ref.shape[0]

    @pl.when(pl.program_id(1) == 0)
    def _():
        xbuf_ref[:, 0:8, :] = jnp.zeros((nb, 8, xbuf_ref.shape[2]), F32)
        gstate_ref[...] = jnp.zeros_like(gstate_ref)
        carry_ref[...] = jnp.zeros_like(carry_ref)
        rstate_ref[...] = jnp.zeros_like(rstate_ref)

    a_width = pa_ref.shape[2] // 4
    _interleave([
        _gdn_stages(nb, pa_ref, pdb_ref, *gdn_params, o_ref, xbuf_ref, gstate_ref),
        _rwkv_stages(nb, a_width, pb_ref, *rwkv_params, o_ref, carry_ref, rstate_ref, obuf_ref),
    ])


def _mixer_ab(pa, pdb, pb, gdn_params, rwkv_params):
    b, t, a_cols = pa.shape
    b_cols = pb.shape[2]
    a_width = a_cols // 4
    b_width = (b_cols - 3 * LANES) // 3
    a_heads = a_width // A_HEAD_DIM
    b_heads = b_width // B_HEAD_DIM
    nb = MIXER_BATCH if b % MIXER_BATCH == 0 else 1
    params = list(gdn_params) + list(rwkv_params)
    return pl.pallas_call(
        functools.partial(_mixer_ab_kernel, len(gdn_params), len(rwkv_params)),
        grid=(b // nb, t // CHUNK),
        in_specs=[pl.BlockSpec((nb, CHUNK, a_cols), lambda i, c: (i, c, 0)),
                  pl.BlockSpec((nb, CHUNK, LANES), lambda i, c: (i, c, 0)),
                  pl.BlockSpec((nb, CHUNK, b_cols), lambda i, c: (i, c, 0))]
                 + [pl.BlockSpec(a.shape, lambda i, c: (0, 0)) for a in params],
        out_specs=pl.BlockSpec((nb, CHUNK, a_width + b_width), lambda i, c: (i, c, 0)),
        out_shape=jax.ShapeDtypeStruct((b, t, a_width + b_width), BF16),
        scratch_shapes=[pltpu.VMEM((nb, 8 + CHUNK, 3 * a_width), F32),
                        pltpu.VMEM((nb * a_heads, A_HEAD_DIM, A_HEAD_DIM), F32),
                        pltpu.VMEM((nb, 8, b_cols), F32),
                        pltpu.VMEM((nb * b_heads, B_HEAD_DIM, B_HEAD_DIM), F32),
                        pltpu.VMEM((nb, CHUNK, b_width), F32)],
        compiler_params=_cparams(("parallel", "arbitrary")),
        name="mixer_ab",
    )(pa, pdb, pb, *params)


def _hgrn_tables():
    i = np.arange(CHUNK)[:, None]
    t = np.arange(CHUNK)[None, :]
    blocks, pair = [], []
    for ls in range(HGRN_LEVELS - 1, -1, -1):
        s = 1 << ls
        mid = (i >> (ls + 1) << (ls + 1)) + s
        q_rows, k_rows = _hgrn_level_rows(ls)
        blocks.append(((i >= mid) & (t >= mid) & (t <= i))[q_rows])
        blocks.append(((i < mid) & (t > i) & (t <= mid - 1))[k_rows])
        pair.append((((i >> ls) ^ (t >> ls)) == 1) & (i > t))
    blocks.append(t <= i)
    blocks.append(t > i)
    sums = np.concatenate(blocks, axis=0).astype(np.float32)
    return np.concatenate([sums] * 3, axis=1), np.stack(pair).astype(np.float32)


def _hgrn_level_rows(ls):
    rows = np.arange(CHUNK)
    if (1 << ls) < SUBLANES:
        return rows, rows
    upper = ((rows >> ls) & 1) == 1
    return rows[upper], rows[~upper]


def _hgrn_kernel(p_ref, loglb_ref, log1mlb_ref, onorm_ref, sums_ref, pair_ref, o_ref, state_ref):
    nb = p_ref.shape[0]
    n_heads = state_ref.shape[0] // nb
    hd = C_KEY_DIM
    kw = n_heads * hd

    @pl.when(pl.program_id(1) == 0)
    def _():
        state_ref[...] = jnp.zeros_like(state_ref)

    def take_rows(a, rows):
        runs, start = [], 0
        for n in range(1, len(rows) + 1):
            if n == len(rows) or rows[n] != rows[n - 1] + 1:
                runs.append(a[int(rows[start]):int(rows[n - 1]) + 1])
                start = n
        return runs[0] if len(runs) == 1 else jnp.concatenate(runs, axis=0)

    def spread_rows(a, rows):
        if len(rows) == CHUNK:
            return a
        pieces, pos, taken = [], 0, 0
        while pos < CHUNK:
            run = 1
            inside = pos in rows
            while pos + run < CHUNK and ((pos + run) in rows) == inside:
                run += 1
            if inside:
                pieces.append(a[taken:taken + run])
                taken += run
            else:
                pieces.append(jnp.zeros((run, a.shape[1]), a.dtype))
            pos += run
        return jnp.concatenate(pieces, axis=0)

    qs, ks, exps = [], [], []
    for bb in range(nb):
        qs.append(_silu(p_ref[bb, :, 0:kw]))
        log_f = _log_forget(p_ref[bb, :, kw:2 * kw], loglb_ref[...], log1mlb_ref[...])
        ks.append(1.0 - jnp.exp(log_f))
        exps.append(_dot(sums_ref[...], jnp.concatenate(_split3(log_f), axis=0)))

    idx = [(bb, h) for bb in range(nb) for h in range(n_heads)]
    states = [state_ref[bb * n_heads + h] for bb, h in idx]
    scores = [None] * len(idx)
    offset = 0
    for lvl in range(HGRN_LEVELS):
        q_rows, k_rows = _hgrn_level_rows(HGRN_LEVELS - 1 - lvl)
        q_rows, k_rows = [int(r) for r in q_rows], [int(r) for r in k_rows]
        e_q = [e[offset:offset + len(q_rows)] for e in exps]
        e_k = [e[offset + len(q_rows):offset + len(q_rows) + len(k_rows)] for e in exps]
        offset += len(q_rows) + len(k_rows)
        q_l = [(take_rows(qs[bb], q_rows) * jnp.exp(e_q[bb])).astype(BF16) for bb in range(nb)]
        k_l = [spread_rows(take_rows(ks[bb], k_rows) * jnp.exp(e_k[bb]), k_rows).astype(BF16) for bb in range(nb)]
        mask = None if (lvl == 0 and len(q_rows) < CHUNK) else pair_ref[lvl]
        for n, (bb, h) in enumerate(idx):
            sl = slice(h * hd, (h + 1) * hd)
            part = spread_rows(_dot_nt(q_l[bb][:, sl], k_l[bb][:, sl]), q_rows)
            if mask is not None:
                part = part * mask
            scores[n] = part if scores[n] is None else scores[n] + part
    g_cum = [e[offset:offset + CHUNK] for e in exps]
    q_g = [qs[bb] * jnp.exp(g_cum[bb]) for bb in range(nb)]
    k_r = [ks[bb] * jnp.exp(exps[bb][offset + CHUNK:offset + 2 * CHUNK]) for bb in range(nb)]
    diag = [qs[bb] * ks[bb] for bb in range(nb)]
    outs = []
    for n, (bb, h) in enumerate(idx):
        sl = slice(h * hd, (h + 1) * hd)
        v = p_ref[bb, :, 2 * kw + h * hd: 2 * kw + (h + 1) * hd]
        o = _dot_nt(q_g[bb][:, sl], states[n]) + _dot(scores[n], v)
        outs.append(o + jnp.sum(diag[bb][:, sl], axis=-1, keepdims=True) * v)
        p_last = jnp.exp(g_cum[bb][CHUNK - 1:CHUNK, sl])
        state_ref[bb * n_heads + h] = states[n] * p_last + _dot_tn(v, k_r[bb][:, sl])
    for n, (bb, h) in enumerate(idx):
        gate = p_ref[bb, :, 3 * kw + h * hd: 3 * kw + (h + 1) * hd]
        o_ref[bb, :, h * hd:(h + 1) * hd] = (_rms(outs[n], onorm_ref[...]) * _silu(gate)).astype(o_ref.dtype)


def _hgrn(proj, log_lb, log1m_lb, onorm):
    b, t, cols = proj.shape
    kw = cols // 4
    n_heads = kw // C_KEY_DIM
    nb = MIXER_BATCH if b % MIXER_BATCH == 0 else 1
    sums, pair = _hgrn_tables()
    sums = jnp.asarray(sums, BF16)
    pair = jnp.asarray(pair, F32)
    return pl.pallas_call(
        _hgrn_kernel,
        grid=(b // nb, t // CHUNK),
        in_specs=[pl.BlockSpec((nb, CHUNK, cols), lambda i, c: (i, c, 0)),
                  pl.BlockSpec((1, kw), lambda i, c: (0, 0)),
                  pl.BlockSpec((1, kw), lambda i, c: (0, 0)),
                  pl.BlockSpec((1, C_KEY_DIM), lambda i, c: (0, 0)),
                  pl.BlockSpec(sums.shape, lambda i, c: (0, 0)),
                  pl.BlockSpec(pair.shape, lambda i, c: (0, 0, 0))],
        out_specs=pl.BlockSpec((nb, CHUNK, kw), lambda i, c: (i, c, 0)),
        out_shape=jax.ShapeDtypeStruct((b, t, kw), BF16),
        scratch_shapes=[pltpu.VMEM((nb * n_heads, C_KEY_DIM, C_KEY_DIM), F32)],
        compiler_params=_cparams(("parallel", "arbitrary")),
        name="hgrn2",
    )(proj, log_lb, log1m_lb, onorm, sums, pair)


def _route(logits):
    lane = _iota2(logits.shape, 1)
    neg = -jnp.inf
    gmask = lane < N_GROUPS
    gl = jnp.where(gmask, logits, neg)
    gmax = jnp.max(gl, axis=-1, keepdims=True)
    gsel = jnp.min(jnp.where(gl == gmax, lane, LANES), axis=-1, keepdims=True)
    group_gate = 1.0 / jnp.sum(jnp.where(gmask, jnp.exp(gl - gmax), 0.0), axis=-1, keepdims=True)
    lo = N_GROUPS + EXPERTS_PER_GROUP * gsel
    emask = (lane >= lo) & (lane < lo + EXPERTS_PER_GROUP)
    el = jnp.where(emask, logits, neg)
    ee = jnp.where(emask, jnp.exp(el - jnp.max(el, axis=-1, keepdims=True)), 0.0)
    prob = ee / jnp.sum(ee, axis=-1, keepdims=True)
    p1 = jnp.max(jnp.where(emask, prob, -1.0), axis=-1, keepdims=True)
    i1 = jnp.min(jnp.where(emask & (prob == p1), lane, LANES), axis=-1, keepdims=True)
    rest = emask & (lane != i1)
    p2 = jnp.max(jnp.where(rest, prob, -1.0), axis=-1, keepdims=True)
    i2 = jnp.min(jnp.where(rest & (prob == p2), lane, LANES), axis=-1, keepdims=True)
    scale = group_gate / (p1 + p2)
    comb = jnp.where(lane == i1, p1 * scale, 0.0) + jnp.where(lane == i2, p2 * scale, 0.0)
    return comb, gsel


def _sorted_rows(tile_rows):
    return tile_rows + N_GROUPS * SEG_ALIGN


def _xattn_route_kernel(x_ref, a_ref, wmix_ref, gain_ref, wq_ref, k_ref, v_ref, wo_ref, gffn_ref, wr_ref, br_ref,
                        x2_ref, xs_ref, cs_ref, info_ref, seg_ref):
    n_sub = xs_ref.shape[0]
    tq = x_ref.shape[0] // n_sub
    d = x_ref.shape[1]
    hd = d // XA_HEADS
    rows = xs_ref.shape[1]

    def tile_stages(u):
        tok = slice(u * tq, (u + 1) * tq)
        x = x_ref[tok, :] + _dot(a_ref[tok, :], wmix_ref[...])
        q = _dot(_rms(x, gain_ref[...]).astype(BF16), wq_ref[...]).astype(BF16)
        yield
        outs = []
        for h in range(XA_HEADS):
            sl = slice(h * hd, (h + 1) * hd)
            s = _dot_nt(q[:, sl], k_ref[:, sl]) * (hd ** -0.5)
            p = jnp.exp(s - jnp.max(s, axis=-1, keepdims=True))
            p = p / jnp.sum(p, axis=-1, keepdims=True)
            outs.append(_dot(p.astype(BF16), v_ref[:, sl]).astype(BF16))
        yield
        x2 = x + _dot(jnp.concatenate(outs, axis=-1), wo_ref[...])
        x2_ref[tok, :] = x2
        xn = _rms(x2, gffn_ref[...])
        xn_hi = xn.astype(BF16)
        xn_lo = (xn - xn_hi.astype(F32)).astype(BF16)
        hi_part = _dot(xn_hi, wr_ref[...])
        logits = hi_part[:, 0:LANES] + hi_part[:, LANES:2 * LANES] + _dot(xn_lo, wr_ref[:, 0:LANES])
        yield
        comb, gsel = _route(logits + br_ref[...])
        lane = _iota2((tq, LANES), 1)
        onehot = (lane == gsel).astype(F32)
        earlier = (_iota2((tq, tq), 0) > _iota2((tq, tq), 1)).astype(BF16)
        before = _dot(earlier, onehot.astype(BF16))
        yield
        rank = jnp.sum(onehot * before, axis=-1, keepdims=True)
        counts = before[tq - 1:tq, :] + onehot[tq - 1:tq, :]
        padded = jnp.floor((counts + (SEG_ALIGN - 1)) * (1.0 / SEG_ALIGN)) * SEG_ALIGN
        lane1 = _iota2((1, LANES), 1)
        starts = jnp.zeros((1, LANES), F32)
        for g in range(N_GROUPS - 1):
            starts = starts + jnp.where(lane1 > g, padded[:, g:g + 1], 0.0)
        dest = jnp.sum(onehot * starts, axis=-1, keepdims=True) + rank
        place = (_iota2((tq, rows), 1).astype(F32) == dest).astype(BF16)
        comb_hi = comb.astype(BF16)
        comb_lo = (comb - comb_hi.astype(F32)).astype(BF16)
        placed = _dot_tn(place, jnp.concatenate([xn_hi, comb_hi, comb_lo], axis=1))
        xs_ref[u] = placed[:, 0:d].astype(BF16)
        cs_ref[u] = placed[:, d:d + LANES] + placed[:, d + LANES:d + 2 * LANES]
        info_ref[tok, :] = jnp.where(lane == 0, dest, 0.0)
        seg_ref[u] = jnp.concatenate([starts, padded, jnp.zeros((6, LANES), F32)], axis=0)

    _interleave([tile_stages(u) for u in range(n_sub)])


def _xattn_route(x, act, w_mix, gain, wq, kv, wo, gain_ffn, w_router, b_router, layer, tq=512):
    b, t, d = x.shape
    m = kv.shape[1]
    tq = min(tq, t)
    tiles_per_row = t // tq
    n_tiles = b * tiles_per_row
    rows = _sorted_rows(tq)
    n_sub = XATTN_TILES_PER_STEP if tiles_per_row % XATTN_TILES_PER_STEP == 0 else 1
    steps_per_row = tiles_per_row // n_sub

    def tile(i, j):
        return (i * steps_per_row + j, 0, 0)

    return pl.pallas_call(
        _xattn_route_kernel,
        grid=(b, steps_per_row),
        in_specs=[pl.BlockSpec((None, n_sub * tq, d), lambda i, j: (i, j, 0)),
                  pl.BlockSpec((None, n_sub * tq, act.shape[2]), lambda i, j: (i, j, 0)),
                  pl.BlockSpec(w_mix.shape, lambda i, j: (0, 0)),
                  pl.BlockSpec((1, d), lambda i, j: (0, 0)),
                  pl.BlockSpec((None, d, d), lambda i, j: (layer, 0, 0)),
                  pl.BlockSpec((None, m, d), lambda i, j: (i, 0, 0)),
                  pl.BlockSpec((None, m, d), lambda i, j: (i, 0, 1)),
                  pl.BlockSpec((None, d, d), lambda i, j: (layer, 0, 0)),
                  pl.BlockSpec((1, d), lambda i, j: (0, 0)),
                  pl.BlockSpec((d, 2 * LANES), lambda i, j: (0, 0)),
                  pl.BlockSpec((1, LANES), lambda i, j: (0, 0))],
        out_specs=[pl.BlockSpec((None, n_sub * tq, d), lambda i, j: (i, j, 0)),
                   pl.BlockSpec((n_sub, rows, d), tile),
                   pl.BlockSpec((n_sub, rows, LANES), tile),
                   pl.BlockSpec((None, n_sub * tq, LANES), lambda i, j: (i, j, 0)),
                   pl.BlockSpec((n_sub, 8, LANES), tile)],
        out_shape=[jax.ShapeDtypeStruct((b, t, d), F32),
                   jax.ShapeDtypeStruct((n_tiles, rows, d), BF16),
                   jax.ShapeDtypeStruct((n_tiles, rows, LANES), F32),
                   jax.ShapeDtypeStruct((b, t, LANES), F32),
                   jax.ShapeDtypeStruct((n_tiles, 8, LANES), F32)],
        compiler_params=_cparams(("parallel", "parallel")),
        name="xattn_route",
    )(x, act, w_mix, gain.reshape(1, d), wq, kv, kv, wo, gain_ffn.reshape(1, d), w_router, b_router)


def _moe_ffn_kernel(seg_ref, xs_ref, cs_ref, w1_ref, w3_ref, w2_ref, ys_ref, xbuf_ref, cbuf_ref, ybuf_ref):
    win = pl.program_id(0)
    grp = pl.program_id(1)
    n_win = xs_ref.shape[0]

    @pl.when((win == 0) & (grp == 0))
    def _():
        xbuf_ref[...] = jnp.zeros_like(xbuf_ref)
        cbuf_ref[...] = jnp.zeros_like(cbuf_ref)

    @pl.when(grp == 0)
    def _():
        ys_ref[...] = jnp.zeros_like(ys_ref)

    def segment(t):
        base = (win * n_win + t) * 8
        return seg_ref[base + grp], seg_ref[base + N_GROUPS + grp]

    def copy_segments(copy_piece):
        packed = jnp.int32(0)
        for t in range(n_win):
            start, size = segment(t)

            def body(j, carry, t=t, start=start, packed=packed):
                copy_piece(t, pl.multiple_of(start + j * SEG_ALIGN, SEG_ALIGN),
                           pl.multiple_of(packed + j * SEG_ALIGN, SEG_ALIGN))
                return carry

            lax.fori_loop(0, size // SEG_ALIGN, body, 0)
            packed = packed + size
        return packed

    def gather(t, src, dst):
        xbuf_ref[pl.ds(dst, SEG_ALIGN), :] = xs_ref[t, pl.ds(src, SEG_ALIGN), :]
        cbuf_ref[pl.ds(dst, SEG_ALIGN), :] = cs_ref[t, pl.ds(src, SEG_ALIGN), :]

    total = copy_segments(gather)

    def ffn_piece(r0, n_rows):
        xc = xbuf_ref[pl.ds(r0, n_rows), :]
        cb = cbuf_ref[pl.ds(r0, n_rows), :]
        lane = _iota2(cb.shape, 1)
        hids = []
        for e in range(EXPERTS_PER_GROUP):
            weight = jnp.sum(jnp.where(lane == N_GROUPS + EXPERTS_PER_GROUP * grp + e, cb, 0.0),
                             axis=-1, keepdims=True)
            hids.append((_silu(_dot(xc, w1_ref[e])) * _dot(xc, w3_ref[e]) * weight).astype(BF16))
        ybuf_ref[pl.ds(r0, n_rows), :] = _dot(jnp.concatenate(hids, axis=-1), w2_ref[...]).astype(ybuf_ref.dtype)

    def chunk(c, carry):
        ffn_piece(pl.multiple_of(c * MOE_CHUNK, MOE_CHUNK), MOE_CHUNK)
        return carry

    n_full = total // MOE_CHUNK
    lax.fori_loop(0, n_full, chunk, 0)
    rest = total - n_full * MOE_CHUNK
    rest_start = pl.multiple_of(n_full * MOE_CHUNK, MOE_CHUNK)
    lower = 0
    for n_rows in (MOE_CHUNK // 2, MOE_CHUNK):
        @pl.when((rest > lower) & (rest <= n_rows))
        def _(n_rows=n_rows):
            ffn_piece(rest_start, n_rows)
        lower = n_rows

    def scatter(t, dst, src):
        ys_ref[t, pl.ds(dst, SEG_ALIGN), :] = ybuf_ref[pl.ds(src, SEG_ALIGN), :]

    copy_segments(scatter)


def _moe_ffn(seg, xs, cs, w1, w3, w2g, layer):
    n_tiles, rows, d = xs.shape
    ff = w1.shape[3]
    n_win = next(w for w in (MOE_WINDOW, 2, 1) if n_tiles % w == 0)
    buf_rows = -(-(n_win * rows) // MOE_CHUNK) * MOE_CHUNK
    grid_spec = pltpu.PrefetchScalarGridSpec(
        num_scalar_prefetch=1,
        grid=(n_tiles // n_win, N_GROUPS),
        in_specs=[pl.BlockSpec((n_win, rows, d), lambda w, g, seg: (w, 0, 0)),
                  pl.BlockSpec((n_win, rows, LANES), lambda w, g, seg: (w, 0, 0)),
                  pl.BlockSpec((None, EXPERTS_PER_GROUP, d, ff), lambda w, g, seg: (layer, g, 0, 0)),
                  pl.BlockSpec((None, EXPERTS_PER_GROUP, d, ff), lambda w, g, seg: (layer, g, 0, 0)),
                  pl.BlockSpec((None, None, EXPERTS_PER_GROUP * ff, d), lambda w, g, seg: (layer, g, 0, 0))],
        out_specs=pl.BlockSpec((n_win, rows, d), lambda w, g, seg: (w, 0, 0)),
        scratch_shapes=[pltpu.VMEM((buf_rows, d), BF16),
                        pltpu.VMEM((buf_rows, LANES), F32),
                        pltpu.VMEM((buf_rows, d), BF16)])
    return pl.pallas_call(
        _moe_ffn_kernel,
        grid_spec=grid_spec,
        out_shape=jax.ShapeDtypeStruct((n_tiles, rows, d), BF16),
        compiler_params=_cparams(("arbitrary", "arbitrary")),
        name="moe_ffn",
    )(seg, xs, cs, w1, w3, w2g)


def _unsorted_sum(x_ref, info_ref, ys_ref):
    info = info_ref[...]
    dest = jnp.sum(jnp.where(_iota2(info.shape, 1) == 0, info, 0.0), axis=-1, keepdims=True)
    rows = ys_ref.shape[0]
    place = (_iota2((info.shape[0], rows), 1).astype(F32) == dest).astype(BF16)
    return x_ref[...] + _dot(place, ys_ref[...])


def _moe_unsort_kernel(n_w, x_ref, info_ref, ys_ref, gain_ref, *refs):
    w_refs, o_refs = refs[:n_w], refs[n_w:]
    if n_w == 0:
        tq = x_ref.shape[0] // ys_ref.shape[0]
        for u in range(ys_ref.shape[0]):
            tok = slice(u * tq, (u + 1) * tq)
            y = _unsorted_sum(x_ref.at[tok, :], info_ref.at[tok, :], ys_ref.at[u])
            o_refs[0][tok, :] = _rms(y, gain_ref[...])
        return
    y = _unsorted_sum(x_ref, info_ref, ys_ref.at[0])
    yn = _rms(y, gain_ref[...])
    o_refs[0][...] = y
    yn = yn.astype(BF16)
    for w_ref, o_ref in zip(w_refs, o_refs[1:]):
        n = w_ref.shape[1]
        for c0 in range(0, n, 512):
            c1 = min(n, c0 + 512)
            o_ref[:, c0:c1] = _dot(yn, w_ref[:, c0:c1])


def _moe_unsort(x, info, ys, gain, ws=()):
    b, t, d = x.shape
    n_tiles, rows, _ = ys.shape
    tiles_per_row = n_tiles // b
    tq = t // tiles_per_row
    n_sub = 2 if (not ws and tiles_per_row % 2 == 0) else 1
    steps_per_row = tiles_per_row // n_sub

    def token_block(cols):
        return pl.BlockSpec((None, n_sub * tq, cols), lambda i, j: (i, j, 0))

    out_cols = [d] + [w.shape[1] for w in ws]
    outs = pl.pallas_call(
        functools.partial(_moe_unsort_kernel, len(ws)),
        grid=(b, steps_per_row),
        in_specs=[token_block(d), token_block(LANES),
                  pl.BlockSpec((n_sub, rows, d), lambda i, j: (i * steps_per_row + j, 0, 0)),
                  pl.BlockSpec((1, d), lambda i, j: (0, 0))]
                 + [pl.BlockSpec(w.shape, lambda i, j: (0, 0)) for w in ws],
        out_specs=[token_block(c) for c in out_cols],
        out_shape=[jax.ShapeDtypeStruct((b, t, c), F32) for c in out_cols],
        compiler_params=_cparams(("parallel", "parallel")),
        name="moe_unsort",
    )(x, info, ys, gain.reshape(1, d), *ws)
    return outs


def _pad_cols(w, n):
    return jnp.pad(w, ((0, 0), (0, n - w.shape[1])))


def _pad_rows(w, n):
    return jnp.pad(w, ((0, n - w.shape[0]), (0, 0)))


def _row(v, n=None):
    v = v.reshape(1, -1).astype(F32)
    return v if n is None else _pad_cols(v, n)


def kernel(x, mem, norm_mix, norm_xattn, norm_mem, norm_ffn, norm_final, ab_w_in, ab_conv, ab_a_log, ab_dt_bias, ab_onorm, ab_shift_mu, ab_w0, ab_w2, ab_a0, ab_a2, ab_g2, ab_k_k, ab_k_a, ab_r_k, ab_ln_w, ab_ln_b, ab_w_out, c_w_in, c_lb_param, c_onorm, c_w_out, xa_wq, xa_wkv, xa_wo, moe_w_group, moe_b_group, moe_w_expert, moe_b_expert, moe_w1, moe_w3, moe_w2):
    bsz, seq, d = x.shape
    n_mem = mem.shape[1]
    depth = norm_mix.shape[0]
    a_width = d // 2
    b_width = d // 2
    a_heads = a_width // A_HEAD_DIM
    a_cols = 4 * a_width + 2 * a_heads
    n_tok = bsz * seq

    lb_p = jax.nn.softmax(c_lb_param.astype(F32), axis=0)
    lower_bounds = jnp.cumsum(lb_p, axis=0) - lb_p[0]

    def project(x_cur, moe_pending, gain, ws):
        if moe_pending is None:
            projs = _norm_proj(x_cur.reshape(n_tok, d), gain, ws)
            return x_cur, [p.reshape(bsz, seq, -1) for p in projs]
        x_new, *projs = _moe_unsort(*moe_pending, gain, ws)
        return x_new, projs

    xa_wq_bf, xa_wkv_bf, xa_wo_bf = xa_wq.astype(BF16), xa_wkv.astype(BF16), xa_wo.astype(BF16)
    ff = moe_w2.shape[2]
    moe_w1_bf = moe_w1.astype(BF16)
    moe_w3_bf = moe_w3.astype(BF16)
    moe_w2_bf = moe_w2.astype(BF16).reshape(depth, N_GROUPS, EXPERTS_PER_GROUP * ff, d)

    x2 = x
    moe_pending = None
    mem2 = mem.reshape(bsz * n_mem, d)
    for layer in range(depth):
        if layer % 2 == 0:
            e = layer // 2
            w_in = ab_w_in[e]
            w_a = w_in[:, :4 * a_width].astype(BF16)
            w_db = _pad_cols(w_in[:, 4 * a_width:a_cols], LANES).astype(BF16)
            wb = w_in[:, a_cols:]
            o1, o2 = 3 * b_width, 3 * b_width + B_DECAY_LORA
            o3 = o2 + B_ICLR_LORA
            w_b = jnp.concatenate([wb[:, :o1], _pad_cols(wb[:, o1:o2], LANES), _pad_cols(wb[:, o2:o3], LANES),
                                   wb[:, o3:]], axis=1).astype(BF16)
            mu = ab_shift_mu[e].reshape(1, -1)
            mu_p = jnp.concatenate([mu[:, :o1], _pad_cols(mu[:, o1:o2], LANES), _pad_cols(mu[:, o2:o3], LANES),
                                    mu[:, o3:]], axis=1)
            head_id = jnp.arange(b_width) // B_HEAD_DIM
            bd = (head_id[:, None] == head_id[None, :]).astype(F32)
            gdn_params = [ab_conv[e], _row(ab_a_log[e], LANES), _row(ab_dt_bias[e], LANES), _row(ab_onorm[e])]
            rwkv_params = [mu_p, _row(ab_w0[e]), _pad_rows(ab_w2[e], LANES), _row(ab_a0[e]),
                           _pad_rows(ab_a2[e], LANES), ab_g2[e], _row(ab_k_k[e]), _row(ab_k_a[e]),
                           _row(ab_r_k[e]), _row(ab_ln_w[e]), _row(ab_ln_b[e]), bd]
            x2, (pa, pdb, pb) = project(x2, moe_pending, norm_mix[layer], [w_a, w_db, w_b])
            mix = _mixer_ab(pa, pdb, pb, gdn_params, rwkv_params)
            w_mix = ab_w_out[e].astype(BF16)
        else:
            o = layer // 2
            x2, (proj,) = project(x2, moe_pending, norm_mix[layer], [c_w_in[o].astype(BF16)])
            lb = lower_bounds[layer].reshape(1, -1)
            mix = _hgrn(proj, jnp.log(lb), jnp.log1p(-lb), _row(c_onorm[o]))
            w_mix = c_w_out[o].astype(BF16)

        (kv_mem,) = _norm_proj(mem2, norm_mem[layer], [xa_wkv_bf[layer]], out_dtype=BF16)
        w_r = _pad_cols(jnp.concatenate([moe_w_group[layer], moe_w_expert[layer]], axis=1), LANES).astype(F32)
        w_r_hi = w_r.astype(BF16)
        w_r_lo = (w_r - w_r_hi.astype(F32)).astype(BF16)
        b_router = _row(jnp.concatenate([moe_b_group[layer], moe_b_expert[layer]]), LANES)
        x3, xs, cs, info, seg = _xattn_route(
            x2, mix, w_mix, norm_xattn[layer], xa_wq_bf, kv_mem.reshape(bsz, n_mem, 2 * d), xa_wo_bf,
            norm_ffn[layer], jnp.concatenate([w_r_hi, w_r_lo], axis=1), b_router, layer)
        seg_i = jnp.concatenate([seg[:, 0, :N_GROUPS], seg[:, 1, :N_GROUPS]], axis=1).astype(jnp.int32).reshape(-1)
        ys = _moe_ffn(seg_i, xs, cs, moe_w1_bf, moe_w3_bf, moe_w2_bf, layer)
        moe_pending = (x3, info, ys)
    (out,) = _moe_unsort(*moe_pending, norm_final)
    return out
```

```python
import functools

import jax
import jax.numpy as jnp
import numpy as np
from jax import lax
from jax.experimental import pallas as pl
from jax.experimental.pallas import tpu as pltpu

F32 = jnp.float32
BF16 = jnp.bfloat16

CHUNK = 64
HGRN_LEVELS = CHUNK.bit_length() - 1
NORM_EPS = 1e-6
GROUP_NORM_EPS = 64e-5
CONV_WIDTH = 4

A_HEAD_DIM = 128
B_HEAD_DIM = 64
B_DECAY_LORA = 64
B_ICLR_LORA = 64
B_GATE_LORA = 128
C_KEY_DIM = 128
XA_HEADS = 4
N_GROUPS = 4
EXPERTS_PER_GROUP = 4
N_EXPERTS = N_GROUPS * EXPERTS_PER_GROUP

LANES = 128
SUBLANES = 8
VMEM_LIMIT = 56 * 1024 * 1024
RWKV_HEAD_START = 3
MIXER_BATCH = 4
SEG_ALIGN = 16
MOE_WINDOW = 4
MOE_CHUNK = 288
XATTN_TILES_PER_STEP = 2


def _cparams(sem):
    return pltpu.CompilerParams(dimension_semantics=sem, vmem_limit_bytes=VMEM_LIMIT)


def _dot_dims(a, b, dims):
    return lax.dot_general(a.astype(BF16), b.astype(BF16), (dims, ((), ())), preferred_element_type=F32)


def _dot(a, b):
    return _dot_dims(a, b, ((1,), (0,)))


def _dot_nt(a, b):
    return _dot_dims(a, b, ((1,), (1,)))


def _dot_tn(a, b):
    return _dot_dims(a, b, ((0,), (0,)))


def _rms(x, gain):
    return x * lax.rsqrt(jnp.mean(x * x, axis=-1, keepdims=True) + NORM_EPS) * gain


def _sigmoid(x):
    return 1.0 / (1.0 + jnp.exp(-x))


def _silu(x):
    return x * _sigmoid(x)


def _softplus(x):
    return jnp.maximum(x, 0.0) + jnp.log(1.0 + jnp.exp(-jnp.abs(x)))


def _iota2(shape, dim):
    return lax.broadcasted_iota(jnp.int32, shape, dim)


def _split3(v):
    v1 = v.astype(BF16)
    v2 = (v - v1.astype(F32)).astype(BF16)
    v3 = (v - v1.astype(F32) - v2.astype(F32)).astype(BF16)
    return v1, v2, v3


def _table_dot(table, v):
    return _dot(jnp.concatenate([table.astype(BF16)] * 3, axis=1), jnp.concatenate(_split3(v), axis=0))


def _table_dot_tn(v, table):
    return _dot_tn(jnp.concatenate(_split3(v), axis=0), jnp.concatenate([table.astype(BF16)] * 3, axis=0))


def _interleave(stage_generators):
    live = list(stage_generators)
    while live:
        for gen in list(live):
            try:
                next(gen)
            except StopIteration:
                live.remove(gen)


def _tri_inverse_stages(mats):
    n = mats[0].shape[0]
    row = _iota2((n, n), 0)
    col = _iota2((n, n), 1)
    eye = (row == col).astype(F32)
    xs = [eye - jnp.where((row >> 1) == (col >> 1), a, 0.0) for a in mats]
    for level in range(1, n.bit_length() - 1):
        joins = ((row >> (level + 1)) == (col >> (level + 1))) & ((row >> level) != (col >> level))
        ts = [_dot(jnp.where(joins, a, 0.0), x) for a, x in zip(mats, xs)]
        yield
        xs = [x - _dot(x, t) for x, t in zip(xs, ts)]
        yield
    return xs


def _norm_proj_kernel(n_w, x_ref, g_ref, *refs):
    w_refs, o_refs = refs[:n_w], refs[n_w:]
    xn = _rms(x_ref[...], g_ref[...]).astype(BF16)
    for w_ref, o_ref in zip(w_refs, o_refs):
        n = w_ref.shape[1]
        for c0 in range(0, n, 512):
            c1 = min(n, c0 + 512)
            o_ref[:, c0:c1] = _dot(xn, w_ref[:, c0:c1]).astype(o_ref.dtype)


def _norm_proj(x, gain, ws, out_dtype=F32, tm=512):
    n, d = x.shape
    tm = min(tm, n)
    return pl.pallas_call(
        functools.partial(_norm_proj_kernel, len(ws)),
        grid=(n // tm,),
        in_specs=[pl.BlockSpec((tm, d), lambda i: (i, 0)),
                  pl.BlockSpec((1, d), lambda i: (0, 0))]
                 + [pl.BlockSpec(w.shape, lambda i: (0, 0)) for w in ws],
        out_specs=[pl.BlockSpec((tm, w.shape[1]), lambda i: (i, 0)) for w in ws],
        out_shape=[jax.ShapeDtypeStruct((n, w.shape[1]), out_dtype) for w in ws],
        compiler_params=_cparams(("parallel",)),
        name="norm_proj",
    )(x, gain.reshape(1, d), *ws)


def _log_forget(f, log_lb, log1m_lb):
    a1 = jnp.broadcast_to(log_lb, f.shape)
    b1 = log1m_lb - _softplus(-f)
    return jnp.maximum(a1, b1) + jnp.log(1.0 + jnp.exp(-jnp.abs(a1 - b1)))


def _gdn_stages(nb, pa_ref, pdb_ref, convw_ref, alog_ref, dtb_ref, onorm_ref, o_ref, xbuf_ref, state_ref):
    n_heads = state_ref.shape[0] // nb
    hd = A_HEAD_DIM
    width = n_heads * hd
    row = _iota2((CHUNK, CHUNK), 0)
    col = _iota2((CHUNK, CHUNK), 1)
    incl = row >= col
    strict = row > col
    tril = incl.astype(F32)
    triu = (row <= col).astype(F32)

    qs, ks, vs, betas, gs, g_lasts, decays = [], [], [], [], [], [], []
    for bb in range(nb):
        x = pa_ref[bb, :, 0:3 * width]
        xbuf_ref[bb, 8:8 + CHUNK, :] = x
        acc = convw_ref[0:1, :] * xbuf_ref[bb, 5:5 + CHUNK, :]
        for j in range(1, CONV_WIDTH):
            acc = acc + convw_ref[j:j + 1, :] * xbuf_ref[bb, 5 + j:5 + j + CHUNK, :]
        xbuf_ref[bb, 0:8, :] = x[CHUNK - 8:CHUNK, :]
        qkv = _silu(acc)
        pdb = pdb_ref[bb]
        log_alpha = -jnp.exp(alog_ref[...]) * _softplus(pdb + dtb_ref[...])
        beta_all = _sigmoid(pdb)
        g_all = _table_dot(tril, log_alpha)
        g_all_t = _table_dot_tn(log_alpha, triu)
        for h in range(n_heads):
            q = qkv[:, h * hd:(h + 1) * hd]
            k = qkv[:, width + h * hd: width + (h + 1) * hd]
            qs.append(q * lax.rsqrt(jnp.sum(q * q, axis=-1, keepdims=True) + 1e-6) * (hd ** -0.5))
            ks.append(k * lax.rsqrt(jnp.sum(k * k, axis=-1, keepdims=True) + 1e-6))
            vs.append(qkv[:, 2 * width + h * hd: 2 * width + (h + 1) * hd])
            betas.append(beta_all[:, n_heads + h:n_heads + h + 1])
            g = g_all[:, h:h + 1]
            gs.append(g)
            g_lasts.append(g_all[CHUNK - 1:CHUNK, h:h + 1])
            decays.append(jnp.where(incl, jnp.exp(jnp.where(incl, g - g_all_t[h:h + 1, :], 0.0)), 0.0))
    yield
    idx = range(nb * n_heads)
    k_betas = [ks[i] * betas[i] for i in idx]
    egs = [jnp.exp(g) for g in gs]
    a_lows = [jnp.where(strict, _dot_nt(k_betas[i], ks[i]) * decays[i], 0.0) for i in idx]
    qks = [_dot_nt(qs[i], ks[i]) * decays[i] for i in idx]
    states = [state_ref[i] for i in idx]
    o_inters = [_dot(qs[i] * egs[i], states[i]) for i in idx]
    yield
    t_invs = yield from _tri_inverse_stages(a_lows)
    us = [_dot(t_invs[i], vs[i] * betas[i]) for i in idx]
    ws = [_dot(t_invs[i], k_betas[i] * egs[i]) for i in idx]
    yield
    v_news = [us[i] - _dot(ws[i], states[i]) for i in idx]
    yield
    os_ = [o_inters[i] + _dot(qks[i], v_news[i]) for i in idx]
    for i in idx:
        state_ref[i] = (states[i] * jnp.exp(g_lasts[i])
                        + _dot_tn(ks[i] * jnp.exp(g_lasts[i] - gs[i]), v_news[i]))
    yield
    for i in idx:
        bb, h = divmod(i, n_heads)
        gate = pa_ref[bb, :, 3 * width + h * hd: 3 * width + (h + 1) * hd]
        o_ref[bb, :, h * hd:(h + 1) * hd] = (_rms(os_[i], onorm_ref[...]) * _silu(gate)).astype(o_ref.dtype)


def _rwkv_stages(nb, col0, pb_ref, mu_ref, w0_ref, w2_ref, a0_ref, a2_ref, g2_ref, kk_ref, ka_ref, rk_ref,
                 lnw_ref, lnb_ref, bd_ref, o_ref, carry_ref, state_ref, obuf_ref):
    n_heads = state_ref.shape[0] // nb
    hd = B_HEAD_DIM
    width = n_heads * hd
    row = _iota2((CHUNK, CHUNK), 0)
    col = _iota2((CHUNK, CHUNK), 1)
    strict = row > col
    tril = (row >= col).astype(F32)
    incl2 = _iota2((CHUNK, 2 * CHUNK), 0) >= (_iota2((CHUNK, 2 * CHUNK), 1) & (CHUNK - 1))
    bd = bd_ref[...]

    pss = []
    for bb in range(nb):
        p = pb_ref[bb]
        prev = jnp.where(_iota2(p.shape, 0) == 0, carry_ref[bb, 0:1, :], pltpu.roll(p, 1, 0))
        carry_ref[bb, 0:1, :] = p[CHUNK - 1:CHUNK, :]
        pss.append(p + mu_ref[...] * (prev - p))
    rs = [ps[:, 0:width] for ps in pss]
    kraw = [ps[:, width:2 * width] for ps in pss]
    vfull = [ps[:, 2 * width:3 * width] for ps in pss]
    w_raws = [w0_ref[...] + _dot(jnp.tanh(ps[:, 3 * width:3 * width + LANES]), w2_ref[...]) for ps in pss]
    iclrs = [_sigmoid(a0_ref[...] + _dot(ps[:, 3 * width + LANES:3 * width + 2 * LANES], a2_ref[...]))
             for ps in pss]
    gates = [_dot(_sigmoid(ps[:, 3 * width + 2 * LANES:3 * width + 3 * LANES]), g2_ref[...]) for ps in pss]
    kks = [k * kk_ref[...] for k in kraw]
    kk_ss = [_dot(kk * kk, bd) for kk in kks]
    yield
    log_ws = [-jnp.exp(-_softplus(-w_raw) - 0.5) for w_raw in w_raws]
    g_cums = [_table_dot(tril, log_w) for log_w in log_ws]
    yield
    lhss, rhss, bl_kls, p_cs, vhs, k2s = [], [], [], [], [], []
    for bb in range(nb):
        kk = kks[bb] * lax.rsqrt(kk_ss[bb] + 1e-6)
        k2 = kraw[bb] * (1.0 + (iclrs[bb] - 1.0) * ka_ref[...])
        k2s.append(k2)
        bvec = kk * iclrs[bb]
        g = g_cums[bb]
        g_last = g[CHUNK - 1:CHUNK, :]
        eng = jnp.exp(-g)
        egl = jnp.exp(g_last - g)
        r_t = rs[bb] * jnp.exp(g)
        kk_t = kk * jnp.exp(g - log_ws[bb])
        b_h, k_h = bvec * eng, k2 * eng
        b_l, k_l = bvec * egl, k2 * egl
        p_c = jnp.exp(g_last)
        for h in range(n_heads):
            sl = slice(h * hd, (h + 1) * hd)
            lhss.append(jnp.concatenate([kk_t[:, sl], r_t[:, sl]], axis=0))
            rhss.append(jnp.concatenate([b_h[:, sl], k_h[:, sl]], axis=0))
            bl_kls.append(jnp.concatenate([b_l[:, sl], k_l[:, sl]], axis=0))
            p_cs.append(p_c[:, sl])
            vhs.append(vfull[bb][:, sl])
    idx = range(nb * n_heads)
    ms = [_dot_nt(lhss[i], rhss[i]) for i in idx]
    states = [state_ref[i] for i in idx]
    lss = [_dot_nt(lhss[i], states[i]) for i in idx]
    yield
    a_bbs = [jnp.where(strict, m[0:CHUNK, 0:CHUNK], 0.0) for m in ms]
    a_bks = [jnp.where(strict, m[0:CHUNK, CHUNK:2 * CHUNK], 0.0) for m in ms]
    a_rs = [jnp.where(incl2, m[CHUNK:2 * CHUNK, :], 0.0) for m in ms]
    rhs_u = [lss[i][0:CHUNK] + _dot(a_bks[i], vhs[i]) for i in idx]
    t_invs = yield from _tri_inverse_stages(a_bbs)
    uvs = [jnp.concatenate([-_dot(t_invs[i], rhs_u[i]), vhs[i]], axis=0) for i in idx]
    yield
    for i in idx:
        bb, h = divmod(i, n_heads)
        obuf_ref[bb, :, h * hd:(h + 1) * hd] = lss[i][CHUNK:2 * CHUNK] + _dot(a_rs[i], uvs[i])
        state_ref[i] = states[i] * p_cs[i] + _dot_tn(uvs[i], bl_kls[i])
    yield
    inv_n = 1.0 / hd
    os_ = [obuf_ref[bb] for bb in range(nb)]
    means = [_dot(o, bd) * inv_n for o in os_]
    bonus = [_dot(rs[bb] * k2s[bb] * rk_ref[...], bd) * vfull[bb] for bb in range(nb)]
    yield
    cens = [o - mean for o, mean in zip(os_, means)]
    vars_ = [_dot(cen * cen, bd) * inv_n for cen in cens]
    yield
    for bb in range(nb):
        ln = cens[bb] * lax.rsqrt(vars_[bb] + GROUP_NORM_EPS) * lnw_ref[...] + lnb_ref[...]
        o_ref[bb, :, col0:col0 + width] = ((ln + bonus[bb]) * gates[bb]).astype(o_ref.dtype)


def _mixer_ab_kernel(n_gdn, n_rwkv, pa_ref, pdb_ref, pb_ref, *refs):
    gdn_params = refs[:n_gdn]
    rwkv_params = refs[n_gdn:n_gdn + n_rwkv]
    o_ref, xbuf_ref, gstate_ref, carry_ref, rstate_ref, obuf_ref = refs[n_gdn + n_rwkv:]
    nb = pa_ref.shape[0]

    @pl.when(pl.program_id(1) == 0)
    def _():
        xbuf_ref[:, 0:8, :] = jnp.zeros((nb, 8, xbuf_ref.shape[2]), F32)
        gstate_ref[...] = jnp.zeros_like(gstate_ref)
        carry_ref[...] = jnp.zeros_like(carry_ref)
        rstate_ref[...] = jnp.zeros_like(rstate_ref)

    a_width = pa_ref.shape[2] // 4
    rwkv = _rwkv_stages(nb, a_width, pb_ref, *rwkv_params, o_ref, carry_ref, rstate_ref, obuf_ref)
    for _ in range(RWKV_HEAD_START):
        next(rwkv)
    _interleave([rwkv, _gdn_stages(nb, pa_ref, pdb_ref, *gdn_params, o_ref, xbuf_ref, gstate_ref)])


def _mixer_ab(pa, pdb, pb, gdn_params, rwkv_params):
    b, t, a_cols = pa.shape
    b_cols = pb.shape[2]
    a_width = a_cols // 4
    b_width = (b_cols - 3 * LANES) // 3
    a_heads = a_width // A_HEAD_DIM
    b_heads = b_width // B_HEAD_DIM
    nb = MIXER_BATCH if b % MIXER_BATCH == 0 else 1
    params = list(gdn_params) + list(rwkv_params)
    return pl.pallas_call(
        functools.partial(_mixer_ab_kernel, len(gdn_params), len(rwkv_params)),
        grid=(b // nb, t // CHUNK),
        in_specs=[pl.BlockSpec((nb, CHUNK, a_cols), lambda i, c: (i, c, 0)),
                  pl.BlockSpec((nb, CHUNK, LANES), lambda i, c: (i, c, 0)),
                  pl.BlockSpec((nb, CHUNK, b_cols), lambda i, c: (i, c, 0))]
                 + [pl.BlockSpec(a.shape, lambda i, c: (0, 0)) for a in params],
        out_specs=pl.BlockSpec((nb, CHUNK, a_width + b_width), lambda i, c: (i, c, 0)),
        out_shape=jax.ShapeDtypeStruct((b, t, a_width + b_width), BF16),
        scratch_shapes=[pltpu.VMEM((nb, 8 + CHUNK, 3 * a_width), F32),
                        pltpu.VMEM((nb * a_heads, A_HEAD_DIM, A_HEAD_DIM), F32),
                        pltpu.VMEM((nb, 8, b_cols), F32),
                        pltpu.VMEM((nb * b_heads, B_HEAD_DIM, B_HEAD_DIM), F32),
                        pltpu.VMEM((nb, CHUNK, b_width), F32)],
        compiler_params=_cparams(("parallel", "arbitrary")),
        name="mixer_ab",
    )(pa, pdb, pb, *params)


def _hgrn_tables():
    i = np.arange(CHUNK)[:, None]
    t = np.arange(CHUNK)[None, :]
    blocks, pair = [], []
    for ls in range(HGRN_LEVELS - 1, -1, -1):
        s = 1 << ls
        mid = (i >> (ls + 1) << (ls + 1)) + s
        q_rows, k_rows = _hgrn_level_rows(ls)
        blocks.append(((i >= mid) & (t >= mid) & (t <= i))[q_rows])
        blocks.append(((i < mid) & (t > i) & (t <= mid - 1))[k_rows])
        pair.append((((i >> ls) ^ (t >> ls)) == 1) & (i > t))
    blocks.append(t <= i)
    blocks.append(t > i)
    sums = np.concatenate(blocks, axis=0).astype(np.float32)
    return np.concatenate([sums] * 3, axis=1), np.stack(pair).astype(np.float32)


def _hgrn_level_rows(ls):
    rows = np.arange(CHUNK)
    if (1 << ls) < SUBLANES:
        return rows, rows
    upper = ((rows >> ls) & 1) == 1
    return rows[upper], rows[~upper]


def _hgrn_kernel(p_ref, loglb_ref, log1mlb_ref, onorm_ref, sums_ref, pair_ref, o_ref, state_ref):
    nb = p_ref.shape[0]
    n_heads = state_ref.shape[0] // nb
    hd = C_KEY_DIM
    kw = n_heads * hd

    @pl.when(pl.program_id(1) == 0)
    def _():
        state_ref[...] = jnp.zeros_like(state_ref)

    def take_rows(a, rows):
        runs, start = [], 0
        for n in range(1, len(rows) + 1):
            if n == len(rows) or rows[n] != rows[n - 1] + 1:
                runs.append(a[int(rows[start]):int(rows[n - 1]) + 1])
                start = n
        return runs[0] if len(runs) == 1 else jnp.concatenate(runs, axis=0)

    def spread_rows(a, rows):
        if len(rows) == CHUNK:
            return a
        pieces, pos, taken = [], 0, 0
        while pos < CHUNK:
            run = 1
            inside = pos in rows
            while pos + run < CHUNK and ((pos + run) in rows) == inside:
                run += 1
            if inside:
                pieces.append(a[taken:taken + run])
                taken += run
            else:
                pieces.append(jnp.zeros((run, a.shape[1]), a.dtype))
            pos += run
        return jnp.concatenate(pieces, axis=0)

    qs, ks, exps = [], [], []
    for bb in range(nb):
        qs.append(_silu(p_ref[bb, :, 0:kw]))
        log_f = _log_forget(p_ref[bb, :, kw:2 * kw], loglb_ref[...], log1mlb_ref[...])
        ks.append(1.0 - jnp.exp(log_f))
        exps.append(_dot(sums_ref[...], jnp.concatenate(_split3(log_f), axis=0)))

    idx = [(bb, h) for bb in range(nb) for h in range(n_heads)]
    states = [state_ref[bb * n_heads + h] for bb, h in idx]
    scores = [None] * len(idx)
    offset = 0
    for lvl in range(HGRN_LEVELS):
        q_rows, k_rows = _hgrn_level_rows(HGRN_LEVELS - 1 - lvl)
        q_rows, k_rows = [int(r) for r in q_rows], [int(r) for r in k_rows]
        e_q = [e[offset:offset + len(q_rows)] for e in exps]
        e_k = [e[offset + len(q_rows):offset + len(q_rows) + len(k_rows)] for e in exps]
        offset += len(q_rows) + len(k_rows)
        q_l = [(take_rows(qs[bb], q_rows) * jnp.exp(e_q[bb])).astype(BF16) for bb in range(nb)]
        k_l = [spread_rows(take_rows(ks[bb], k_rows) * jnp.exp(e_k[bb]), k_rows).astype(BF16) for bb in range(nb)]
        mask = None if (lvl == 0 and len(q_rows) < CHUNK) else pair_ref[lvl]
        for n, (bb, h) in enumerate(idx):
            sl = slice(h * hd, (h + 1) * hd)
            part = spread_rows(_dot_nt(q_l[bb][:, sl], k_l[bb][:, sl]), q_rows)
            if mask is not None:
                part = part * mask
            scores[n] = part if scores[n] is None else scores[n] + part
    g_cum = [e[offset:offset + CHUNK] for e in exps]
    q_g = [qs[bb] * jnp.exp(g_cum[bb]) for bb in range(nb)]
    k_r = [ks[bb] * jnp.exp(exps[bb][offset + CHUNK:offset + 2 * CHUNK]) for bb in range(nb)]
    diag = [qs[bb] * ks[bb] for bb in range(nb)]
    outs = []
    for n, (bb, h) in enumerate(idx):
        sl = slice(h * hd, (h + 1) * hd)
        v = p_ref[bb, :, 2 * kw + h * hd: 2 * kw + (h + 1) * hd]
        o = _dot_nt(q_g[bb][:, sl], states[n]) + _dot(scores[n], v)
        outs.append(o + jnp.sum(diag[bb][:, sl], axis=-1, keepdims=True) * v)
        p_last = jnp.exp(g_cum[bb][CHUNK - 1:CHUNK, sl])
        state_ref[bb * n_heads + h] = states[n] * p_last + _dot_tn(v, k_r[bb][:, sl])
    for n, (bb, h) in enumerate(idx):
        gate = p_ref[bb, :, 3 * kw + h * hd: 3 * kw + (h + 1) * hd]
        o_ref[bb, :, h * hd:(h + 1) * hd] = (_rms(outs[n], onorm_ref[...]) * _silu(gate)).astype(o_ref.dtype)


def _hgrn(proj, log_lb, log1m_lb, onorm):
    b, t, cols = proj.shape
    kw = cols // 4
    n_heads = kw // C_KEY_DIM
    nb = MIXER_BATCH if b % MIXER_BATCH == 0 else 1
    sums, pair = _hgrn_tables()
    sums = jnp.asarray(sums, BF16)
    pair = jnp.asarray(pair, F32)
    return pl.pallas_call(
        _hgrn_kernel,
        grid=(b // nb, t // CHUNK),
        in_specs=[pl.BlockSpec((nb, CHUNK, cols), lambda i, c: (i, c, 0)),
                  pl.BlockSpec((1, kw), lambda i, c: (0, 0)),
                  pl.BlockSpec((1, kw), lambda i, c: (0, 0)),
                  pl.BlockSpec((1, C_KEY_DIM), lambda i, c: (0, 0)),
                  pl.BlockSpec(sums.shape, lambda i, c: (0, 0)),
                  pl.BlockSpec(pair.shape, lambda i, c: (0, 0, 0))],
        out_specs=pl.BlockSpec((nb, CHUNK, kw), lambda i, c: (i, c, 0)),
        out_shape=jax.ShapeDtypeStruct((b, t, kw), BF16),
        scratch_shapes=[pltpu.VMEM((nb * n_heads, C_KEY_DIM, C_KEY_DIM), F32)],
        compiler_params=_cparams(("parallel", "arbitrary")),
        name="hgrn2",
    )(proj, log_lb, log1m_lb, onorm, sums, pair)


def _route(logits):
    lane = _iota2(logits.shape, 1)
    neg = -jnp.inf
    gmask = lane < N_GROUPS
    gl = jnp.where(gmask, logits, neg)
    gmax = jnp.max(gl, axis=-1, keepdims=True)
    gsel = jnp.min(jnp.where(gl == gmax, lane, LANES), axis=-1, keepdims=True)
    group_gate = 1.0 / jnp.sum(jnp.where(gmask, jnp.exp(gl - gmax), 0.0), axis=-1, keepdims=True)
    lo = N_GROUPS + EXPERTS_PER_GROUP * gsel
    emask = (lane >= lo) & (lane < lo + EXPERTS_PER_GROUP)
    el = jnp.where(emask, logits, neg)
    ee = jnp.where(emask, jnp.exp(el - jnp.max(el, axis=-1, keepdims=True)), 0.0)
    prob = ee / jnp.sum(ee, axis=-1, keepdims=True)
    p1 = jnp.max(jnp.where(emask, prob, -1.0), axis=-1, keepdims=True)
    i1 = jnp.min(jnp.where(emask & (prob == p1), lane, LANES), axis=-1, keepdims=True)
    rest = emask & (lane != i1)
    p2 = jnp.max(jnp.where(rest, prob, -1.0), axis=-1, keepdims=True)
    i2 = jnp.min(jnp.where(rest & (prob == p2), lane, LANES), axis=-1, keepdims=True)
    scale = group_gate / (p1 + p2)
    comb = jnp.where(lane == i1, p1 * scale, 0.0) + jnp.where(lane == i2, p2 * scale, 0.0)
    return comb, gsel


def _sorted_rows(tile_rows):
    return tile_rows + N_GROUPS * SEG_ALIGN


def _xattn_route_kernel(x_ref, a_ref, wmix_ref, gain_ref, wq_ref, k_ref, v_ref, wo_ref, gffn_ref, wr_ref, br_ref,
                        x2_ref, xs_ref, cs_ref, info_ref, seg_ref):
    n_sub = xs_ref.shape[0]
    tq = x_ref.shape[0] // n_sub
    d = x_ref.shape[1]
    hd = d // XA_HEADS
    rows = xs_ref.shape[1]

    def tile_stages(u):
        tok = slice(u * tq, (u + 1) * tq)
        x = x_ref[tok, :] + _dot(a_ref[tok, :], wmix_ref[...])
        q = _dot(_rms(x, gain_ref[...]).astype(BF16), wq_ref[...]).astype(BF16)
        yield
        outs = []
        for h in range(XA_HEADS):
            sl = slice(h * hd, (h + 1) * hd)
            s = _dot_nt(q[:, sl], k_ref[:, sl]) * (hd ** -0.5)
            p = jnp.exp(s - jnp.max(s, axis=-1, keepdims=True))
            p = p / jnp.sum(p, axis=-1, keepdims=True)
            outs.append(_dot(p.astype(BF16), v_ref[:, sl]).astype(BF16))
        yield
        x2 = x + _dot(jnp.concatenate(outs, axis=-1), wo_ref[...])
        x2_ref[tok, :] = x2
        xn = _rms(x2, gffn_ref[...])
        xn_hi = xn.astype(BF16)
        xn_lo = (xn - xn_hi.astype(F32)).astype(BF16)
        hi_part = _dot(xn_hi, wr_ref[...])
        logits = hi_part[:, 0:LANES] + hi_part[:, LANES:2 * LANES] + _dot(xn_lo, wr_ref[:, 0:LANES])
        yield
        comb, gsel = _route(logits + br_ref[...])
        lane = _iota2((tq, LANES), 1)
        onehot = (lane == gsel).astype(F32)
        earlier = (_iota2((tq, tq), 0) > _iota2((tq, tq), 1)).astype(BF16)
        before = _dot(earlier, onehot.astype(BF16))
        yield
        rank = jnp.sum(onehot * before, axis=-1, keepdims=True)
        counts = before[tq - 1:tq, :] + onehot[tq - 1:tq, :]
        padded = jnp.floor((counts + (SEG_ALIGN - 1)) * (1.0 / SEG_ALIGN)) * SEG_ALIGN
        lane1 = _iota2((1, LANES), 1)
        starts = jnp.zeros((1, LANES), F32)
        for g in range(N_GROUPS - 1):
            starts = starts + jnp.where(lane1 > g, padded[:, g:g + 1], 0.0)
        dest = jnp.sum(onehot * starts, axis=-1, keepdims=True) + rank
        place = (_iota2((tq, rows), 1).astype(F32) == dest).astype(BF16)
        comb_hi = comb.astype(BF16)
        comb_lo = (comb - comb_hi.astype(F32)).astype(BF16)
        placed = _dot_tn(place, jnp.concatenate([xn_hi, comb_hi, comb_lo], axis=1))
        xs_ref[u] = placed[:, 0:d].astype(BF16)
        cs_ref[u] = placed[:, d:d + LANES] + placed[:, d + LANES:d + 2 * LANES]
        info_ref[tok, :] = jnp.where(lane == 0, dest, 0.0)
        seg_ref[u] = jnp.concatenate([starts, padded, jnp.zeros((6, LANES), F32)], axis=0)

    _interleave([tile_stages(u) for u in range(n_sub)])


def _xattn_route(x, act, w_mix, gain, wq, kv, wo, gain_ffn, w_router, b_router, layer, tq=512):
    b, t, d = x.shape
    m = kv.shape[1]
    tq = min(tq, t)
    tiles_per_row = t // tq
    n_tiles = b * tiles_per_row
    rows = _sorted_rows(tq)
    n_sub = XATTN_TILES_PER_STEP if tiles_per_row % XATTN_TILES_PER_STEP == 0 else 1
    steps_per_row = tiles_per_row // n_sub

    def tile(i, j):
        return (i * steps_per_row + j, 0, 0)

    return pl.pallas_call(
        _xattn_route_kernel,
        grid=(b, steps_per_row),
        in_specs=[pl.BlockSpec((None, n_sub * tq, d), lambda i, j: (i, j, 0)),
                  pl.BlockSpec((None, n_sub * tq, act.shape[2]), lambda i, j: (i, j, 0)),
                  pl.BlockSpec(w_mix.shape, lambda i, j: (0, 0)),
                  pl.BlockSpec((1, d), lambda i, j: (0, 0)),
                  pl.BlockSpec((None, d, d), lambda i, j: (layer, 0, 0)),
                  pl.BlockSpec((None, m, d), lambda i, j: (i, 0, 0)),
                  pl.BlockSpec((None, m, d), lambda i, j: (i, 0, 1)),
                  pl.BlockSpec((None, d, d), lambda i, j: (layer, 0, 0)),
                  pl.BlockSpec((1, d), lambda i, j: (0, 0)),
                  pl.BlockSpec((d, 2 * LANES), lambda i, j: (0, 0)),
                  pl.BlockSpec((1, LANES), lambda i, j: (0, 0))],
        out_specs=[pl.BlockSpec((None, n_sub * tq, d), lambda i, j: (i, j, 0)),
                   pl.BlockSpec((n_sub, rows, d), tile),
                   pl.BlockSpec((n_sub, rows, LANES), tile),
                   pl.BlockSpec((None, n_sub * tq, LANES), lambda i, j: (i, j, 0)),
                   pl.BlockSpec((n_sub, 8, LANES), tile)],
        out_shape=[jax.ShapeDtypeStruct((b, t, d), F32),
                   jax.ShapeDtypeStruct((n_tiles, rows, d), BF16),
                   jax.ShapeDtypeStruct((n_tiles, rows, LANES), F32),
                   jax.ShapeDtypeStruct((b, t, LANES), F32),
                   jax.ShapeDtypeStruct((n_tiles, 8, LANES), F32)],
        compiler_params=_cparams(("parallel", "parallel")),
        name="xattn_route",
    )(x, act, w_mix, gain.reshape(1, d), wq, kv, kv, wo, gain_ffn.reshape(1, d), w_router, b_router)


def _moe_ffn_kernel(seg_ref, xs_ref, cs_ref, w1_ref, w3_ref, w2_ref, ys_ref, xbuf_ref, cbuf_ref, ybuf_ref):
    win = pl.program_id(0)
    grp = pl.program_id(1)
    n_win = xs_ref.shape[0]

    @pl.when((win == 0) & (grp == 0))
    def _():
        xbuf_ref[...] = jnp.zeros_like(xbuf_ref)
        cbuf_ref[...] = jnp.zeros_like(cbuf_ref)

    @pl.when(grp == 0)
    def _():
        ys_ref[...] = jnp.zeros_like(ys_ref)

    def segment(t):
        base = (win * n_win + t) * 8
        return seg_ref[base + grp], seg_ref[base + N_GROUPS + grp]

    def copy_segments(copy_piece):
        packed = jnp.int32(0)
        for t in range(n_win):
            start, size = segment(t)

            def body(j, carry, t=t, start=start, packed=packed):
                copy_piece(t, pl.multiple_of(start + j * SEG_ALIGN, SEG_ALIGN),
                           pl.multiple_of(packed + j * SEG_ALIGN, SEG_ALIGN))
                return carry

            lax.fori_loop(0, size // SEG_ALIGN, body, 0)
            packed = packed + size
        return packed

    def gather(t, src, dst):
        xbuf_ref[pl.ds(dst, SEG_ALIGN), :] = xs_ref[t, pl.ds(src, SEG_ALIGN), :]
        cbuf_ref[pl.ds(dst, SEG_ALIGN), :] = cs_ref[t, pl.ds(src, SEG_ALIGN), :]

    total = copy_segments(gather)

    def ffn_piece(r0, n_rows):
        xc = xbuf_ref[pl.ds(r0, n_rows), :]
        cb = cbuf_ref[pl.ds(r0, n_rows), :]
        lane = _iota2(cb.shape, 1)
        hids = []
        for e in range(EXPERTS_PER_GROUP):
            weight = jnp.sum(jnp.where(lane == N_GROUPS + EXPERTS_PER_GROUP * grp + e, cb, 0.0),
                             axis=-1, keepdims=True)
            hids.append((_silu(_dot(xc, w1_ref[e])) * _dot(xc, w3_ref[e]) * weight).astype(BF16))
        ybuf_ref[pl.ds(r0, n_rows), :] = _dot(jnp.concatenate(hids, axis=-1), w2_ref[...]).astype(ybuf_ref.dtype)

    def chunk(c, carry):
        ffn_piece(pl.multiple_of(c * MOE_CHUNK, MOE_CHUNK), MOE_CHUNK)
        return carry

    n_full = total // MOE_CHUNK
    lax.fori_loop(0, n_full, chunk, 0)
    rest = total - n_full * MOE_CHUNK
    rest_start = pl.multiple_of(n_full * MOE_CHUNK, MOE_CHUNK)
    lower = 0
    for n_rows in (MOE_CHUNK // 2, MOE_CHUNK):
        @pl.when((rest > lower) & (rest <= n_rows))
        def _(n_rows=n_rows):
            ffn_piece(rest_start, n_rows)
        lower = n_rows

    def scatter(t, dst, src):
        ys_ref[t, pl.ds(dst, SEG_ALIGN), :] = ybuf_ref[pl.ds(src, SEG_ALIGN), :]

    copy_segments(scatter)


def _moe_ffn(seg, xs, cs, w1, w3, w2g, layer):
    n_tiles, rows, d = xs.shape
    ff = w1.shape[3]
    n_win = next(w for w in (MOE_WINDOW, 2, 1) if n_tiles % w == 0)
    buf_rows = -(-(n_win * rows) // MOE_CHUNK) * MOE_CHUNK
    grid_spec = pltpu.PrefetchScalarGridSpec(
        num_scalar_prefetch=1,
        grid=(n_tiles // n_win, N_GROUPS),
        in_specs=[pl.BlockSpec((n_win, rows, d), lambda w, g, seg: (w, 0, 0)),
                  pl.BlockSpec((n_win, rows, LANES), lambda w, g, seg: (w, 0, 0)),
                  pl.BlockSpec((None, EXPERTS_PER_GROUP, d, ff), lambda w, g, seg: (layer, g, 0, 0)),
                  pl.BlockSpec((None, EXPERTS_PER_GROUP, d, ff), lambda w, g, seg: (layer, g, 0, 0)),
                  pl.BlockSpec((None, None, EXPERTS_PER_GROUP * ff, d), lambda w, g, seg: (layer, g, 0, 0))],
        out_specs=pl.BlockSpec((n_win, rows, d), lambda w, g, seg: (w, 0, 0)),
        scratch_shapes=[pltpu.VMEM((buf_rows, d), BF16),
                        pltpu.VMEM((buf_rows, LANES), F32),
                        pltpu.VMEM((buf_rows, d), BF16)])
    return pl.pallas_call(
        _moe_ffn_kernel,
        grid_spec=grid_spec,
        out_shape=jax.ShapeDtypeStruct((n_tiles, rows, d), BF16),
        compiler_params=_cparams(("arbitrary", "arbitrary")),
        name="moe_ffn",
    )(seg, xs, cs, w1, w3, w2g)


def _unsorted_sum(x_ref, info_ref, ys_ref):
    info = info_ref[...]
    dest = jnp.sum(jnp.where(_iota2(info.shape, 1) == 0, info, 0.0), axis=-1, keepdims=True)
    rows = ys_ref.shape[0]
    place = (_iota2((info.shape[0], rows), 1).astype(F32) == dest).astype(BF16)
    return x_ref[...] + _dot(place, ys_ref[...])


def _moe_unsort_kernel(n_w, x_ref, info_ref, ys_ref, gain_ref, *refs):
    w_refs, o_refs = refs[:n_w], refs[n_w:]
    if n_w == 0:
        tq = x_ref.shape[0] // ys_ref.shape[0]
        for u in range(ys_ref.shape[0]):
            tok = slice(u * tq, (u + 1) * tq)
            y = _unsorted_sum(x_ref.at[tok, :], info_ref.at[tok, :], ys_ref.at[u])
            o_refs[0][tok, :] = _rms(y, gain_ref[...])
        return
    y = _unsorted_sum(x_ref, info_ref, ys_ref.at[0])
    yn = _rms(y, gain_ref[...])
    o_refs[0][...] = y
    yn = yn.astype(BF16)
    for w_ref, o_ref in zip(w_refs, o_refs[1:]):
        n = w_ref.shape[1]
        for c0 in range(0, n, 512):
            c1 = min(n, c0 + 512)
            o_ref[:, c0:c1] = _dot(yn, w_ref[:, c0:c1])


def _moe_unsort(x, info, ys, gain, ws=()):
    b, t, d = x.shape
    n_tiles, rows, _ = ys.shape
    tiles_per_row = n_tiles // b
    tq = t // tiles_per_row
    n_sub = 2 if (not ws and tiles_per_row % 2 == 0) else 1
    steps_per_row = tiles_per_row // n_sub

    def token_block(cols):
        return pl.BlockSpec((None, n_sub * tq, cols), lambda i, j: (i, j, 0))

    out_cols = [d] + [w.shape[1] for w in ws]
    outs = pl.pallas_call(
        functools.partial(_moe_unsort_kernel, len(ws)),
        grid=(b, steps_per_row),
        in_specs=[token_block(d), token_block(LANES),
                  pl.BlockSpec((n_sub, rows, d), lambda i, j: (i * steps_per_row + j, 0, 0)),
                  pl.BlockSpec((1, d), lambda i, j: (0, 0))]
                 + [pl.BlockSpec(w.shape, lambda i, j: (0, 0)) for w in ws],
        out_specs=[token_block(c) for c in out_cols],
        out_shape=[jax.ShapeDtypeStruct((b, t, c), F32) for c in out_cols],
        compiler_params=_cparams(("parallel", "parallel")),
        name="moe_unsort",
    )(x, info, ys, gain.reshape(1, d), *ws)
    return outs


def _pad_cols(w, n):
    return jnp.pad(w, ((0, 0), (0, n - w.shape[1])))


def _pad_rows(w, n):
    return jnp.pad(w, ((0, n - w.shape[0]), (0, 0)))


def _row(v, n=None):
    v = v.reshape(1, -1).astype(F32)
    return v if n is None else _pad_cols(v, n)


def kernel(x, mem, norm_mix, norm_xattn, norm_mem, norm_ffn, norm_final, ab_w_in, ab_conv, ab_a_log, ab_dt_bias, ab_onorm, ab_shift_mu, ab_w0, ab_w2, ab_a0, ab_a2, ab_g2, ab_k_k, ab_k_a, ab_r_k, ab_ln_w, ab_ln_b, ab_w_out, c_w_in, c_lb_param, c_onorm, c_w_out, xa_wq, xa_wkv, xa_wo, moe_w_group, moe_b_group, moe_w_expert, moe_b_expert, moe_w1, moe_w3, moe_w2):
    bsz, seq, d = x.shape
    n_mem = mem.shape[1]
    depth = norm_mix.shape[0]
    a_width = d // 2
    b_width = d // 2
    a_heads = a_width // A_HEAD_DIM
    a_cols = 4 * a_width + 2 * a_heads
    n_tok = bsz * seq

    lb_p = jax.nn.softmax(c_lb_param.astype(F32), axis=0)
    lower_bounds = jnp.cumsum(lb_p, axis=0) - lb_p[0]

    def project(x_cur, moe_pending, gain, ws):
        if moe_pending is None:
            projs = _norm_proj(x_cur.reshape(n_tok, d), gain, ws)
            return x_cur, [p.reshape(bsz, seq, -1) for p in projs]
        x_new, *projs = _moe_unsort(*moe_pending, gain, ws)
        return x_new, projs

    xa_wq_bf, xa_wkv_bf, xa_wo_bf = xa_wq.astype(BF16), xa_wkv.astype(BF16), xa_wo.astype(BF16)
    ff = moe_w2.shape[2]
    moe_w1_bf = moe_w1.astype(BF16)
    moe_w3_bf = moe_w3.astype(BF16)
    moe_w2_bf = moe_w2.astype(BF16).reshape(depth, N_GROUPS, EXPERTS_PER_GROUP * ff, d)

    x2 = x
    moe_pending = None
    mem2 = mem.reshape(bsz * n_mem, d)
    for layer in range(depth):
        if layer % 2 == 0:
            e = layer // 2
            w_in = ab_w_in[e]
            w_a = w_in[:, :4 * a_width].astype(BF16)
            w_db = _pad_cols(w_in[:, 4 * a_width:a_cols], LANES).astype(BF16)
            wb = w_in[:, a_cols:]
            o1, o2 = 3 * b_width, 3 * b_width + B_DECAY_LORA
            o3 = o2 + B_ICLR_LORA
            w_b = jnp.concatenate([wb[:, :o1], _pad_cols(wb[:, o1:o2], LANES), _pad_cols(wb[:, o2:o3], LANES),
                                   wb[:, o3:]], axis=1).astype(BF16)
            mu = ab_shift_mu[e].reshape(1, -1)
            mu_p = jnp.concatenate([mu[:, :o1], _pad_cols(mu[:, o1:o2], LANES), _pad_cols(mu[:, o2:o3], LANES),
                                    mu[:, o3:]], axis=1)
            head_id = jnp.arange(b_width) // B_HEAD_DIM
            bd = (head_id[:, None] == head_id[None, :]).astype(F32)
            gdn_params = [ab_conv[e], _row(ab_a_log[e], LANES), _row(ab_dt_bias[e], LANES), _row(ab_onorm[e])]
            rwkv_params = [mu_p, _row(ab_w0[e]), _pad_rows(ab_w2[e], LANES), _row(ab_a0[e]),
                           _pad_rows(ab_a2[e], LANES), ab_g2[e], _row(ab_k_k[e]), _row(ab_k_a[e]),
                           _row(ab_r_k[e]), _row(ab_ln_w[e]), _row(ab_ln_b[e]), bd]
            x2, (pa, pdb, pb) = project(x2, moe_pending, norm_mix[layer], [w_a, w_db, w_b])
            mix = _mixer_ab(pa, pdb, pb, gdn_params, rwkv_params)
            w_mix = ab_w_out[e].astype(BF16)
        else:
            o = layer // 2
            x2, (proj,) = project(x2, moe_pending, norm_mix[layer], [c_w_in[o].astype(BF16)])
            lb = lower_bounds[layer].reshape(1, -1)
            mix = _hgrn(proj, jnp.log(lb), jnp.log1p(-lb), _row(c_onorm[o]))
            w_mix = c_w_out[o].astype(BF16)

        (kv_mem,) = _norm_proj(mem2, norm_mem[layer], [xa_wkv_bf[layer]], out_dtype=BF16)
        w_r = _pad_cols(jnp.concatenate([moe_w_group[layer], moe_w_expert[layer]], axis=1), LANES).astype(F32)
        w_r_hi = w_r.astype(BF16)
        w_r_lo = (w_r - w_r_hi.astype(F32)).astype(BF16)
        b_router = _row(jnp.concatenate([moe_b_group[layer], moe_b_expert[layer]]), LANES)
        x3, xs, cs, info, seg = _xattn_route(
            x2, mix, w_mix, norm_xattn[layer], xa_wq_bf, kv_mem.reshape(bsz, n_mem, 2 * d), xa_wo_bf,
            norm_ffn[layer], jnp.concatenate([w_r_hi, w_r_lo], axis=1), b_router, layer)
        seg_i = jnp.concatenate([seg[:, 0, :N_GROUPS], seg[:, 1, :N_GROUPS]], axis=1).astype(jnp.int32).reshape(-1)
        ys = _moe_ffn(seg_i, xs, cs, moe_w1_bf, moe_w3_bf, moe_w2_bf, layer)
        moe_pending = (x3, info, ys)
    (out,) = _moe_unsort(*moe_pending, norm_final)
    return out
```
